```python
import math
import jax, jax.numpy as jnp
from jax import lax
import numpy as np

D_MODEL = 2048
BATCH = 2
SEQ = 4096
DEPTH = 1

HEAD_DIM = 128
DIL_PATTERNS = ((128, 1), (512, 4), (2048, 16))
A_HEADS_PER_GROUP = 4
A_HEADS = A_HEADS_PER_GROUP * len(DIL_PATTERNS)
A_OUT = A_HEADS_PER_GROUP * HEAD_DIM
B_HEADS = 8
B_WIDTH = B_HEADS * HEAD_DIM
MOBA_BLOCK = 256
MOBA_TOPK = 3
MOBA_CHUNK = 32
N_HEADS_TOTAL = A_HEADS + B_HEADS
N_BUCKETS = 32
MAX_DISTANCE = 2048
D_FF = 4 * D_MODEL
EPS = 1e-6
SCALE = HEAD_DIM ** -0.5
A_QKV = 3 * A_HEADS * HEAD_DIM
B_QKV = 3 * B_WIDTH
IN_WIDTH = A_QKV + B_QKV + 2 * D_MODEL

kernel_name = "hybrid_dilated_moba_gated_block"


def rms_norm(x, g):
    x32 = x.astype(jnp.float32)
    y = x32 * lax.rsqrt(jnp.mean(x32 * x32, axis=-1, keepdims=True) + EPS)
    return y.astype(x.dtype) * g


def rel_bucket(dist):
    max_exact = N_BUCKETS // 2
    d = jnp.maximum(dist, 0)
    df = jnp.maximum(d, 1).astype(jnp.float32)
    large = max_exact + (jnp.log(df / max_exact) / math.log(MAX_DISTANCE / max_exact)
                         * (N_BUCKETS - max_exact)).astype(jnp.int32)
    large = jnp.minimum(large, N_BUCKETS - 1)
    return jnp.where(d < max_exact, d, large)


def dilated_group(q, k, v, bias_tab, window, dilation):
    Bsz, H, S, hd = q.shape
    r = dilation
    W = window // r
    Q = W
    L = S // r
    nb = -(-L // Q)
    Lp = nb * Q

    def to_sub(t):
        t = t.reshape(Bsz, H, L, r, hd).transpose(0, 1, 3, 2, 4)
        t = jnp.pad(t, ((0, 0), (0, 0), (0, 0), (0, Lp - L), (0, 0)))
        return t.reshape(Bsz, H, r, nb, Q, hd)

    qs, ks, vs = to_sub(q), to_sub(k), to_sub(v)

    def with_prev(t):
        prev = jnp.pad(t, ((0, 0), (0, 0), (0, 0), (1, 0), (0, 0), (0, 0)))[:, :, :, :nb]
        return jnp.concatenate([prev, t], axis=4)

    kk, vv = with_prev(ks), with_prev(vs)
    a = jnp.arange(Q)[:, None]
    j = jnp.arange(2 * Q)[None, :]
    delta = a + Q - j
    in_band = (delta >= 0) & (delta <= W)
    valid_start = (jnp.arange(nb)[:, None, None] > 0) | (j[None] >= Q)
    mask = in_band[None] & valid_start
    bias = bias_tab[rel_bucket(delta * r)].transpose(2, 0, 1)

    logits = (jnp.einsum('bhrnqd,bhrnkd->bhrnqk', qs, kk).astype(jnp.float32) * SCALE
              + bias[:, None, None].astype(jnp.float32))
    logits = jnp.where(mask, logits, -jnp.inf)
    m = jnp.max(logits, axis=-1, keepdims=True)
    p = jnp.exp(logits - m)
    den = jnp.sum(p, axis=-1, keepdims=True)
    out = jnp.einsum('bhrnqk,bhrnkd->bhrnqd', (p / den).astype(v.dtype), vv)
    lse = (m + jnp.log(den))[..., 0]

    out = out.reshape(Bsz, H, r, Lp, hd)[:, :, :, :L].transpose(0, 1, 3, 2, 4).reshape(Bsz, H, S, hd)
    lse = lse.reshape(Bsz, H, r, Lp)[:, :, :, :L].transpose(0, 1, 3, 2).reshape(Bsz, H, S)
    return out, lse


def moba_attention(q, k, v, bias_tab):
    Bsz, H, S, hd = q.shape
    BS = MOBA_BLOCK
    nb = -(-S // BS)
    Sp = nb * BS
    pad = ((0, 0), (0, 0), (0, Sp - S), (0, 0))
    kb = jnp.pad(k, pad).reshape(Bsz, H, nb, BS, hd)
    vb = jnp.pad(v, pad).reshape(Bsz, H, nb, BS, hd)

    kmean = jnp.mean(kb.astype(jnp.float32), axis=3)
    gate = jnp.einsum('bhsd,bhnd->bhsn', q.astype(jnp.float32), kmean)
    pos = jnp.arange(S)
    past = jnp.arange(nb)[None, :] < (pos // BS)[:, None]
    gate = jnp.where(past, gate, -jnp.inf)
    n_sel = min(MOBA_TOPK, nb)
    top_s, top_i = lax.top_k(gate, n_sel)
    top_ok = top_s > -jnp.inf

    C = MOBA_CHUNK
    nc = S // C
    b_ix = jnp.arange(Bsz)[:, None, None, None]
    h_ix = jnp.arange(H)[None, :, None, None]
    bias_h = bias_tab.T

    def chunk(c):
        s0 = c * C
        qc = lax.dynamic_slice_in_dim(q, s0, C, axis=2)
        ic = lax.dynamic_slice_in_dim(top_i, s0, C, axis=2)
        okc = lax.dynamic_slice_in_dim(top_ok, s0, C, axis=2)
        qpos = s0 + jnp.arange(C)
        kg = kb[b_ix, h_ix, ic]
        vg = vb[b_ix, h_ix, ic]
        dist_sel = qpos[:, None, None] - (ic[..., None] * BS + jnp.arange(BS))
        l_sel = (jnp.einsum('bhcd,bhckjd->bhckj', qc, kg).astype(jnp.float32) * SCALE
                 + bias_h[h_ix[..., None], rel_bucket(dist_sel)].astype(jnp.float32))
        l_sel = jnp.where(okc[..., None], l_sel, -jnp.inf).reshape(Bsz, H, C, n_sel * BS)

        blk = s0 // BS
        ko = lax.dynamic_slice_in_dim(kb, blk, 1, axis=2)[:, :, 0]
        vo = lax.dynamic_slice_in_dim(vb, blk, 1, axis=2)[:, :, 0]
        dist_own = qpos[:, None] - (blk * BS + jnp.arange(BS))[None, :]
        l_own = (jnp.einsum('bhcd,bhjd->bhcj', qc, ko).astype(jnp.float32) * SCALE
                 + bias_h[:, rel_bucket(dist_own)][None].astype(jnp.float32))
        l_own = jnp.where(dist_own >= 0, l_own, -jnp.inf)

        p = jax.nn.softmax(jnp.concatenate([l_sel, l_own], axis=-1), axis=-1).astype(v.dtype)
        p_sel = p[..., :n_sel * BS].reshape(Bsz, H, C, n_sel, BS)
        p_own = p[..., n_sel * BS:]
        return (jnp.einsum('bhckj,bhckjd->bhcd', p_sel, vg)
                + jnp.einsum('bhcj,bhjd->bhcd', p_own, vo))

    out = lax.map(chunk, jnp.arange(nc))
    return out.transpose(1, 2, 0, 3, 4).reshape(Bsz, H, S, hd)


def head_rms(t, g):
    t32 = t.astype(jnp.float32)
    y = t32 * lax.rsqrt(jnp.mean(t32 * t32, axis=-1, keepdims=True) + EPS)
    return y.astype(t.dtype) * g


def setup_inputs(seed: int = 0) -> dict:
    key = jax.random.key(seed)
    ks = jax.random.split(key, 16)
    f32 = jnp.float32
    nrm = lambda k, shape, fan: jax.random.normal(k, shape, f32) * (fan ** -0.5)
    gain = lambda k, n: 1.0 + 0.02 * jax.random.normal(k, (n,), f32)
    return {
        "x": jax.random.normal(ks[0], (BATCH, SEQ, D_MODEL), f32),
        "g_mix": gain(ks[1], D_MODEL),
        "w_in": nrm(ks[2], (D_MODEL, IN_WIDTH), D_MODEL),
        "q_norm_a": gain(ks[3], HEAD_DIM),
        "k_norm_a": gain(ks[4], HEAD_DIM),
        "q_norm_b": gain(ks[5], HEAD_DIM),
        "k_norm_b": gain(ks[6], HEAD_DIM),
        "rel_bias": 0.5 * jax.random.normal(ks[7], (N_BUCKETS, N_HEADS_TOTAL), f32),
        "w_branch_a": nrm(ks[8], (A_OUT, D_MODEL), A_OUT),
        "w_branch_b": nrm(ks[9], (B_WIDTH, D_MODEL), B_WIDTH),
        "w_out": nrm(ks[10], (D_MODEL, D_MODEL), D_MODEL),
        "g_mlp": gain(ks[11], D_MODEL),
        "w_up": nrm(ks[12], (D_MODEL, D_FF), D_MODEL),
        "w_down": nrm(ks[13], (D_FF, D_MODEL), D_FF),
    }


def reference(x, g_mix, w_in, q_norm_a, k_norm_a, q_norm_b, k_norm_b, rel_bias,
              w_branch_a, w_branch_b, w_out, g_mlp, w_up, w_down):
    Bsz, S, D = x.shape
    for _ in range(DEPTH):
        h = rms_norm(x, g_mix)
        proj = h @ w_in
        a_qkv = proj[..., :A_QKV].reshape(Bsz, S, 3, A_HEADS, HEAD_DIM)
        b_qkv = proj[..., A_QKV:A_QKV + B_QKV].reshape(Bsz, S, 3, B_HEADS, HEAD_DIM)
        g_a = proj[..., A_QKV + B_QKV:A_QKV + B_QKV + D_MODEL]
        g_b = proj[..., A_QKV + B_QKV + D_MODEL:]

        qa = head_rms(a_qkv[:, :, 0], q_norm_a).transpose(0, 2, 1, 3)
        ka = head_rms(a_qkv[:, :, 1], k_norm_a).transpose(0, 2, 1, 3)
        va = a_qkv[:, :, 2].transpose(0, 2, 1, 3)
        outs, lses = [], []
        for gi, (win, dil) in enumerate(DIL_PATTERNS):
            hs = slice(gi * A_HEADS_PER_GROUP, (gi + 1) * A_HEADS_PER_GROUP)
            o, l = dilated_group(qa[:, hs], ka[:, hs], va[:, hs], rel_bias[:, hs], win, dil)
            outs.append(o)
            lses.append(l)
        wts = jax.nn.softmax(jnp.stack(lses, axis=0), axis=0)
        ya = jnp.sum(wts[..., None] * jnp.stack(outs, axis=0).astype(jnp.float32), axis=0).astype(x.dtype)
        ya = ya.transpose(0, 2, 1, 3).reshape(Bsz, S, A_OUT) @ w_branch_a

        qb = head_rms(b_qkv[:, :, 0], q_norm_b).transpose(0, 2, 1, 3)
        kb = head_rms(b_qkv[:, :, 1], k_norm_b).transpose(0, 2, 1, 3)
        vb = b_qkv[:, :, 2].transpose(0, 2, 1, 3)
        yb = moba_attention(qb, kb, vb, rel_bias[:, A_HEADS:])
        yb = yb.transpose(0, 2, 1, 3).reshape(Bsz, S, B_WIDTH) @ w_branch_b

        mixed = jax.nn.sigmoid(g_a) * ya + jax.nn.sigmoid(g_b) * yb
        x = x + mixed @ w_out

        h2 = rms_norm(x, g_mlp)
        x = x + jnp.square(jax.nn.relu(h2 @ w_up)) @ w_down
    return x
```

```python
import functools
import math

import jax
import jax.numpy as jnp
from jax import lax
from jax.experimental import pallas as pl
from jax.experimental.pallas import tpu as pltpu

HEAD_DIM = 128
DIL_PATTERNS = ((128, 1), (512, 4), (2048, 16))
N_GROUPS = len(DIL_PATTERNS)
A_HEADS_PER_GROUP = 4
A_HEADS = A_HEADS_PER_GROUP * N_GROUPS
A_OUT = A_HEADS_PER_GROUP * HEAD_DIM
A_BLOCK = 128
B_HEADS = 8
B_WIDTH = B_HEADS * HEAD_DIM
MOBA_BLOCK = 256
MOBA_TOPK = 3
N_BUCKETS = 32
MAX_DISTANCE = 2048
EPS = 1e-6
SCALE = HEAD_DIM ** -0.5
A_QKV = 3 * A_HEADS * HEAD_DIM
B_QKV = 3 * B_WIDTH
QKV_WIDTH = A_QKV + B_QKV
QKV_PAD = 8192
LOG2E = math.log2(math.e)
NEG = -1e30
VMEM_LIMIT = 56 * 1024 * 1024


def _bucket_thresholds():
    max_exact = N_BUCKETS // 2

    def bucket(d):
        if d < max_exact:
            return d
        v = int(math.log(d / max_exact) / math.log(MAX_DISTANCE / max_exact) * (N_BUCKETS - max_exact))
        return min(max_exact + v, N_BUCKETS - 1)

    thr, d = [0], 0
    for b in range(1, N_BUCKETS):
        while bucket(d) < b:
            d += 1
        thr.append(d)
    return tuple(thr)


BUCKET_THRESHOLDS = _bucket_thresholds()
MOBA_BIAS_TILES = -(-(BUCKET_THRESHOLDS[-1] - 1) // MOBA_BLOCK) + 2


def _bias_lookup(dist, tab_ref, col):
    val = jnp.full(dist.shape, tab_ref[0, col], jnp.float32)
    for b in range(1, N_BUCKETS):
        val = jnp.where(dist >= BUCKET_THRESHOLDS[b], tab_ref[b, col], val)
    return val * LOG2E


def _dil_bias_kernel(tab_ref, o_ref):
    g, h = pl.program_id(0), pl.program_id(1)
    a = lax.broadcasted_iota(jnp.int32, (A_BLOCK, 2 * A_BLOCK), 0)
    j = lax.broadcasted_iota(jnp.int32, (A_BLOCK, 2 * A_BLOCK), 1)
    delta = a + A_BLOCK - j
    dilation = lax.shift_left(jnp.int32(1), 2 * g)
    val = _bias_lookup(delta * dilation, tab_ref, g * A_HEADS_PER_GROUP + h)
    o_ref[0, 0] = jnp.where((delta >= 0) & (delta <= A_BLOCK), val, NEG)


def _moba_bias_kernel(tab_ref, o_ref):
    h, c = pl.program_id(0), pl.program_id(1)
    j = lax.broadcasted_iota(jnp.int32, (MOBA_BLOCK, MOBA_BLOCK), 0)
    s = lax.broadcasted_iota(jnp.int32, (MOBA_BLOCK, MOBA_BLOCK), 1)
    dist = c * MOBA_BLOCK + s - j
    val = _bias_lookup(dist, tab_ref, A_HEADS + h)
    o_ref[0, 0] = jnp.where(dist >= 0, val, NEG)


def _bias_tables(rel_bias):
    smem = pl.BlockSpec(memory_space=pltpu.SMEM)
    dil = pl.pallas_call(
        _dil_bias_kernel,
        grid=(N_GROUPS, A_HEADS_PER_GROUP),
        in_specs=[smem],
        out_specs=pl.BlockSpec((1, 1, A_BLOCK, 2 * A_BLOCK), lambda g, h: (g, h, 0, 0)),
        out_shape=jax.ShapeDtypeStruct((N_GROUPS, A_HEADS_PER_GROUP, A_BLOCK, 2 * A_BLOCK), jnp.float32),
        name="dil_bias",
    )(rel_bias)
    moba = pl.pallas_call(
        _moba_bias_kernel,
        grid=(B_HEADS, MOBA_BIAS_TILES),
        in_specs=[smem],
        out_specs=pl.BlockSpec((1, 1, MOBA_BLOCK, MOBA_BLOCK), lambda h, c: (h, c, 0, 0)),
        out_shape=jax.ShapeDtypeStruct((B_HEADS, MOBA_BIAS_TILES, MOBA_BLOCK, MOBA_BLOCK), jnp.float32),
        name="moba_bias",
    )(rel_bias)
    return dil, moba


def _rms_rows(x, g):
    return (x * lax.rsqrt(jnp.mean(x * x, axis=-1, keepdims=True) + EPS)) * g


def _in_proj_kernel(x_ref, g_ref, w_ref, gain_ref, o_ref, h_ref, *, tn):
    j = pl.program_id(1)

    @pl.when(j == 0)
    def _():
        h_ref[...] = _rms_rows(x_ref[...], g_ref[...]).astype(h_ref.dtype)

    acc = jnp.dot(h_ref[...], w_ref[...], preferred_element_type=jnp.float32)
    col0 = j * tn
    is_norm = (col0 < 2 * A_HEADS * HEAD_DIM) | ((col0 >= A_QKV) & (col0 < A_QKV + 2 * B_WIDTH))
    is_gate = col0 >= QKV_WIDTH

    @pl.when(is_norm)
    def _():
        for c in range(tn // HEAD_DIM):
            cols = slice(c * HEAD_DIM, (c + 1) * HEAD_DIM)
            o_ref[:, cols] = _rms_rows(acc[:, cols], gain_ref[:, cols]).astype(o_ref.dtype)

    @pl.when(is_gate)
    def _():
        o_ref[...] = jax.nn.sigmoid(acc).astype(o_ref.dtype)

    @pl.when(jnp.logical_not(is_norm | is_gate))
    def _():
        o_ref[...] = acc.astype(o_ref.dtype)


def _in_proj(x2, g_mix, w_in, gains, *, tm, tn):
    T, D = x2.shape
    n_qkv_tiles = QKV_WIDTH // tn
    n_tiles = w_in.shape[1] // tn
    out_width = QKV_PAD + 2 * D
    out_tile = lambda j: jnp.where(j < n_qkv_tiles, j, j + (QKV_PAD - QKV_WIDTH) // tn)
    return pl.pallas_call(
        functools.partial(_in_proj_kernel, tn=tn),
        grid=(T // tm, n_tiles),
        in_specs=[
            pl.BlockSpec((tm, D), lambda i, j: (i, 0)),
            pl.BlockSpec((1, D), lambda i, j: (0, 0)),
            pl.BlockSpec((D, tn), lambda i, j: (0, j)),
            pl.BlockSpec((1, tn), lambda i, j: (0, j)),
        ],
        out_specs=pl.BlockSpec((tm, tn), lambda i, j: (i, out_tile(j))),
        out_shape=jax.ShapeDtypeStruct((T, out_width), jnp.bfloat16),
        scratch_shapes=[pltpu.VMEM((tm, D), jnp.bfloat16)],
        compiler_params=pltpu.CompilerParams(
            dimension_semantics=("arbitrary", "arbitrary"), vmem_limit_bytes=VMEM_LIMIT),
        name="in_proj",
    )(x2, g_mix, w_in, gains)


def _dilated_kernel(q_ref, kp_ref, kc_ref, vp_ref, vc_ref, bias_ref, o_ref, lse_ref):
    n = pl.program_id(2)
    prev_mask = jnp.where(n > 0, 0.0, NEG)
    lane = lax.broadcasted_iota(jnp.int32, (A_BLOCK, HEAD_DIM), 1)
    lse_tile = jnp.zeros((A_BLOCK, HEAD_DIM), jnp.float32)
    nt = (((1,), (1,)), ((), ()))
    for h in range(A_HEADS_PER_GROUP):
        cols = slice(h * HEAD_DIM, (h + 1) * HEAD_DIM)
        q = q_ref[0, :, cols]
        s_prev = (lax.dot_general(q, kp_ref[0, :, cols], nt, preferred_element_type=jnp.float32)
                  + bias_ref[0, h, :, :A_BLOCK] + prev_mask)
        s_cur = (lax.dot_general(q, kc_ref[0, :, cols], nt, preferred_element_type=jnp.float32)
                 + bias_ref[0, h, :, A_BLOCK:])
        m = jnp.maximum(jnp.max(s_prev, axis=-1, keepdims=True), jnp.max(s_cur, axis=-1, keepdims=True))
        p_prev = jnp.exp2(s_prev - m)
        p_cur = jnp.exp2(s_cur - m)
        den = jnp.sum(p_prev, axis=-1, keepdims=True) + jnp.sum(p_cur, axis=-1, keepdims=True)
        o = (jnp.dot(p_prev.astype(jnp.bfloat16), vp_ref[0, :, cols], preferred_element_type=jnp.float32)
             + jnp.dot(p_cur.astype(jnp.bfloat16), vc_ref[0, :, cols], preferred_element_type=jnp.float32))
        o_ref[0, :, cols] = (o / den).astype(o_ref.dtype)
        lse_tile = jnp.where(lane == h, m + jnp.log2(den), lse_tile)
    lse_ref[0] = lse_tile


def _dilated_group(proj3, bias, group, *, batch, seq):
    _, dilation = DIL_PATTERNS[group]
    r = dilation
    wp = proj3.shape[-1]
    L = seq // r
    nb = L // A_BLOCK
    view = proj3.reshape(batch, L, r * wp)
    cpb = wp // A_OUT
    q_blk, k_blk, v_blk = group, N_GROUPS + group, 2 * N_GROUPS + group

    def spec(sec, prev):
        if prev:
            return pl.BlockSpec((1, A_BLOCK, A_OUT), lambda b, rr, n: (b, jnp.maximum(n - 1, 0), rr * cpb + sec))
        return pl.BlockSpec((1, A_BLOCK, A_OUT), lambda b, rr, n: (b, n, rr * cpb + sec))

    out, lse = pl.pallas_call(
        _dilated_kernel,
        grid=(batch, r, nb),
        in_specs=[spec(q_blk, False), spec(k_blk, True), spec(k_blk, False), spec(v_blk, True), spec(v_blk, False),
                  pl.BlockSpec((1, A_HEADS_PER_GROUP, A_BLOCK, 2 * A_BLOCK), lambda b, rr, n: (group, 0, 0, 0))],
        out_specs=[pl.BlockSpec((1, A_BLOCK, A_OUT), lambda b, rr, n: (b, n, rr)),
                   pl.BlockSpec((1, A_BLOCK, HEAD_DIM), lambda b, rr, n: (b, n, rr))],
        out_shape=[jax.ShapeDtypeStruct((batch, L, r * A_OUT), jnp.bfloat16),
                   jax.ShapeDtypeStruct((batch, L, r * HEAD_DIM), jnp.float32)],
        compiler_params=pltpu.CompilerParams(dimension_semantics=("arbitrary",) * 3),
        name=f"dilated_{group}",
    )(view, view, view, view, view, bias)
    return out.reshape(batch * seq, A_OUT), lse.reshape(batch * seq, HEAD_DIM)


def _split3(x):
    hi = x.astype(jnp.bfloat16)
    r1 = x - hi.astype(jnp.float32)
    mid = r1.astype(jnp.bfloat16)
    lo = (r1 - mid.astype(jnp.float32)).astype(jnp.bfloat16)
    return hi, mid, lo


def _moba_kernel(q_ref, k_ref, v_ref, bias_ref, o_ref, kmean_ref, vt_ref, sel_ref, *, nb):
    i = pl.program_id(2)
    BS = MOBA_BLOCK
    nt = (((1,), (1,)), ((), ()))

    @pl.when(i == 0)
    def _():
        row = lax.broadcasted_iota(jnp.int32, (nb, nb * BS), 0)
        col = lax.broadcasted_iota(jnp.int32, (nb, nb * BS), 1)
        lo_edge = row * BS
        avg = jnp.where((col >= lo_edge) & (col < lo_edge + BS), 1.0 / BS, 0.0).astype(jnp.bfloat16)
        kmean = jnp.dot(avg, k_ref[...], preferred_element_type=jnp.float32)
        for t, part in enumerate(_split3(kmean)):
            kmean_ref[t] = part
        vt_ref[...] = v_ref[...].T

    q = q_ref[...]
    gate = sum(lax.dot_general(kmean_ref[t], q, nt, preferred_element_type=jnp.float32) for t in range(3))
    blk = lax.broadcasted_iota(jnp.int32, (nb, BS), 0)
    past = blk < i
    gate = jnp.where(past, gate, -jnp.inf)
    rank = jnp.zeros((nb, BS), jnp.int32)
    for m in range(nb):
        gm = gate[m:m + 1, :]
        beats = (gm > gate) | ((gm == gate) & (blk > m))
        rank = rank + beats.astype(jnp.int32)
    sel_ref[...] = jnp.where(past & (rank < MOBA_TOPK), 0.0, NEG)

    own = pl.multiple_of(i * BS, BS)
    s_t = lax.dot_general(k_ref[pl.ds(own, BS), :], q, nt, preferred_element_type=jnp.float32) + bias_ref[0, 0]
    m0 = jnp.max(s_t, axis=0, keepdims=True)
    p = jnp.exp2(s_t - m0)
    l0 = jnp.sum(p, axis=0, keepdims=True)
    acc0 = jnp.dot(vt_ref[:, pl.ds(own, BS)], p.astype(jnp.bfloat16), preferred_element_type=jnp.float32)

    def body(n, carry):
        m_run, l_run, acc = carry
        start = pl.multiple_of(n * BS, BS)
        tile = jnp.minimum(i - n, MOBA_BIAS_TILES - 1)
        s_t = (lax.dot_general(k_ref[pl.ds(start, BS), :], q, nt, preferred_element_type=jnp.float32)
               + bias_ref[0, tile] + sel_ref[pl.ds(n, 1), :])
        m_new = jnp.maximum(m_run, jnp.max(s_t, axis=0, keepdims=True))
        alpha = jnp.exp2(m_run - m_new)
        p = jnp.exp2(s_t - m_new)
        l_new = alpha * l_run + jnp.sum(p, axis=0, keepdims=True)
        pv = jnp.dot(vt_ref[:, pl.ds(start, BS)], p.astype(jnp.bfloat16), preferred_element_type=jnp.float32)
        return m_new, l_new, alpha * acc + pv

    _, l_fin, acc = lax.fori_loop(0, i, body, (m0, l0, acc0))
    o_ref[...] = (acc / l_fin).T.astype(o_ref.dtype)


def _moba(proj2, bias, *, batch, seq):
    BS = MOBA_BLOCK
    nb = seq // BS
    c0 = A_QKV // HEAD_DIM
    kv_spec = lambda sec: pl.BlockSpec((seq, HEAD_DIM), lambda b, h, i: (b, c0 + sec * B_HEADS + h))
    return pl.pallas_call(
        functools.partial(_moba_kernel, nb=nb),
        grid=(batch, B_HEADS, nb),
        in_specs=[pl.BlockSpec((BS, HEAD_DIM), lambda b, h, i: (b * nb + i, c0 + h)),
                  kv_spec(1), kv_spec(2),
                  pl.BlockSpec((1, MOBA_BIAS_TILES, BS, BS), lambda b, h, i: (h, 0, 0, 0))],
        out_specs=pl.BlockSpec((BS, HEAD_DIM), lambda b, h, i: (b * nb + i, h)),
        out_shape=jax.ShapeDtypeStruct((batch * seq, B_WIDTH), jnp.bfloat16),
        scratch_shapes=[pltpu.VMEM((3, nb, HEAD_DIM), jnp.bfloat16),
                        pltpu.VMEM((HEAD_DIM, seq), jnp.bfloat16),
                        pltpu.VMEM((nb, BS), jnp.float32)],
        compiler_params=pltpu.CompilerParams(dimension_semantics=("arbitrary",) * 3),
        name="moba",
    )(proj2, proj2, proj2, bias)


def _merge_kernel(o0_ref, o1_ref, o2_ref, l0_ref, l1_ref, l2_ref, yb_ref, ga_ref, gb_ref, x_ref,
                  wa_ref, wb_ref, wo_ref, out_ref):
    l0, l1, l2 = l0_ref[...], l1_ref[...], l2_ref[...]
    mx = jnp.maximum(jnp.maximum(l0, l1), l2)
    e0, e1, e2 = jnp.exp2(l0 - mx), jnp.exp2(l1 - mx), jnp.exp2(l2 - mx)
    inv = 1.0 / (e0 + e1 + e2)
    parts = []
    for h in range(A_HEADS_PER_GROUP):
        cols = slice(h * HEAD_DIM, (h + 1) * HEAD_DIM)
        ya = ((e0 * inv)[:, h:h + 1] * o0_ref[:, cols].astype(jnp.float32)
              + (e1 * inv)[:, h:h + 1] * o1_ref[:, cols].astype(jnp.float32)
              + (e2 * inv)[:, h:h + 1] * o2_ref[:, cols].astype(jnp.float32))
        parts.append(ya.astype(jnp.bfloat16))
    ya = jnp.concatenate(parts, axis=-1)
    pa = jnp.dot(ya, wa_ref[...], preferred_element_type=jnp.float32)
    pb = jnp.dot(yb_ref[...], wb_ref[...], preferred_element_type=jnp.float32)
    mixed = ga_ref[...].astype(jnp.float32) * pa + gb_ref[...].astype(jnp.float32) * pb
    out_ref[...] = x_ref[...] + jnp.dot(mixed.astype(jnp.bfloat16), wo_ref[...], preferred_element_type=jnp.float32)


def _merge(outs, lses, yb, proj2, x2, wa, wb, wo, *, tm):
    T, D = x2.shape
    gate0 = QKV_PAD // D
    row = lambda w: pl.BlockSpec((tm, w), lambda i: (i, 0))
    full = lambda a: pl.BlockSpec(a.shape, lambda i: (0, 0))
    return pl.pallas_call(
        _merge_kernel,
        grid=(T // tm,),
        in_specs=[row(A_OUT)] * 3 + [row(HEAD_DIM)] * 3 + [row(B_WIDTH),
                  pl.BlockSpec((tm, D), lambda i: (i, gate0)), pl.BlockSpec((tm, D), lambda i: (i, gate0 + 1)),
                  row(D), full(wa), full(wb), full(wo)],
        out_specs=row(D),
        out_shape=jax.ShapeDtypeStruct((T, D), jnp.float32),
        compiler_params=pltpu.CompilerParams(dimension_semantics=("arbitrary",), vmem_limit_bytes=VMEM_LIMIT),
        name="merge",
    )(*outs, *lses, yb, proj2, proj2, x2, wa, wb, wo)


def _mlp_kernel(x_ref, g_ref, wu_ref, wd_ref, o_ref, h_ref):
    f = pl.program_id(1)

    @pl.when(f == 0)
    def _():
        x = x_ref[...]
        h_ref[...] = _rms_rows(x, g_ref[...]).astype(h_ref.dtype)
        o_ref[...] = x

    u = jnp.dot(h_ref[...], wu_ref[...], preferred_element_type=jnp.float32)
    u = jnp.square(jnp.maximum(u, 0.0)).astype(jnp.bfloat16)
    o_ref[...] += jnp.dot(u, wd_ref[...], preferred_element_type=jnp.float32)


def _mlp(x2, g_mlp, w_up, w_down, *, tm, tf):
    T, D = x2.shape
    F = w_up.shape[1]
    return pl.pallas_call(
        _mlp_kernel,
        grid=(T // tm, F // tf),
        in_specs=[pl.BlockSpec((tm, D), lambda i, f: (i, 0)),
                  pl.BlockSpec((1, D), lambda i, f: (0, 0)),
                  pl.BlockSpec((D, tf), lambda i, f: (0, f)),
                  pl.BlockSpec((tf, D), lambda i, f: (f, 0))],
        out_specs=pl.BlockSpec((tm, D), lambda i, f: (i, 0)),
        out_shape=jax.ShapeDtypeStruct((T, D), jnp.float32),
        scratch_shapes=[pltpu.VMEM((tm, D), jnp.bfloat16)],
        compiler_params=pltpu.CompilerParams(
            dimension_semantics=("arbitrary", "arbitrary"), vmem_limit_bytes=VMEM_LIMIT),
        name="mlp",
    )(x2, g_mlp, w_up, w_down)


def _column_gains(q_norm_a, k_norm_a, q_norm_b, k_norm_b, width):
    qs = SCALE * LOG2E
    gains = jnp.concatenate([
        jnp.tile(q_norm_a * qs, A_HEADS), jnp.tile(k_norm_a, A_HEADS), jnp.ones((A_HEADS * HEAD_DIM,), jnp.float32),
        jnp.tile(q_norm_b * qs, B_HEADS), jnp.tile(k_norm_b, B_HEADS), jnp.ones((B_WIDTH,), jnp.float32),
        jnp.ones((width - QKV_WIDTH,), jnp.float32)])
    return gains.reshape(1, width)


def kernel(x, g_mix, w_in, q_norm_a, k_norm_a, q_norm_b, k_norm_b, rel_bias,
           w_branch_a, w_branch_b, w_out, g_mlp, w_up, w_down):
    batch, seq, d_model = x.shape
    T = batch * seq
    assert w_in.shape == (d_model, QKV_WIDTH + 2 * d_model)
    assert seq % (DIL_PATTERNS[-1][1] * A_BLOCK) == 0 and QKV_PAD % d_model == 0
    bf16 = jnp.bfloat16
    x2 = x.reshape(T, d_model)
    tm = min(1024, T)

    dil_bias, moba_bias = _bias_tables(rel_bias)
    gains = _column_gains(q_norm_a, k_norm_a, q_norm_b, k_norm_b, w_in.shape[1])
    proj = _in_proj(x2, g_mix.reshape(1, -1), w_in.astype(bf16), gains, tm=tm, tn=min(512, d_model * 2))

    proj3 = proj.reshape(batch, seq, proj.shape[-1])
    outs, lses = zip(*[_dilated_group(proj3, dil_bias, g, batch=batch, seq=seq) for g in range(N_GROUPS)])
    yb = _moba(proj, moba_bias, batch=batch, seq=seq)

    x_mid = _merge(outs, lses, yb, proj, x2, w_branch_a.astype(bf16), w_branch_b.astype(bf16),
                   w_out.astype(bf16), tm=min(512, T))
    y = _mlp(x_mid, g_mlp.reshape(1, -1), w_up.astype(bf16), w_down.astype(bf16), tm=tm, tf=min(512, w_up.shape[1]))
    return y.reshape(batch, seq, d_model)
```

```python
import functools
import math

import jax
import jax.numpy as jnp
from jax import lax
from jax.experimental import pallas as pl
from jax.experimental.pallas import tpu as pltpu

HEAD_DIM = 128
DIL_PATTERNS = ((128, 1), (512, 4), (2048, 16))
N_GROUPS = len(DIL_PATTERNS)
A_HEADS_PER_GROUP = 4
A_HEADS = A_HEADS_PER_GROUP * N_GROUPS
A_OUT = A_HEADS_PER_GROUP * HEAD_DIM
A_BLOCK = 128
B_HEADS = 8
B_WIDTH = B_HEADS * HEAD_DIM
MOBA_BLOCK = 256
MOBA_TOPK = 3
MOBA_GROUP = 4
N_BUCKETS = 32
MAX_DISTANCE = 2048
EPS = 1e-6
SCALE = HEAD_DIM ** -0.5
A_QKV = 3 * A_HEADS * HEAD_DIM
B_QKV = 3 * B_WIDTH
QKV_WIDTH = A_QKV + B_QKV
QKV_PAD = 8192
LOG2E = math.log2(math.e)
NEG = -1e30
VMEM_LIMIT = 56 * 1024 * 1024


def _bucket_thresholds():
    max_exact = N_BUCKETS // 2

    def bucket(d):
        if d < max_exact:
            return d
        v = int(math.log(d / max_exact) / math.log(MAX_DISTANCE / max_exact) * (N_BUCKETS - max_exact))
        return min(max_exact + v, N_BUCKETS - 1)

    thr, d = [0], 0
    for b in range(1, N_BUCKETS):
        while bucket(d) < b:
            d += 1
        thr.append(d)
    return tuple(thr)


BUCKET_THRESHOLDS = _bucket_thresholds()
MOBA_BIAS_TILES = -(-(BUCKET_THRESHOLDS[-1] - 1) // MOBA_BLOCK) + 2


def _bias_lookup(dist, tab_ref, col):
    val = jnp.full(dist.shape, tab_ref[0, col], jnp.float32)
    for b in range(1, N_BUCKETS):
        val = jnp.where(dist >= BUCKET_THRESHOLDS[b], tab_ref[b, col], val)
    return val * LOG2E


def _dil_bias_kernel(tab_ref, o_ref):
    g, h = pl.program_id(0), pl.program_id(1)
    a = lax.broadcasted_iota(jnp.int32, (A_BLOCK, 2 * A_BLOCK), 0)
    j = lax.broadcasted_iota(jnp.int32, (A_BLOCK, 2 * A_BLOCK), 1)
    delta = a + A_BLOCK - j
    dilation = lax.shift_left(jnp.int32(1), 2 * g)
    val = _bias_lookup(delta * dilation, tab_ref, g * A_HEADS_PER_GROUP + h)
    o_ref[0, 0] = jnp.where((delta >= 0) & (delta <= A_BLOCK), val, NEG)


def _moba_bias_kernel(tab_ref, o_ref):
    h, c = pl.program_id(0), pl.program_id(1)
    j = lax.broadcasted_iota(jnp.int32, (MOBA_BLOCK, MOBA_BLOCK), 0)
    s = lax.broadcasted_iota(jnp.int32, (MOBA_BLOCK, MOBA_BLOCK), 1)
    dist = c * MOBA_BLOCK + s - j
    val = _bias_lookup(dist, tab_ref, A_HEADS + h)
    o_ref[0, 0] = jnp.where(dist >= 0, val, NEG)


def _bias_tables(rel_bias):
    smem = pl.BlockSpec(memory_space=pltpu.SMEM)
    dil = pl.pallas_call(
        _dil_bias_kernel,
        grid=(N_GROUPS, A_HEADS_PER_GROUP),
        in_specs=[smem],
        out_specs=pl.BlockSpec((1, 1, A_BLOCK, 2 * A_BLOCK), lambda g, h: (g, h, 0, 0)),
        out_shape=jax.ShapeDtypeStruct((N_GROUPS, A_HEADS_PER_GROUP, A_BLOCK, 2 * A_BLOCK), jnp.float32),
        name="dil_bias",
    )(rel_bias)
    moba = pl.pallas_call(
        _moba_bias_kernel,
        grid=(B_HEADS, MOBA_BIAS_TILES),
        in_specs=[smem],
        out_specs=pl.BlockSpec((1, 1, MOBA_BLOCK, MOBA_BLOCK), lambda h, c: (h, c, 0, 0)),
        out_shape=jax.ShapeDtypeStruct((B_HEADS, MOBA_BIAS_TILES, MOBA_BLOCK, MOBA_BLOCK), jnp.float32),
        name="moba_bias",
    )(rel_bias)
    return dil, moba


def _rms_rows(x, g):
    return (x * lax.rsqrt(jnp.mean(x * x, axis=-1, keepdims=True) + EPS)) * g


def _in_proj_kernel(x_ref, g_ref, w_ref, gain_ref, o_ref, h_ref, *, tn):
    j = pl.program_id(1)

    @pl.when(j == 0)
    def _():
        h_ref[...] = _rms_rows(x_ref[...], g_ref[...]).astype(h_ref.dtype)

    acc = jnp.dot(h_ref[...], w_ref[...], preferred_element_type=jnp.float32)
    col0 = j * tn
    is_norm = (col0 < 2 * A_HEADS * HEAD_DIM) | ((col0 >= A_QKV) & (col0 < A_QKV + 2 * B_WIDTH))
    is_gate = col0 >= QKV_WIDTH

    @pl.when(is_norm)
    def _():
        for c in range(tn // HEAD_DIM):
            cols = slice(c * HEAD_DIM, (c + 1) * HEAD_DIM)
            o_ref[:, cols] = _rms_rows(acc[:, cols], gain_ref[:, cols]).astype(o_ref.dtype)

    @pl.when(is_gate)
    def _():
        o_ref[...] = jax.nn.sigmoid(acc).astype(o_ref.dtype)

    @pl.when(jnp.logical_not(is_norm | is_gate))
    def _():
        o_ref[...] = acc.astype(o_ref.dtype)


def _in_proj(x2, g_mix, w_in, gains, *, tm, tn):
    T, D = x2.shape
    n_qkv_tiles = QKV_WIDTH // tn
    n_tiles = w_in.shape[1] // tn
    out_width = QKV_PAD + 2 * D
    out_tile = lambda j: jnp.where(j < n_qkv_tiles, j, j + (QKV_PAD - QKV_WIDTH) // tn)
    return pl.pallas_call(
        functools.partial(_in_proj_kernel, tn=tn),
        grid=(T // tm, n_tiles),
        in_specs=[
            pl.BlockSpec((tm, D), lambda i, j: (i, 0)),
            pl.BlockSpec((1, D), lambda i, j: (0, 0)),
            pl.BlockSpec((D, tn), lambda i, j: (0, j)),
            pl.BlockSpec((1, tn), lambda i, j: (0, j)),
        ],
        out_specs=pl.BlockSpec((tm, tn), lambda i, j: (i, out_tile(j))),
        out_shape=jax.ShapeDtypeStruct((T, out_width), jnp.bfloat16),
        scratch_shapes=[pltpu.VMEM((tm, D), jnp.bfloat16)],
        compiler_params=pltpu.CompilerParams(
            dimension_semantics=("arbitrary", "arbitrary"), vmem_limit_bytes=VMEM_LIMIT),
        name="in_proj",
    )(x2, g_mix, w_in, gains)


CHUNK = DIL_PATTERNS[-1][1] * A_BLOCK
BLOCKS_PER_CHUNK = CHUNK // A_BLOCK


def _deinterleave(dst_ref, dst_row0, src_ref, stage_ref, r):
    if r == 1:
        dst_ref[pl.ds(dst_row0, CHUNK), :] = src_ref[...]
        return
    n = CHUNK // r
    for h in range(A_HEADS_PER_GROUP):
        cols = slice(h * HEAD_DIM, (h + 1) * HEAD_DIM)
        stage_ref[h] = src_ref[:, cols].astype(jnp.float32)
        for rr in range(r):
            rows = pl.ds(pl.multiple_of(dst_row0 + rr * n, A_BLOCK), n)
            dst_ref[rows, cols] = stage_ref[h, pl.ds(rr, n, stride=r), :].astype(dst_ref.dtype)


def _dilated_kernel(q_ref, k_ref, v_ref, bias_ref, o_ref, lse_ref, stage_ref, qd_ref, kd_ref, vd_ref, od_ref, ld_ref,
                    *, r):
    c = pl.program_id(1)
    slot = lax.rem(c, 2)
    cur0 = pl.multiple_of(slot * CHUNK, CHUNK)
    prev0 = pl.multiple_of((1 - slot) * CHUNK, CHUNK)

    @pl.when(c == 0)
    def _():
        kd_ref[pl.ds(prev0, CHUNK), :] = jnp.zeros((CHUNK, A_OUT), kd_ref.dtype)
        vd_ref[pl.ds(prev0, CHUNK), :] = jnp.zeros((CHUNK, A_OUT), vd_ref.dtype)

    _deinterleave(qd_ref, 0, q_ref, stage_ref, r)
    _deinterleave(kd_ref, cur0, k_ref, stage_ref, r)
    _deinterleave(vd_ref, cur0, v_ref, stage_ref, r)

    bpc = BLOCKS_PER_CHUNK // r
    lane = lax.broadcasted_iota(jnp.int32, (A_BLOCK, HEAD_DIM), 1)
    nt = (((1,), (1,)), ((), ()))

    def block(u, carry):
        nl = lax.rem(u, bpc)
        row = pl.multiple_of(u * A_BLOCK, A_BLOCK)
        cur = pl.multiple_of(cur0 + row, A_BLOCK)
        prev = pl.multiple_of(jnp.where(nl > 0, cur - A_BLOCK, prev0 + row + (bpc - 1) * A_BLOCK), A_BLOCK)
        prev_mask = jnp.where((c == 0) & (nl == 0), NEG, 0.0)
        lse_tile = jnp.zeros((A_BLOCK, HEAD_DIM), jnp.float32)
        for h in range(A_HEADS_PER_GROUP):
            cols = slice(h * HEAD_DIM, (h + 1) * HEAD_DIM)
            q = qd_ref[pl.ds(row, A_BLOCK), cols]
            s_prev = (lax.dot_general(q, kd_ref[pl.ds(prev, A_BLOCK), cols], nt, preferred_element_type=jnp.float32)
                      + bias_ref[0, h, :, :A_BLOCK] + prev_mask)
            s_cur = (lax.dot_general(q, kd_ref[pl.ds(cur, A_BLOCK), cols], nt, preferred_element_type=jnp.float32)
                     + bias_ref[0, h, :, A_BLOCK:])
            m = jnp.maximum(jnp.max(s_prev, axis=-1, keepdims=True), jnp.max(s_cur, axis=-1, keepdims=True))
            p_prev = jnp.exp2(s_prev - m)
            p_cur = jnp.exp2(s_cur - m)
            den = jnp.sum(p_prev, axis=-1, keepdims=True) + jnp.sum(p_cur, axis=-1, keepdims=True)
            o = (jnp.dot(p_prev.astype(jnp.bfloat16), vd_ref[pl.ds(prev, A_BLOCK), cols],
                         preferred_element_type=jnp.float32)
                 + jnp.dot(p_cur.astype(jnp.bfloat16), vd_ref[pl.ds(cur, A_BLOCK), cols],
                           preferred_element_type=jnp.float32))
            od_ref[pl.ds(row, A_BLOCK), cols] = o / den
            lse_tile = jnp.where(lane == h, m + jnp.log2(den), lse_tile)
        ld_ref[pl.ds(row, A_BLOCK), :] = lse_tile
        return carry

    lax.fori_loop(0, BLOCKS_PER_CHUNK, block, 0)

    if r == 1:
        o_ref[...] = od_ref[...].astype(o_ref.dtype)
        lse_ref[...] = ld_ref[...]
    else:
        n = CHUNK // r
        for rr in range(r):
            lse_ref[pl.ds(rr, n, stride=r), :] = ld_ref[rr * n:(rr + 1) * n, :]
        for h in range(A_HEADS_PER_GROUP):
            cols = slice(h * HEAD_DIM, (h + 1) * HEAD_DIM)
            for rr in range(r):
                stage_ref[h, pl.ds(rr, n, stride=r), :] = od_ref[rr * n:(rr + 1) * n, cols]
            o_ref[:, cols] = stage_ref[h].astype(o_ref.dtype)


def _dilated_group(proj2, bias, group, *, batch, seq):
    _, r = DIL_PATTERNS[group]
    nc = seq // CHUNK
    chunk = lambda sec: pl.BlockSpec((CHUNK, A_OUT), lambda b, c: (b * nc + c, sec * N_GROUPS + group))
    return pl.pallas_call(
        functools.partial(_dilated_kernel, r=r),
        grid=(batch, nc),
        in_specs=[chunk(0), chunk(1), chunk(2),
                  pl.BlockSpec((1, A_HEADS_PER_GROUP, A_BLOCK, 2 * A_BLOCK), lambda b, c: (group, 0, 0, 0))],
        out_specs=[pl.BlockSpec((CHUNK, A_OUT), lambda b, c: (b * nc + c, 0)),
                   pl.BlockSpec((CHUNK, HEAD_DIM), lambda b, c: (b * nc + c, 0))],
        out_shape=[jax.ShapeDtypeStruct((batch * seq, A_OUT), jnp.bfloat16),
                   jax.ShapeDtypeStruct((batch * seq, HEAD_DIM), jnp.float32)],
        scratch_shapes=[pltpu.VMEM((A_HEADS_PER_GROUP, CHUNK, HEAD_DIM), jnp.float32),
                        pltpu.VMEM((CHUNK, A_OUT), jnp.bfloat16),
                        pltpu.VMEM((2 * CHUNK, A_OUT), jnp.bfloat16),
                        pltpu.VMEM((2 * CHUNK, A_OUT), jnp.bfloat16),
                        pltpu.VMEM((CHUNK, A_OUT), jnp.float32),
                        pltpu.VMEM((CHUNK, HEAD_DIM), jnp.float32)],
        compiler_params=pltpu.CompilerParams(
            dimension_semantics=("arbitrary", "arbitrary"), vmem_limit_bytes=VMEM_LIMIT),
        name=f"dilated_{group}",
    )(proj2, proj2, proj2, bias)


def _split3(x):
    hi = x.astype(jnp.bfloat16)
    r1 = x - hi.astype(jnp.float32)
    mid = r1.astype(jnp.bfloat16)
    lo = (r1 - mid.astype(jnp.float32)).astype(jnp.bfloat16)
    return hi, mid, lo


def _moba_kernel(q_ref, k_ref, v_ref, bias_ref, o_ref, kmean_ref, vt_ref, sel_ref, s_ref, *, nb, group):
    i = pl.program_id(2)
    BS = MOBA_BLOCK
    nt = (((1,), (1,)), ((), ()))

    @pl.when(i == 0)
    def _():
        row = lax.broadcasted_iota(jnp.int32, (nb, nb * BS), 0)
        col = lax.broadcasted_iota(jnp.int32, (nb, nb * BS), 1)
        lo_edge = row * BS
        avg = jnp.where((col >= lo_edge) & (col < lo_edge + BS), 1.0 / BS, 0.0).astype(jnp.bfloat16)
        kmean = jnp.dot(avg, k_ref[...], preferred_element_type=jnp.float32)
        for t, part in enumerate(_split3(kmean)):
            kmean_ref[t] = part
        vt_ref[...] = v_ref[...].T

    q = q_ref[...]
    gate = sum(lax.dot_general(kmean_ref[t], q, nt, preferred_element_type=jnp.float32) for t in range(3))
    blk = lax.broadcasted_iota(jnp.int32, (nb, BS), 0)
    past = blk < i
    gate = jnp.where(past, gate, -jnp.inf)
    rank = jnp.zeros((nb, BS), jnp.int32)
    for m in range(nb):
        gm = gate[m:m + 1, :]
        beats = (gm > gate) | ((gm == gate) & (blk > m))
        rank = rank + beats.astype(jnp.int32)
    sel_ref[...] = jnp.where((past & (rank < MOBA_TOPK)) | (blk == i), 0.0, NEG)

    qt = q.T
    rows_per_group = group * BS
    n_groups = i // group + 1

    def logits_pass(g, m8):
        rows = pl.ds(pl.multiple_of(g * rows_per_group, rows_per_group), rows_per_group)
        s_t = jnp.dot(k_ref[rows, :], qt, preferred_element_type=jnp.float32)
        for u in range(group):
            n = g * group + u
            tile = jnp.clip(i - n, 0, MOBA_BIAS_TILES - 1)
            s_u = s_t[u * BS:(u + 1) * BS] + bias_ref[0, tile] + sel_ref[pl.ds(n, 1), :]
            s_ref[pl.ds(pl.multiple_of(n * BS, BS), BS), :] = s_u
            m8 = jnp.maximum(m8, jnp.max(s_u.reshape(BS // 8, 8, BS), axis=0))
        return m8

    m8 = lax.fori_loop(0, n_groups, logits_pass, jnp.full((8, BS), -jnp.inf, jnp.float32))
    m_fin = jnp.max(m8, axis=0, keepdims=True)

    def value_pass(g, carry):
        l8, acc = carry
        rows = pl.ds(pl.multiple_of(g * rows_per_group, rows_per_group), rows_per_group)
        p = jnp.exp2(s_ref[rows, :] - m_fin)
        l8 = l8 + jnp.sum(p.reshape(rows_per_group // 8, 8, BS), axis=0)
        acc = acc + jnp.dot(vt_ref[:, rows], p.astype(jnp.bfloat16), preferred_element_type=jnp.float32)
        return l8, acc

    l8, acc = lax.fori_loop(0, n_groups, value_pass,
                            (jnp.zeros((8, BS), jnp.float32), jnp.zeros((HEAD_DIM, BS), jnp.float32)))
    l_fin = jnp.sum(l8, axis=0, keepdims=True)
    o_ref[...] = (acc / l_fin).T.astype(o_ref.dtype)


def _moba(proj2, bias, *, batch, seq):
    BS = MOBA_BLOCK
    nb = seq // BS
    c0 = A_QKV // HEAD_DIM
    kv_spec = lambda sec: pl.BlockSpec((seq, HEAD_DIM), lambda b, h, i: (b, c0 + sec * B_HEADS + h))
    group = math.gcd(nb, MOBA_GROUP)
    return pl.pallas_call(
        functools.partial(_moba_kernel, nb=nb, group=group),
        grid=(batch, B_HEADS, nb),
        in_specs=[pl.BlockSpec((BS, HEAD_DIM), lambda b, h, i: (b * nb + i, c0 + h)),
                  kv_spec(1), kv_spec(2),
                  pl.BlockSpec((1, MOBA_BIAS_TILES, BS, BS), lambda b, h, i: (h, 0, 0, 0))],
        out_specs=pl.BlockSpec((BS, HEAD_DIM), lambda b, h, i: (b * nb + i, h)),
        out_shape=jax.ShapeDtypeStruct((batch * seq, B_WIDTH), jnp.bfloat16),
        scratch_shapes=[pltpu.VMEM((3, nb, HEAD_DIM), jnp.bfloat16),
                        pltpu.VMEM((HEAD_DIM, seq), jnp.bfloat16),
                        pltpu.VMEM((nb, BS), jnp.float32),
                        pltpu.VMEM((seq, BS), jnp.float32)],
        compiler_params=pltpu.CompilerParams(dimension_semantics=("arbitrary",) * 3),
        name="moba",
    )(proj2, proj2, proj2, bias)


def _merge_kernel(o0_ref, o1_ref, o2_ref, l0_ref, l1_ref, l2_ref, yb_ref, ga_ref, gb_ref, x_ref,
                  wa_ref, wb_ref, wo_ref, out_ref):
    l0, l1, l2 = l0_ref[...], l1_ref[...], l2_ref[...]
    mx = jnp.maximum(jnp.maximum(l0, l1), l2)
    e0, e1, e2 = jnp.exp2(l0 - mx), jnp.exp2(l1 - mx), jnp.exp2(l2 - mx)
    inv = 1.0 / (e0 + e1 + e2)
    parts = []
    for h in range(A_HEADS_PER_GROUP):
        cols = slice(h * HEAD_DIM, (h + 1) * HEAD_DIM)
        ya = ((e0 * inv)[:, h:h + 1] * o0_ref[:, cols].astype(jnp.float32)
              + (e1 * inv)[:, h:h + 1] * o1_ref[:, cols].astype(jnp.float32)
              + (e2 * inv)[:, h:h + 1] * o2_ref[:, cols].astype(jnp.float32))
        parts.append(ya.astype(jnp.bfloat16))
    ya = jnp.concatenate(parts, axis=-1)
    pa = jnp.dot(ya, wa_ref[...], preferred_element_type=jnp.float32)
    pb = jnp.dot(yb_ref[...], wb_ref[...], preferred_element_type=jnp.float32)
    mixed = ga_ref[...].astype(jnp.float32) * pa + gb_ref[...].astype(jnp.float32) * pb
    out_ref[...] = x_ref[...] + jnp.dot(mixed.astype(jnp.bfloat16), wo_ref[...], preferred_element_type=jnp.float32)


def _merge(outs, lses, yb, proj2, x2, wa, wb, wo, *, tm):
    T, D = x2.shape
    gate0 = QKV_PAD // D
    row = lambda w: pl.BlockSpec((tm, w), lambda i: (i, 0))
    full = lambda a: pl.BlockSpec(a.shape, lambda i: (0, 0))
    return pl.pallas_call(
        _merge_kernel,
        grid=(T // tm,),
        in_specs=[row(A_OUT)] * 3 + [row(HEAD_DIM)] * 3 + [row(B_WIDTH),
                  pl.BlockSpec((tm, D), lambda i: (i, gate0)), pl.BlockSpec((tm, D), lambda i: (i, gate0 + 1)),
                  row(D), full(wa), full(wb), full(wo)],
        out_specs=row(D),
        out_shape=jax.ShapeDtypeStruct((T, D), jnp.float32),
        compiler_params=pltpu.CompilerParams(dimension_semantics=("arbitrary",), vmem_limit_bytes=VMEM_LIMIT),
        name="merge",
    )(*outs, *lses, yb, proj2, proj2, x2, wa, wb, wo)


def _mlp_kernel(x_ref, g_ref, wu_ref, wd_ref, o_ref, h_ref):
    f = pl.program_id(1)

    @pl.when(f == 0)
    def _():
        x = x_ref[...]
        h_ref[...] = _rms_rows(x, g_ref[...]).astype(h_ref.dtype)
        o_ref[...] = x

    u = jnp.dot(h_ref[...], wu_ref[...], preferred_element_type=jnp.float32)
    u = jnp.square(jnp.maximum(u, 0.0)).astype(jnp.bfloat16)
    o_ref[...] += jnp.dot(u, wd_ref[...], preferred_element_type=jnp.float32)


def _mlp(x2, g_mlp, w_up, w_down, *, tm, tf):
    T, D = x2.shape
    F = w_up.shape[1]
    return pl.pallas_call(
        _mlp_kernel,
        grid=(T // tm, F // tf),
        in_specs=[pl.BlockSpec((tm, D), lambda i, f: (i, 0)),
                  pl.BlockSpec((1, D), lambda i, f: (0, 0)),
                  pl.BlockSpec((D, tf), lambda i, f: (0, f)),
                  pl.BlockSpec((tf, D), lambda i, f: (f, 0))],
        out_specs=pl.BlockSpec((tm, D), lambda i, f: (i, 0)),
        out_shape=jax.ShapeDtypeStruct((T, D), jnp.float32),
        scratch_shapes=[pltpu.VMEM((tm, D), jnp.bfloat16)],
        compiler_params=pltpu.CompilerParams(
            dimension_semantics=("arbitrary", "arbitrary"), vmem_limit_bytes=VMEM_LIMIT),
        name="mlp",
    )(x2, g_mlp, w_up, w_down)


def _column_gains(q_norm_a, k_norm_a, q_norm_b, k_norm_b, width):
    qs = SCALE * LOG2E
    gains = jnp.concatenate([
        jnp.tile(q_norm_a * qs, A_HEADS), jnp.tile(k_norm_a, A_HEADS), jnp.ones((A_HEADS * HEAD_DIM,), jnp.float32),
        jnp.tile(q_norm_b * qs, B_HEADS), jnp.tile(k_norm_b, B_HEADS), jnp.ones((B_WIDTH,), jnp.float32),
        jnp.ones((width - QKV_WIDTH,), jnp.float32)])
    return gains.reshape(1, width)


def kernel(x, g_mix, w_in, q_norm_a, k_norm_a, q_norm_b, k_norm_b, rel_bias,
           w_branch_a, w_branch_b, w_out, g_mlp, w_up, w_down):
    batch, seq, d_model = x.shape
    T = batch * seq
    assert w_in.shape == (d_model, QKV_WIDTH + 2 * d_model)
    assert seq % (DIL_PATTERNS[-1][1] * A_BLOCK) == 0 and QKV_PAD % d_model == 0
    bf16 = jnp.bfloat16
    x2 = x.reshape(T, d_model)
    tm = min(1024, T)

    dil_bias, moba_bias = _bias_tables(rel_bias)
    gains = _column_gains(q_norm_a, k_norm_a, q_norm_b, k_norm_b, w_in.shape[1])
    proj = _in_proj(x2, g_mix.reshape(1, -1), w_in.astype(bf16), gains, tm=tm, tn=min(512, d_model * 2))

    outs, lses = zip(*[_dilated_group(proj, dil_bias, g, batch=batch, seq=seq) for g in range(N_GROUPS)])
    yb = _moba(proj, moba_bias, batch=batch, seq=seq)

    x_mid = _merge(outs, lses, yb, proj, x2, w_branch_a.astype(bf16), w_branch_b.astype(bf16),
                   w_out.astype(bf16), tm=min(512, T))
    y = _mlp(x_mid, g_mlp.reshape(1, -1), w_up.astype(bf16), w_down.astype(bf16), tm=tm, tf=min(512, w_up.shape[1]))
    return y.reshape(batch, seq, d_model)
```

```python
import functools
import math

import jax
import jax.numpy as jnp
from jax import lax
from jax.experimental import pallas as pl
from jax.experimental.pallas import tpu as pltpu

HEAD_DIM = 128
DIL_PATTERNS = ((128, 1), (512, 4), (2048, 16))
N_GROUPS = len(DIL_PATTERNS)
A_HEADS_PER_GROUP = 4
A_HEADS = A_HEADS_PER_GROUP * N_GROUPS
A_OUT = A_HEADS_PER_GROUP * HEAD_DIM
A_BLOCK = 128
B_HEADS = 8
B_WIDTH = B_HEADS * HEAD_DIM
MOBA_BLOCK = 256
MOBA_TOPK = 3
MOBA_GROUP = 4
MOBA_AUX = 16
N_BUCKETS = 32
MAX_DISTANCE = 2048
EPS = 1e-6
SCALE = HEAD_DIM ** -0.5
A_QKV = 3 * A_HEADS * HEAD_DIM
B_QKV = 3 * B_WIDTH
QKV_WIDTH = A_QKV + B_QKV
QKV_PAD = 8192
LOG2E = math.log2(math.e)
NEG = -1e30
VMEM_LIMIT = 56 * 1024 * 1024


def _bucket_thresholds():
    max_exact = N_BUCKETS // 2

    def bucket(d):
        if d < max_exact:
            return d
        v = int(math.log(d / max_exact) / math.log(MAX_DISTANCE / max_exact) * (N_BUCKETS - max_exact))
        return min(max_exact + v, N_BUCKETS - 1)

    thr, d = [0], 0
    for b in range(1, N_BUCKETS):
        while bucket(d) < b:
            d += 1
        thr.append(d)
    return tuple(thr)


BUCKET_THRESHOLDS = _bucket_thresholds()
MOBA_BIAS_TILES = -(-(BUCKET_THRESHOLDS[-1] - 1) // MOBA_BLOCK) + 2


def _bias_lookup(dist, tab_ref, col):
    val = jnp.full(dist.shape, tab_ref[0, col], jnp.float32)
    for b in range(1, N_BUCKETS):
        val = jnp.where(dist >= BUCKET_THRESHOLDS[b], tab_ref[b, col], val)
    return val * LOG2E


def _dil_bias_kernel(tab_ref, o_ref):
    g, h = pl.program_id(0), pl.program_id(1)
    a = lax.broadcasted_iota(jnp.int32, (A_BLOCK, 2 * A_BLOCK), 0)
    j = lax.broadcasted_iota(jnp.int32, (A_BLOCK, 2 * A_BLOCK), 1)
    delta = a + A_BLOCK - j
    dilation = lax.shift_left(jnp.int32(1), 2 * g)
    val = _bias_lookup(delta * dilation, tab_ref, g * A_HEADS_PER_GROUP + h)
    o_ref[0, 0] = jnp.where((delta >= 0) & (delta <= A_BLOCK), val, NEG)


def _moba_bias_kernel(tab_ref, o_ref):
    h, c = pl.program_id(0), pl.program_id(1)
    j = lax.broadcasted_iota(jnp.int32, (MOBA_BLOCK, MOBA_BLOCK), 0)
    s = lax.broadcasted_iota(jnp.int32, (MOBA_BLOCK, MOBA_BLOCK), 1)
    dist = c * MOBA_BLOCK + s - j
    val = _bias_lookup(dist, tab_ref, A_HEADS + h)
    o_ref[0, 0] = jnp.where(dist >= 0, val, NEG)


def _bias_tables(rel_bias):
    smem = pl.BlockSpec(memory_space=pltpu.SMEM)
    dil = pl.pallas_call(
        _dil_bias_kernel,
        grid=(N_GROUPS, A_HEADS_PER_GROUP),
        in_specs=[smem],
        out_specs=pl.BlockSpec((1, 1, A_BLOCK, 2 * A_BLOCK), lambda g, h: (g, h, 0, 0)),
        out_shape=jax.ShapeDtypeStruct((N_GROUPS, A_HEADS_PER_GROUP, A_BLOCK, 2 * A_BLOCK), jnp.float32),
        name="dil_bias",
    )(rel_bias)
    moba = pl.pallas_call(
        _moba_bias_kernel,
        grid=(B_HEADS, MOBA_BIAS_TILES),
        in_specs=[smem],
        out_specs=pl.BlockSpec((1, 1, MOBA_BLOCK, MOBA_BLOCK), lambda h, c: (h, c, 0, 0)),
        out_shape=jax.ShapeDtypeStruct((B_HEADS, MOBA_BIAS_TILES, MOBA_BLOCK, MOBA_BLOCK), jnp.float32),
        name="moba_bias",
    )(rel_bias)
    return dil, moba


def _rms_rows(x, g):
    return (x * lax.rsqrt(jnp.mean(x * x, axis=-1, keepdims=True) + EPS)) * g


def _in_proj_kernel(x_ref, g_ref, w_ref, gain_ref, o_ref, h_ref, *, tn):
    j = pl.program_id(1)

    @pl.when(j == 0)
    def _():
        h_ref[...] = _rms_rows(x_ref[...], g_ref[...]).astype(h_ref.dtype)

    col0 = j * tn
    is_norm = (col0 < 2 * A_HEADS * HEAD_DIM) | ((col0 >= A_QKV) & (col0 < A_QKV + 2 * B_WIDTH))
    is_pad = (col0 >= QKV_WIDTH) & (col0 < QKV_PAD)
    is_gate = col0 >= QKV_PAD

    @pl.when(is_pad)
    def _():
        o_ref[...] = jnp.zeros(o_ref.shape, o_ref.dtype)

    @pl.when(jnp.logical_not(is_pad))
    def _():
        acc = jnp.dot(h_ref[...], w_ref[...], preferred_element_type=jnp.float32)

        @pl.when(is_norm)
        def _():
            for c in range(tn // HEAD_DIM):
                cols = slice(c * HEAD_DIM, (c + 1) * HEAD_DIM)
                o_ref[:, cols] = _rms_rows(acc[:, cols], gain_ref[:, cols]).astype(o_ref.dtype)

        @pl.when(is_gate)
        def _():
            o_ref[...] = jax.nn.sigmoid(acc).astype(o_ref.dtype)

        @pl.when(jnp.logical_not(is_norm | is_gate))
        def _():
            o_ref[...] = acc.astype(o_ref.dtype)


def _in_proj(x2, g_mix, w_in, gains, *, tm, tn):
    T, D = x2.shape
    n_qkv_tiles = QKV_WIDTH // tn
    n_pad_tiles = (QKV_PAD - QKV_WIDTH) // tn
    out_width = QKV_PAD + 2 * D
    w_tile = lambda j: jnp.where(j < n_qkv_tiles, j, jnp.maximum(j - n_pad_tiles, n_qkv_tiles - 1))
    return pl.pallas_call(
        functools.partial(_in_proj_kernel, tn=tn),
        grid=(T // tm, out_width // tn),
        in_specs=[
            pl.BlockSpec((tm, D), lambda i, j: (i, 0)),
            pl.BlockSpec((1, D), lambda i, j: (0, 0)),
            pl.BlockSpec((D, tn), lambda i, j: (0, w_tile(j))),
            pl.BlockSpec((1, tn), lambda i, j: (0, w_tile(j))),
        ],
        out_specs=pl.BlockSpec((tm, tn), lambda i, j: (i, j)),
        out_shape=jax.ShapeDtypeStruct((T, out_width), jnp.bfloat16),
        scratch_shapes=[pltpu.VMEM((tm, D), jnp.bfloat16)],
        compiler_params=pltpu.CompilerParams(
            dimension_semantics=("arbitrary", "arbitrary"), vmem_limit_bytes=VMEM_LIMIT),
        name="in_proj",
    )(x2, g_mix, w_in, gains)


CHUNK = DIL_PATTERNS[-1][1] * A_BLOCK
BLOCKS_PER_CHUNK = CHUNK // A_BLOCK


def _deinterleave(dst_ref, dst_row0, src_ref, stage_ref, r):
    if r == 1:
        dst_ref[pl.ds(dst_row0, CHUNK), :] = src_ref[...]
        return
    n = CHUNK // r
    for h in range(A_HEADS_PER_GROUP):
        cols = slice(h * HEAD_DIM, (h + 1) * HEAD_DIM)
        stage_ref[h] = src_ref[:, cols].astype(jnp.float32)
        for rr in range(r):
            rows = pl.ds(pl.multiple_of(dst_row0 + rr * n, A_BLOCK), n)
            dst_ref[rows, cols] = stage_ref[h, pl.ds(rr, n, stride=r), :].astype(dst_ref.dtype)


def _dilated_kernel(q_ref, k_ref, v_ref, bias_ref, o_ref, lse_ref, stage_ref, qd_ref, kd_ref, vd_ref, od_ref, ld_ref,
                    *, r):
    c = pl.program_id(1)
    slot = lax.rem(c, 2)
    cur0 = pl.multiple_of(slot * CHUNK, CHUNK)
    prev0 = pl.multiple_of((1 - slot) * CHUNK, CHUNK)

    @pl.when(c == 0)
    def _():
        kd_ref[pl.ds(prev0, CHUNK), :] = jnp.zeros((CHUNK, A_OUT), kd_ref.dtype)
        vd_ref[pl.ds(prev0, CHUNK), :] = jnp.zeros((CHUNK, A_OUT), vd_ref.dtype)

    _deinterleave(qd_ref, 0, q_ref, stage_ref, r)
    _deinterleave(kd_ref, cur0, k_ref, stage_ref, r)
    _deinterleave(vd_ref, cur0, v_ref, stage_ref, r)

    bpc = BLOCKS_PER_CHUNK // r
    lane = lax.broadcasted_iota(jnp.int32, (A_BLOCK, HEAD_DIM), 1)
    nt = (((1,), (1,)), ((), ()))

    def block(u, carry):
        nl = lax.rem(u, bpc)
        row = pl.multiple_of(u * A_BLOCK, A_BLOCK)
        cur = pl.multiple_of(cur0 + row, A_BLOCK)
        prev = pl.multiple_of(jnp.where(nl > 0, cur - A_BLOCK, prev0 + row + (bpc - 1) * A_BLOCK), A_BLOCK)
        prev_mask = jnp.where((c == 0) & (nl == 0), NEG, 0.0)
        lse_tile = jnp.zeros((A_BLOCK, HEAD_DIM), jnp.float32)
        for h in range(A_HEADS_PER_GROUP):
            cols = slice(h * HEAD_DIM, (h + 1) * HEAD_DIM)
            q = qd_ref[pl.ds(row, A_BLOCK), cols]
            s_prev = (lax.dot_general(q, kd_ref[pl.ds(prev, A_BLOCK), cols], nt, preferred_element_type=jnp.float32)
                      + bias_ref[0, h, :, :A_BLOCK] + prev_mask)
            s_cur = (lax.dot_general(q, kd_ref[pl.ds(cur, A_BLOCK), cols], nt, preferred_element_type=jnp.float32)
                     + bias_ref[0, h, :, A_BLOCK:])
            m = jnp.maximum(jnp.max(s_prev, axis=-1, keepdims=True), jnp.max(s_cur, axis=-1, keepdims=True))
            p_prev = jnp.exp2(s_prev - m)
            p_cur = jnp.exp2(s_cur - m)
            den = jnp.sum(p_prev, axis=-1, keepdims=True) + jnp.sum(p_cur, axis=-1, keepdims=True)
            o = (jnp.dot(p_prev.astype(jnp.bfloat16), vd_ref[pl.ds(prev, A_BLOCK), cols],
                         preferred_element_type=jnp.float32)
                 + jnp.dot(p_cur.astype(jnp.bfloat16), vd_ref[pl.ds(cur, A_BLOCK), cols],
                           preferred_element_type=jnp.float32))
            od_ref[pl.ds(row, A_BLOCK), cols] = o / den
            lse_tile = jnp.where(lane == h, m + jnp.log2(den), lse_tile)
        ld_ref[pl.ds(row, A_BLOCK), :] = lse_tile
        return carry

    lax.fori_loop(0, BLOCKS_PER_CHUNK, block, 0)

    if r == 1:
        o_ref[...] = od_ref[...].astype(o_ref.dtype)
        lse_ref[...] = ld_ref[...]
    else:
        n = CHUNK // r
        for rr in range(r):
            lse_ref[pl.ds(rr, n, stride=r), :] = ld_ref[rr * n:(rr + 1) * n, :]
        for h in range(A_HEADS_PER_GROUP):
            cols = slice(h * HEAD_DIM, (h + 1) * HEAD_DIM)
            for rr in range(r):
                stage_ref[h, pl.ds(rr, n, stride=r), :] = od_ref[rr * n:(rr + 1) * n, cols]
            o_ref[:, cols] = stage_ref[h].astype(o_ref.dtype)


def _dilated_group(proj2, bias, group, *, batch, seq):
    _, r = DIL_PATTERNS[group]
    nc = seq // CHUNK
    chunk = lambda sec: pl.BlockSpec((CHUNK, A_OUT), lambda b, c: (b * nc + c, sec * N_GROUPS + group))
    return pl.pallas_call(
        functools.partial(_dilated_kernel, r=r),
        grid=(batch, nc),
        in_specs=[chunk(0), chunk(1), chunk(2),
                  pl.BlockSpec((1, A_HEADS_PER_GROUP, A_BLOCK, 2 * A_BLOCK), lambda b, c: (group, 0, 0, 0))],
        out_specs=[pl.BlockSpec((CHUNK, A_OUT), lambda b, c: (b * nc + c, 0)),
                   pl.BlockSpec((CHUNK, HEAD_DIM), lambda b, c: (b * nc + c, 0))],
        out_shape=[jax.ShapeDtypeStruct((batch * seq, A_OUT), jnp.bfloat16),
                   jax.ShapeDtypeStruct((batch * seq, HEAD_DIM), jnp.float32)],
        scratch_shapes=[pltpu.VMEM((A_HEADS_PER_GROUP, CHUNK, HEAD_DIM), jnp.float32),
                        pltpu.VMEM((CHUNK, A_OUT), jnp.bfloat16),
                        pltpu.VMEM((2 * CHUNK, A_OUT), jnp.bfloat16),
                        pltpu.VMEM((2 * CHUNK, A_OUT), jnp.bfloat16),
                        pltpu.VMEM((CHUNK, A_OUT), jnp.float32),
                        pltpu.VMEM((CHUNK, HEAD_DIM), jnp.float32)],
        compiler_params=pltpu.CompilerParams(
            dimension_semantics=("arbitrary", "arbitrary"), vmem_limit_bytes=VMEM_LIMIT),
        name=f"dilated_{group}",
    )(proj2, proj2, proj2, bias)


def _split3(x):
    hi = x.astype(jnp.bfloat16)
    r1 = x - hi.astype(jnp.float32)
    mid = r1.astype(jnp.bfloat16)
    lo = (r1 - mid.astype(jnp.float32)).astype(jnp.bfloat16)
    return hi, mid, lo


MOBA_STREAMS = 2


def _moba_items(nb, group):
    streams = [[] for _ in range(MOBA_STREAMS)]
    for i in sorted(range(nb), key=lambda i: -(i // group)):
        min(streams, key=len).extend((i, g) for g in range(i // group + 1))
    assert len({len(s) for s in streams}) == 1 and len(streams[0]) % 2 == 0
    return streams


def _moba_kernel(items_ref, q_ref, k_ref, v_ref, bias_ref, o_ref, kmean_ref, ka_ref, qa_ref, va_ref, s_ref,
                 st_ref, *, nb, group, n_items):
    BS = MOBA_BLOCK
    hd = HEAD_DIM
    seq = nb * BS
    rows_per_group = group * BS
    nt = (((1,), (1,)), ((), ()))

    row = lax.broadcasted_iota(jnp.int32, (nb, seq), 0)
    col = lax.broadcasted_iota(jnp.int32, (nb, seq), 1)
    lo_edge = row * BS
    avg = jnp.where((col >= lo_edge) & (col < lo_edge + BS), 1.0 / BS, 0.0).astype(jnp.bfloat16)
    kmean = jnp.dot(avg, k_ref[...], preferred_element_type=jnp.float32)
    for t, part in enumerate(_split3(kmean)):
        kmean_ref[t] = part

    q_all = q_ref[...]
    gate = sum(lax.dot_general(kmean_ref[t], q_all, nt, preferred_element_type=jnp.float32) for t in range(3))
    own = col // BS
    past = row < own
    gate = jnp.where(past, gate, -jnp.inf)
    rank = jnp.zeros((nb, seq), jnp.int32)
    for m in range(nb):
        gm = gate[m:m + 1, :]
        beats = (gm > gate) | ((gm == gate) & (row > m))
        rank = rank + beats.astype(jnp.int32)
    sel = jnp.where((past & (rank < MOBA_TOPK)) | (row == own), 0.0, NEG)

    qa_ref[:hd, :] = q_all.T
    qa_ref[hd:hd + nb, :] = sel.astype(qa_ref.dtype)
    qa_ref[hd + nb:, :] = jnp.zeros((hd - nb, seq), qa_ref.dtype)
    ka_ref[:, :hd] = k_ref[...]
    key_blk = lax.broadcasted_iota(jnp.int32, (seq, hd), 0) // BS
    ka_ref[:, hd:] = jnp.where(lax.broadcasted_iota(jnp.int32, (seq, hd), 1) == key_blk, 1.0, 0.0).astype(ka_ref.dtype)
    va_ref[:hd, :] = v_ref[...].T
    va_ref[hd:, :] = jnp.where(lax.broadcasted_iota(jnp.int32, (MOBA_AUX, seq), 0) == 0, 1.0, 0.0).astype(va_ref.dtype)

    def logits(st, w, slot):
        i, g = items_ref[st, w, 0], items_ref[st, w, 1]
        rows = pl.ds(pl.multiple_of(g * rows_per_group, rows_per_group), rows_per_group)
        qcols = pl.ds(pl.multiple_of(i * BS, BS), BS)
        s_t = jnp.dot(ka_ref[rows, :], qa_ref[:, qcols], preferred_element_type=jnp.float32)
        m8 = jnp.full((8, BS), -jnp.inf, jnp.float32)
        for u in range(group):
            tile = jnp.clip(i - (g * group + u), 0, MOBA_BIAS_TILES - 1)
            s_u = s_t[u * BS:(u + 1) * BS] + bias_ref[0, tile]
            s_ref[2 * st + slot, u * BS:(u + 1) * BS, :] = s_u
            m8 = jnp.maximum(m8, jnp.max(s_u.reshape(BS // 8, 8, BS), axis=0))
        return m8

    def accumulate(st, w, slot, m8, state):
        m_run, acc = state
        i, g = items_ref[st, w, 0], items_ref[st, w, 1]
        rows = pl.ds(pl.multiple_of(g * rows_per_group, rows_per_group), rows_per_group)
        m_prev = jnp.where(g == 0, -jnp.inf, m_run)
        m_new = jnp.maximum(m_prev, jnp.max(m8, axis=0, keepdims=True))
        alpha = jnp.exp2(m_prev - m_new)
        p = jnp.exp2(s_ref[2 * st + slot] - m_new).astype(jnp.bfloat16)
        acc = alpha * acc + jnp.dot(va_ref[:, rows], p, preferred_element_type=jnp.float32)
        st_ref[i] = acc
        return m_new, acc

    streams = range(MOBA_STREAMS)

    def pair(t, carry):
        m8a, state = carry
        w = 2 * t
        m8b = [logits(st, w + 1, 1) for st in streams]
        state = [accumulate(st, w, 0, m8a[st], state[st]) for st in streams]
        m8a = [logits(st, jnp.minimum(w + 2, n_items - 1), 0) for st in streams]
        state = [accumulate(st, w + 1, 1, m8b[st], state[st]) for st in streams]
        return m8a, state

    state0 = (jnp.full((1, BS), -jnp.inf, jnp.float32), jnp.zeros((hd + MOBA_AUX, BS), jnp.float32))
    lax.fori_loop(0, n_items // 2, pair, ([logits(st, 0, 0) for st in streams], [state0 for _ in streams]))

    for i in range(nb):
        acc = st_ref[i]
        o_ref[i * BS:(i + 1) * BS, :] = (acc[:hd] / acc[hd:hd + 1]).T.astype(o_ref.dtype)


def _moba(proj2, bias, *, batch, seq):
    BS = MOBA_BLOCK
    nb = seq // BS
    assert nb <= HEAD_DIM
    c0 = A_QKV // HEAD_DIM
    group = math.gcd(nb, MOBA_GROUP)
    items = _moba_items(nb, group)
    head_spec = lambda sec: pl.BlockSpec((seq, HEAD_DIM), lambda b, h, items: (b, c0 + sec * B_HEADS + h))
    return pl.pallas_call(
        functools.partial(_moba_kernel, nb=nb, group=group, n_items=len(items[0])),
        grid_spec=pltpu.PrefetchScalarGridSpec(
            num_scalar_prefetch=1,
            grid=(batch, B_HEADS),
            in_specs=[head_spec(0), head_spec(1), head_spec(2),
                      pl.BlockSpec((1, MOBA_BIAS_TILES, BS, BS), lambda b, h, items: (h, 0, 0, 0))],
            out_specs=pl.BlockSpec((seq, HEAD_DIM), lambda b, h, items: (b, h)),
            scratch_shapes=[pltpu.VMEM((3, nb, HEAD_DIM), jnp.bfloat16),
                            pltpu.VMEM((seq, 2 * HEAD_DIM), jnp.bfloat16),
                            pltpu.VMEM((2 * HEAD_DIM, seq), jnp.bfloat16),
                            pltpu.VMEM((HEAD_DIM + MOBA_AUX, seq), jnp.bfloat16),
                            pltpu.VMEM((2 * MOBA_STREAMS, group * BS, BS), jnp.float32),
                            pltpu.VMEM((nb, HEAD_DIM + MOBA_AUX, BS), jnp.float32)]),
        out_shape=jax.ShapeDtypeStruct((batch * seq, B_WIDTH), jnp.bfloat16),
        compiler_params=pltpu.CompilerParams(
            dimension_semantics=("arbitrary", "arbitrary"), vmem_limit_bytes=VMEM_LIMIT),
        name="moba",
    )(jnp.asarray(items, jnp.int32), proj2, proj2, proj2, bias)


def _merge_kernel(o0_ref, o1_ref, o2_ref, l0_ref, l1_ref, l2_ref, yb_ref, ga_ref, gb_ref, x_ref,
                  wa_ref, wb_ref, wo_ref, out_ref):
    l0, l1, l2 = l0_ref[...], l1_ref[...], l2_ref[...]
    mx = jnp.maximum(jnp.maximum(l0, l1), l2)
    e0, e1, e2 = jnp.exp2(l0 - mx), jnp.exp2(l1 - mx), jnp.exp2(l2 - mx)
    inv = 1.0 / (e0 + e1 + e2)
    parts = []
    for h in range(A_HEADS_PER_GROUP):
        cols = slice(h * HEAD_DIM, (h + 1) * HEAD_DIM)
        ya = ((e0 * inv)[:, h:h + 1] * o0_ref[:, cols].astype(jnp.float32)
              + (e1 * inv)[:, h:h + 1] * o1_ref[:, cols].astype(jnp.float32)
              + (e2 * inv)[:, h:h + 1] * o2_ref[:, cols].astype(jnp.float32))
        parts.append(ya.astype(jnp.bfloat16))
    ya = jnp.concatenate(parts, axis=-1)
    pa = jnp.dot(ya, wa_ref[...], preferred_element_type=jnp.float32)
    pb = jnp.dot(yb_ref[...], wb_ref[...], preferred_element_type=jnp.float32)
    mixed = ga_ref[...].astype(jnp.float32) * pa + gb_ref[...].astype(jnp.float32) * pb
    out_ref[...] = x_ref[...] + jnp.dot(mixed.astype(jnp.bfloat16), wo_ref[...], preferred_element_type=jnp.float32)


def _merge(outs, lses, yb, proj2, x2, wa, wb, wo, *, tm):
    T, D = x2.shape
    gate0 = QKV_PAD // D
    row = lambda w: pl.BlockSpec((tm, w), lambda i: (i, 0))
    full = lambda a: pl.BlockSpec(a.shape, lambda i: (0, 0))
    return pl.pallas_call(
        _merge_kernel,
        grid=(T // tm,),
        in_specs=[row(A_OUT)] * 3 + [row(HEAD_DIM)] * 3 + [row(B_WIDTH),
                  pl.BlockSpec((tm, D), lambda i: (i, gate0)), pl.BlockSpec((tm, D), lambda i: (i, gate0 + 1)),
                  row(D), full(wa), full(wb), full(wo)],
        out_specs=row(D),
        out_shape=jax.ShapeDtypeStruct((T, D), jnp.float32),
        compiler_params=pltpu.CompilerParams(dimension_semantics=("arbitrary",), vmem_limit_bytes=VMEM_LIMIT),
        name="merge",
    )(*outs, *lses, yb, proj2, proj2, x2, wa, wb, wo)


def _mlp_kernel(x_ref, g_ref, wu_ref, wd_ref, o_ref, h_ref):
    f = pl.program_id(1)

    @pl.when(f == 0)
    def _():
        x = x_ref[...]
        h_ref[...] = _rms_rows(x, g_ref[...]).astype(h_ref.dtype)
        o_ref[...] = x

    u = jnp.dot(h_ref[...], wu_ref[...], preferred_element_type=jnp.float32)
    u = jnp.square(jnp.maximum(u, 0.0)).astype(jnp.bfloat16)
    o_ref[...] += jnp.dot(u, wd_ref[...], preferred_element_type=jnp.float32)


def _mlp(x2, g_mlp, w_up, w_down, *, tm, tf):
    T, D = x2.shape
    F = w_up.shape[1]
    return pl.pallas_call(
        _mlp_kernel,
        grid=(T // tm, F // tf),
        in_specs=[pl.BlockSpec((tm, D), lambda i, f: (i, 0)),
                  pl.BlockSpec((1, D), lambda i, f: (0, 0)),
                  pl.BlockSpec((D, tf), lambda i, f: (0, f)),
                  pl.BlockSpec((tf, D), lambda i, f: (f, 0))],
        out_specs=pl.BlockSpec((tm, D), lambda i, f: (i, 0)),
        out_shape=jax.ShapeDtypeStruct((T, D), jnp.float32),
        scratch_shapes=[pltpu.VMEM((tm, D), jnp.bfloat16)],
        compiler_params=pltpu.CompilerParams(
            dimension_semantics=("arbitrary", "arbitrary"), vmem_limit_bytes=VMEM_LIMIT),
        name="mlp",
    )(x2, g_mlp, w_up, w_down)


def _column_gains(q_norm_a, k_norm_a, q_norm_b, k_norm_b, width):
    qs = SCALE * LOG2E
    gains = jnp.concatenate([
        jnp.tile(q_norm_a * qs, A_HEADS), jnp.tile(k_norm_a, A_HEADS), jnp.ones((A_HEADS * HEAD_DIM,), jnp.float32),
        jnp.tile(q_norm_b * qs, B_HEADS), jnp.tile(k_norm_b, B_HEADS), jnp.ones((B_WIDTH,), jnp.float32),
        jnp.ones((width - QKV_WIDTH,), jnp.float32)])
    return gains.reshape(1, width)


def kernel(x, g_mix, w_in, q_norm_a, k_norm_a, q_norm_b, k_norm_b, rel_bias,
           w_branch_a, w_branch_b, w_out, g_mlp, w_up, w_down):
    batch, seq, d_model = x.shape
    T = batch * seq
    assert w_in.shape == (d_model, QKV_WIDTH + 2 * d_model)
    assert seq % (DIL_PATTERNS[-1][1] * A_BLOCK) == 0 and QKV_PAD % d_model == 0
    bf16 = jnp.bfloat16
    x2 = x.reshape(T, d_model)
    tm = min(1024, T)

    dil_bias, moba_bias = _bias_tables(rel_bias)
    gains = _column_gains(q_norm_a, k_norm_a, q_norm_b, k_norm_b, w_in.shape[1])
    proj = _in_proj(x2, g_mix.reshape(1, -1), w_in.astype(bf16), gains, tm=tm, tn=min(512, d_model * 2))

    outs, lses = zip(*[_dilated_group(proj, dil_bias, g, batch=batch, seq=seq) for g in range(N_GROUPS)])
    yb = _moba(proj, moba_bias, batch=batch, seq=seq)

    x_mid = _merge(outs, lses, yb, proj, x2, w_branch_a.astype(bf16), w_branch_b.astype(bf16),
                   w_out.astype(bf16), tm=min(512, T))
    y = _mlp(x_mid, g_mlp.reshape(1, -1), w_up.astype(bf16), w_down.astype(bf16), tm=tm, tf=min(512, w_up.shape[1]))
    return y.reshape(batch, seq, d_model)
```

```python
import functools
import math

import jax
import jax.numpy as jnp
from jax import lax
from jax.experimental import pallas as pl
from jax.experimental.pallas import tpu as pltpu

HEAD_DIM = 128
DIL_PATTERNS = ((128, 1), (512, 4), (2048, 16))
N_GROUPS = len(DIL_PATTERNS)
A_HEADS_PER_GROUP = 4
A_HEADS = A_HEADS_PER_GROUP * N_GROUPS
A_OUT = A_HEADS_PER_GROUP * HEAD_DIM
A_BLOCK = 128
B_HEADS = 8
B_WIDTH = B_HEADS * HEAD_DIM
MOBA_BLOCK = 256
MOBA_TOPK = 3
MOBA_GROUP = 4
MOBA_AUX = 16
N_BUCKETS = 32
MAX_DISTANCE = 2048
EPS = 1e-6
SCALE = HEAD_DIM ** -0.5
A_QKV = 3 * A_HEADS * HEAD_DIM
B_QKV = 3 * B_WIDTH
QKV_WIDTH = A_QKV + B_QKV
QKV_PAD = 8192
LOG2E = math.log2(math.e)
NEG = -1e30
VMEM_LIMIT = 56 * 1024 * 1024


def _bucket_thresholds():
    max_exact = N_BUCKETS // 2

    def bucket(d):
        if d < max_exact:
            return d
        v = int(math.log(d / max_exact) / math.log(MAX_DISTANCE / max_exact) * (N_BUCKETS - max_exact))
        return min(max_exact + v, N_BUCKETS - 1)

    thr, d = [0], 0
    for b in range(1, N_BUCKETS):
        while bucket(d) < b:
            d += 1
        thr.append(d)
    return tuple(thr)


BUCKET_THRESHOLDS = _bucket_thresholds()
MOBA_BIAS_TILES = -(-(BUCKET_THRESHOLDS[-1] - 1) // MOBA_BLOCK) + 2


def _bias_lookup(dist, tab_ref, col, dmin=0, dmax=None):
    first = sum(dmin >= t for t in BUCKET_THRESHOLDS[1:])
    val = jnp.full(dist.shape, tab_ref[first, col], jnp.float32)
    for b in range(first + 1, N_BUCKETS):
        if dmax is None or BUCKET_THRESHOLDS[b] <= dmax:
            val = jnp.where(dist >= BUCKET_THRESHOLDS[b], tab_ref[b, col], val)
    return val * LOG2E


def _dil_bias_kernel(tab_ref, o_ref):
    g, h = pl.program_id(0), pl.program_id(1)
    a = lax.broadcasted_iota(jnp.int32, (A_BLOCK, 2 * A_BLOCK), 0)
    j = lax.broadcasted_iota(jnp.int32, (A_BLOCK, 2 * A_BLOCK), 1)
    delta = a + A_BLOCK - j
    dilation = lax.shift_left(jnp.int32(1), 2 * g)
    val = _bias_lookup(delta * dilation, tab_ref, g * A_HEADS_PER_GROUP + h)
    o_ref[0, 0] = jnp.where((delta >= 0) & (delta <= A_BLOCK), val, NEG)


def _moba_bias_kernel(tab_ref, o_ref):
    h, c = pl.program_id(0), pl.program_id(1)
    j = lax.broadcasted_iota(jnp.int32, (MOBA_BLOCK, MOBA_BLOCK), 0)
    s = lax.broadcasted_iota(jnp.int32, (MOBA_BLOCK, MOBA_BLOCK), 1)
    for tile in range(MOBA_BIAS_TILES):
        @pl.when(c == tile)
        def _():
            dist = tile * MOBA_BLOCK + s - j
            lo, hi = (tile - 1) * MOBA_BLOCK + 1, (tile + 1) * MOBA_BLOCK - 1
            val = _bias_lookup(dist, tab_ref, A_HEADS + h, max(lo, 0), hi)
            o_ref[0, 0] = jnp.where(dist >= 0, val, NEG)


def _bias_tables(rel_bias):
    smem = pl.BlockSpec(memory_space=pltpu.SMEM)
    dil = pl.pallas_call(
        _dil_bias_kernel,
        grid=(N_GROUPS, A_HEADS_PER_GROUP),
        in_specs=[smem],
        out_specs=pl.BlockSpec((1, 1, A_BLOCK, 2 * A_BLOCK), lambda g, h: (g, h, 0, 0)),
        out_shape=jax.ShapeDtypeStruct((N_GROUPS, A_HEADS_PER_GROUP, A_BLOCK, 2 * A_BLOCK), jnp.float32),
        name="dil_bias",
    )(rel_bias)
    moba = pl.pallas_call(
        _moba_bias_kernel,
        grid=(B_HEADS, MOBA_BIAS_TILES),
        in_specs=[smem],
        out_specs=pl.BlockSpec((1, 1, MOBA_BLOCK, MOBA_BLOCK), lambda h, c: (h, c, 0, 0)),
        out_shape=jax.ShapeDtypeStruct((B_HEADS, MOBA_BIAS_TILES, MOBA_BLOCK, MOBA_BLOCK), jnp.float32),
        name="moba_bias",
    )(rel_bias)
    return dil, moba


IN_PROJ_ROW_CHUNKS = 4


def _rms_rows(x, g):
    return (x * lax.rsqrt(jnp.mean(x * x, axis=-1, keepdims=True) + EPS)) * g


def _in_proj_kernel(x_ref, g_ref, w_ref, gain_ref, o_ref, h_ref, *, tn):
    j = pl.program_id(1)

    @pl.when(j == 0)
    def _():
        h_ref[...] = _rms_rows(x_ref[...], g_ref[...]).astype(h_ref.dtype)

    col0 = j * tn
    is_norm = (col0 < 2 * A_HEADS * HEAD_DIM) | ((col0 >= A_QKV) & (col0 < A_QKV + 2 * B_WIDTH))
    is_pad = (col0 >= QKV_WIDTH) & (col0 < QKV_PAD)
    is_gate = col0 >= QKV_PAD

    def head_norm(acc):
        return jnp.concatenate(
            [_rms_rows(acc[:, c * HEAD_DIM:(c + 1) * HEAD_DIM], gain_ref[:, c * HEAD_DIM:(c + 1) * HEAD_DIM])
             for c in range(tn // HEAD_DIM)], axis=-1)

    def project(epilogue):
        tm = o_ref.shape[0]
        rc = tm // IN_PROJ_ROW_CHUNKS
        for c in range(IN_PROJ_ROW_CHUNKS):
            rows = slice(c * rc, (c + 1) * rc)
            acc = jnp.dot(h_ref[rows, :], w_ref[...], preferred_element_type=jnp.float32)
            o_ref[rows, :] = epilogue(acc).astype(o_ref.dtype)

    @pl.when(is_pad)
    def _():
        o_ref[...] = jnp.zeros(o_ref.shape, o_ref.dtype)

    @pl.when(is_norm)
    def _():
        project(head_norm)

    @pl.when(is_gate)
    def _():
        project(lambda acc: 0.5 * jnp.tanh(0.5 * acc) + 0.5)

    @pl.when(jnp.logical_not(is_pad | is_norm | is_gate))
    def _():
        project(lambda acc: acc)


def _in_proj(x2, g_mix, w_in, gains, *, tm, tn):
    T, D = x2.shape
    n_qkv_tiles = QKV_WIDTH // tn
    n_pad_tiles = (QKV_PAD - QKV_WIDTH) // tn
    out_width = QKV_PAD + 2 * D
    w_tile = lambda j: jnp.where(j < n_qkv_tiles, j, jnp.maximum(j - n_pad_tiles, n_qkv_tiles - 1))
    return pl.pallas_call(
        functools.partial(_in_proj_kernel, tn=tn),
        grid=(T // tm, out_width // tn),
        in_specs=[
            pl.BlockSpec((tm, D), lambda i, j: (i, 0)),
            pl.BlockSpec((1, D), lambda i, j: (0, 0)),
            pl.BlockSpec((D, tn), lambda i, j: (0, w_tile(j))),
            pl.BlockSpec((1, tn), lambda i, j: (0, w_tile(j))),
        ],
        out_specs=pl.BlockSpec((tm, tn), lambda i, j: (i, j)),
        out_shape=jax.ShapeDtypeStruct((T, out_width), jnp.bfloat16),
        scratch_shapes=[pltpu.VMEM((tm, D), jnp.bfloat16)],
        compiler_params=pltpu.CompilerParams(
            dimension_semantics=("arbitrary", "arbitrary"), vmem_limit_bytes=VMEM_LIMIT),
        name="in_proj",
    )(x2, g_mix, w_in, gains)


CHUNK = DIL_PATTERNS[-1][1] * A_BLOCK
BLOCKS_PER_CHUNK = CHUNK // A_BLOCK


def _deinterleave(dst_ref, first_block, src_ref, stage_ref, r):
    bpc = BLOCKS_PER_CHUNK // r
    if r == 1:
        dst_ref[0, first_block:first_block + bpc] = src_ref[...].reshape(bpc, A_BLOCK, A_OUT)
        return
    for h in range(A_HEADS_PER_GROUP):
        cols = slice(h * HEAD_DIM, (h + 1) * HEAD_DIM)
        stage_ref[h] = src_ref[:, cols].astype(jnp.float32)
        for rr in range(r):
            rows = stage_ref[h, pl.ds(rr, CHUNK // r, stride=r), :].astype(dst_ref.dtype)
            dst_ref[rr, first_block:first_block + bpc, :, cols] = rows.reshape(bpc, A_BLOCK, HEAD_DIM)


def _dilated_kernel(q_ref, k_ref, v_ref, bias_ref, o_ref, lse_ref, stage_ref, qd_ref, kd_ref, vd_ref, *, r):
    c = pl.program_id(1)
    bpc = BLOCKS_PER_CHUNK // r
    nblk = BLOCKS_PER_CHUNK

    @pl.when(c == 0)
    def _():
        kd_ref[:, 0] = jnp.zeros((r, A_BLOCK, A_OUT), kd_ref.dtype)
        vd_ref[:, 0] = jnp.zeros((r, A_BLOCK, A_OUT), vd_ref.dtype)

    @pl.when(c > 0)
    def _():
        kd_ref[:, 0] = kd_ref[:, bpc]
        vd_ref[:, 0] = vd_ref[:, bpc]

    _deinterleave(qd_ref, 0, q_ref, stage_ref, r)
    _deinterleave(kd_ref, 1, k_ref, stage_ref, r)
    _deinterleave(vd_ref, 1, v_ref, stage_ref, r)

    blk = lax.broadcasted_iota(jnp.int32, (nblk, A_BLOCK, A_BLOCK), 0)
    no_prev = (lax.rem(blk, bpc) == 0) & (c == 0)
    lane = lax.broadcasted_iota(jnp.int32, (nblk, A_BLOCK, HEAD_DIM), 2)
    bqk = (((2,), (2,)), ((0,), (0,)))
    bqd = (((2,), (1,)), ((0,), (0,)))
    lse = jnp.zeros((nblk, A_BLOCK, HEAD_DIM), jnp.float32)
    for h in range(A_HEADS_PER_GROUP):
        cols = slice(h * HEAD_DIM, (h + 1) * HEAD_DIM)
        q = qd_ref[:, :, :, cols].reshape(nblk, A_BLOCK, HEAD_DIM)
        k_prev = kd_ref[:, 0:bpc, :, cols].reshape(nblk, A_BLOCK, HEAD_DIM)
        k_cur = kd_ref[:, 1:bpc + 1, :, cols].reshape(nblk, A_BLOCK, HEAD_DIM)
        v_prev = vd_ref[:, 0:bpc, :, cols].reshape(nblk, A_BLOCK, HEAD_DIM)
        v_cur = vd_ref[:, 1:bpc + 1, :, cols].reshape(nblk, A_BLOCK, HEAD_DIM)
        s_prev = lax.dot_general(q, k_prev, bqk, preferred_element_type=jnp.float32) + bias_ref[0, h, :, :A_BLOCK]
        s_prev = jnp.where(no_prev, NEG, s_prev)
        s_cur = lax.dot_general(q, k_cur, bqk, preferred_element_type=jnp.float32) + bias_ref[0, h, :, A_BLOCK:]
        m = jnp.max(jnp.maximum(s_prev, s_cur), axis=-1, keepdims=True)
        p_prev = jnp.exp2(s_prev - m)
        p_cur = jnp.exp2(s_cur - m)
        den = jnp.sum(p_prev + p_cur, axis=-1, keepdims=True)
        o = (lax.dot_general(p_prev.astype(jnp.bfloat16), v_prev, bqd, preferred_element_type=jnp.float32)
             + lax.dot_general(p_cur.astype(jnp.bfloat16), v_cur, bqd, preferred_element_type=jnp.float32)) / den
        lse = jnp.where(lane == h, m + jnp.log2(den), lse)
        for u in range(nblk):
            rr, nl = divmod(u, bpc)
            tokens = pl.ds(rr + nl * (r * A_BLOCK), A_BLOCK, stride=r) if r > 1 else pl.ds(u * A_BLOCK, A_BLOCK)
            stage_ref[h, tokens, :] = o[u]
        o_ref[:, cols] = stage_ref[h].astype(o_ref.dtype)
    for u in range(nblk):
        rr, nl = divmod(u, bpc)
        tokens = pl.ds(rr + nl * (r * A_BLOCK), A_BLOCK, stride=r) if r > 1 else pl.ds(u * A_BLOCK, A_BLOCK)
        lse_ref[tokens, :] = lse[u]


def _dilated_group(proj2, bias, group, *, batch, seq):
    _, r = DIL_PATTERNS[group]
    nc = seq // CHUNK
    bpc = BLOCKS_PER_CHUNK // r
    chunk = lambda sec: pl.BlockSpec((CHUNK, A_OUT), lambda b, c: (b * nc + c, sec * N_GROUPS + group))
    return pl.pallas_call(
        functools.partial(_dilated_kernel, r=r),
        grid=(batch, nc),
        in_specs=[chunk(0), chunk(1), chunk(2),
                  pl.BlockSpec((1, A_HEADS_PER_GROUP, A_BLOCK, 2 * A_BLOCK), lambda b, c: (group, 0, 0, 0))],
        out_specs=[pl.BlockSpec((CHUNK, A_OUT), lambda b, c: (b * nc + c, 0)),
                   pl.BlockSpec((CHUNK, HEAD_DIM), lambda b, c: (b * nc + c, 0))],
        out_shape=[jax.ShapeDtypeStruct((batch * seq, A_OUT), jnp.bfloat16),
                   jax.ShapeDtypeStruct((batch * seq, HEAD_DIM), jnp.float32)],
        scratch_shapes=[pltpu.VMEM((A_HEADS_PER_GROUP, CHUNK, HEAD_DIM), jnp.float32),
                        pltpu.VMEM((r, bpc, A_BLOCK, A_OUT), jnp.bfloat16),
                        pltpu.VMEM((r, bpc + 1, A_BLOCK, A_OUT), jnp.bfloat16),
                        pltpu.VMEM((r, bpc + 1, A_BLOCK, A_OUT), jnp.bfloat16)],
        compiler_params=pltpu.CompilerParams(
            dimension_semantics=("arbitrary", "arbitrary"), vmem_limit_bytes=VMEM_LIMIT),
        name=f"dilated_{group}",
    )(proj2, proj2, proj2, bias)


def _split3(x):
    hi = x.astype(jnp.bfloat16)
    r1 = x - hi.astype(jnp.float32)
    mid = r1.astype(jnp.bfloat16)
    lo = (r1 - mid.astype(jnp.float32)).astype(jnp.bfloat16)
    return hi, mid, lo


MOBA_STREAMS = 2


def _moba_items(nb, group):
    streams = [[] for _ in range(MOBA_STREAMS)]
    for i in sorted(range(nb), key=lambda i: -(i // group)):
        min(streams, key=len).extend((i, g) for g in range(i // group + 1))
    assert len({len(s) for s in streams}) == 1 and len(streams[0]) % 2 == 0
    return streams


def _moba_kernel(items_ref, q_ref, k_ref, v_ref, bias_ref, o_ref, kmean_ref, ka_ref, qa_ref, va_ref, s_ref,
                 st_ref, *, nb, group, n_items):
    BS = MOBA_BLOCK
    hd = HEAD_DIM
    seq = nb * BS
    rows_per_group = group * BS
    nt = (((1,), (1,)), ((), ()))

    row = lax.broadcasted_iota(jnp.int32, (nb, seq), 0)
    col = lax.broadcasted_iota(jnp.int32, (nb, seq), 1)
    lo_edge = row * BS
    avg = jnp.where((col >= lo_edge) & (col < lo_edge + BS), 1.0 / BS, 0.0).astype(jnp.bfloat16)
    kmean = jnp.dot(avg, k_ref[...], preferred_element_type=jnp.float32)
    for t, part in enumerate(_split3(kmean)):
        kmean_ref[t] = part

    q_all = q_ref[...]
    gate = sum(lax.dot_general(kmean_ref[t], q_all, nt, preferred_element_type=jnp.float32) for t in range(3))
    own = col // BS
    past = row < own
    gate = jnp.where(past, gate, -jnp.inf)
    rank = jnp.zeros((nb, seq), jnp.int32)
    for m in range(nb):
        gm = gate[m:m + 1, :]
        beats = (gm > gate) | ((gm == gate) & (row > m))
        rank = rank + beats.astype(jnp.int32)
    sel = jnp.where((past & (rank < MOBA_TOPK)) | (row == own), 0.0, NEG)

    qa_ref[:hd, :] = q_all.T
    qa_ref[hd:hd + nb, :] = sel.astype(qa_ref.dtype)
    qa_ref[hd + nb:, :] = jnp.zeros((hd - nb, seq), qa_ref.dtype)
    ka_ref[:, :hd] = k_ref[...]
    key_blk = lax.broadcasted_iota(jnp.int32, (seq, hd), 0) // BS
    ka_ref[:, hd:] = jnp.where(lax.broadcasted_iota(jnp.int32, (seq, hd), 1) == key_blk, 1.0, 0.0).astype(ka_ref.dtype)
    va_ref[:hd, :] = v_ref[...].T
    va_ref[hd:, :] = jnp.where(lax.broadcasted_iota(jnp.int32, (MOBA_AUX, seq), 0) == 0, 1.0, 0.0).astype(va_ref.dtype)

    def logits(st, w, slot):
        i, g = items_ref[st, w, 0], items_ref[st, w, 1]
        rows = pl.ds(pl.multiple_of(g * rows_per_group, rows_per_group), rows_per_group)
        qcols = pl.ds(pl.multiple_of(i * BS, BS), BS)
        s_t = jnp.dot(ka_ref[rows, :], qa_ref[:, qcols], preferred_element_type=jnp.float32)
        m8 = jnp.full((8, BS), -jnp.inf, jnp.float32)
        for u in range(group):
            tile = jnp.clip(i - (g * group + u), 0, MOBA_BIAS_TILES - 1)
            s_u = s_t[u * BS:(u + 1) * BS] + bias_ref[0, tile]
            s_ref[2 * st + slot, u * BS:(u + 1) * BS, :] = s_u
            m8 = jnp.maximum(m8, jnp.max(s_u.reshape(BS // 8, 8, BS), axis=0))
        return m8

    def accumulate(st, w, slot, m8, state):
        m_run, acc = state
        i, g = items_ref[st, w, 0], items_ref[st, w, 1]
        rows = pl.ds(pl.multiple_of(g * rows_per_group, rows_per_group), rows_per_group)
        m_prev = jnp.where(g == 0, -jnp.inf, m_run)
        m_new = jnp.maximum(m_prev, jnp.max(m8, axis=0, keepdims=True))
        alpha = jnp.exp2(m_prev - m_new)
        p = jnp.exp2(s_ref[2 * st + slot] - m_new).astype(jnp.bfloat16)
        acc = alpha * acc + jnp.dot(va_ref[:, rows], p, preferred_element_type=jnp.float32)
        st_ref[i] = acc
        return m_new, acc

    streams = range(MOBA_STREAMS)

    def pair(t, carry):
        m8a, state = carry
        w = 2 * t
        m8b = [logits(st, w + 1, 1) for st in streams]
        state = [accumulate(st, w, 0, m8a[st], state[st]) for st in streams]
        m8a = [logits(st, jnp.minimum(w + 2, n_items - 1), 0) for st in streams]
        state = [accumulate(st, w + 1, 1, m8b[st], state[st]) for st in streams]
        return m8a, state

    state0 = (jnp.full((1, BS), -jnp.inf, jnp.float32), jnp.zeros((hd + MOBA_AUX, BS), jnp.float32))
    lax.fori_loop(0, n_items // 2, pair, ([logits(st, 0, 0) for st in streams], [state0 for _ in streams]))

    for i in range(nb):
        acc = st_ref[i]
        o_ref[i * BS:(i + 1) * BS, :] = (acc[:hd] / acc[hd:hd + 1]).T.astype(o_ref.dtype)


def _moba(proj2, bias, *, batch, seq):
    BS = MOBA_BLOCK
    nb = seq // BS
    assert nb <= HEAD_DIM
    c0 = A_QKV // HEAD_DIM
    group = math.gcd(nb, MOBA_GROUP)
    items = _moba_items(nb, group)
    head_spec = lambda sec: pl.BlockSpec((seq, HEAD_DIM), lambda b, h, items: (b, c0 + sec * B_HEADS + h))
    return pl.pallas_call(
        functools.partial(_moba_kernel, nb=nb, group=group, n_items=len(items[0])),
        grid_spec=pltpu.PrefetchScalarGridSpec(
            num_scalar_prefetch=1,
            grid=(batch, B_HEADS),
            in_specs=[head_spec(0), head_spec(1), head_spec(2),
                      pl.BlockSpec((1, MOBA_BIAS_TILES, BS, BS), lambda b, h, items: (h, 0, 0, 0))],
            out_specs=pl.BlockSpec((seq, HEAD_DIM), lambda b, h, items: (b, h)),
            scratch_shapes=[pltpu.VMEM((3, nb, HEAD_DIM), jnp.bfloat16),
                            pltpu.VMEM((seq, 2 * HEAD_DIM), jnp.bfloat16),
                            pltpu.VMEM((2 * HEAD_DIM, seq), jnp.bfloat16),
                            pltpu.VMEM((HEAD_DIM + MOBA_AUX, seq), jnp.bfloat16),
                            pltpu.VMEM((2 * MOBA_STREAMS, group * BS, BS), jnp.float32),
                            pltpu.VMEM((nb, HEAD_DIM + MOBA_AUX, BS), jnp.float32)]),
        out_shape=jax.ShapeDtypeStruct((batch * seq, B_WIDTH), jnp.bfloat16),
        compiler_params=pltpu.CompilerParams(
            dimension_semantics=("arbitrary", "arbitrary"), vmem_limit_bytes=VMEM_LIMIT),
        name="moba",
    )(jnp.asarray(items, jnp.int32), proj2, proj2, proj2, bias)


def _merge_kernel(o0_ref, o1_ref, o2_ref, l0_ref, l1_ref, l2_ref, yb_ref, ga_ref, gb_ref, x_ref,
                  wa_ref, wb_ref, wo_ref, out_ref):
    l0, l1, l2 = l0_ref[...], l1_ref[...], l2_ref[...]
    mx = jnp.maximum(jnp.maximum(l0, l1), l2)
    e0, e1, e2 = jnp.exp2(l0 - mx), jnp.exp2(l1 - mx), jnp.exp2(l2 - mx)
    inv = 1.0 / (e0 + e1 + e2)
    parts = []
    for h in range(A_HEADS_PER_GROUP):
        cols = slice(h * HEAD_DIM, (h + 1) * HEAD_DIM)
        ya = ((e0 * inv)[:, h:h + 1] * o0_ref[:, cols].astype(jnp.float32)
              + (e1 * inv)[:, h:h + 1] * o1_ref[:, cols].astype(jnp.float32)
              + (e2 * inv)[:, h:h + 1] * o2_ref[:, cols].astype(jnp.float32))
        parts.append(ya.astype(jnp.bfloat16))
    ya = jnp.concatenate(parts, axis=-1)
    pa = jnp.dot(ya, wa_ref[...], preferred_element_type=jnp.float32)
    pb = jnp.dot(yb_ref[...], wb_ref[...], preferred_element_type=jnp.float32)
    mixed = ga_ref[...].astype(jnp.float32) * pa + gb_ref[...].astype(jnp.float32) * pb
    out_ref[...] = x_ref[...] + jnp.dot(mixed.astype(jnp.bfloat16), wo_ref[...], preferred_element_type=jnp.float32)


def _merge(outs, lses, yb, proj2, x2, wa, wb, wo, *, tm):
    T, D = x2.shape
    gate0 = QKV_PAD // D
    row = lambda w: pl.BlockSpec((tm, w), lambda i: (i, 0))
    full = lambda a: pl.BlockSpec(a.shape, lambda i: (0, 0))
    return pl.pallas_call(
        _merge_kernel,
        grid=(T // tm,),
        in_specs=[row(A_OUT)] * 3 + [row(HEAD_DIM)] * 3 + [row(B_WIDTH),
                  pl.BlockSpec((tm, D), lambda i: (i, gate0)), pl.BlockSpec((tm, D), lambda i: (i, gate0 + 1)),
                  row(D), full(wa), full(wb), full(wo)],
        out_specs=row(D),
        out_shape=jax.ShapeDtypeStruct((T, D), jnp.float32),
        compiler_params=pltpu.CompilerParams(dimension_semantics=("arbitrary",), vmem_limit_bytes=VMEM_LIMIT),
        name="merge",
    )(*outs, *lses, yb, proj2, proj2, x2, wa, wb, wo)


def _mlp_kernel(x_ref, g_ref, wu_ref, wd_ref, o_ref, h_ref):
    f = pl.program_id(1)

    @pl.when(f == 0)
    def _():
        x = x_ref[...]
        h_ref[...] = _rms_rows(x, g_ref[...]).astype(h_ref.dtype)
        o_ref[...] = x

    u = jnp.dot(h_ref[...], wu_ref[...], preferred_element_type=jnp.float32)
    u = jnp.square(jnp.maximum(u, 0.0)).astype(jnp.bfloat16)
    o_ref[...] += jnp.dot(u, wd_ref[...], preferred_element_type=jnp.float32)


def _mlp(x2, g_mlp, w_up, w_down, *, tm, tf):
    T, D = x2.shape
    F = w_up.shape[1]
    return pl.pallas_call(
        _mlp_kernel,
        grid=(T // tm, F // tf),
        in_specs=[pl.BlockSpec((tm, D), lambda i, f: (i, 0)),
                  pl.BlockSpec((1, D), lambda i, f: (0, 0)),
                  pl.BlockSpec((D, tf), lambda i, f: (0, f)),
                  pl.BlockSpec((tf, D), lambda i, f: (f, 0))],
        out_specs=pl.BlockSpec((tm, D), lambda i, f: (i, 0)),
        out_shape=jax.ShapeDtypeStruct((T, D), jnp.float32),
        scratch_shapes=[pltpu.VMEM((tm, D), jnp.bfloat16)],
        compiler_params=pltpu.CompilerParams(
            dimension_semantics=("arbitrary", "arbitrary"), vmem_limit_bytes=VMEM_LIMIT),
        name="mlp",
    )(x2, g_mlp, w_up, w_down)


def _column_gains(q_norm_a, k_norm_a, q_norm_b, k_norm_b, width):
    qs = SCALE * LOG2E
    gains = jnp.concatenate([
        jnp.tile(q_norm_a * qs, A_HEADS), jnp.tile(k_norm_a, A_HEADS), jnp.ones((A_HEADS * HEAD_DIM,), jnp.float32),
        jnp.tile(q_norm_b * qs, B_HEADS), jnp.tile(k_norm_b, B_HEADS), jnp.ones((B_WIDTH,), jnp.float32),
        jnp.ones((width - QKV_WIDTH,), jnp.float32)])
    return gains.reshape(1, width)


def kernel(x, g_mix, w_in, q_norm_a, k_norm_a, q_norm_b, k_norm_b, rel_bias,
           w_branch_a, w_branch_b, w_out, g_mlp, w_up, w_down):
    batch, seq, d_model = x.shape
    T = batch * seq
    assert w_in.shape == (d_model, QKV_WIDTH + 2 * d_model)
    assert seq % (DIL_PATTERNS[-1][1] * A_BLOCK) == 0 and QKV_PAD % d_model == 0
    bf16 = jnp.bfloat16
    x2 = x.reshape(T, d_model)
    tm = min(1024, T)

    dil_bias, moba_bias = _bias_tables(rel_bias)
    gains = _column_gains(q_norm_a, k_norm_a, q_norm_b, k_norm_b, w_in.shape[1])
    proj = _in_proj(x2, g_mix.reshape(1, -1), w_in.astype(bf16), gains, tm=tm, tn=min(512, d_model * 2))

    outs, lses = zip(*[_dilated_group(proj, dil_bias, g, batch=batch, seq=seq) for g in range(N_GROUPS)])
    yb = _moba(proj, moba_bias, batch=batch, seq=seq)

    x_mid = _merge(outs, lses, yb, proj, x2, w_branch_a.astype(bf16), w_branch_b.astype(bf16),
                   w_out.astype(bf16), tm=min(512, T))
    y = _mlp(x_mid, g_mlp.reshape(1, -1), w_up.astype(bf16), w_down.astype(bf16), tm=tm, tf=min(512, w_up.shape[1]))
    return y.reshape(batch, seq, d_model)
```

```python
import functools
import math

import jax
import jax.numpy as jnp
from jax import lax
from jax.experimental import pallas as pl
from jax.experimental.pallas import tpu as pltpu

HEAD_DIM = 128
DIL_PATTERNS = ((128, 1), (512, 4), (2048, 16))
N_GROUPS = len(DIL_PATTERNS)
A_HEADS_PER_GROUP = 4
A_HEADS = A_HEADS_PER_GROUP * N_GROUPS
A_OUT = A_HEADS_PER_GROUP * HEAD_DIM
A_BLOCK = 128
B_HEADS = 8
B_WIDTH = B_HEADS * HEAD_DIM
MOBA_BLOCK = 256
MOBA_TOPK = 3
MOBA_GROUP = 4
MOBA_AUX = 16
N_BUCKETS = 32
MAX_DISTANCE = 2048
EPS = 1e-6
SCALE = HEAD_DIM ** -0.5
A_QKV = 3 * A_HEADS * HEAD_DIM
B_QKV = 3 * B_WIDTH
QKV_WIDTH = A_QKV + B_QKV
QKV_PAD = 8192
LOG2E = math.log2(math.e)
NEG = -1e30
VMEM_LIMIT = 56 * 1024 * 1024


def _bucket_thresholds():
    max_exact = N_BUCKETS // 2

    def bucket(d):
        if d < max_exact:
            return d
        v = int(math.log(d / max_exact) / math.log(MAX_DISTANCE / max_exact) * (N_BUCKETS - max_exact))
        return min(max_exact + v, N_BUCKETS - 1)

    thr, d = [0], 0
    for b in range(1, N_BUCKETS):
        while bucket(d) < b:
            d += 1
        thr.append(d)
    return tuple(thr)


BUCKET_THRESHOLDS = _bucket_thresholds()
MOBA_BIAS_TILES = -(-(BUCKET_THRESHOLDS[-1] - 1) // MOBA_BLOCK) + 2


def _bias_lookup(dist, tab_ref, col, dmin=0, dmax=None):
    first = sum(dmin >= t for t in BUCKET_THRESHOLDS[1:])
    val = jnp.full(dist.shape, tab_ref[first, col], jnp.float32)
    for b in range(first + 1, N_BUCKETS):
        if dmax is None or BUCKET_THRESHOLDS[b] <= dmax:
            val = jnp.where(dist >= BUCKET_THRESHOLDS[b], tab_ref[b, col], val)
    return val * LOG2E


def _dil_bias_kernel(tab_ref, o_ref):
    g, h = pl.program_id(0), pl.program_id(1)
    a = lax.broadcasted_iota(jnp.int32, (A_BLOCK, 2 * A_BLOCK), 0)
    j = lax.broadcasted_iota(jnp.int32, (A_BLOCK, 2 * A_BLOCK), 1)
    delta = a + A_BLOCK - j
    dilation = lax.shift_left(jnp.int32(1), 2 * g)
    val = _bias_lookup(delta * dilation, tab_ref, g * A_HEADS_PER_GROUP + h)
    o_ref[0, 0] = jnp.where((delta >= 0) & (delta <= A_BLOCK), val, NEG)


def _moba_bias_kernel(tab_ref, o_ref):
    h = pl.program_id(0)
    j = lax.broadcasted_iota(jnp.int32, (MOBA_BLOCK, MOBA_BLOCK), 0)
    s = lax.broadcasted_iota(jnp.int32, (MOBA_BLOCK, MOBA_BLOCK), 1)
    for c in range(MOBA_BIAS_TILES):
        dist = c * MOBA_BLOCK + s - j
        lo, hi = (c - 1) * MOBA_BLOCK + 1, (c + 1) * MOBA_BLOCK - 1
        val = _bias_lookup(dist, tab_ref, A_HEADS + h, max(lo, 0), hi)
        o_ref[0, c] = jnp.where(dist >= 0, val, NEG)


def _bias_tables(rel_bias):
    smem = pl.BlockSpec(memory_space=pltpu.SMEM)
    dil = pl.pallas_call(
        _dil_bias_kernel,
        grid=(N_GROUPS, A_HEADS_PER_GROUP),
        in_specs=[smem],
        out_specs=pl.BlockSpec((1, 1, A_BLOCK, 2 * A_BLOCK), lambda g, h: (g, h, 0, 0)),
        out_shape=jax.ShapeDtypeStruct((N_GROUPS, A_HEADS_PER_GROUP, A_BLOCK, 2 * A_BLOCK), jnp.float32),
        name="dil_bias",
    )(rel_bias)
    moba = pl.pallas_call(
        _moba_bias_kernel,
        grid=(B_HEADS,),
        in_specs=[smem],
        out_specs=pl.BlockSpec((1, MOBA_BIAS_TILES, MOBA_BLOCK, MOBA_BLOCK), lambda h: (h, 0, 0, 0)),
        out_shape=jax.ShapeDtypeStruct((B_HEADS, MOBA_BIAS_TILES, MOBA_BLOCK, MOBA_BLOCK), jnp.float32),
        name="moba_bias",
    )(rel_bias)
    return dil, moba


IN_PROJ_ROW_CHUNKS = 4
IN_PROJ_K_CHUNKS = 4


def _rms_rows(x, g):
    return (x * lax.rsqrt(jnp.mean(x * x, axis=-1, keepdims=True) + EPS)) * g


def _in_proj_kernel(x_ref, g_ref, w_ref, gain_ref, o_ref, h_ref, wbf_ref, *, tn):
    j = pl.program_id(1)

    @pl.when(j == 0)
    def _():
        h_ref[...] = _rms_rows(x_ref[...], g_ref[...]).astype(h_ref.dtype)

    col0 = j * tn
    is_norm = (col0 < 2 * A_HEADS * HEAD_DIM) | ((col0 >= A_QKV) & (col0 < A_QKV + 2 * B_WIDTH))
    is_pad = (col0 >= QKV_WIDTH) & (col0 < QKV_PAD)
    is_gate = col0 >= QKV_PAD

    def head_norm(acc):
        return jnp.concatenate(
            [_rms_rows(acc[:, c * HEAD_DIM:(c + 1) * HEAD_DIM], gain_ref[:, c * HEAD_DIM:(c + 1) * HEAD_DIM])
             for c in range(tn // HEAD_DIM)], axis=-1)

    def project(epilogue):
        tm, d = h_ref.shape
        rc = tm // IN_PROJ_ROW_CHUNKS
        kc = d // IN_PROJ_K_CHUNKS
        for c in range(IN_PROJ_ROW_CHUNKS):
            rows = slice(c * rc, (c + 1) * rc)
            acc = None
            for k in range(IN_PROJ_K_CHUNKS):
                ks = slice(k * kc, (k + 1) * kc)
                if c == 0:
                    wbf_ref[ks, :] = w_ref[ks, :].astype(wbf_ref.dtype)
                part = jnp.dot(h_ref[rows, ks], wbf_ref[ks, :], preferred_element_type=jnp.float32)
                acc = part if acc is None else acc + part
            o_ref[rows, :] = epilogue(acc).astype(o_ref.dtype)

    @pl.when(is_pad)
    def _():
        o_ref[...] = jnp.zeros(o_ref.shape, o_ref.dtype)

    @pl.when(is_norm)
    def _():
        project(head_norm)

    @pl.when(is_gate)
    def _():
        project(lambda acc: 0.5 * jnp.tanh(0.5 * acc) + 0.5)

    @pl.when(jnp.logical_not(is_pad | is_norm | is_gate))
    def _():
        project(lambda acc: acc)


def _in_proj(x2, g_mix, w_in, gains, *, tm, tn):
    T, D = x2.shape
    n_qkv_tiles = QKV_WIDTH // tn
    n_pad_tiles = (QKV_PAD - QKV_WIDTH) // tn
    out_width = QKV_PAD + 2 * D
    w_tile = lambda j: jnp.where(j < n_qkv_tiles, j, jnp.maximum(j - n_pad_tiles, n_qkv_tiles - 1))
    return pl.pallas_call(
        functools.partial(_in_proj_kernel, tn=tn),
        grid=(T // tm, out_width // tn),
        in_specs=[
            pl.BlockSpec((tm, D), lambda i, j: (i, 0)),
            pl.BlockSpec((1, D), lambda i, j: (0, 0)),
            pl.BlockSpec((D, tn), lambda i, j: (0, w_tile(j))),
            pl.BlockSpec((1, tn), lambda i, j: (0, w_tile(j))),
        ],
        out_specs=pl.BlockSpec((tm, tn), lambda i, j: (i, j)),
        out_shape=jax.ShapeDtypeStruct((T, out_width), jnp.bfloat16),
        scratch_shapes=[pltpu.VMEM((tm, D), jnp.bfloat16), pltpu.VMEM((D, tn), jnp.bfloat16)],
        compiler_params=pltpu.CompilerParams(
            dimension_semantics=("arbitrary", "arbitrary"), vmem_limit_bytes=VMEM_LIMIT),
        name="in_proj",
    )(x2, g_mix, w_in, gains)


CHUNK = DIL_PATTERNS[-1][1] * A_BLOCK
BLOCKS_PER_CHUNK = CHUNK // A_BLOCK


def _deinterleave(dst_ref, first_block, src_ref, stage_ref, r):
    bpc = BLOCKS_PER_CHUNK // r
    if r == 1:
        dst_ref[0, first_block:first_block + bpc] = src_ref[...].reshape(bpc, A_BLOCK, A_OUT)
        return
    for h in range(A_HEADS_PER_GROUP):
        cols = slice(h * HEAD_DIM, (h + 1) * HEAD_DIM)
        stage_ref[h] = src_ref[:, cols].astype(jnp.float32)
        for rr in range(r):
            rows = stage_ref[h, pl.ds(rr, CHUNK // r, stride=r), :].astype(dst_ref.dtype)
            dst_ref[rr, first_block:first_block + bpc, :, cols] = rows.reshape(bpc, A_BLOCK, HEAD_DIM)


def _dilated_kernel(q_ref, k_ref, v_ref, bias_ref, o_ref, lse_ref, stage_ref, qd_ref, kd_ref, vd_ref, *, r):
    c = pl.program_id(1)
    bpc = BLOCKS_PER_CHUNK // r
    nblk = BLOCKS_PER_CHUNK

    @pl.when(c == 0)
    def _():
        kd_ref[:, 0] = jnp.zeros((r, A_BLOCK, A_OUT), kd_ref.dtype)
        vd_ref[:, 0] = jnp.zeros((r, A_BLOCK, A_OUT), vd_ref.dtype)

    @pl.when(c > 0)
    def _():
        kd_ref[:, 0] = kd_ref[:, bpc]
        vd_ref[:, 0] = vd_ref[:, bpc]

    _deinterleave(qd_ref, 0, q_ref, stage_ref, r)
    _deinterleave(kd_ref, 1, k_ref, stage_ref, r)
    _deinterleave(vd_ref, 1, v_ref, stage_ref, r)

    blk = lax.broadcasted_iota(jnp.int32, (nblk, A_BLOCK, A_BLOCK), 0)
    no_prev = (lax.rem(blk, bpc) == 0) & (c == 0)
    lane = lax.broadcasted_iota(jnp.int32, (nblk, A_BLOCK, HEAD_DIM), 2)
    bqk = (((2,), (2,)), ((0,), (0,)))
    bqd = (((2,), (1,)), ((0,), (0,)))
    lse = jnp.zeros((nblk, A_BLOCK, HEAD_DIM), jnp.float32)
    for h in range(A_HEADS_PER_GROUP):
        cols = slice(h * HEAD_DIM, (h + 1) * HEAD_DIM)
        q = qd_ref[:, :, :, cols].reshape(nblk, A_BLOCK, HEAD_DIM)
        k_prev = kd_ref[:, 0:bpc, :, cols].reshape(nblk, A_BLOCK, HEAD_DIM)
        k_cur = kd_ref[:, 1:bpc + 1, :, cols].reshape(nblk, A_BLOCK, HEAD_DIM)
        v_prev = vd_ref[:, 0:bpc, :, cols].reshape(nblk, A_BLOCK, HEAD_DIM)
        v_cur = vd_ref[:, 1:bpc + 1, :, cols].reshape(nblk, A_BLOCK, HEAD_DIM)
        s_prev = lax.dot_general(q, k_prev, bqk, preferred_element_type=jnp.float32) + bias_ref[0, h, :, :A_BLOCK]
        s_prev = jnp.where(no_prev, NEG, s_prev)
        s_cur = lax.dot_general(q, k_cur, bqk, preferred_element_type=jnp.float32) + bias_ref[0, h, :, A_BLOCK:]
        m = jnp.max(jnp.maximum(s_prev, s_cur), axis=-1, keepdims=True)
        p_prev = jnp.exp2(s_prev - m)
        p_cur = jnp.exp2(s_cur - m)
        den = jnp.sum(p_prev + p_cur, axis=-1, keepdims=True)
        o = (lax.dot_general(p_prev.astype(jnp.bfloat16), v_prev, bqd, preferred_element_type=jnp.float32)
             + lax.dot_general(p_cur.astype(jnp.bfloat16), v_cur, bqd, preferred_element_type=jnp.float32)) / den
        lse = jnp.where(lane == h, m + jnp.log2(den), lse)
        for u in range(nblk):
            rr, nl = divmod(u, bpc)
            tokens = pl.ds(rr + nl * (r * A_BLOCK), A_BLOCK, stride=r) if r > 1 else pl.ds(u * A_BLOCK, A_BLOCK)
            stage_ref[h, tokens, :] = o[u]
        o_ref[:, cols] = stage_ref[h].astype(o_ref.dtype)
    for u in range(nblk):
        rr, nl = divmod(u, bpc)
        tokens = pl.ds(rr + nl * (r * A_BLOCK), A_BLOCK, stride=r) if r > 1 else pl.ds(u * A_BLOCK, A_BLOCK)
        lse_ref[tokens, :] = lse[u]


def _dilated_group(proj2, bias, group, *, batch, seq):
    _, r = DIL_PATTERNS[group]
    nc = seq // CHUNK
    bpc = BLOCKS_PER_CHUNK // r
    chunk = lambda sec: pl.BlockSpec((CHUNK, A_OUT), lambda b, c: (b * nc + c, sec * N_GROUPS + group))
    return pl.pallas_call(
        functools.partial(_dilated_kernel, r=r),
        grid=(batch, nc),
        in_specs=[chunk(0), chunk(1), chunk(2),
                  pl.BlockSpec((1, A_HEADS_PER_GROUP, A_BLOCK, 2 * A_BLOCK), lambda b, c: (group, 0, 0, 0))],
        out_specs=[pl.BlockSpec((CHUNK, A_OUT), lambda b, c: (b * nc + c, 0)),
                   pl.BlockSpec((CHUNK, HEAD_DIM), lambda b, c: (b * nc + c, 0))],
        out_shape=[jax.ShapeDtypeStruct((batch * seq, A_OUT), jnp.bfloat16),
                   jax.ShapeDtypeStruct((batch * seq, HEAD_DIM), jnp.float32)],
        scratch_shapes=[pltpu.VMEM((A_HEADS_PER_GROUP, CHUNK, HEAD_DIM), jnp.float32),
                        pltpu.VMEM((r, bpc, A_BLOCK, A_OUT), jnp.bfloat16),
                        pltpu.VMEM((r, bpc + 1, A_BLOCK, A_OUT), jnp.bfloat16),
                        pltpu.VMEM((r, bpc + 1, A_BLOCK, A_OUT), jnp.bfloat16)],
        compiler_params=pltpu.CompilerParams(
            dimension_semantics=("arbitrary", "arbitrary"), vmem_limit_bytes=VMEM_LIMIT),
        name=f"dilated_{group}",
    )(proj2, proj2, proj2, bias)


def _split3(x):
    hi = x.astype(jnp.bfloat16)
    r1 = x - hi.astype(jnp.float32)
    mid = r1.astype(jnp.bfloat16)
    lo = (r1 - mid.astype(jnp.float32)).astype(jnp.bfloat16)
    return hi, mid, lo


MOBA_STREAMS = 2


def _moba_items(nb, group):
    streams = [[] for _ in range(MOBA_STREAMS)]
    for i in sorted(range(nb), key=lambda i: -(i // group)):
        min(streams, key=len).extend((i, g) for g in range(i // group + 1))
    assert len({len(s) for s in streams}) == 1 and len(streams[0]) % 2 == 0
    return streams


def _moba_kernel(items_ref, q_ref, k_ref, v_ref, bias_ref, o_ref, kmean_ref, ka_ref, qa_ref, va_ref, s_ref,
                 st_ref, *, nb, group, n_items):
    BS = MOBA_BLOCK
    hd = HEAD_DIM
    seq = nb * BS
    rows_per_group = group * BS
    nt = (((1,), (1,)), ((), ()))

    row = lax.broadcasted_iota(jnp.int32, (nb, seq), 0)
    col = lax.broadcasted_iota(jnp.int32, (nb, seq), 1)
    lo_edge = row * BS
    avg = jnp.where((col >= lo_edge) & (col < lo_edge + BS), 1.0 / BS, 0.0).astype(jnp.bfloat16)
    kmean = jnp.dot(avg, k_ref[...], preferred_element_type=jnp.float32)
    for t, part in enumerate(_split3(kmean)):
        kmean_ref[t] = part

    q_all = q_ref[...]
    gate = sum(lax.dot_general(kmean_ref[t], q_all, nt, preferred_element_type=jnp.float32) for t in range(3))
    own = col // BS
    past = row < own
    gate = jnp.where(past, gate, -jnp.inf)
    rank = jnp.zeros((nb, seq), jnp.int32)
    for m in range(nb):
        gm = gate[m:m + 1, :]
        beats = (gm > gate) | ((gm == gate) & (row > m))
        rank = rank + beats.astype(jnp.int32)
    sel = jnp.where((past & (rank < MOBA_TOPK)) | (row == own), 0.0, NEG)

    qa_ref[:hd, :] = q_all.T
    qa_ref[hd:hd + nb, :] = sel.astype(qa_ref.dtype)
    qa_ref[hd + nb:, :] = jnp.zeros((hd - nb, seq), qa_ref.dtype)
    ka_ref[:, :hd] = k_ref[...]
    key_blk = lax.broadcasted_iota(jnp.int32, (seq, hd), 0) // BS
    ka_ref[:, hd:] = jnp.where(lax.broadcasted_iota(jnp.int32, (seq, hd), 1) == key_blk, 1.0, 0.0).astype(ka_ref.dtype)
    va_ref[:hd, :] = v_ref[...].T
    va_ref[hd:, :] = jnp.where(lax.broadcasted_iota(jnp.int32, (MOBA_AUX, seq), 0) == 0, 1.0, 0.0).astype(va_ref.dtype)

    def logits(st, w, slot):
        i, g = items_ref[st, w, 0], items_ref[st, w, 1]
        rows = pl.ds(pl.multiple_of(g * rows_per_group, rows_per_group), rows_per_group)
        qcols = pl.ds(pl.multiple_of(i * BS, BS), BS)
        s_t = jnp.dot(ka_ref[rows, :], qa_ref[:, qcols], preferred_element_type=jnp.float32)
        m8 = jnp.full((8, BS), -jnp.inf, jnp.float32)
        for u in range(group):
            tile = jnp.clip(i - (g * group + u), 0, MOBA_BIAS_TILES - 1)
            s_u = s_t[u * BS:(u + 1) * BS] + bias_ref[0, tile]
            s_ref[2 * st + slot, u * BS:(u + 1) * BS, :] = s_u
            m8 = jnp.maximum(m8, jnp.max(s_u.reshape(BS // 8, 8, BS), axis=0))
        return m8

    def accumulate(st, w, slot, m8, state):
        m_run, acc = state
        i, g = items_ref[st, w, 0], items_ref[st, w, 1]
        rows = pl.ds(pl.multiple_of(g * rows_per_group, rows_per_group), rows_per_group)
        m_prev = jnp.where(g == 0, -jnp.inf, m_run)
        m_new = jnp.maximum(m_prev, jnp.max(m8, axis=0, keepdims=True))
        alpha = jnp.exp2(m_prev - m_new)
        p = jnp.exp2(s_ref[2 * st + slot] - m_new).astype(jnp.bfloat16)
        acc = alpha * acc + jnp.dot(va_ref[:, rows], p, preferred_element_type=jnp.float32)
        st_ref[i] = acc
        return m_new, acc

    streams = range(MOBA_STREAMS)

    def pair(t, carry):
        m8a, state = carry
        w = 2 * t
        m8b = [logits(st, w + 1, 1) for st in streams]
        state = [accumulate(st, w, 0, m8a[st], state[st]) for st in streams]
        m8a = [logits(st, jnp.minimum(w + 2, n_items - 1), 0) for st in streams]
        state = [accumulate(st, w + 1, 1, m8b[st], state[st]) for st in streams]
        return m8a, state

    state0 = (jnp.full((1, BS), -jnp.inf, jnp.float32), jnp.zeros((hd + MOBA_AUX, BS), jnp.float32))
    lax.fori_loop(0, n_items // 2, pair, ([logits(st, 0, 0) for st in streams], [state0 for _ in streams]))

    for i in range(nb):
        acc = st_ref[i]
        o_ref[i * BS:(i + 1) * BS, :] = (acc[:hd] / acc[hd:hd + 1]).T.astype(o_ref.dtype)


def _moba(proj2, bias, *, batch, seq):
    BS = MOBA_BLOCK
    nb = seq // BS
    assert nb <= HEAD_DIM
    c0 = A_QKV // HEAD_DIM
    group = math.gcd(nb, MOBA_GROUP)
    items = _moba_items(nb, group)
    head_spec = lambda sec: pl.BlockSpec((seq, HEAD_DIM), lambda b, h, items: (b, c0 + sec * B_HEADS + h))
    return pl.pallas_call(
        functools.partial(_moba_kernel, nb=nb, group=group, n_items=len(items[0])),
        grid_spec=pltpu.PrefetchScalarGridSpec(
            num_scalar_prefetch=1,
            grid=(batch, B_HEADS),
            in_specs=[head_spec(0), head_spec(1), head_spec(2),
                      pl.BlockSpec((1, MOBA_BIAS_TILES, BS, BS), lambda b, h, items: (h, 0, 0, 0))],
            out_specs=pl.BlockSpec((seq, HEAD_DIM), lambda b, h, items: (b, h)),
            scratch_shapes=[pltpu.VMEM((3, nb, HEAD_DIM), jnp.bfloat16),
                            pltpu.VMEM((seq, 2 * HEAD_DIM), jnp.bfloat16),
                            pltpu.VMEM((2 * HEAD_DIM, seq), jnp.bfloat16),
                            pltpu.VMEM((HEAD_DIM + MOBA_AUX, seq), jnp.bfloat16),
                            pltpu.VMEM((2 * MOBA_STREAMS, group * BS, BS), jnp.float32),
                            pltpu.VMEM((nb, HEAD_DIM + MOBA_AUX, BS), jnp.float32)]),
        out_shape=jax.ShapeDtypeStruct((batch * seq, B_WIDTH), jnp.bfloat16),
        compiler_params=pltpu.CompilerParams(
            dimension_semantics=("arbitrary", "arbitrary"), vmem_limit_bytes=VMEM_LIMIT),
        name="moba",
    )(jnp.asarray(items, jnp.int32), proj2, proj2, proj2, bias)


def _merge_kernel(o0_ref, o1_ref, o2_ref, l0_ref, l1_ref, l2_ref, yb_ref, ga_ref, gb_ref, x_ref,
                  wa_ref, wb_ref, wo_ref, out_ref):
    l0, l1, l2 = l0_ref[...], l1_ref[...], l2_ref[...]
    mx = jnp.maximum(jnp.maximum(l0, l1), l2)
    e0, e1, e2 = jnp.exp2(l0 - mx), jnp.exp2(l1 - mx), jnp.exp2(l2 - mx)
    inv = 1.0 / (e0 + e1 + e2)
    parts = []
    for h in range(A_HEADS_PER_GROUP):
        cols = slice(h * HEAD_DIM, (h + 1) * HEAD_DIM)
        ya = ((e0 * inv)[:, h:h + 1] * o0_ref[:, cols].astype(jnp.float32)
              + (e1 * inv)[:, h:h + 1] * o1_ref[:, cols].astype(jnp.float32)
              + (e2 * inv)[:, h:h + 1] * o2_ref[:, cols].astype(jnp.float32))
        parts.append(ya.astype(jnp.bfloat16))
    ya = jnp.concatenate(parts, axis=-1)
    pa = jnp.dot(ya, wa_ref[...], preferred_element_type=jnp.float32)
    pb = jnp.dot(yb_ref[...], wb_ref[...], preferred_element_type=jnp.float32)
    mixed = ga_ref[...].astype(jnp.float32) * pa + gb_ref[...].astype(jnp.float32) * pb
    out_ref[...] = x_ref[...] + jnp.dot(mixed.astype(jnp.bfloat16), wo_ref[...], preferred_element_type=jnp.float32)


def _merge(outs, lses, yb, proj2, x2, wa, wb, wo, *, tm):
    T, D = x2.shape
    gate0 = QKV_PAD // D
    row = lambda w: pl.BlockSpec((tm, w), lambda i: (i, 0))
    full = lambda a: pl.BlockSpec(a.shape, lambda i: (0, 0))
    return pl.pallas_call(
        _merge_kernel,
        grid=(T // tm,),
        in_specs=[row(A_OUT)] * 3 + [row(HEAD_DIM)] * 3 + [row(B_WIDTH),
                  pl.BlockSpec((tm, D), lambda i: (i, gate0)), pl.BlockSpec((tm, D), lambda i: (i, gate0 + 1)),
                  row(D), full(wa), full(wb), full(wo)],
        out_specs=row(D),
        out_shape=jax.ShapeDtypeStruct((T, D), jnp.float32),
        compiler_params=pltpu.CompilerParams(dimension_semantics=("arbitrary",), vmem_limit_bytes=VMEM_LIMIT),
        name="merge",
    )(*outs, *lses, yb, proj2, proj2, x2, wa, wb, wo)


def _mlp_kernel(x_ref, g_ref, wu_ref, wd_ref, o_ref, h_ref):
    f = pl.program_id(1)

    @pl.when(f == 0)
    def _():
        x = x_ref[...]
        h_ref[...] = _rms_rows(x, g_ref[...]).astype(h_ref.dtype)
        o_ref[...] = x

    u = jnp.dot(h_ref[...], wu_ref[...], preferred_element_type=jnp.float32)
    u = jnp.square(jnp.maximum(u, 0.0)).astype(jnp.bfloat16)
    o_ref[...] += jnp.dot(u, wd_ref[...], preferred_element_type=jnp.float32)


def _mlp(x2, g_mlp, w_up, w_down, *, tm, tf):
    T, D = x2.shape
    F = w_up.shape[1]
    return pl.pallas_call(
        _mlp_kernel,
        grid=(T // tm, F // tf),
        in_specs=[pl.BlockSpec((tm, D), lambda i, f: (i, 0)),
                  pl.BlockSpec((1, D), lambda i, f: (0, 0)),
                  pl.BlockSpec((D, tf), lambda i, f: (0, f)),
                  pl.BlockSpec((tf, D), lambda i, f: (f, 0))],
        out_specs=pl.BlockSpec((tm, D), lambda i, f: (i, 0)),
        out_shape=jax.ShapeDtypeStruct((T, D), jnp.float32),
        scratch_shapes=[pltpu.VMEM((tm, D), jnp.bfloat16)],
        compiler_params=pltpu.CompilerParams(
            dimension_semantics=("arbitrary", "arbitrary"), vmem_limit_bytes=VMEM_LIMIT),
        name="mlp",
    )(x2, g_mlp, w_up, w_down)


def _column_gains(q_norm_a, k_norm_a, q_norm_b, k_norm_b, width):
    qs = SCALE * LOG2E
    gains = jnp.concatenate([
        jnp.tile(q_norm_a * qs, A_HEADS), jnp.tile(k_norm_a, A_HEADS), jnp.ones((A_HEADS * HEAD_DIM,), jnp.float32),
        jnp.tile(q_norm_b * qs, B_HEADS), jnp.tile(k_norm_b, B_HEADS), jnp.ones((B_WIDTH,), jnp.float32),
        jnp.ones((width - QKV_WIDTH,), jnp.float32)])
    return gains.reshape(1, width)


def kernel(x, g_mix, w_in, q_norm_a, k_norm_a, q_norm_b, k_norm_b, rel_bias,
           w_branch_a, w_branch_b, w_out, g_mlp, w_up, w_down):
    batch, seq, d_model = x.shape
    T = batch * seq
    assert w_in.shape == (d_model, QKV_WIDTH + 2 * d_model)
    assert seq % (DIL_PATTERNS[-1][1] * A_BLOCK) == 0 and QKV_PAD % d_model == 0
    bf16 = jnp.bfloat16
    x2 = x.reshape(T, d_model)
    tm = min(1024, T)

    dil_bias, moba_bias = _bias_tables(rel_bias)
    gains = _column_gains(q_norm_a, k_norm_a, q_norm_b, k_norm_b, w_in.shape[1])
    proj = _in_proj(x2, g_mix.reshape(1, -1), w_in, gains, tm=tm, tn=min(512, d_model * 2))

    outs, lses = zip(*[_dilated_group(proj, dil_bias, g, batch=batch, seq=seq) for g in range(N_GROUPS)])
    yb = _moba(proj, moba_bias, batch=batch, seq=seq)

    x_mid = _merge(outs, lses, yb, proj, x2, w_branch_a.astype(bf16), w_branch_b.astype(bf16),
                   w_out.astype(bf16), tm=min(512, T))
    y = _mlp(x_mid, g_mlp.reshape(1, -1), w_up.astype(bf16), w_down.astype(bf16), tm=tm, tf=min(512, w_up.shape[1]))
    return y.reshape(batch, seq, d_model)
```

```python
import functools
import math

import jax
import jax.numpy as jnp
from jax import lax
from jax.experimental import pallas as pl
from jax.experimental.pallas import tpu as pltpu

HEAD_DIM = 128
DIL_PATTERNS = ((128, 1), (512, 4), (2048, 16))
N_GROUPS = len(DIL_PATTERNS)
A_HEADS_PER_GROUP = 4
A_HEADS = A_HEADS_PER_GROUP * N_GROUPS
A_OUT = A_HEADS_PER_GROUP * HEAD_DIM
A_BLOCK = 128
B_HEADS = 8
B_WIDTH = B_HEADS * HEAD_DIM
MOBA_BLOCK = 256
MOBA_TOPK = 3
MOBA_GROUP = 4
MOBA_AUX = 16
N_BUCKETS = 32
MAX_DISTANCE = 2048
EPS = 1e-6
SCALE = HEAD_DIM ** -0.5
A_QKV = 3 * A_HEADS * HEAD_DIM
B_QKV = 3 * B_WIDTH
QKV_WIDTH = A_QKV + B_QKV
QKV_PAD = 8192
LOG2E = math.log2(math.e)
NEG = -1e30
VMEM_LIMIT = 56 * 1024 * 1024
MLP_VMEM_LIMIT = 60 * 1024 * 1024


def _bucket_thresholds():
    max_exact = N_BUCKETS // 2

    def bucket(d):
        if d < max_exact:
            return d
        v = int(math.log(d / max_exact) / math.log(MAX_DISTANCE / max_exact) * (N_BUCKETS - max_exact))
        return min(max_exact + v, N_BUCKETS - 1)

    thr, d = [0], 0
    for b in range(1, N_BUCKETS):
        while bucket(d) < b:
            d += 1
        thr.append(d)
    return tuple(thr)


BUCKET_THRESHOLDS = _bucket_thresholds()
MOBA_BIAS_TILES = -(-(BUCKET_THRESHOLDS[-1] - 1) // MOBA_BLOCK) + 2


def _bias_lookup(dist, tab_ref, col, dmin=0, dmax=None):
    first = sum(dmin >= t for t in BUCKET_THRESHOLDS[1:])
    val = jnp.full(dist.shape, tab_ref[first, col], jnp.float32)
    for b in range(first + 1, N_BUCKETS):
        if dmax is None or BUCKET_THRESHOLDS[b] <= dmax:
            val = jnp.where(dist >= BUCKET_THRESHOLDS[b], tab_ref[b, col], val)
    return val * LOG2E


def _dil_bias_kernel(tab_ref, o_ref):
    g, h = pl.program_id(0), pl.program_id(1)
    a = lax.broadcasted_iota(jnp.int32, (A_BLOCK, 2 * A_BLOCK), 0)
    j = lax.broadcasted_iota(jnp.int32, (A_BLOCK, 2 * A_BLOCK), 1)
    delta = a + A_BLOCK - j
    dilation = lax.shift_left(jnp.int32(1), 2 * g)
    val = _bias_lookup(delta * dilation, tab_ref, g * A_HEADS_PER_GROUP + h)
    o_ref[0, 0] = jnp.where((delta >= 0) & (delta <= A_BLOCK), val, NEG)


def _moba_bias_kernel(tab_ref, o_ref):
    h = pl.program_id(0)
    j = lax.broadcasted_iota(jnp.int32, (MOBA_BLOCK, MOBA_BLOCK), 0)
    s = lax.broadcasted_iota(jnp.int32, (MOBA_BLOCK, MOBA_BLOCK), 1)
    for c in range(MOBA_BIAS_TILES):
        dist = c * MOBA_BLOCK + s - j
        lo, hi = (c - 1) * MOBA_BLOCK + 1, (c + 1) * MOBA_BLOCK - 1
        val = _bias_lookup(dist, tab_ref, A_HEADS + h, max(lo, 0), hi)
        o_ref[0, c] = jnp.where(dist >= 0, val, NEG)


def _bias_tables(rel_bias):
    smem = pl.BlockSpec(memory_space=pltpu.SMEM)
    dil = pl.pallas_call(
        _dil_bias_kernel,
        grid=(N_GROUPS, A_HEADS_PER_GROUP),
        in_specs=[smem],
        out_specs=pl.BlockSpec((1, 1, A_BLOCK, 2 * A_BLOCK), lambda g, h: (g, h, 0, 0)),
        out_shape=jax.ShapeDtypeStruct((N_GROUPS, A_HEADS_PER_GROUP, A_BLOCK, 2 * A_BLOCK), jnp.float32),
        name="dil_bias",
    )(rel_bias)
    moba = pl.pallas_call(
        _moba_bias_kernel,
        grid=(B_HEADS,),
        in_specs=[smem],
        out_specs=pl.BlockSpec((1, MOBA_BIAS_TILES, MOBA_BLOCK, MOBA_BLOCK), lambda h: (h, 0, 0, 0)),
        out_shape=jax.ShapeDtypeStruct((B_HEADS, MOBA_BIAS_TILES, MOBA_BLOCK, MOBA_BLOCK), jnp.float32),
        name="moba_bias",
    )(rel_bias)
    return dil, moba


IN_PROJ_ROW_CHUNKS = 4
IN_PROJ_K_CHUNKS = 4


def _rms_rows(x, g):
    return (x * lax.rsqrt(jnp.mean(x * x, axis=-1, keepdims=True) + EPS)) * g


def _in_proj_kernel(x_ref, g_ref, w_ref, gain_ref, o_ref, h_ref, wbf_ref, *, tn):
    j = pl.program_id(1)

    @pl.when(j == 0)
    def _():
        h_ref[...] = _rms_rows(x_ref[...], g_ref[...]).astype(h_ref.dtype)

    col0 = j * tn
    is_norm = (col0 < 2 * A_HEADS * HEAD_DIM) | ((col0 >= A_QKV) & (col0 < A_QKV + 2 * B_WIDTH))
    is_pad = (col0 >= QKV_WIDTH) & (col0 < QKV_PAD)
    is_gate = col0 >= QKV_PAD

    def head_norm(acc):
        return jnp.concatenate(
            [_rms_rows(acc[:, c * HEAD_DIM:(c + 1) * HEAD_DIM], gain_ref[:, c * HEAD_DIM:(c + 1) * HEAD_DIM])
             for c in range(tn // HEAD_DIM)], axis=-1)

    def project(epilogue):
        tm, d = h_ref.shape
        rc = tm // IN_PROJ_ROW_CHUNKS
        kc = d // IN_PROJ_K_CHUNKS
        for c in range(IN_PROJ_ROW_CHUNKS):
            rows = slice(c * rc, (c + 1) * rc)
            acc = None
            for k in range(IN_PROJ_K_CHUNKS):
                ks = slice(k * kc, (k + 1) * kc)
                if c == 0:
                    wbf_ref[ks, :] = w_ref[ks, :].astype(wbf_ref.dtype)
                part = jnp.dot(h_ref[rows, ks], wbf_ref[ks, :], preferred_element_type=jnp.float32)
                acc = part if acc is None else acc + part
            o_ref[rows, :] = epilogue(acc).astype(o_ref.dtype)

    @pl.when(is_pad)
    def _():
        o_ref[...] = jnp.zeros(o_ref.shape, o_ref.dtype)

    @pl.when(is_norm)
    def _():
        project(head_norm)

    @pl.when(is_gate)
    def _():
        project(lambda acc: 0.5 * jnp.tanh(0.5 * acc) + 0.5)

    @pl.when(jnp.logical_not(is_pad | is_norm | is_gate))
    def _():
        project(lambda acc: acc)


def _in_proj(x2, g_mix, w_in, gains, *, tm, tn):
    T, D = x2.shape
    n_qkv_tiles = QKV_WIDTH // tn
    n_pad_tiles = (QKV_PAD - QKV_WIDTH) // tn
    out_width = QKV_PAD + 2 * D
    w_tile = lambda j: jnp.where(j < n_qkv_tiles, j, jnp.maximum(j - n_pad_tiles, n_qkv_tiles - 1))
    return pl.pallas_call(
        functools.partial(_in_proj_kernel, tn=tn),
        grid=(T // tm, out_width // tn),
        in_specs=[
            pl.BlockSpec((tm, D), lambda i, j: (i, 0)),
            pl.BlockSpec((1, D), lambda i, j: (0, 0)),
            pl.BlockSpec((D, tn), lambda i, j: (0, w_tile(j))),
            pl.BlockSpec((1, tn), lambda i, j: (0, w_tile(j))),
        ],
        out_specs=pl.BlockSpec((tm, tn), lambda i, j: (i, j)),
        out_shape=jax.ShapeDtypeStruct((T, out_width), jnp.bfloat16),
        scratch_shapes=[pltpu.VMEM((tm, D), jnp.bfloat16), pltpu.VMEM((D, tn), jnp.bfloat16)],
        compiler_params=pltpu.CompilerParams(
            dimension_semantics=("arbitrary", "arbitrary"), vmem_limit_bytes=VMEM_LIMIT),
        name="in_proj",
    )(x2, g_mix, w_in, gains)


CHUNK = DIL_PATTERNS[-1][1] * A_BLOCK
BLOCKS_PER_CHUNK = CHUNK // A_BLOCK


def _deinterleave(dst_ref, first_block, src_ref, stage_ref, r):
    bpc = BLOCKS_PER_CHUNK // r
    if r == 1:
        dst_ref[0, first_block:first_block + bpc] = src_ref[...].reshape(bpc, A_BLOCK, A_OUT)
        return
    for h in range(A_HEADS_PER_GROUP):
        cols = slice(h * HEAD_DIM, (h + 1) * HEAD_DIM)
        stage_ref[h] = src_ref[:, cols].astype(jnp.float32)
        for rr in range(r):
            rows = stage_ref[h, pl.ds(rr, CHUNK // r, stride=r), :].astype(dst_ref.dtype)
            dst_ref[rr, first_block:first_block + bpc, :, cols] = rows.reshape(bpc, A_BLOCK, HEAD_DIM)


def _dilated_kernel(q_ref, k_ref, v_ref, bias_ref, o_ref, lse_ref, stage_ref, qd_ref, kd_ref, vd_ref, *, r):
    c = pl.program_id(1)
    bpc = BLOCKS_PER_CHUNK // r
    nblk = BLOCKS_PER_CHUNK

    @pl.when(c == 0)
    def _():
        kd_ref[:, 0] = jnp.zeros((r, A_BLOCK, A_OUT), kd_ref.dtype)
        vd_ref[:, 0] = jnp.zeros((r, A_BLOCK, A_OUT), vd_ref.dtype)

    @pl.when(c > 0)
    def _():
        kd_ref[:, 0] = kd_ref[:, bpc]
        vd_ref[:, 0] = vd_ref[:, bpc]

    _deinterleave(qd_ref, 0, q_ref, stage_ref, r)
    _deinterleave(kd_ref, 1, k_ref, stage_ref, r)
    _deinterleave(vd_ref, 1, v_ref, stage_ref, r)

    blk = lax.broadcasted_iota(jnp.int32, (nblk, A_BLOCK, A_BLOCK), 0)
    no_prev = (lax.rem(blk, bpc) == 0) & (c == 0)
    lane = lax.broadcasted_iota(jnp.int32, (nblk, A_BLOCK, HEAD_DIM), 2)
    bqk = (((2,), (2,)), ((0,), (0,)))
    bqd = (((2,), (1,)), ((0,), (0,)))
    lse = jnp.zeros((nblk, A_BLOCK, HEAD_DIM), jnp.float32)
    for h in range(A_HEADS_PER_GROUP):
        cols = slice(h * HEAD_DIM, (h + 1) * HEAD_DIM)
        q = qd_ref[:, :, :, cols].reshape(nblk, A_BLOCK, HEAD_DIM)
        k_prev = kd_ref[:, 0:bpc, :, cols].reshape(nblk, A_BLOCK, HEAD_DIM)
        k_cur = kd_ref[:, 1:bpc + 1, :, cols].reshape(nblk, A_BLOCK, HEAD_DIM)
        v_prev = vd_ref[:, 0:bpc, :, cols].reshape(nblk, A_BLOCK, HEAD_DIM)
        v_cur = vd_ref[:, 1:bpc + 1, :, cols].reshape(nblk, A_BLOCK, HEAD_DIM)
        s_prev = lax.dot_general(q, k_prev, bqk, preferred_element_type=jnp.float32) + bias_ref[0, h, :, :A_BLOCK]
        s_prev = jnp.where(no_prev, NEG, s_prev)
        s_cur = lax.dot_general(q, k_cur, bqk, preferred_element_type=jnp.float32) + bias_ref[0, h, :, A_BLOCK:]
        m = jnp.max(jnp.maximum(s_prev, s_cur), axis=-1, keepdims=True)
        p_prev = jnp.exp2(s_prev - m)
        p_cur = jnp.exp2(s_cur - m)
        den = jnp.sum(p_prev + p_cur, axis=-1, keepdims=True)
        o = (lax.dot_general(p_prev.astype(jnp.bfloat16), v_prev, bqd, preferred_element_type=jnp.float32)
             + lax.dot_general(p_cur.astype(jnp.bfloat16), v_cur, bqd, preferred_element_type=jnp.float32)) / den
        lse = jnp.where(lane == h, m + jnp.log2(den), lse)
        for u in range(nblk):
            rr, nl = divmod(u, bpc)
            tokens = pl.ds(rr + nl * (r * A_BLOCK), A_BLOCK, stride=r) if r > 1 else pl.ds(u * A_BLOCK, A_BLOCK)
            stage_ref[h, tokens, :] = o[u]
        o_ref[:, cols] = stage_ref[h].astype(o_ref.dtype)
    for u in range(nblk):
        rr, nl = divmod(u, bpc)
        tokens = pl.ds(rr + nl * (r * A_BLOCK), A_BLOCK, stride=r) if r > 1 else pl.ds(u * A_BLOCK, A_BLOCK)
        lse_ref[tokens, :] = lse[u]


def _dilated_group(proj2, bias, group, *, batch, seq):
    _, r = DIL_PATTERNS[group]
    nc = seq // CHUNK
    bpc = BLOCKS_PER_CHUNK // r
    chunk = lambda sec: pl.BlockSpec((CHUNK, A_OUT), lambda b, c: (b * nc + c, sec * N_GROUPS + group))
    return pl.pallas_call(
        functools.partial(_dilated_kernel, r=r),
        grid=(batch, nc),
        in_specs=[chunk(0), chunk(1), chunk(2),
                  pl.BlockSpec((1, A_HEADS_PER_GROUP, A_BLOCK, 2 * A_BLOCK), lambda b, c: (group, 0, 0, 0))],
        out_specs=[pl.BlockSpec((CHUNK, A_OUT), lambda b, c: (b * nc + c, 0)),
                   pl.BlockSpec((CHUNK, HEAD_DIM), lambda b, c: (b * nc + c, 0))],
        out_shape=[jax.ShapeDtypeStruct((batch * seq, A_OUT), jnp.bfloat16),
                   jax.ShapeDtypeStruct((batch * seq, HEAD_DIM), jnp.float32)],
        scratch_shapes=[pltpu.VMEM((A_HEADS_PER_GROUP, CHUNK, HEAD_DIM), jnp.float32),
                        pltpu.VMEM((r, bpc, A_BLOCK, A_OUT), jnp.bfloat16),
                        pltpu.VMEM((r, bpc + 1, A_BLOCK, A_OUT), jnp.bfloat16),
                        pltpu.VMEM((r, bpc + 1, A_BLOCK, A_OUT), jnp.bfloat16)],
        compiler_params=pltpu.CompilerParams(
            dimension_semantics=("arbitrary", "arbitrary"), vmem_limit_bytes=VMEM_LIMIT),
        name=f"dilated_{group}",
    )(proj2, proj2, proj2, bias)


def _split3(x):
    hi = x.astype(jnp.bfloat16)
    r1 = x - hi.astype(jnp.float32)
    mid = r1.astype(jnp.bfloat16)
    lo = (r1 - mid.astype(jnp.float32)).astype(jnp.bfloat16)
    return hi, mid, lo


MOBA_STREAMS = 2


def _moba_items(nb, group):
    streams = [[] for _ in range(MOBA_STREAMS)]
    for i in sorted(range(nb), key=lambda i: -(i // group)):
        min(streams, key=len).extend((i, g) for g in range(i // group + 1))
    assert len({len(s) for s in streams}) == 1 and len(streams[0]) % 2 == 0
    return streams


def _moba_kernel(items_ref, q_ref, k_ref, v_ref, bias_ref, o_ref, kmean_ref, ka_ref, qa_ref, va_ref, s_ref,
                 st_ref, *, nb, group, n_items):
    BS = MOBA_BLOCK
    hd = HEAD_DIM
    seq = nb * BS
    rows_per_group = group * BS
    nt = (((1,), (1,)), ((), ()))

    row = lax.broadcasted_iota(jnp.int32, (nb, seq), 0)
    col = lax.broadcasted_iota(jnp.int32, (nb, seq), 1)
    lo_edge = row * BS
    avg = jnp.where((col >= lo_edge) & (col < lo_edge + BS), 1.0 / BS, 0.0).astype(jnp.bfloat16)
    kmean = jnp.dot(avg, k_ref[...], preferred_element_type=jnp.float32)
    for t, part in enumerate(_split3(kmean)):
        kmean_ref[t] = part

    q_all = q_ref[...]
    gate = sum(lax.dot_general(kmean_ref[t], q_all, nt, preferred_element_type=jnp.float32) for t in range(3))
    own = col // BS
    past = row < own
    gate = jnp.where(past, gate, -jnp.inf)
    rank = jnp.zeros((nb, seq), jnp.int32)
    for m in range(nb):
        gm = gate[m:m + 1, :]
        beats = (gm > gate) | ((gm == gate) & (row > m))
        rank = rank + beats.astype(jnp.int32)
    sel = jnp.where((past & (rank < MOBA_TOPK)) | (row == own), 0.0, NEG)

    qa_ref[:hd, :] = q_all.T
    qa_ref[hd:hd + nb, :] = sel.astype(qa_ref.dtype)
    qa_ref[hd + nb:, :] = jnp.zeros((hd - nb, seq), qa_ref.dtype)
    ka_ref[:, :hd] = k_ref[...]
    key_blk = lax.broadcasted_iota(jnp.int32, (seq, hd), 0) // BS
    ka_ref[:, hd:] = jnp.where(lax.broadcasted_iota(jnp.int32, (seq, hd), 1) == key_blk, 1.0, 0.0).astype(ka_ref.dtype)
    va_ref[:hd, :] = v_ref[...].T
    va_ref[hd:, :] = jnp.where(lax.broadcasted_iota(jnp.int32, (MOBA_AUX, seq), 0) == 0, 1.0, 0.0).astype(va_ref.dtype)

    def logits(st, w, slot):
        i, g = items_ref[st, w, 0], items_ref[st, w, 1]
        rows = pl.ds(pl.multiple_of(g * rows_per_group, rows_per_group), rows_per_group)
        qcols = pl.ds(pl.multiple_of(i * BS, BS), BS)
        s_t = jnp.dot(ka_ref[rows, :], qa_ref[:, qcols], preferred_element_type=jnp.float32)
        m8 = jnp.full((8, BS), -jnp.inf, jnp.float32)
        for u in range(group):
            tile = jnp.clip(i - (g * group + u), 0, MOBA_BIAS_TILES - 1)
            s_u = s_t[u * BS:(u + 1) * BS] + bias_ref[0, tile]
            s_ref[2 * st + slot, u * BS:(u + 1) * BS, :] = s_u
            m8 = jnp.maximum(m8, jnp.max(s_u.reshape(BS // 8, 8, BS), axis=0))
        return m8

    def accumulate(st, w, slot, m8, state):
        m_run, acc = state
        i, g = items_ref[st, w, 0], items_ref[st, w, 1]
        rows = pl.ds(pl.multiple_of(g * rows_per_group, rows_per_group), rows_per_group)
        m_prev = jnp.where(g == 0, -jnp.inf, m_run)
        m_new = jnp.maximum(m_prev, jnp.max(m8, axis=0, keepdims=True))
        alpha = jnp.exp2(m_prev - m_new)
        p = jnp.exp2(s_ref[2 * st + slot] - m_new).astype(jnp.bfloat16)
        acc = alpha * acc + jnp.dot(va_ref[:, rows], p, preferred_element_type=jnp.float32)
        st_ref[i] = acc
        return m_new, acc

    streams = range(MOBA_STREAMS)

    def pair(t, carry):
        m8a, state = carry
        w = 2 * t
        m8b = [logits(st, w + 1, 1) for st in streams]
        state = [accumulate(st, w, 0, m8a[st], state[st]) for st in streams]
        m8a = [logits(st, jnp.minimum(w + 2, n_items - 1), 0) for st in streams]
        state = [accumulate(st, w + 1, 1, m8b[st], state[st]) for st in streams]
        return m8a, state

    state0 = (jnp.full((1, BS), -jnp.inf, jnp.float32), jnp.zeros((hd + MOBA_AUX, BS), jnp.float32))
    lax.fori_loop(0, n_items // 2, pair, ([logits(st, 0, 0) for st in streams], [state0 for _ in streams]))

    for i in range(nb):
        acc = st_ref[i]
        o_ref[i * BS:(i + 1) * BS, :] = (acc[:hd] / acc[hd:hd + 1]).T.astype(o_ref.dtype)


def _moba(proj2, bias, *, batch, seq):
    BS = MOBA_BLOCK
    nb = seq // BS
    assert nb <= HEAD_DIM
    c0 = A_QKV // HEAD_DIM
    group = math.gcd(nb, MOBA_GROUP)
    items = _moba_items(nb, group)
    head_spec = lambda sec: pl.BlockSpec((seq, HEAD_DIM), lambda b, h, items: (b, c0 + sec * B_HEADS + h))
    return pl.pallas_call(
        functools.partial(_moba_kernel, nb=nb, group=group, n_items=len(items[0])),
        grid_spec=pltpu.PrefetchScalarGridSpec(
            num_scalar_prefetch=1,
            grid=(batch, B_HEADS),
            in_specs=[head_spec(0), head_spec(1), head_spec(2),
                      pl.BlockSpec((1, MOBA_BIAS_TILES, BS, BS), lambda b, h, items: (h, 0, 0, 0))],
            out_specs=pl.BlockSpec((seq, HEAD_DIM), lambda b, h, items: (b, h)),
            scratch_shapes=[pltpu.VMEM((3, nb, HEAD_DIM), jnp.bfloat16),
                            pltpu.VMEM((seq, 2 * HEAD_DIM), jnp.bfloat16),
                            pltpu.VMEM((2 * HEAD_DIM, seq), jnp.bfloat16),
                            pltpu.VMEM((HEAD_DIM + MOBA_AUX, seq), jnp.bfloat16),
                            pltpu.VMEM((2 * MOBA_STREAMS, group * BS, BS), jnp.float32),
                            pltpu.VMEM((nb, HEAD_DIM + MOBA_AUX, BS), jnp.float32)]),
        out_shape=jax.ShapeDtypeStruct((batch * seq, B_WIDTH), jnp.bfloat16),
        compiler_params=pltpu.CompilerParams(
            dimension_semantics=("arbitrary", "arbitrary"), vmem_limit_bytes=VMEM_LIMIT),
        name="moba",
    )(jnp.asarray(items, jnp.int32), proj2, proj2, proj2, bias)


def _merge_kernel(o0_ref, o1_ref, o2_ref, l0_ref, l1_ref, l2_ref, yb_ref, ga_ref, gb_ref, x_ref,
                  wa_ref, wb_ref, wo_ref, out_ref):
    l0, l1, l2 = l0_ref[...], l1_ref[...], l2_ref[...]
    mx = jnp.maximum(jnp.maximum(l0, l1), l2)
    e0, e1, e2 = jnp.exp2(l0 - mx), jnp.exp2(l1 - mx), jnp.exp2(l2 - mx)
    inv = 1.0 / (e0 + e1 + e2)
    parts = []
    for h in range(A_HEADS_PER_GROUP):
        cols = slice(h * HEAD_DIM, (h + 1) * HEAD_DIM)
        ya = ((e0 * inv)[:, h:h + 1] * o0_ref[:, cols].astype(jnp.float32)
              + (e1 * inv)[:, h:h + 1] * o1_ref[:, cols].astype(jnp.float32)
              + (e2 * inv)[:, h:h + 1] * o2_ref[:, cols].astype(jnp.float32))
        parts.append(ya.astype(jnp.bfloat16))
    ya = jnp.concatenate(parts, axis=-1)
    pa = jnp.dot(ya, wa_ref[...], preferred_element_type=jnp.float32)
    pb = jnp.dot(yb_ref[...], wb_ref[...], preferred_element_type=jnp.float32)
    mixed = ga_ref[...].astype(jnp.float32) * pa + gb_ref[...].astype(jnp.float32) * pb
    out_ref[...] = x_ref[...] + jnp.dot(mixed.astype(jnp.bfloat16), wo_ref[...], preferred_element_type=jnp.float32)


def _merge(outs, lses, yb, proj2, x2, wa, wb, wo, *, tm):
    T, D = x2.shape
    gate0 = QKV_PAD // D
    row = lambda w: pl.BlockSpec((tm, w), lambda i: (i, 0))
    full = lambda a: pl.BlockSpec(a.shape, lambda i: (0, 0))
    return pl.pallas_call(
        _merge_kernel,
        grid=(T // tm,),
        in_specs=[row(A_OUT)] * 3 + [row(HEAD_DIM)] * 3 + [row(B_WIDTH),
                  pl.BlockSpec((tm, D), lambda i: (i, gate0)), pl.BlockSpec((tm, D), lambda i: (i, gate0 + 1)),
                  row(D), full(wa), full(wb), full(wo)],
        out_specs=row(D),
        out_shape=jax.ShapeDtypeStruct((T, D), jnp.float32),
        compiler_params=pltpu.CompilerParams(dimension_semantics=("arbitrary",), vmem_limit_bytes=VMEM_LIMIT),
        name="merge",
    )(*outs, *lses, yb, proj2, proj2, x2, wa, wb, wo)


MLP_ROW_CHUNKS = 2
MLP_K_CHUNKS = 4


def _mlp_kernel(x_ref, g_ref, wu_ref, wd_ref, o_ref, h_ref, wubf_ref, wdbf_ref):
    f = pl.program_id(1)

    @pl.when(f == 0)
    def _():
        x = x_ref[...]
        h_ref[...] = _rms_rows(x, g_ref[...]).astype(h_ref.dtype)
        o_ref[...] = x

    tm, d = h_ref.shape
    rc = tm // MLP_ROW_CHUNKS
    kc = d // MLP_K_CHUNKS
    for c in range(MLP_ROW_CHUNKS):
        rows = slice(c * rc, (c + 1) * rc)
        u = None
        for k in range(MLP_K_CHUNKS):
            ks = slice(k * kc, (k + 1) * kc)
            if c == 0:
                wubf_ref[ks, :] = wu_ref[ks, :].astype(wubf_ref.dtype)
                wdbf_ref[:, ks] = wd_ref[:, ks].astype(wdbf_ref.dtype)
            part = jnp.dot(h_ref[rows, ks], wubf_ref[ks, :], preferred_element_type=jnp.float32)
            u = part if u is None else u + part
        u = jnp.square(jnp.maximum(u, 0.0)).astype(jnp.bfloat16)
        o_ref[rows, :] += jnp.dot(u, wdbf_ref[...], preferred_element_type=jnp.float32)


def _mlp(x2, g_mlp, w_up, w_down, *, tm, tf):
    T, D = x2.shape
    F = w_up.shape[1]
    return pl.pallas_call(
        _mlp_kernel,
        grid=(T // tm, F // tf),
        in_specs=[pl.BlockSpec((tm, D), lambda i, f: (i, 0), pipeline_mode=pl.Buffered(1)),
                  pl.BlockSpec((1, D), lambda i, f: (0, 0)),
                  pl.BlockSpec((D, tf), lambda i, f: (0, f)),
                  pl.BlockSpec((tf, D), lambda i, f: (f, 0))],
        out_specs=pl.BlockSpec((tm, D), lambda i, f: (i, 0)),
        out_shape=jax.ShapeDtypeStruct((T, D), jnp.float32),
        scratch_shapes=[pltpu.VMEM((tm, D), jnp.bfloat16),
                        pltpu.VMEM((D, tf), jnp.bfloat16), pltpu.VMEM((tf, D), jnp.bfloat16)],
        compiler_params=pltpu.CompilerParams(
            dimension_semantics=("arbitrary", "arbitrary"), vmem_limit_bytes=MLP_VMEM_LIMIT),
        name="mlp",
    )(x2, g_mlp, w_up, w_down)


def _column_gains(q_norm_a, k_norm_a, q_norm_b, k_norm_b, width):
    qs = SCALE * LOG2E
    gains = jnp.concatenate([
        jnp.tile(q_norm_a * qs, A_HEADS), jnp.tile(k_norm_a, A_HEADS), jnp.ones((A_HEADS * HEAD_DIM,), jnp.float32),
        jnp.tile(q_norm_b * qs, B_HEADS), jnp.tile(k_norm_b, B_HEADS), jnp.ones((B_WIDTH,), jnp.float32),
        jnp.ones((width - QKV_WIDTH,), jnp.float32)])
    return gains.reshape(1, width)


def kernel(x, g_mix, w_in, q_norm_a, k_norm_a, q_norm_b, k_norm_b, rel_bias,
           w_branch_a, w_branch_b, w_out, g_mlp, w_up, w_down):
    batch, seq, d_model = x.shape
    T = batch * seq
    assert w_in.shape == (d_model, QKV_WIDTH + 2 * d_model)
    assert seq % (DIL_PATTERNS[-1][1] * A_BLOCK) == 0 and QKV_PAD % d_model == 0
    bf16 = jnp.bfloat16
    x2 = x.reshape(T, d_model)
    tm = min(1024, T)

    dil_bias, moba_bias = _bias_tables(rel_bias)
    gains = _column_gains(q_norm_a, k_norm_a, q_norm_b, k_norm_b, w_in.shape[1])
    proj = _in_proj(x2, g_mix.reshape(1, -1), w_in, gains, tm=tm, tn=min(512, d_model * 2))

    outs, lses = zip(*[_dilated_group(proj, dil_bias, g, batch=batch, seq=seq) for g in range(N_GROUPS)])
    yb = _moba(proj, moba_bias, batch=batch, seq=seq)

    x_mid = _merge(outs, lses, yb, proj, x2, w_branch_a.astype(bf16), w_branch_b.astype(bf16),
                   w_out.astype(bf16), tm=min(512, T))
    y = _mlp(x_mid, g_mlp.reshape(1, -1), w_up, w_down, tm=tm, tf=min(512, w_up.shape[1]))
    return y.reshape(batch, seq, d_model)
```

```python
import functools
import math

import jax
import jax.numpy as jnp
from jax import lax
from jax.experimental import pallas as pl
from jax.experimental.pallas import tpu as pltpu

HEAD_DIM = 128
DIL_PATTERNS = ((128, 1), (512, 4), (2048, 16))
N_GROUPS = len(DIL_PATTERNS)
A_HEADS_PER_GROUP = 4
A_HEADS = A_HEADS_PER_GROUP * N_GROUPS
A_OUT = A_HEADS_PER_GROUP * HEAD_DIM
A_BLOCK = 128
B_HEADS = 8
B_WIDTH = B_HEADS * HEAD_DIM
MOBA_BLOCK = 256
MOBA_TOPK = 3
MOBA_GROUP = 4
MOBA_AUX = 16
N_BUCKETS = 32
MAX_DISTANCE = 2048
EPS = 1e-6
SCALE = HEAD_DIM ** -0.5
A_QKV = 3 * A_HEADS * HEAD_DIM
B_QKV = 3 * B_WIDTH
QKV_WIDTH = A_QKV + B_QKV
QKV_PAD = 8192
LOG2E = math.log2(math.e)
NEG = -1e30
VMEM_LIMIT = 56 * 1024 * 1024
MLP_VMEM_LIMIT = 60 * 1024 * 1024


def _bucket_thresholds():
    max_exact = N_BUCKETS // 2

    def bucket(d):
        if d < max_exact:
            return d
        v = int(math.log(d / max_exact) / math.log(MAX_DISTANCE / max_exact) * (N_BUCKETS - max_exact))
        return min(max_exact + v, N_BUCKETS - 1)

    thr, d = [0], 0
    for b in range(1, N_BUCKETS):
        while bucket(d) < b:
            d += 1
        thr.append(d)
    return tuple(thr)


BUCKET_THRESHOLDS = _bucket_thresholds()
MOBA_BIAS_TILES = -(-(BUCKET_THRESHOLDS[-1] - 1) // MOBA_BLOCK) + 2


def _bias_lookup(dist, tab_ref, col, dmin=0, dmax=None):
    first = sum(dmin >= t for t in BUCKET_THRESHOLDS[1:])
    val = jnp.full(dist.shape, tab_ref[first, col], jnp.float32)
    for b in range(first + 1, N_BUCKETS):
        if dmax is None or BUCKET_THRESHOLDS[b] <= dmax:
            val = jnp.where(dist >= BUCKET_THRESHOLDS[b], tab_ref[b, col], val)
    return val * LOG2E


def _dil_bias_kernel(tab_ref, o_ref):
    g, h = pl.program_id(0), pl.program_id(1)
    a = lax.broadcasted_iota(jnp.int32, (A_BLOCK, 2 * A_BLOCK), 0)
    j = lax.broadcasted_iota(jnp.int32, (A_BLOCK, 2 * A_BLOCK), 1)
    delta = a + A_BLOCK - j
    dilation = lax.shift_left(jnp.int32(1), 2 * g)
    val = _bias_lookup(delta * dilation, tab_ref, g * A_HEADS_PER_GROUP + h)
    o_ref[0, 0] = jnp.where((delta >= 0) & (delta <= A_BLOCK), val, NEG)


def _moba_bias_kernel(tab_ref, o_ref):
    h = pl.program_id(0)
    j = lax.broadcasted_iota(jnp.int32, (MOBA_BLOCK, MOBA_BLOCK), 0)
    s = lax.broadcasted_iota(jnp.int32, (MOBA_BLOCK, MOBA_BLOCK), 1)
    for c in range(MOBA_BIAS_TILES):
        dist = c * MOBA_BLOCK + s - j
        lo, hi = (c - 1) * MOBA_BLOCK + 1, (c + 1) * MOBA_BLOCK - 1
        val = _bias_lookup(dist, tab_ref, A_HEADS + h, max(lo, 0), hi)
        o_ref[0, c] = jnp.where(dist >= 0, val, NEG)


def _bias_tables(rel_bias):
    smem = pl.BlockSpec(memory_space=pltpu.SMEM)
    dil = pl.pallas_call(
        _dil_bias_kernel,
        grid=(N_GROUPS, A_HEADS_PER_GROUP),
        in_specs=[smem],
        out_specs=pl.BlockSpec((1, 1, A_BLOCK, 2 * A_BLOCK), lambda g, h: (g, h, 0, 0)),
        out_shape=jax.ShapeDtypeStruct((N_GROUPS, A_HEADS_PER_GROUP, A_BLOCK, 2 * A_BLOCK), jnp.float32),
        name="dil_bias",
    )(rel_bias)
    moba = pl.pallas_call(
        _moba_bias_kernel,
        grid=(B_HEADS,),
        in_specs=[smem],
        out_specs=pl.BlockSpec((1, MOBA_BIAS_TILES, MOBA_BLOCK, MOBA_BLOCK), lambda h: (h, 0, 0, 0)),
        out_shape=jax.ShapeDtypeStruct((B_HEADS, MOBA_BIAS_TILES, MOBA_BLOCK, MOBA_BLOCK), jnp.float32),
        name="moba_bias",
    )(rel_bias)
    return dil, moba


IN_PROJ_ROW_CHUNKS = 4
IN_PROJ_K_CHUNKS = 4


def _rms_rows(x, g):
    return (x * lax.rsqrt(jnp.mean(x * x, axis=-1, keepdims=True) + EPS)) * g


def _in_proj_kernel(x_ref, g_ref, w_ref, gain_ref, o_ref, h_ref, wbf_ref, *, tn):
    j = pl.program_id(1)

    @pl.when(j == 0)
    def _():
        h_ref[...] = _rms_rows(x_ref[...], g_ref[...]).astype(h_ref.dtype)

    col0 = j * tn
    is_norm = (col0 < 2 * A_HEADS * HEAD_DIM) | ((col0 >= A_QKV) & (col0 < A_QKV + 2 * B_WIDTH))
    is_pad = (col0 >= QKV_WIDTH) & (col0 < QKV_PAD)
    is_gate = col0 >= QKV_PAD

    def head_norm(acc):
        return jnp.concatenate(
            [_rms_rows(acc[:, c * HEAD_DIM:(c + 1) * HEAD_DIM], gain_ref[:, c * HEAD_DIM:(c + 1) * HEAD_DIM])
             for c in range(tn // HEAD_DIM)], axis=-1)

    def project(epilogue):
        tm, d = h_ref.shape
        rc = tm // IN_PROJ_ROW_CHUNKS
        kc = d // IN_PROJ_K_CHUNKS
        for c in range(IN_PROJ_ROW_CHUNKS):
            rows = slice(c * rc, (c + 1) * rc)
            acc = None
            for k in range(IN_PROJ_K_CHUNKS):
                ks = slice(k * kc, (k + 1) * kc)
                if c == 0:
                    wbf_ref[ks, :] = w_ref[ks, :].astype(wbf_ref.dtype)
                part = jnp.dot(h_ref[rows, ks], wbf_ref[ks, :], preferred_element_type=jnp.float32)
                acc = part if acc is None else acc + part
            o_ref[rows, :] = epilogue(acc).astype(o_ref.dtype)

    @pl.when(is_pad)
    def _():
        o_ref[...] = jnp.zeros(o_ref.shape, o_ref.dtype)

    @pl.when(is_norm)
    def _():
        project(head_norm)

    @pl.when(is_gate)
    def _():
        project(lambda acc: 0.5 * jnp.tanh(0.5 * acc) + 0.5)

    @pl.when(jnp.logical_not(is_pad | is_norm | is_gate))
    def _():
        project(lambda acc: acc)


def _in_proj(x2, g_mix, w_in, gains, *, tm, tn):
    T, D = x2.shape
    n_qkv_tiles = QKV_WIDTH // tn
    n_pad_tiles = (QKV_PAD - QKV_WIDTH) // tn
    out_width = QKV_PAD + 2 * D
    w_tile = lambda j: jnp.where(j < n_qkv_tiles, j, jnp.maximum(j - n_pad_tiles, n_qkv_tiles - 1))
    return pl.pallas_call(
        functools.partial(_in_proj_kernel, tn=tn),
        grid=(T // tm, out_width // tn),
        in_specs=[
            pl.BlockSpec((tm, D), lambda i, j: (i, 0)),
            pl.BlockSpec((1, D), lambda i, j: (0, 0)),
            pl.BlockSpec((D, tn), lambda i, j: (0, w_tile(j))),
            pl.BlockSpec((1, tn), lambda i, j: (0, w_tile(j))),
        ],
        out_specs=pl.BlockSpec((tm, tn), lambda i, j: (i, j)),
        out_shape=jax.ShapeDtypeStruct((T, out_width), jnp.bfloat16),
        scratch_shapes=[pltpu.VMEM((tm, D), jnp.bfloat16), pltpu.VMEM((D, tn), jnp.bfloat16)],
        compiler_params=pltpu.CompilerParams(
            dimension_semantics=("arbitrary", "arbitrary"), vmem_limit_bytes=VMEM_LIMIT),
        name="in_proj",
    )(x2, g_mix, w_in, gains)


CHUNK = DIL_PATTERNS[-1][1] * A_BLOCK
BLOCKS_PER_CHUNK = CHUNK // A_BLOCK


def _deinterleave(dst_ref, first_block, src_ref, stage_ref, r):
    bpc = BLOCKS_PER_CHUNK // r
    if r == 1:
        dst_ref[0, first_block:first_block + bpc] = src_ref[...].reshape(bpc, A_BLOCK, A_OUT)
        return
    for h in range(A_HEADS_PER_GROUP):
        cols = slice(h * HEAD_DIM, (h + 1) * HEAD_DIM)
        stage_ref[h] = src_ref[:, cols].astype(jnp.float32)
        for rr in range(r):
            rows = stage_ref[h, pl.ds(rr, CHUNK // r, stride=r), :].astype(dst_ref.dtype)
            dst_ref[rr, first_block:first_block + bpc, :, cols] = rows.reshape(bpc, A_BLOCK, HEAD_DIM)


def _dilated_kernel(q_ref, k_ref, v_ref, bias_ref, o_ref, lse_ref, stage_ref, qd_ref, kd_ref, vd_ref, *, r):
    c = pl.program_id(1)
    bpc = BLOCKS_PER_CHUNK // r
    nblk = BLOCKS_PER_CHUNK

    @pl.when(c == 0)
    def _():
        kd_ref[:, 0] = jnp.zeros((r, A_BLOCK, A_OUT), kd_ref.dtype)
        vd_ref[:, 0] = jnp.zeros((r, A_BLOCK, A_OUT), vd_ref.dtype)

    @pl.when(c > 0)
    def _():
        kd_ref[:, 0] = kd_ref[:, bpc]
        vd_ref[:, 0] = vd_ref[:, bpc]

    _deinterleave(qd_ref, 0, q_ref, stage_ref, r)
    _deinterleave(kd_ref, 1, k_ref, stage_ref, r)
    _deinterleave(vd_ref, 1, v_ref, stage_ref, r)

    blk = lax.broadcasted_iota(jnp.int32, (nblk, A_BLOCK, A_BLOCK), 0)
    no_prev = (lax.rem(blk, bpc) == 0) & (c == 0)
    lane = lax.broadcasted_iota(jnp.int32, (nblk, A_BLOCK, HEAD_DIM), 2)
    bqk = (((2,), (2,)), ((0,), (0,)))
    bqd = (((2,), (1,)), ((0,), (0,)))
    lse = jnp.zeros((nblk, A_BLOCK, HEAD_DIM), jnp.float32)
    for h in range(A_HEADS_PER_GROUP):
        cols = slice(h * HEAD_DIM, (h + 1) * HEAD_DIM)
        q = qd_ref[:, :, :, cols].reshape(nblk, A_BLOCK, HEAD_DIM)
        k_prev = kd_ref[:, 0:bpc, :, cols].reshape(nblk, A_BLOCK, HEAD_DIM)
        k_cur = kd_ref[:, 1:bpc + 1, :, cols].reshape(nblk, A_BLOCK, HEAD_DIM)
        v_prev = vd_ref[:, 0:bpc, :, cols].reshape(nblk, A_BLOCK, HEAD_DIM)
        v_cur = vd_ref[:, 1:bpc + 1, :, cols].reshape(nblk, A_BLOCK, HEAD_DIM)
        s_prev = lax.dot_general(q, k_prev, bqk, preferred_element_type=jnp.float32) + bias_ref[0, h, :, :A_BLOCK]
        s_prev = jnp.where(no_prev, NEG, s_prev)
        s_cur = lax.dot_general(q, k_cur, bqk, preferred_element_type=jnp.float32) + bias_ref[0, h, :, A_BLOCK:]
        m = jnp.max(jnp.maximum(s_prev, s_cur), axis=-1, keepdims=True)
        p_prev = jnp.exp2(s_prev - m)
        p_cur = jnp.exp2(s_cur - m)
        den = jnp.sum(p_prev + p_cur, axis=-1, keepdims=True)
        o = (lax.dot_general(p_prev.astype(jnp.bfloat16), v_prev, bqd, preferred_element_type=jnp.float32)
             + lax.dot_general(p_cur.astype(jnp.bfloat16), v_cur, bqd, preferred_element_type=jnp.float32)) / den
        lse = jnp.where(lane == h, m + jnp.log2(den), lse)
        for u in range(nblk):
            rr, nl = divmod(u, bpc)
            tokens = pl.ds(rr + nl * (r * A_BLOCK), A_BLOCK, stride=r) if r > 1 else pl.ds(u * A_BLOCK, A_BLOCK)
            stage_ref[h, tokens, :] = o[u]
        o_ref[:, cols] = stage_ref[h].astype(o_ref.dtype)
    for u in range(nblk):
        rr, nl = divmod(u, bpc)
        tokens = pl.ds(rr + nl * (r * A_BLOCK), A_BLOCK, stride=r) if r > 1 else pl.ds(u * A_BLOCK, A_BLOCK)
        lse_ref[tokens, :] = lse[u]


def _dilated_group(proj2, bias, group, *, batch, seq):
    _, r = DIL_PATTERNS[group]
    nc = seq // CHUNK
    bpc = BLOCKS_PER_CHUNK // r
    chunk = lambda sec: pl.BlockSpec((CHUNK, A_OUT), lambda b, c: (b * nc + c, sec * N_GROUPS + group))
    return pl.pallas_call(
        functools.partial(_dilated_kernel, r=r),
        grid=(batch, nc),
        in_specs=[chunk(0), chunk(1), chunk(2),
                  pl.BlockSpec((1, A_HEADS_PER_GROUP, A_BLOCK, 2 * A_BLOCK), lambda b, c: (group, 0, 0, 0))],
        out_specs=[pl.BlockSpec((CHUNK, A_OUT), lambda b, c: (b * nc + c, 0)),
                   pl.BlockSpec((CHUNK, HEAD_DIM), lambda b, c: (b * nc + c, 0))],
        out_shape=[jax.ShapeDtypeStruct((batch * seq, A_OUT), jnp.bfloat16),
                   jax.ShapeDtypeStruct((batch * seq, HEAD_DIM), jnp.float32)],
        scratch_shapes=[pltpu.VMEM((A_HEADS_PER_GROUP, CHUNK, HEAD_DIM), jnp.float32),
                        pltpu.VMEM((r, bpc, A_BLOCK, A_OUT), jnp.bfloat16),
                        pltpu.VMEM((r, bpc + 1, A_BLOCK, A_OUT), jnp.bfloat16),
                        pltpu.VMEM((r, bpc + 1, A_BLOCK, A_OUT), jnp.bfloat16)],
        compiler_params=pltpu.CompilerParams(
            dimension_semantics=("arbitrary", "arbitrary"), vmem_limit_bytes=VMEM_LIMIT),
        name=f"dilated_{group}",
    )(proj2, proj2, proj2, bias)


def _split3(x):
    hi = x.astype(jnp.bfloat16)
    r1 = x - hi.astype(jnp.float32)
    mid = r1.astype(jnp.bfloat16)
    lo = (r1 - mid.astype(jnp.float32)).astype(jnp.bfloat16)
    return hi, mid, lo


MOBA_STREAMS = 2


def _moba_items(nb, group):
    streams = [[] for _ in range(MOBA_STREAMS)]
    for i in sorted(range(nb), key=lambda i: -(i // group)):
        min(streams, key=len).extend((i, g) for g in range(i // group + 1))
    assert len({len(s) for s in streams}) == 1 and len(streams[0]) % 2 == 0
    return streams


def _moba_kernel(items_ref, q_ref, k_ref, v_ref, bias_ref, o_ref, kmean_ref, ka_ref, qa_ref, va_ref, s_ref,
                 st_ref, *, nb, group, n_items):
    BS = MOBA_BLOCK
    hd = HEAD_DIM
    seq = nb * BS
    rows_per_group = group * BS
    nt = (((1,), (1,)), ((), ()))

    row = lax.broadcasted_iota(jnp.int32, (nb, seq), 0)
    col = lax.broadcasted_iota(jnp.int32, (nb, seq), 1)
    lo_edge = row * BS
    avg = jnp.where((col >= lo_edge) & (col < lo_edge + BS), 1.0 / BS, 0.0).astype(jnp.bfloat16)
    kmean = jnp.dot(avg, k_ref[...], preferred_element_type=jnp.float32)
    for t, part in enumerate(_split3(kmean)):
        kmean_ref[t] = part

    q_all = q_ref[...]
    gate = sum(lax.dot_general(kmean_ref[t], q_all, nt, preferred_element_type=jnp.float32) for t in range(3))
    own = col // BS
    past = row < own
    gate = jnp.where(past, gate, -jnp.inf)
    rank = jnp.zeros((nb, seq), jnp.int32)
    for m in range(nb):
        gm = gate[m:m + 1, :]
        beats = (gm > gate) | ((gm == gate) & (row > m))
        rank = rank + beats.astype(jnp.int32)
    sel = jnp.where((past & (rank < MOBA_TOPK)) | (row == own), 0.0, NEG)

    qa_ref[:hd, :] = q_all.T
    qa_ref[hd:hd + nb, :] = sel.astype(qa_ref.dtype)
    qa_ref[hd + nb:, :] = jnp.zeros((hd - nb, seq), qa_ref.dtype)
    ka_ref[:, :hd] = k_ref[...]
    key_blk = lax.broadcasted_iota(jnp.int32, (seq, hd), 0) // BS
    ka_ref[:, hd:] = jnp.where(lax.broadcasted_iota(jnp.int32, (seq, hd), 1) == key_blk, 1.0, 0.0).astype(ka_ref.dtype)
    va_ref[:hd, :] = v_ref[...].T
    va_ref[hd:, :] = jnp.where(lax.broadcasted_iota(jnp.int32, (MOBA_AUX, seq), 0) == 0, 1.0, 0.0).astype(va_ref.dtype)

    def logits(st, w, slot):
        i, g = items_ref[st, w, 0], items_ref[st, w, 1]
        rows = pl.ds(pl.multiple_of(g * rows_per_group, rows_per_group), rows_per_group)
        qcols = pl.ds(pl.multiple_of(i * BS, BS), BS)
        s_t = jnp.dot(ka_ref[rows, :], qa_ref[:, qcols], preferred_element_type=jnp.float32)
        m8 = jnp.full((8, BS), -jnp.inf, jnp.float32)
        for u in range(group):
            tile = jnp.clip(i - (g * group + u), 0, MOBA_BIAS_TILES - 1)
            s_u = s_t[u * BS:(u + 1) * BS] + bias_ref[0, tile]
            s_ref[2 * st + slot, u * BS:(u + 1) * BS, :] = s_u
            m8 = jnp.maximum(m8, jnp.max(s_u.reshape(BS // 8, 8, BS), axis=0))
        return m8

    def accumulate(st, w, slot, m8, state):
        m_run, acc = state
        i, g = items_ref[st, w, 0], items_ref[st, w, 1]
        rows = pl.ds(pl.multiple_of(g * rows_per_group, rows_per_group), rows_per_group)
        m_prev = jnp.where(g == 0, -jnp.inf, m_run)
        m_new = jnp.maximum(m_prev, jnp.max(m8, axis=0, keepdims=True))
        alpha = jnp.exp2(m_prev - m_new)
        p = jnp.exp2(s_ref[2 * st + slot] - m_new).astype(jnp.bfloat16)
        acc = alpha * acc + jnp.dot(va_ref[:, rows], p, preferred_element_type=jnp.float32)
        st_ref[i] = acc
        return m_new, acc

    streams = range(MOBA_STREAMS)

    def pair(t, carry):
        m8a, state = carry
        w = 2 * t
        m8b = [logits(st, w + 1, 1) for st in streams]
        state = [accumulate(st, w, 0, m8a[st], state[st]) for st in streams]
        m8a = [logits(st, jnp.minimum(w + 2, n_items - 1), 0) for st in streams]
        state = [accumulate(st, w + 1, 1, m8b[st], state[st]) for st in streams]
        return m8a, state

    state0 = (jnp.full((1, BS), -jnp.inf, jnp.float32), jnp.zeros((hd + MOBA_AUX, BS), jnp.float32))
    lax.fori_loop(0, n_items // 2, pair, ([logits(st, 0, 0) for st in streams], [state0 for _ in streams]))

    for i in range(nb):
        acc = st_ref[i]
        o_ref[i * BS:(i + 1) * BS, :] = (acc[:hd] / acc[hd:hd + 1]).T.astype(o_ref.dtype)


def _moba(proj2, bias, *, batch, seq):
    BS = MOBA_BLOCK
    nb = seq // BS
    assert nb <= HEAD_DIM
    c0 = A_QKV // HEAD_DIM
    group = math.gcd(nb, MOBA_GROUP)
    items = _moba_items(nb, group)
    head_spec = lambda sec: pl.BlockSpec((seq, HEAD_DIM), lambda b, h, items: (b, c0 + sec * B_HEADS + h))
    return pl.pallas_call(
        functools.partial(_moba_kernel, nb=nb, group=group, n_items=len(items[0])),
        grid_spec=pltpu.PrefetchScalarGridSpec(
            num_scalar_prefetch=1,
            grid=(batch, B_HEADS),
            in_specs=[head_spec(0), head_spec(1), head_spec(2),
                      pl.BlockSpec((1, MOBA_BIAS_TILES, BS, BS), lambda b, h, items: (h, 0, 0, 0))],
            out_specs=pl.BlockSpec((seq, HEAD_DIM), lambda b, h, items: (b, h)),
            scratch_shapes=[pltpu.VMEM((3, nb, HEAD_DIM), jnp.bfloat16),
                            pltpu.VMEM((seq, 2 * HEAD_DIM), jnp.bfloat16),
                            pltpu.VMEM((2 * HEAD_DIM, seq), jnp.bfloat16),
                            pltpu.VMEM((HEAD_DIM + MOBA_AUX, seq), jnp.bfloat16),
                            pltpu.VMEM((2 * MOBA_STREAMS, group * BS, BS), jnp.float32),
                            pltpu.VMEM((nb, HEAD_DIM + MOBA_AUX, BS), jnp.float32)]),
        out_shape=jax.ShapeDtypeStruct((batch * seq, B_WIDTH), jnp.bfloat16),
        compiler_params=pltpu.CompilerParams(
            dimension_semantics=("arbitrary", "arbitrary"), vmem_limit_bytes=VMEM_LIMIT),
        name="moba",
    )(jnp.asarray(items, jnp.int32), proj2, proj2, proj2, bias)


def _merge_kernel(o0_ref, o1_ref, o2_ref, l0_ref, l1_ref, l2_ref, yb_ref, ga_ref, gb_ref, x_ref,
                  wa_ref, wb_ref, wo_ref, g_mlp_ref, out_ref, h_ref):
    l0, l1, l2 = l0_ref[...], l1_ref[...], l2_ref[...]
    mx = jnp.maximum(jnp.maximum(l0, l1), l2)
    e0, e1, e2 = jnp.exp2(l0 - mx), jnp.exp2(l1 - mx), jnp.exp2(l2 - mx)
    inv = 1.0 / (e0 + e1 + e2)
    parts = []
    for h in range(A_HEADS_PER_GROUP):
        cols = slice(h * HEAD_DIM, (h + 1) * HEAD_DIM)
        ya = ((e0 * inv)[:, h:h + 1] * o0_ref[:, cols].astype(jnp.float32)
              + (e1 * inv)[:, h:h + 1] * o1_ref[:, cols].astype(jnp.float32)
              + (e2 * inv)[:, h:h + 1] * o2_ref[:, cols].astype(jnp.float32))
        parts.append(ya.astype(jnp.bfloat16))
    ya = jnp.concatenate(parts, axis=-1)
    pa = jnp.dot(ya, wa_ref[...], preferred_element_type=jnp.float32)
    pb = jnp.dot(yb_ref[...], wb_ref[...], preferred_element_type=jnp.float32)
    mixed = ga_ref[...].astype(jnp.float32) * pa + gb_ref[...].astype(jnp.float32) * pb
    out = x_ref[...] + jnp.dot(mixed.astype(jnp.bfloat16), wo_ref[...], preferred_element_type=jnp.float32)
    out_ref[...] = out
    h_ref[...] = _rms_rows(out, g_mlp_ref[...]).astype(h_ref.dtype)


def _merge(outs, lses, yb, proj2, x2, wa, wb, wo, g_mlp, *, tm):
    T, D = x2.shape
    gate0 = QKV_PAD // D
    row = lambda w: pl.BlockSpec((tm, w), lambda i: (i, 0))
    full = lambda a: pl.BlockSpec(a.shape, lambda i: (0, 0))
    return pl.pallas_call(
        _merge_kernel,
        grid=(T // tm,),
        in_specs=[row(A_OUT)] * 3 + [row(HEAD_DIM)] * 3 + [row(B_WIDTH),
                  pl.BlockSpec((tm, D), lambda i: (i, gate0)), pl.BlockSpec((tm, D), lambda i: (i, gate0 + 1)),
                  row(D), full(wa), full(wb), full(wo), full(g_mlp)],
        out_specs=[row(D), row(D)],
        out_shape=[jax.ShapeDtypeStruct((T, D), jnp.float32), jax.ShapeDtypeStruct((T, D), jnp.bfloat16)],
        compiler_params=pltpu.CompilerParams(dimension_semantics=("arbitrary",), vmem_limit_bytes=VMEM_LIMIT),
        name="merge",
    )(*outs, *lses, yb, proj2, proj2, x2, wa, wb, wo, g_mlp)


MLP_ROW_CHUNKS = 2
MLP_K_CHUNKS = 4


def _mlp_kernel(h_ref, x_ref, wu_ref, wd_ref, o_ref, wubf_ref, wdbf_ref):
    f = pl.program_id(1)

    @pl.when(f == 0)
    def _():
        o_ref[...] = jnp.zeros(o_ref.shape, o_ref.dtype)

    slab = x_ref.shape[1]
    cols = pl.ds(pl.multiple_of(f * slab, slab), slab)
    o_ref[:, cols] += x_ref[...]

    tm, d = h_ref.shape
    rc = tm // MLP_ROW_CHUNKS
    kc = d // MLP_K_CHUNKS
    for c in range(MLP_ROW_CHUNKS):
        rows = slice(c * rc, (c + 1) * rc)
        u = None
        for k in range(MLP_K_CHUNKS):
            ks = slice(k * kc, (k + 1) * kc)
            if c == 0:
                wubf_ref[ks, :] = wu_ref[ks, :].astype(wubf_ref.dtype)
                wdbf_ref[:, ks] = wd_ref[:, ks].astype(wdbf_ref.dtype)
            part = jnp.dot(h_ref[rows, ks], wubf_ref[ks, :], preferred_element_type=jnp.float32)
            u = part if u is None else u + part
        u = jnp.square(jnp.maximum(u, 0.0)).astype(jnp.bfloat16)
        o_ref[rows, :] += jnp.dot(u, wdbf_ref[...], preferred_element_type=jnp.float32)


def _mlp(h2, x2, w_up, w_down, *, tm, tf):
    T, D = x2.shape
    F = w_up.shape[1]
    n_f = F // tf
    slab = D // n_f
    assert slab % HEAD_DIM == 0
    return pl.pallas_call(
        _mlp_kernel,
        grid=(T // tm, n_f),
        in_specs=[pl.BlockSpec((tm, D), lambda i, f: (i, 0)),
                  pl.BlockSpec((tm, slab), lambda i, f: (i, f)),
                  pl.BlockSpec((D, tf), lambda i, f: (0, f)),
                  pl.BlockSpec((tf, D), lambda i, f: (f, 0))],
        out_specs=pl.BlockSpec((tm, D), lambda i, f: (i, 0)),
        out_shape=jax.ShapeDtypeStruct((T, D), jnp.float32),
        scratch_shapes=[pltpu.VMEM((D, tf), jnp.bfloat16), pltpu.VMEM((tf, D), jnp.bfloat16)],
        compiler_params=pltpu.CompilerParams(
            dimension_semantics=("arbitrary", "arbitrary"), vmem_limit_bytes=VMEM_LIMIT),
        name="mlp",
    )(h2, x2, w_up, w_down)


def _column_gains(q_norm_a, k_norm_a, q_norm_b, k_norm_b, width):
    qs = SCALE * LOG2E
    gains = jnp.concatenate([
        jnp.tile(q_norm_a * qs, A_HEADS), jnp.tile(k_norm_a, A_HEADS), jnp.ones((A_HEADS * HEAD_DIM,), jnp.float32),
        jnp.tile(q_norm_b * qs, B_HEADS), jnp.tile(k_norm_b, B_HEADS), jnp.ones((B_WIDTH,), jnp.float32),
        jnp.ones((width - QKV_WIDTH,), jnp.float32)])
    return gains.reshape(1, width)


def kernel(x, g_mix, w_in, q_norm_a, k_norm_a, q_norm_b, k_norm_b, rel_bias,
           w_branch_a, w_branch_b, w_out, g_mlp, w_up, w_down):
    batch, seq, d_model = x.shape
    T = batch * seq
    assert w_in.shape == (d_model, QKV_WIDTH + 2 * d_model)
    assert seq % (DIL_PATTERNS[-1][1] * A_BLOCK) == 0 and QKV_PAD % d_model == 0
    bf16 = jnp.bfloat16
    x2 = x.reshape(T, d_model)
    tm = min(1024, T)

    dil_bias, moba_bias = _bias_tables(rel_bias)
    gains = _column_gains(q_norm_a, k_norm_a, q_norm_b, k_norm_b, w_in.shape[1])
    proj = _in_proj(x2, g_mix.reshape(1, -1), w_in, gains, tm=tm, tn=min(512, d_model * 2))

    outs, lses = zip(*[_dilated_group(proj, dil_bias, g, batch=batch, seq=seq) for g in range(N_GROUPS)])
    yb = _moba(proj, moba_bias, batch=batch, seq=seq)

    x_mid, h_mid = _merge(outs, lses, yb, proj, x2, w_branch_a.astype(bf16), w_branch_b.astype(bf16),
                          w_out.astype(bf16), g_mlp.reshape(1, -1), tm=min(512, T))
    y = _mlp(h_mid, x_mid, w_up, w_down, tm=tm, tf=min(512, w_up.shape[1]))
    return y.reshape(batch, seq, d_model)
```

```python
import functools
import math

import jax
import jax.numpy as jnp
from jax import lax
from jax.experimental import pallas as pl
from jax.experimental.pallas import tpu as pltpu

HEAD_DIM = 128
DIL_PATTERNS = ((128, 1), (512, 4), (2048, 16))
N_GROUPS = len(DIL_PATTERNS)
A_HEADS_PER_GROUP = 4
A_HEADS = A_HEADS_PER_GROUP * N_GROUPS
A_OUT = A_HEADS_PER_GROUP * HEAD_DIM
A_BLOCK = 128
B_HEADS = 8
B_WIDTH = B_HEADS * HEAD_DIM
MOBA_BLOCK = 256
MOBA_TOPK = 3
MOBA_GROUP = 2
MOBA_AUX = 16
N_BUCKETS = 32
MAX_DISTANCE = 2048
EPS = 1e-6
SCALE = HEAD_DIM ** -0.5
A_QKV = 3 * A_HEADS * HEAD_DIM
B_QKV = 3 * B_WIDTH
QKV_WIDTH = A_QKV + B_QKV
QKV_PAD = 8192
LOG2E = math.log2(math.e)
NEG = -1e30
VMEM_LIMIT = 56 * 1024 * 1024
MLP_VMEM_LIMIT = 60 * 1024 * 1024


def _bucket_thresholds():
    max_exact = N_BUCKETS // 2

    def bucket(d):
        if d < max_exact:
            return d
        v = int(math.log(d / max_exact) / math.log(MAX_DISTANCE / max_exact) * (N_BUCKETS - max_exact))
        return min(max_exact + v, N_BUCKETS - 1)

    thr, d = [0], 0
    for b in range(1, N_BUCKETS):
        while bucket(d) < b:
            d += 1
        thr.append(d)
    return tuple(thr)


BUCKET_THRESHOLDS = _bucket_thresholds()
MOBA_BIAS_TILES = -(-(BUCKET_THRESHOLDS[-1] - 1) // MOBA_BLOCK) + 2


def _bias_lookup(dist, tab_ref, col, dmin=0, dmax=None):
    first = sum(dmin >= t for t in BUCKET_THRESHOLDS[1:])
    val = jnp.full(dist.shape, tab_ref[first, col], jnp.float32)
    for b in range(first + 1, N_BUCKETS):
        if dmax is None or BUCKET_THRESHOLDS[b] <= dmax:
            val = jnp.where(dist >= BUCKET_THRESHOLDS[b], tab_ref[b, col], val)
    return val * LOG2E


def _dil_bias_kernel(tab_ref, o_ref):
    g, h = pl.program_id(0), pl.program_id(1)
    a = lax.broadcasted_iota(jnp.int32, (A_BLOCK, 2 * A_BLOCK), 0)
    j = lax.broadcasted_iota(jnp.int32, (A_BLOCK, 2 * A_BLOCK), 1)
    delta = a + A_BLOCK - j
    dilation = lax.shift_left(jnp.int32(1), 2 * g)
    val = _bias_lookup(delta * dilation, tab_ref, g * A_HEADS_PER_GROUP + h)
    o_ref[0, 0] = jnp.where((delta >= 0) & (delta <= A_BLOCK), val, NEG)


def _moba_bias_kernel(tab_ref, o_ref):
    h = pl.program_id(0)
    j = lax.broadcasted_iota(jnp.int32, (MOBA_BLOCK, MOBA_BLOCK), 0)
    s = lax.broadcasted_iota(jnp.int32, (MOBA_BLOCK, MOBA_BLOCK), 1)
    for c in range(MOBA_BIAS_TILES):
        dist = c * MOBA_BLOCK + s - j
        lo, hi = (c - 1) * MOBA_BLOCK + 1, (c + 1) * MOBA_BLOCK - 1
        val = _bias_lookup(dist, tab_ref, A_HEADS + h, max(lo, 0), hi)
        o_ref[0, c] = jnp.where(dist >= 0, val, NEG)


def _bias_tables(rel_bias):
    smem = pl.BlockSpec(memory_space=pltpu.SMEM)
    dil = pl.pallas_call(
        _dil_bias_kernel,
        grid=(N_GROUPS, A_HEADS_PER_GROUP),
        in_specs=[smem],
        out_specs=pl.BlockSpec((1, 1, A_BLOCK, 2 * A_BLOCK), lambda g, h: (g, h, 0, 0)),
        out_shape=jax.ShapeDtypeStruct((N_GROUPS, A_HEADS_PER_GROUP, A_BLOCK, 2 * A_BLOCK), jnp.float32),
        name="dil_bias",
    )(rel_bias)
    moba = pl.pallas_call(
        _moba_bias_kernel,
        grid=(B_HEADS,),
        in_specs=[smem],
        out_specs=pl.BlockSpec((1, MOBA_BIAS_TILES, MOBA_BLOCK, MOBA_BLOCK), lambda h: (h, 0, 0, 0)),
        out_shape=jax.ShapeDtypeStruct((B_HEADS, MOBA_BIAS_TILES, MOBA_BLOCK, MOBA_BLOCK), jnp.float32),
        name="moba_bias",
    )(rel_bias)
    return dil, moba


IN_PROJ_ROW_CHUNKS = 4
IN_PROJ_K_CHUNKS = 4


def _rms_rows(x, g):
    return (x * lax.rsqrt(jnp.mean(x * x, axis=-1, keepdims=True) + EPS)) * g


def _in_proj_kernel(x_ref, g_ref, w_ref, gain_ref, o_ref, h_ref, wbf_ref, *, tn):
    j = pl.program_id(1)

    @pl.when(j == 0)
    def _():
        h_ref[...] = _rms_rows(x_ref[...], g_ref[...]).astype(h_ref.dtype)

    col0 = j * tn
    is_norm = (col0 < 2 * A_HEADS * HEAD_DIM) | ((col0 >= A_QKV) & (col0 < A_QKV + 2 * B_WIDTH))
    is_pad = (col0 >= QKV_WIDTH) & (col0 < QKV_PAD)
    is_gate = col0 >= QKV_PAD

    def head_norm(acc):
        return jnp.concatenate(
            [_rms_rows(acc[:, c * HEAD_DIM:(c + 1) * HEAD_DIM], gain_ref[:, c * HEAD_DIM:(c + 1) * HEAD_DIM])
             for c in range(tn // HEAD_DIM)], axis=-1)

    def project(epilogue):
        tm, d = h_ref.shape
        rc = tm // IN_PROJ_ROW_CHUNKS
        kc = d // IN_PROJ_K_CHUNKS
        for c in range(IN_PROJ_ROW_CHUNKS):
            rows = slice(c * rc, (c + 1) * rc)
            acc = None
            for k in range(IN_PROJ_K_CHUNKS):
                ks = slice(k * kc, (k + 1) * kc)
                if c == 0:
                    wbf_ref[ks, :] = w_ref[ks, :].astype(wbf_ref.dtype)
                part = jnp.dot(h_ref[rows, ks], wbf_ref[ks, :], preferred_element_type=jnp.float32)
                acc = part if acc is None else acc + part
            o_ref[rows, :] = epilogue(acc).astype(o_ref.dtype)

    @pl.when(is_pad)
    def _():
        o_ref[...] = jnp.zeros(o_ref.shape, o_ref.dtype)

    @pl.when(is_norm)
    def _():
        project(head_norm)

    @pl.when(is_gate)
    def _():
        project(lambda acc: 0.5 * jnp.tanh(0.5 * acc) + 0.5)

    @pl.when(jnp.logical_not(is_pad | is_norm | is_gate))
    def _():
        project(lambda acc: acc)


def _in_proj(x2, g_mix, w_in, gains, *, tm, tn):
    T, D = x2.shape
    n_qkv_tiles = QKV_WIDTH // tn
    n_pad_tiles = (QKV_PAD - QKV_WIDTH) // tn
    out_width = QKV_PAD + 2 * D
    w_tile = lambda j: jnp.where(j < n_qkv_tiles, j, jnp.maximum(j - n_pad_tiles, n_qkv_tiles - 1))
    return pl.pallas_call(
        functools.partial(_in_proj_kernel, tn=tn),
        grid=(T // tm, out_width // tn),
        in_specs=[
            pl.BlockSpec((tm, D), lambda i, j: (i, 0)),
            pl.BlockSpec((1, D), lambda i, j: (0, 0)),
            pl.BlockSpec((D, tn), lambda i, j: (0, w_tile(j))),
            pl.BlockSpec((1, tn), lambda i, j: (0, w_tile(j))),
        ],
        out_specs=pl.BlockSpec((tm, tn), lambda i, j: (i, j)),
        out_shape=jax.ShapeDtypeStruct((T, out_width), jnp.bfloat16),
        scratch_shapes=[pltpu.VMEM((tm, D), jnp.bfloat16), pltpu.VMEM((D, tn), jnp.bfloat16)],
        compiler_params=pltpu.CompilerParams(
            dimension_semantics=("arbitrary", "arbitrary"), vmem_limit_bytes=VMEM_LIMIT),
        name="in_proj",
    )(x2, g_mix, w_in, gains)


CHUNK = DIL_PATTERNS[-1][1] * A_BLOCK
BLOCKS_PER_CHUNK = CHUNK // A_BLOCK


def _deinterleave(dst_ref, first_block, src_ref, stage_ref, r):
    bpc = BLOCKS_PER_CHUNK // r
    if r == 1:
        dst_ref[0, first_block:first_block + bpc] = src_ref[...].reshape(bpc, A_BLOCK, A_OUT)
        return
    for h in range(A_HEADS_PER_GROUP):
        cols = slice(h * HEAD_DIM, (h + 1) * HEAD_DIM)
        stage_ref[h] = src_ref[:, cols].astype(jnp.float32)
        for rr in range(r):
            rows = stage_ref[h, pl.ds(rr, CHUNK // r, stride=r), :].astype(dst_ref.dtype)
            dst_ref[rr, first_block:first_block + bpc, :, cols] = rows.reshape(bpc, A_BLOCK, HEAD_DIM)


def _dilated_kernel(q_ref, k_ref, v_ref, bias_ref, o_ref, lse_ref, stage_ref, qd_ref, kd_ref, vd_ref, *, r):
    c = pl.program_id(1)
    bpc = BLOCKS_PER_CHUNK // r
    nblk = BLOCKS_PER_CHUNK

    @pl.when(c == 0)
    def _():
        kd_ref[:, 0] = jnp.zeros((r, A_BLOCK, A_OUT), kd_ref.dtype)
        vd_ref[:, 0] = jnp.zeros((r, A_BLOCK, A_OUT), vd_ref.dtype)

    @pl.when(c > 0)
    def _():
        kd_ref[:, 0] = kd_ref[:, bpc]
        vd_ref[:, 0] = vd_ref[:, bpc]

    _deinterleave(qd_ref, 0, q_ref, stage_ref, r)
    _deinterleave(kd_ref, 1, k_ref, stage_ref, r)
    _deinterleave(vd_ref, 1, v_ref, stage_ref, r)

    blk = lax.broadcasted_iota(jnp.int32, (nblk, A_BLOCK, A_BLOCK), 0)
    no_prev = (lax.rem(blk, bpc) == 0) & (c == 0)
    lane = lax.broadcasted_iota(jnp.int32, (nblk, A_BLOCK, HEAD_DIM), 2)
    bqk = (((2,), (2,)), ((0,), (0,)))
    bqd = (((2,), (1,)), ((0,), (0,)))
    lse = jnp.zeros((nblk, A_BLOCK, HEAD_DIM), jnp.float32)
    for h in range(A_HEADS_PER_GROUP):
        cols = slice(h * HEAD_DIM, (h + 1) * HEAD_DIM)
        q = qd_ref[:, :, :, cols].reshape(nblk, A_BLOCK, HEAD_DIM)
        k_prev = kd_ref[:, 0:bpc, :, cols].reshape(nblk, A_BLOCK, HEAD_DIM)
        k_cur = kd_ref[:, 1:bpc + 1, :, cols].reshape(nblk, A_BLOCK, HEAD_DIM)
        v_prev = vd_ref[:, 0:bpc, :, cols].reshape(nblk, A_BLOCK, HEAD_DIM)
        v_cur = vd_ref[:, 1:bpc + 1, :, cols].reshape(nblk, A_BLOCK, HEAD_DIM)
        s_prev = lax.dot_general(q, k_prev, bqk, preferred_element_type=jnp.float32) + bias_ref[0, h, :, :A_BLOCK]
        s_prev = jnp.where(no_prev, NEG, s_prev)
        s_cur = lax.dot_general(q, k_cur, bqk, preferred_element_type=jnp.float32) + bias_ref[0, h, :, A_BLOCK:]
        m = jnp.max(jnp.maximum(s_prev, s_cur), axis=-1, keepdims=True)
        p_prev = jnp.exp2(s_prev - m)
        p_cur = jnp.exp2(s_cur - m)
        den = jnp.sum(p_prev + p_cur, axis=-1, keepdims=True)
        o = (lax.dot_general(p_prev.astype(jnp.bfloat16), v_prev, bqd, preferred_element_type=jnp.float32)
             + lax.dot_general(p_cur.astype(jnp.bfloat16), v_cur, bqd, preferred_element_type=jnp.float32)) / den
        lse = jnp.where(lane == h, m + jnp.log2(den), lse)
        for u in range(nblk):
            rr, nl = divmod(u, bpc)
            tokens = pl.ds(rr + nl * (r * A_BLOCK), A_BLOCK, stride=r) if r > 1 else pl.ds(u * A_BLOCK, A_BLOCK)
            stage_ref[h, tokens, :] = o[u]
        o_ref[:, cols] = stage_ref[h].astype(o_ref.dtype)
    for u in range(nblk):
        rr, nl = divmod(u, bpc)
        tokens = pl.ds(rr + nl * (r * A_BLOCK), A_BLOCK, stride=r) if r > 1 else pl.ds(u * A_BLOCK, A_BLOCK)
        lse_ref[tokens, :] = lse[u]


def _dilated_group(proj2, bias, group, *, batch, seq):
    _, r = DIL_PATTERNS[group]
    nc = seq // CHUNK
    bpc = BLOCKS_PER_CHUNK // r
    chunk = lambda sec: pl.BlockSpec((CHUNK, A_OUT), lambda b, c: (b * nc + c, sec * N_GROUPS + group))
    return pl.pallas_call(
        functools.partial(_dilated_kernel, r=r),
        grid=(batch, nc),
        in_specs=[chunk(0), chunk(1), chunk(2),
                  pl.BlockSpec((1, A_HEADS_PER_GROUP, A_BLOCK, 2 * A_BLOCK), lambda b, c: (group, 0, 0, 0))],
        out_specs=[pl.BlockSpec((CHUNK, A_OUT), lambda b, c: (b * nc + c, 0)),
                   pl.BlockSpec((CHUNK, HEAD_DIM), lambda b, c: (b * nc + c, 0))],
        out_shape=[jax.ShapeDtypeStruct((batch * seq, A_OUT), jnp.bfloat16),
                   jax.ShapeDtypeStruct((batch * seq, HEAD_DIM), jnp.float32)],
        scratch_shapes=[pltpu.VMEM((A_HEADS_PER_GROUP, CHUNK, HEAD_DIM), jnp.float32),
                        pltpu.VMEM((r, bpc, A_BLOCK, A_OUT), jnp.bfloat16),
                        pltpu.VMEM((r, bpc + 1, A_BLOCK, A_OUT), jnp.bfloat16),
                        pltpu.VMEM((r, bpc + 1, A_BLOCK, A_OUT), jnp.bfloat16)],
        compiler_params=pltpu.CompilerParams(
            dimension_semantics=("arbitrary", "arbitrary"), vmem_limit_bytes=VMEM_LIMIT),
        name=f"dilated_{group}",
    )(proj2, proj2, proj2, bias)


def _split3(x):
    hi = x.astype(jnp.bfloat16)
    r1 = x - hi.astype(jnp.float32)
    mid = r1.astype(jnp.bfloat16)
    lo = (r1 - mid.astype(jnp.float32)).astype(jnp.bfloat16)
    return hi, mid, lo


MOBA_STREAMS = 2


def _moba_items(nb, group):
    streams = [[] for _ in range(MOBA_STREAMS)]
    for i in sorted(range(nb), key=lambda i: -(i // group)):
        min(streams, key=len).extend((i, g) for g in range(i // group + 1))
    assert len({len(s) for s in streams}) == 1 and len(streams[0]) % 2 == 0
    return streams


def _moba_kernel(items_ref, q_ref, k_ref, v_ref, bias_ref, o_ref, avg_ref, kmean_ref, ka_ref, qa_ref, va_ref, s_ref,
                 st_ref, *, nb, group, n_items):
    BS = MOBA_BLOCK
    hd = HEAD_DIM
    seq = nb * BS
    rows_per_group = group * BS
    nt = (((1,), (1,)), ((), ()))

    row = lax.broadcasted_iota(jnp.int32, (nb, seq), 0)
    col = lax.broadcasted_iota(jnp.int32, (nb, seq), 1)

    @pl.when((pl.program_id(0) == 0) & (pl.program_id(1) == 0))
    def _():
        lo_edge = row * BS
        avg_ref[...] = jnp.where((col >= lo_edge) & (col < lo_edge + BS), 1.0 / BS, 0.0).astype(avg_ref.dtype)
        qa_ref[hd + nb:, :] = jnp.zeros((hd - nb, seq), qa_ref.dtype)
        key_blk = lax.broadcasted_iota(jnp.int32, (seq, hd), 0) // BS
        key_lane = lax.broadcasted_iota(jnp.int32, (seq, hd), 1)
        ka_ref[:, hd:] = jnp.where(key_lane == key_blk, 1.0, 0.0).astype(ka_ref.dtype)
        aux_row = lax.broadcasted_iota(jnp.int32, (MOBA_AUX, seq), 0)
        va_ref[hd:, :] = jnp.where(aux_row == 0, 1.0, 0.0).astype(va_ref.dtype)

    kmean = jnp.dot(avg_ref[...], k_ref[...], preferred_element_type=jnp.float32)
    for t, part in enumerate(_split3(kmean)):
        kmean_ref[t] = part

    q_all = q_ref[...]
    gate = sum(lax.dot_general(kmean_ref[t], q_all, nt, preferred_element_type=jnp.float32) for t in range(3))
    own = col // BS
    past = row < own
    gate = jnp.where(past, gate, -jnp.inf)
    sub = lax.broadcasted_iota(jnp.int32, (8, seq), 0)
    ranks = []
    for r0 in range(0, nb, 8):
        g8 = gate[r0:r0 + 8, :]
        rank8 = jnp.zeros((8, seq), jnp.int32)
        for m in range(nb):
            gm = gate[m:m + 1, :]
            if m >= r0 + 7:
                beats = gm > g8
            elif m < r0:
                beats = gm >= g8
            else:
                beats = (gm > g8) | ((gm == g8) & (sub > m - r0))
            rank8 = rank8 + beats.astype(jnp.int32)
        ranks.append(rank8)
    rank = jnp.concatenate(ranks, axis=0)
    sel = jnp.where((past & (rank < MOBA_TOPK)) | (row == own), 0.0, NEG)

    qa_ref[:hd, :] = q_all.T
    qa_ref[hd:hd + nb, :] = sel.astype(qa_ref.dtype)
    ka_ref[:, :hd] = k_ref[...]
    va_ref[:hd, :] = v_ref[...].T

    def logits(st, w, slot):
        i, g = items_ref[st, w, 0], items_ref[st, w, 1]
        rows = pl.ds(pl.multiple_of(g * rows_per_group, rows_per_group), rows_per_group)
        qcols = pl.ds(pl.multiple_of(i * BS, BS), BS)
        s_t = jnp.dot(ka_ref[rows, :], qa_ref[:, qcols], preferred_element_type=jnp.float32)
        m8 = jnp.full((8, BS), -jnp.inf, jnp.float32)
        for u in range(group):
            tile = jnp.clip(i - (g * group + u), 0, MOBA_BIAS_TILES - 1)
            s_u = s_t[u * BS:(u + 1) * BS] + bias_ref[0, tile]
            s_ref[2 * st + slot, u * BS:(u + 1) * BS, :] = s_u
            m8 = jnp.maximum(m8, jnp.max(s_u.reshape(BS // 8, 8, BS), axis=0))
        return m8

    def accumulate(st, w, slot, m8, state):
        m_run, acc = state
        i, g = items_ref[st, w, 0], items_ref[st, w, 1]
        rows = pl.ds(pl.multiple_of(g * rows_per_group, rows_per_group), rows_per_group)
        m_prev = jnp.where(g == 0, -jnp.inf, m_run)
        m_new = jnp.maximum(m_prev, jnp.max(m8, axis=0, keepdims=True))
        alpha = jnp.exp2(m_prev - m_new)
        p = jnp.exp2(s_ref[2 * st + slot] - m_new).astype(jnp.bfloat16)
        acc = alpha * acc + jnp.dot(va_ref[:, rows], p, preferred_element_type=jnp.float32)
        st_ref[i] = acc
        return m_new, acc

    streams = range(MOBA_STREAMS)

    def pair(t, carry):
        m8a, state = carry
        w = 2 * t
        m8b = [logits(st, w + 1, 1) for st in streams]
        state = [accumulate(st, w, 0, m8a[st], state[st]) for st in streams]
        m8a = [logits(st, jnp.minimum(w + 2, n_items - 1), 0) for st in streams]
        state = [accumulate(st, w + 1, 1, m8b[st], state[st]) for st in streams]
        return m8a, state

    state0 = (jnp.full((1, BS), -jnp.inf, jnp.float32), jnp.zeros((hd + MOBA_AUX, BS), jnp.float32))
    lax.fori_loop(0, n_items // 2, pair, ([logits(st, 0, 0) for st in streams], [state0 for _ in streams]))

    for i in range(nb):
        acc = st_ref[i]
        o_ref[i * BS:(i + 1) * BS, :] = (acc[:hd] / acc[hd:hd + 1]).T.astype(o_ref.dtype)


def _moba(proj2, bias, *, batch, seq):
    BS = MOBA_BLOCK
    nb = seq // BS
    assert nb <= HEAD_DIM
    c0 = A_QKV // HEAD_DIM
    group = math.gcd(nb, MOBA_GROUP)
    items = _moba_items(nb, group)
    head_spec = lambda sec: pl.BlockSpec((seq, HEAD_DIM), lambda b, h, items: (b, c0 + sec * B_HEADS + h))
    return pl.pallas_call(
        functools.partial(_moba_kernel, nb=nb, group=group, n_items=len(items[0])),
        grid_spec=pltpu.PrefetchScalarGridSpec(
            num_scalar_prefetch=1,
            grid=(batch, B_HEADS),
            in_specs=[head_spec(0), head_spec(1), head_spec(2),
                      pl.BlockSpec((1, MOBA_BIAS_TILES, BS, BS), lambda b, h, items: (h, 0, 0, 0))],
            out_specs=pl.BlockSpec((seq, HEAD_DIM), lambda b, h, items: (b, h)),
            scratch_shapes=[pltpu.VMEM((nb, seq), jnp.bfloat16),
                            pltpu.VMEM((3, nb, HEAD_DIM), jnp.bfloat16),
                            pltpu.VMEM((seq, 2 * HEAD_DIM), jnp.bfloat16),
                            pltpu.VMEM((2 * HEAD_DIM, seq), jnp.bfloat16),
                            pltpu.VMEM((HEAD_DIM + MOBA_AUX, seq), jnp.bfloat16),
                            pltpu.VMEM((2 * MOBA_STREAMS, group * BS, BS), jnp.float32),
                            pltpu.VMEM((nb, HEAD_DIM + MOBA_AUX, BS), jnp.float32)]),
        out_shape=jax.ShapeDtypeStruct((batch * seq, B_WIDTH), jnp.bfloat16),
        compiler_params=pltpu.CompilerParams(
            dimension_semantics=("arbitrary", "arbitrary"), vmem_limit_bytes=VMEM_LIMIT),
        name="moba",
    )(jnp.asarray(items, jnp.int32), proj2, proj2, proj2, bias)


MERGE_ROW_CHUNKS = 2


def _merge_kernel(o0_ref, o1_ref, o2_ref, l0_ref, l1_ref, l2_ref, yb_ref, ga_ref, gb_ref, x_ref,
                  wa_ref, wb_ref, wo_ref, g_mlp_ref, out_ref, h_ref):
    rc = out_ref.shape[0] // MERGE_ROW_CHUNKS
    for c in range(MERGE_ROW_CHUNKS):
        rows = slice(c * rc, (c + 1) * rc)
        l0, l1, l2 = l0_ref[rows, :], l1_ref[rows, :], l2_ref[rows, :]
        mx = jnp.maximum(jnp.maximum(l0, l1), l2)
        e0, e1, e2 = jnp.exp2(l0 - mx), jnp.exp2(l1 - mx), jnp.exp2(l2 - mx)
        inv = 1.0 / (e0 + e1 + e2)
        parts = []
        for h in range(A_HEADS_PER_GROUP):
            cols = slice(h * HEAD_DIM, (h + 1) * HEAD_DIM)
            ya = ((e0 * inv)[:, h:h + 1] * o0_ref[rows, cols].astype(jnp.float32)
                  + (e1 * inv)[:, h:h + 1] * o1_ref[rows, cols].astype(jnp.float32)
                  + (e2 * inv)[:, h:h + 1] * o2_ref[rows, cols].astype(jnp.float32))
            parts.append(ya.astype(jnp.bfloat16))
        ya = jnp.concatenate(parts, axis=-1)
        pa = jnp.dot(ya, wa_ref[...], preferred_element_type=jnp.float32)
        pb = jnp.dot(yb_ref[rows, :], wb_ref[...], preferred_element_type=jnp.float32)
        mixed = ga_ref[rows, :].astype(jnp.float32) * pa + gb_ref[rows, :].astype(jnp.float32) * pb
        out = x_ref[rows, :] + jnp.dot(mixed.astype(jnp.bfloat16), wo_ref[...], preferred_element_type=jnp.float32)
        out_ref[rows, :] = out
        h_ref[rows, :] = _rms_rows(out, g_mlp_ref[...]).astype(h_ref.dtype)


def _merge(outs, lses, yb, proj2, x2, wa, wb, wo, g_mlp, *, tm):
    T, D = x2.shape
    gate0 = QKV_PAD // D
    row = lambda w: pl.BlockSpec((tm, w), lambda i: (i, 0))
    full = lambda a: pl.BlockSpec(a.shape, lambda i: (0, 0))
    return pl.pallas_call(
        _merge_kernel,
        grid=(T // tm,),
        in_specs=[row(A_OUT)] * 3 + [row(HEAD_DIM)] * 3 + [row(B_WIDTH),
                  pl.BlockSpec((tm, D), lambda i: (i, gate0)), pl.BlockSpec((tm, D), lambda i: (i, gate0 + 1)),
                  row(D), full(wa), full(wb), full(wo), full(g_mlp)],
        out_specs=[row(D), row(D)],
        out_shape=[jax.ShapeDtypeStruct((T, D), jnp.float32), jax.ShapeDtypeStruct((T, D), jnp.bfloat16)],
        compiler_params=pltpu.CompilerParams(dimension_semantics=("arbitrary",), vmem_limit_bytes=VMEM_LIMIT),
        name="merge",
    )(*outs, *lses, yb, proj2, proj2, x2, wa, wb, wo, g_mlp)


MLP_ROW_CHUNKS = 2
MLP_K_CHUNKS = 4


def _mlp_kernel(h_ref, x_ref, wu_ref, wd_ref, o_ref, wubf_ref, wdbf_ref):
    f = pl.program_id(1)

    @pl.when(f == 0)
    def _():
        o_ref[...] = jnp.zeros(o_ref.shape, o_ref.dtype)

    slab = x_ref.shape[1]
    cols = pl.ds(pl.multiple_of(f * slab, slab), slab)
    o_ref[:, cols] += x_ref[...]

    tm, d = h_ref.shape
    rc = tm // MLP_ROW_CHUNKS
    kc = d // MLP_K_CHUNKS
    for c in range(MLP_ROW_CHUNKS):
        rows = slice(c * rc, (c + 1) * rc)
        u = None
        for k in range(MLP_K_CHUNKS):
            ks = slice(k * kc, (k + 1) * kc)
            if c == 0:
                wubf_ref[ks, :] = wu_ref[ks, :].astype(wubf_ref.dtype)
                wdbf_ref[:, ks] = wd_ref[:, ks].astype(wdbf_ref.dtype)
            part = jnp.dot(h_ref[rows, ks], wubf_ref[ks, :], preferred_element_type=jnp.float32)
            u = part if u is None else u + part
        u = jnp.square(jnp.maximum(u, 0.0)).astype(jnp.bfloat16)
        o_ref[rows, :] += jnp.dot(u, wdbf_ref[...], preferred_element_type=jnp.float32)


def _mlp(h2, x2, w_up, w_down, *, tm, tf):
    T, D = x2.shape
    F = w_up.shape[1]
    n_f = F // tf
    slab = D // n_f
    assert slab % HEAD_DIM == 0
    return pl.pallas_call(
        _mlp_kernel,
        grid=(T // tm, n_f),
        in_specs=[pl.BlockSpec((tm, D), lambda i, f: (i, 0)),
                  pl.BlockSpec((tm, slab), lambda i, f: (i, f)),
                  pl.BlockSpec((D, tf), lambda i, f: (0, f)),
                  pl.BlockSpec((tf, D), lambda i, f: (f, 0))],
        out_specs=pl.BlockSpec((tm, D), lambda i, f: (i, 0)),
        out_shape=jax.ShapeDtypeStruct((T, D), jnp.float32),
        scratch_shapes=[pltpu.VMEM((D, tf), jnp.bfloat16), pltpu.VMEM((tf, D), jnp.bfloat16)],
        compiler_params=pltpu.CompilerParams(
            dimension_semantics=("arbitrary", "arbitrary"), vmem_limit_bytes=VMEM_LIMIT),
        name="mlp",
    )(h2, x2, w_up, w_down)


def _column_gains(q_norm_a, k_norm_a, q_norm_b, k_norm_b, width):
    qs = SCALE * LOG2E
    gains = jnp.concatenate([
        jnp.tile(q_norm_a * qs, A_HEADS), jnp.tile(k_norm_a, A_HEADS), jnp.ones((A_HEADS * HEAD_DIM,), jnp.float32),
        jnp.tile(q_norm_b * qs, B_HEADS), jnp.tile(k_norm_b, B_HEADS), jnp.ones((B_WIDTH,), jnp.float32),
        jnp.ones((width - QKV_WIDTH,), jnp.float32)])
    return gains.reshape(1, width)


def kernel(x, g_mix, w_in, q_norm_a, k_norm_a, q_norm_b, k_norm_b, rel_bias,
           w_branch_a, w_branch_b, w_out, g_mlp, w_up, w_down):
    batch, seq, d_model = x.shape
    T = batch * seq
    assert w_in.shape == (d_model, QKV_WIDTH + 2 * d_model)
    assert seq % (DIL_PATTERNS[-1][1] * A_BLOCK) == 0 and QKV_PAD % d_model == 0
    bf16 = jnp.bfloat16
    x2 = x.reshape(T, d_model)
    tm = min(1024, T)

    dil_bias, moba_bias = _bias_tables(rel_bias)
    gains = _column_gains(q_norm_a, k_norm_a, q_norm_b, k_norm_b, w_in.shape[1])
    proj = _in_proj(x2, g_mix.reshape(1, -1), w_in, gains, tm=tm, tn=min(512, d_model * 2))

    outs, lses = zip(*[_dilated_group(proj, dil_bias, g, batch=batch, seq=seq) for g in range(N_GROUPS)])
    yb = _moba(proj, moba_bias, batch=batch, seq=seq)

    x_mid, h_mid = _merge(outs, lses, yb, proj, x2, w_branch_a.astype(bf16), w_branch_b.astype(bf16),
                          w_out.astype(bf16), g_mlp.reshape(1, -1), tm=min(512, T))
    y = _mlp(h_mid, x_mid, w_up, w_down, tm=tm, tf=min(512, w_up.shape[1]))
    return y.reshape(batch, seq, d_model)
```

```python
import functools
import math

import jax
import jax.numpy as jnp
from jax import lax
from jax.experimental import pallas as pl
from jax.experimental.pallas import tpu as pltpu

HEAD_DIM = 128
DIL_PATTERNS = ((128, 1), (512, 4), (2048, 16))
N_GROUPS = len(DIL_PATTERNS)
A_HEADS_PER_GROUP = 4
A_HEADS = A_HEADS_PER_GROUP * N_GROUPS
A_OUT = A_HEADS_PER_GROUP * HEAD_DIM
A_BLOCK = 128
B_HEADS = 8
B_WIDTH = B_HEADS * HEAD_DIM
MOBA_BLOCK = 256
MOBA_TOPK = 3
MOBA_GROUP = 2
MOBA_AUX = 16
N_BUCKETS = 32
MAX_DISTANCE = 2048
EPS = 1e-6
SCALE = HEAD_DIM ** -0.5
A_QKV = 3 * A_HEADS * HEAD_DIM
B_QKV = 3 * B_WIDTH
QKV_WIDTH = A_QKV + B_QKV
QKV_PAD = 8192
LOG2E = math.log2(math.e)
NEG = -1e30
VMEM_LIMIT = 56 * 1024 * 1024
MLP_VMEM_LIMIT = 60 * 1024 * 1024


def _bucket_thresholds():
    max_exact = N_BUCKETS // 2

    def bucket(d):
        if d < max_exact:
            return d
        v = int(math.log(d / max_exact) / math.log(MAX_DISTANCE / max_exact) * (N_BUCKETS - max_exact))
        return min(max_exact + v, N_BUCKETS - 1)

    thr, d = [0], 0
    for b in range(1, N_BUCKETS):
        while bucket(d) < b:
            d += 1
        thr.append(d)
    return tuple(thr)


BUCKET_THRESHOLDS = _bucket_thresholds()
MOBA_BIAS_TILES = -(-(BUCKET_THRESHOLDS[-1] - 1) // MOBA_BLOCK) + 2


def _bias_lookup(dist, tab_ref, col, dmin=0, dmax=None):
    first = sum(dmin >= t for t in BUCKET_THRESHOLDS[1:])
    val = jnp.full(dist.shape, tab_ref[first, col], jnp.float32)
    for b in range(first + 1, N_BUCKETS):
        if dmax is None or BUCKET_THRESHOLDS[b] <= dmax:
            val = jnp.where(dist >= BUCKET_THRESHOLDS[b], tab_ref[b, col], val)
    return val * LOG2E


def _dil_bias_kernel(tab_ref, o_ref):
    g, h = pl.program_id(0), pl.program_id(1)
    a = lax.broadcasted_iota(jnp.int32, (A_BLOCK, 2 * A_BLOCK), 0)
    j = lax.broadcasted_iota(jnp.int32, (A_BLOCK, 2 * A_BLOCK), 1)
    delta = a + A_BLOCK - j
    dilation = lax.shift_left(jnp.int32(1), 2 * g)
    val = _bias_lookup(delta * dilation, tab_ref, g * A_HEADS_PER_GROUP + h)
    o_ref[0, 0] = jnp.where((delta >= 0) & (delta <= A_BLOCK), val, NEG)


def _moba_bias_kernel(tab_ref, o_ref):
    h = pl.program_id(0)
    j = lax.broadcasted_iota(jnp.int32, (MOBA_BLOCK, MOBA_BLOCK), 0)
    s = lax.broadcasted_iota(jnp.int32, (MOBA_BLOCK, MOBA_BLOCK), 1)
    for c in range(MOBA_BIAS_TILES):
        dist = c * MOBA_BLOCK + s - j
        lo, hi = (c - 1) * MOBA_BLOCK + 1, (c + 1) * MOBA_BLOCK - 1
        val = _bias_lookup(dist, tab_ref, A_HEADS + h, max(lo, 0), hi)
        o_ref[0, c] = jnp.where(dist >= 0, val, NEG)


def _bias_tables(rel_bias):
    smem = pl.BlockSpec(memory_space=pltpu.SMEM)
    dil = pl.pallas_call(
        _dil_bias_kernel,
        grid=(N_GROUPS, A_HEADS_PER_GROUP),
        in_specs=[smem],
        out_specs=pl.BlockSpec((1, 1, A_BLOCK, 2 * A_BLOCK), lambda g, h: (g, h, 0, 0)),
        out_shape=jax.ShapeDtypeStruct((N_GROUPS, A_HEADS_PER_GROUP, A_BLOCK, 2 * A_BLOCK), jnp.float32),
        name="dil_bias",
    )(rel_bias)
    moba = pl.pallas_call(
        _moba_bias_kernel,
        grid=(B_HEADS,),
        in_specs=[smem],
        out_specs=pl.BlockSpec((1, MOBA_BIAS_TILES, MOBA_BLOCK, MOBA_BLOCK), lambda h: (h, 0, 0, 0)),
        out_shape=jax.ShapeDtypeStruct((B_HEADS, MOBA_BIAS_TILES, MOBA_BLOCK, MOBA_BLOCK), jnp.float32),
        name="moba_bias",
    )(rel_bias)
    return dil, moba


IN_PROJ_ROW_CHUNKS = 2
IN_PROJ_K_CHUNKS = 4


def _rms_rows(x, g):
    return (x * lax.rsqrt(jnp.mean(x * x, axis=-1, keepdims=True) + EPS)) * g


def _in_proj_kernel(x_ref, g_ref, w_ref, gain_ref, o_ref, h_ref, wbf_ref, *, tn):
    j = pl.program_id(1)

    @pl.when(j == 0)
    def _():
        h_ref[...] = _rms_rows(x_ref[...], g_ref[...]).astype(h_ref.dtype)

    col0 = j * tn
    is_norm = (col0 < 2 * A_HEADS * HEAD_DIM) | ((col0 >= A_QKV) & (col0 < A_QKV + 2 * B_WIDTH))
    is_pad = (col0 >= QKV_WIDTH) & (col0 < QKV_PAD)
    is_gate = col0 >= QKV_PAD

    def head_norm(acc):
        return jnp.concatenate(
            [_rms_rows(acc[:, c * HEAD_DIM:(c + 1) * HEAD_DIM], gain_ref[:, c * HEAD_DIM:(c + 1) * HEAD_DIM])
             for c in range(tn // HEAD_DIM)], axis=-1)

    def project(epilogue):
        tm, d = h_ref.shape
        rc = tm // IN_PROJ_ROW_CHUNKS
        kc = d // IN_PROJ_K_CHUNKS
        for c in range(IN_PROJ_ROW_CHUNKS):
            rows = slice(c * rc, (c + 1) * rc)
            acc = None
            for k in range(IN_PROJ_K_CHUNKS):
                ks = slice(k * kc, (k + 1) * kc)
                if c == 0:
                    wbf_ref[ks, :] = w_ref[ks, :].astype(wbf_ref.dtype)
                part = jnp.dot(h_ref[rows, ks], wbf_ref[ks, :], preferred_element_type=jnp.float32)
                acc = part if acc is None else acc + part
            o_ref[rows, :] = epilogue(acc).astype(o_ref.dtype)

    @pl.when(is_pad)
    def _():
        o_ref[...] = jnp.zeros(o_ref.shape, o_ref.dtype)

    @pl.when(is_norm)
    def _():
        project(head_norm)

    @pl.when(is_gate)
    def _():
        project(lambda acc: 0.5 * jnp.tanh(0.5 * acc) + 0.5)

    @pl.when(jnp.logical_not(is_pad | is_norm | is_gate))
    def _():
        project(lambda acc: acc)


def _in_proj(x2, g_mix, w_in, gains, *, tm, tn):
    T, D = x2.shape
    n_qkv_tiles = QKV_WIDTH // tn
    n_pad_tiles = (QKV_PAD - QKV_WIDTH) // tn
    out_width = QKV_PAD + 2 * D
    w_tile = lambda j: jnp.where(j < n_qkv_tiles, j, jnp.maximum(j - n_pad_tiles, n_qkv_tiles - 1))
    return pl.pallas_call(
        functools.partial(_in_proj_kernel, tn=tn),
        grid=(T // tm, out_width // tn),
        in_specs=[
            pl.BlockSpec((tm, D), lambda i, j: (i, 0)),
            pl.BlockSpec((1, D), lambda i, j: (0, 0)),
            pl.BlockSpec((D, tn), lambda i, j: (0, w_tile(j))),
            pl.BlockSpec((1, tn), lambda i, j: (0, w_tile(j))),
        ],
        out_specs=pl.BlockSpec((tm, tn), lambda i, j: (i, j)),
        out_shape=jax.ShapeDtypeStruct((T, out_width), jnp.bfloat16),
        scratch_shapes=[pltpu.VMEM((tm, D), jnp.bfloat16), pltpu.VMEM((D, tn), jnp.bfloat16)],
        compiler_params=pltpu.CompilerParams(
            dimension_semantics=("arbitrary", "arbitrary"), vmem_limit_bytes=VMEM_LIMIT),
        name="in_proj",
    )(x2, g_mix, w_in, gains)


CHUNK = DIL_PATTERNS[-1][1] * A_BLOCK
BLOCKS_PER_CHUNK = CHUNK // A_BLOCK


def _deinterleave(dst_ref, first_block, src_ref, stage_ref, r):
    bpc = BLOCKS_PER_CHUNK // r
    if r == 1:
        dst_ref[0, first_block:first_block + bpc] = src_ref[...].reshape(bpc, A_BLOCK, A_OUT)
        return
    for h in range(A_HEADS_PER_GROUP):
        cols = slice(h * HEAD_DIM, (h + 1) * HEAD_DIM)
        stage_ref[h] = src_ref[:, cols].astype(jnp.float32)
        for rr in range(r):
            rows = stage_ref[h, pl.ds(rr, CHUNK // r, stride=r), :].astype(dst_ref.dtype)
            dst_ref[rr, first_block:first_block + bpc, :, cols] = rows.reshape(bpc, A_BLOCK, HEAD_DIM)


def _dilated_kernel(q_ref, k_ref, v_ref, bias_ref, o_ref, lse_ref, stage_ref, qd_ref, kd_ref, vd_ref, *, r):
    c = pl.program_id(1)
    bpc = BLOCKS_PER_CHUNK // r
    nblk = BLOCKS_PER_CHUNK

    @pl.when(c == 0)
    def _():
        kd_ref[:, 0] = jnp.zeros((r, A_BLOCK, A_OUT), kd_ref.dtype)
        vd_ref[:, 0] = jnp.zeros((r, A_BLOCK, A_OUT), vd_ref.dtype)

    @pl.when(c > 0)
    def _():
        kd_ref[:, 0] = kd_ref[:, bpc]
        vd_ref[:, 0] = vd_ref[:, bpc]

    _deinterleave(qd_ref, 0, q_ref, stage_ref, r)
    _deinterleave(kd_ref, 1, k_ref, stage_ref, r)
    _deinterleave(vd_ref, 1, v_ref, stage_ref, r)

    blk = lax.broadcasted_iota(jnp.int32, (nblk, A_BLOCK, A_BLOCK), 0)
    no_prev = (lax.rem(blk, bpc) == 0) & (c == 0)
    lane = lax.broadcasted_iota(jnp.int32, (nblk, A_BLOCK, HEAD_DIM), 2)
    bqk = (((2,), (2,)), ((0,), (0,)))
    bqd = (((2,), (1,)), ((0,), (0,)))
    lse = jnp.zeros((nblk, A_BLOCK, HEAD_DIM), jnp.float32)
    for h in range(A_HEADS_PER_GROUP):
        cols = slice(h * HEAD_DIM, (h + 1) * HEAD_DIM)
        q = qd_ref[:, :, :, cols].reshape(nblk, A_BLOCK, HEAD_DIM)
        k_prev = kd_ref[:, 0:bpc, :, cols].reshape(nblk, A_BLOCK, HEAD_DIM)
        k_cur = kd_ref[:, 1:bpc + 1, :, cols].reshape(nblk, A_BLOCK, HEAD_DIM)
        v_prev = vd_ref[:, 0:bpc, :, cols].reshape(nblk, A_BLOCK, HEAD_DIM)
        v_cur = vd_ref[:, 1:bpc + 1, :, cols].reshape(nblk, A_BLOCK, HEAD_DIM)
        s_prev = lax.dot_general(q, k_prev, bqk, preferred_element_type=jnp.float32) + bias_ref[0, h, :, :A_BLOCK]
        s_prev = jnp.where(no_prev, NEG, s_prev)
        s_cur = lax.dot_general(q, k_cur, bqk, preferred_element_type=jnp.float32) + bias_ref[0, h, :, A_BLOCK:]
        m = jnp.max(jnp.maximum(s_prev, s_cur), axis=-1, keepdims=True)
        p_prev = jnp.exp2(s_prev - m)
        p_cur = jnp.exp2(s_cur - m)
        den = jnp.sum(p_prev + p_cur, axis=-1, keepdims=True)
        o = (lax.dot_general(p_prev.astype(jnp.bfloat16), v_prev, bqd, preferred_element_type=jnp.float32)
             + lax.dot_general(p_cur.astype(jnp.bfloat16), v_cur, bqd, preferred_element_type=jnp.float32)) / den
        lse = jnp.where(lane == h, m + jnp.log2(den), lse)
        for u in range(nblk):
            rr, nl = divmod(u, bpc)
            tokens = pl.ds(rr + nl * (r * A_BLOCK), A_BLOCK, stride=r) if r > 1 else pl.ds(u * A_BLOCK, A_BLOCK)
            stage_ref[h, tokens, :] = o[u]
        o_ref[:, cols] = stage_ref[h].astype(o_ref.dtype)
    for u in range(nblk):
        rr, nl = divmod(u, bpc)
        tokens = pl.ds(rr + nl * (r * A_BLOCK), A_BLOCK, stride=r) if r > 1 else pl.ds(u * A_BLOCK, A_BLOCK)
        lse_ref[tokens, :] = lse[u]


def _dilated_group(proj2, bias, group, *, batch, seq):
    _, r = DIL_PATTERNS[group]
    nc = seq // CHUNK
    bpc = BLOCKS_PER_CHUNK // r
    chunk = lambda sec: pl.BlockSpec((CHUNK, A_OUT), lambda b, c: (b * nc + c, sec * N_GROUPS + group))
    return pl.pallas_call(
        functools.partial(_dilated_kernel, r=r),
        grid=(batch, nc),
        in_specs=[chunk(0), chunk(1), chunk(2),
                  pl.BlockSpec((1, A_HEADS_PER_GROUP, A_BLOCK, 2 * A_BLOCK), lambda b, c: (group, 0, 0, 0))],
        out_specs=[pl.BlockSpec((CHUNK, A_OUT), lambda b, c: (b * nc + c, 0)),
                   pl.BlockSpec((CHUNK, HEAD_DIM), lambda b, c: (b * nc + c, 0))],
        out_shape=[jax.ShapeDtypeStruct((batch * seq, A_OUT), jnp.bfloat16),
                   jax.ShapeDtypeStruct((batch * seq, HEAD_DIM), jnp.float32)],
        scratch_shapes=[pltpu.VMEM((A_HEADS_PER_GROUP, CHUNK, HEAD_DIM), jnp.float32),
                        pltpu.VMEM((r, bpc, A_BLOCK, A_OUT), jnp.bfloat16),
                        pltpu.VMEM((r, bpc + 1, A_BLOCK, A_OUT), jnp.bfloat16),
                        pltpu.VMEM((r, bpc + 1, A_BLOCK, A_OUT), jnp.bfloat16)],
        compiler_params=pltpu.CompilerParams(
            dimension_semantics=("arbitrary", "arbitrary"), vmem_limit_bytes=VMEM_LIMIT),
        name=f"dilated_{group}",
    )(proj2, proj2, proj2, bias)


def _split3(x):
    hi = x.astype(jnp.bfloat16)
    r1 = x - hi.astype(jnp.float32)
    mid = r1.astype(jnp.bfloat16)
    lo = (r1 - mid.astype(jnp.float32)).astype(jnp.bfloat16)
    return hi, mid, lo


MOBA_STREAMS = 2


def _moba_items(nb, group):
    streams = [[] for _ in range(MOBA_STREAMS)]
    for i in sorted(range(nb), key=lambda i: -(i // group)):
        min(streams, key=len).extend((i, g) for g in range(i // group + 1))
    assert len({len(s) for s in streams}) == 1 and len(streams[0]) % 2 == 0
    return streams


def _moba_kernel(items_ref, q_ref, k_ref, v_ref, bias_ref, o_ref, avg_ref, kmean_ref, ka_ref, qa_ref, va_ref, s_ref,
                 st_ref, *, nb, group, n_items):
    BS = MOBA_BLOCK
    hd = HEAD_DIM
    seq = nb * BS
    rows_per_group = group * BS
    nt = (((1,), (1,)), ((), ()))

    row = lax.broadcasted_iota(jnp.int32, (nb, seq), 0)
    col = lax.broadcasted_iota(jnp.int32, (nb, seq), 1)

    @pl.when((pl.program_id(0) == 0) & (pl.program_id(1) == 0))
    def _():
        lo_edge = row * BS
        avg_ref[...] = jnp.where((col >= lo_edge) & (col < lo_edge + BS), 1.0 / BS, 0.0).astype(avg_ref.dtype)
        qa_ref[hd + nb:, :] = jnp.zeros((hd - nb, seq), qa_ref.dtype)
        key_blk = lax.broadcasted_iota(jnp.int32, (seq, hd), 0) // BS
        key_lane = lax.broadcasted_iota(jnp.int32, (seq, hd), 1)
        ka_ref[:, hd:] = jnp.where(key_lane == key_blk, 1.0, 0.0).astype(ka_ref.dtype)
        aux_row = lax.broadcasted_iota(jnp.int32, (MOBA_AUX, seq), 0)
        va_ref[hd:, :] = jnp.where(aux_row == 0, 1.0, 0.0).astype(va_ref.dtype)

    kmean = jnp.dot(avg_ref[...], k_ref[...], preferred_element_type=jnp.float32)
    for t, part in enumerate(_split3(kmean)):
        kmean_ref[t] = part

    q_all = q_ref[...]
    gate = sum(lax.dot_general(kmean_ref[t], q_all, nt, preferred_element_type=jnp.float32) for t in range(3))
    own = col // BS
    past = row < own
    gate = jnp.where(past, gate, -jnp.inf)
    sub = lax.broadcasted_iota(jnp.int32, (8, seq), 0)
    ranks = []
    for r0 in range(0, nb, 8):
        g8 = gate[r0:r0 + 8, :]
        rank8 = jnp.zeros((8, seq), jnp.int32)
        for m in range(nb):
            gm = gate[m:m + 1, :]
            if m >= r0 + 7:
                beats = gm > g8
            elif m < r0:
                beats = gm >= g8
            else:
                beats = (gm > g8) | ((gm == g8) & (sub > m - r0))
            rank8 = rank8 + beats.astype(jnp.int32)
        ranks.append(rank8)
    rank = jnp.concatenate(ranks, axis=0)
    sel = jnp.where((past & (rank < MOBA_TOPK)) | (row == own), 0.0, NEG)

    qa_ref[:hd, :] = q_all.T
    qa_ref[hd:hd + nb, :] = sel.astype(qa_ref.dtype)
    ka_ref[:, :hd] = k_ref[...]
    va_ref[:hd, :] = v_ref[...].T

    def logits(st, w, slot):
        i, g = items_ref[st, w, 0], items_ref[st, w, 1]
        rows = pl.ds(pl.multiple_of(g * rows_per_group, rows_per_group), rows_per_group)
        qcols = pl.ds(pl.multiple_of(i * BS, BS), BS)
        s_t = jnp.dot(ka_ref[rows, :], qa_ref[:, qcols], preferred_element_type=jnp.float32)
        m8 = jnp.full((8, BS), -jnp.inf, jnp.float32)
        for u in range(group):
            tile = jnp.clip(i - (g * group + u), 0, MOBA_BIAS_TILES - 1)
            s_u = s_t[u * BS:(u + 1) * BS] + bias_ref[0, tile]
            s_ref[2 * st + slot, u * BS:(u + 1) * BS, :] = s_u
            m8 = jnp.maximum(m8, jnp.max(s_u.reshape(BS // 8, 8, BS), axis=0))
        return m8

    def accumulate(st, w, slot, m8, state):
        m_run, acc = state
        i, g = items_ref[st, w, 0], items_ref[st, w, 1]
        rows = pl.ds(pl.multiple_of(g * rows_per_group, rows_per_group), rows_per_group)
        m_prev = jnp.where(g == 0, -jnp.inf, m_run)
        m_new = jnp.maximum(m_prev, jnp.max(m8, axis=0, keepdims=True))
        alpha = jnp.exp2(m_prev - m_new)
        p = jnp.exp2(s_ref[2 * st + slot] - m_new).astype(jnp.bfloat16)
        acc = alpha * acc + jnp.dot(va_ref[:, rows], p, preferred_element_type=jnp.float32)
        st_ref[i] = acc
        return m_new, acc

    streams = range(MOBA_STREAMS)

    def pair(t, carry):
        m8a, state = carry
        w = 2 * t
        m8b = [logits(st, w + 1, 1) for st in streams]
        state = [accumulate(st, w, 0, m8a[st], state[st]) for st in streams]
        m8a = [logits(st, jnp.minimum(w + 2, n_items - 1), 0) for st in streams]
        state = [accumulate(st, w + 1, 1, m8b[st], state[st]) for st in streams]
        return m8a, state

    state0 = (jnp.full((1, BS), -jnp.inf, jnp.float32), jnp.zeros((hd + MOBA_AUX, BS), jnp.float32))
    lax.fori_loop(0, n_items // 2, pair, ([logits(st, 0, 0) for st in streams], [state0 for _ in streams]))

    for i in range(nb):
        acc = st_ref[i]
        o_ref[i * BS:(i + 1) * BS, :] = (acc[:hd] / acc[hd:hd + 1]).T.astype(o_ref.dtype)


def _moba(proj2, bias, *, batch, seq):
    BS = MOBA_BLOCK
    nb = seq // BS
    assert nb <= HEAD_DIM
    c0 = A_QKV // HEAD_DIM
    group = math.gcd(nb, MOBA_GROUP)
    items = _moba_items(nb, group)
    head_spec = lambda sec: pl.BlockSpec((seq, HEAD_DIM), lambda b, h, items: (b, c0 + sec * B_HEADS + h))
    return pl.pallas_call(
        functools.partial(_moba_kernel, nb=nb, group=group, n_items=len(items[0])),
        grid_spec=pltpu.PrefetchScalarGridSpec(
            num_scalar_prefetch=1,
            grid=(batch, B_HEADS),
            in_specs=[head_spec(0), head_spec(1), head_spec(2),
                      pl.BlockSpec((1, MOBA_BIAS_TILES, BS, BS), lambda b, h, items: (h, 0, 0, 0))],
            out_specs=pl.BlockSpec((seq, HEAD_DIM), lambda b, h, items: (b, h)),
            scratch_shapes=[pltpu.VMEM((nb, seq), jnp.bfloat16),
                            pltpu.VMEM((3, nb, HEAD_DIM), jnp.bfloat16),
                            pltpu.VMEM((seq, 2 * HEAD_DIM), jnp.bfloat16),
                            pltpu.VMEM((2 * HEAD_DIM, seq), jnp.bfloat16),
                            pltpu.VMEM((HEAD_DIM + MOBA_AUX, seq), jnp.bfloat16),
                            pltpu.VMEM((2 * MOBA_STREAMS, group * BS, BS), jnp.float32),
                            pltpu.VMEM((nb, HEAD_DIM + MOBA_AUX, BS), jnp.float32)]),
        out_shape=jax.ShapeDtypeStruct((batch * seq, B_WIDTH), jnp.bfloat16),
        compiler_params=pltpu.CompilerParams(
            dimension_semantics=("arbitrary", "arbitrary"), vmem_limit_bytes=VMEM_LIMIT),
        name="moba",
    )(jnp.asarray(items, jnp.int32), proj2, proj2, proj2, bias)


MERGE_ROW_CHUNKS = 2


def _merge_kernel(o0_ref, o1_ref, o2_ref, l0_ref, l1_ref, l2_ref, yb_ref, ga_ref, gb_ref, x_ref,
                  wa_ref, wb_ref, wo_ref, g_mlp_ref, out_ref, h_ref):
    rc = out_ref.shape[0] // MERGE_ROW_CHUNKS
    for c in range(MERGE_ROW_CHUNKS):
        rows = slice(c * rc, (c + 1) * rc)
        l0, l1, l2 = l0_ref[rows, :], l1_ref[rows, :], l2_ref[rows, :]
        mx = jnp.maximum(jnp.maximum(l0, l1), l2)
        e0, e1, e2 = jnp.exp2(l0 - mx), jnp.exp2(l1 - mx), jnp.exp2(l2 - mx)
        inv = 1.0 / (e0 + e1 + e2)
        parts = []
        for h in range(A_HEADS_PER_GROUP):
            cols = slice(h * HEAD_DIM, (h + 1) * HEAD_DIM)
            ya = ((e0 * inv)[:, h:h + 1] * o0_ref[rows, cols].astype(jnp.float32)
                  + (e1 * inv)[:, h:h + 1] * o1_ref[rows, cols].astype(jnp.float32)
                  + (e2 * inv)[:, h:h + 1] * o2_ref[rows, cols].astype(jnp.float32))
            parts.append(ya.astype(jnp.bfloat16))
        ya = jnp.concatenate(parts, axis=-1)
        pa = jnp.dot(ya, wa_ref[...], preferred_element_type=jnp.float32)
        pb = jnp.dot(yb_ref[rows, :], wb_ref[...], preferred_element_type=jnp.float32)
        mixed = ga_ref[rows, :].astype(jnp.float32) * pa + gb_ref[rows, :].astype(jnp.float32) * pb
        out = x_ref[rows, :] + jnp.dot(mixed.astype(jnp.bfloat16), wo_ref[...], preferred_element_type=jnp.float32)
        out_ref[rows, :] = out
        h_ref[rows, :] = _rms_rows(out, g_mlp_ref[...]).astype(h_ref.dtype)


def _merge(outs, lses, yb, proj2, x2, wa, wb, wo, g_mlp, *, tm):
    T, D = x2.shape
    gate0 = QKV_PAD // D
    row = lambda w: pl.BlockSpec((tm, w), lambda i: (i, 0))
    full = lambda a: pl.BlockSpec(a.shape, lambda i: (0, 0))
    return pl.pallas_call(
        _merge_kernel,
        grid=(T // tm,),
        in_specs=[row(A_OUT)] * 3 + [row(HEAD_DIM)] * 3 + [row(B_WIDTH),
                  pl.BlockSpec((tm, D), lambda i: (i, gate0)), pl.BlockSpec((tm, D), lambda i: (i, gate0 + 1)),
                  row(D), full(wa), full(wb), full(wo), full(g_mlp)],
        out_specs=[row(D), row(D)],
        out_shape=[jax.ShapeDtypeStruct((T, D), jnp.float32), jax.ShapeDtypeStruct((T, D), jnp.bfloat16)],
        compiler_params=pltpu.CompilerParams(dimension_semantics=("arbitrary",), vmem_limit_bytes=VMEM_LIMIT),
        name="merge",
    )(*outs, *lses, yb, proj2, proj2, x2, wa, wb, wo, g_mlp)


MLP_ROW_CHUNKS = 1
MLP_K_CHUNKS = 4


def _mlp_kernel(h_ref, x_ref, wu_ref, wd_ref, o_ref, wubf_ref, wdbf_ref):
    f = pl.program_id(1)

    @pl.when(f == 0)
    def _():
        o_ref[...] = jnp.zeros(o_ref.shape, o_ref.dtype)

    slab = x_ref.shape[1]
    cols = pl.ds(pl.multiple_of(f * slab, slab), slab)
    o_ref[:, cols] += x_ref[...]

    tm, d = h_ref.shape
    rc = tm // MLP_ROW_CHUNKS
    kc = d // MLP_K_CHUNKS
    for c in range(MLP_ROW_CHUNKS):
        rows = slice(c * rc, (c + 1) * rc)
        u = None
        for k in range(MLP_K_CHUNKS):
            ks = slice(k * kc, (k + 1) * kc)
            if c == 0:
                wubf_ref[ks, :] = wu_ref[ks, :].astype(wubf_ref.dtype)
                wdbf_ref[:, ks] = wd_ref[:, ks].astype(wdbf_ref.dtype)
            part = jnp.dot(h_ref[rows, ks], wubf_ref[ks, :], preferred_element_type=jnp.float32)
            u = part if u is None else u + part
        u = jnp.square(jnp.maximum(u, 0.0)).astype(jnp.bfloat16)
        o_ref[rows, :] += jnp.dot(u, wdbf_ref[...], preferred_element_type=jnp.float32)


def _mlp(h2, x2, w_up, w_down, *, tm, tf):
    T, D = x2.shape
    F = w_up.shape[1]
    n_f = F // tf
    slab = D // n_f
    assert slab % HEAD_DIM == 0
    return pl.pallas_call(
        _mlp_kernel,
        grid=(T // tm, n_f),
        in_specs=[pl.BlockSpec((tm, D), lambda i, f: (i, 0)),
                  pl.BlockSpec((tm, slab), lambda i, f: (i, f)),
                  pl.BlockSpec((D, tf), lambda i, f: (0, f)),
                  pl.BlockSpec((tf, D), lambda i, f: (f, 0))],
        out_specs=pl.BlockSpec((tm, D), lambda i, f: (i, 0)),
        out_shape=jax.ShapeDtypeStruct((T, D), jnp.float32),
        scratch_shapes=[pltpu.VMEM((D, tf), jnp.bfloat16), pltpu.VMEM((tf, D), jnp.bfloat16)],
        compiler_params=pltpu.CompilerParams(
            dimension_semantics=("arbitrary", "arbitrary"), vmem_limit_bytes=VMEM_LIMIT),
        name="mlp",
    )(h2, x2, w_up, w_down)


def _column_gains(q_norm_a, k_norm_a, q_norm_b, k_norm_b, width):
    qs = SCALE * LOG2E
    gains = jnp.concatenate([
        jnp.tile(q_norm_a * qs, A_HEADS), jnp.tile(k_norm_a, A_HEADS), jnp.ones((A_HEADS * HEAD_DIM,), jnp.float32),
        jnp.tile(q_norm_b * qs, B_HEADS), jnp.tile(k_norm_b, B_HEADS), jnp.ones((B_WIDTH,), jnp.float32),
        jnp.ones((width - QKV_WIDTH,), jnp.float32)])
    return gains.reshape(1, width)


def kernel(x, g_mix, w_in, q_norm_a, k_norm_a, q_norm_b, k_norm_b, rel_bias,
           w_branch_a, w_branch_b, w_out, g_mlp, w_up, w_down):
    batch, seq, d_model = x.shape
    T = batch * seq
    assert w_in.shape == (d_model, QKV_WIDTH + 2 * d_model)
    assert seq % (DIL_PATTERNS[-1][1] * A_BLOCK) == 0 and QKV_PAD % d_model == 0
    bf16 = jnp.bfloat16
    x2 = x.reshape(T, d_model)
    tm = min(1024, T)

    dil_bias, moba_bias = _bias_tables(rel_bias)
    gains = _column_gains(q_norm_a, k_norm_a, q_norm_b, k_norm_b, w_in.shape[1])
    proj = _in_proj(x2, g_mix.reshape(1, -1), w_in, gains, tm=tm, tn=min(512, d_model * 2))

    outs, lses = zip(*[_dilated_group(proj, dil_bias, g, batch=batch, seq=seq) for g in range(N_GROUPS)])
    yb = _moba(proj, moba_bias, batch=batch, seq=seq)

    x_mid, h_mid = _merge(outs, lses, yb, proj, x2, w_branch_a.astype(bf16), w_branch_b.astype(bf16),
                          w_out.astype(bf16), g_mlp.reshape(1, -1), tm=min(512, T))
    y = _mlp(h_mid, x_mid, w_up, w_down, tm=tm, tf=min(512, w_up.shape[1]))
    return y.reshape(batch, seq, d_model)
```

```python
import functools
import math

import jax
import jax.numpy as jnp
from jax import lax
from jax.experimental import pallas as pl
from jax.experimental.pallas import tpu as pltpu

HEAD_DIM = 128
DIL_PATTERNS = ((128, 1), (512, 4), (2048, 16))
N_GROUPS = len(DIL_PATTERNS)
A_HEADS_PER_GROUP = 4
A_HEADS = A_HEADS_PER_GROUP * N_GROUPS
A_OUT = A_HEADS_PER_GROUP * HEAD_DIM
A_BLOCK = 128
B_HEADS = 8
B_WIDTH = B_HEADS * HEAD_DIM
MOBA_BLOCK = 256
MOBA_TOPK = 3
MOBA_GROUP = 2
MOBA_AUX = 16
N_BUCKETS = 32
MAX_DISTANCE = 2048
EPS = 1e-6
SCALE = HEAD_DIM ** -0.5
A_QKV = 3 * A_HEADS * HEAD_DIM
B_QKV = 3 * B_WIDTH
QKV_WIDTH = A_QKV + B_QKV
QKV_PAD = 8192
LOG2E = math.log2(math.e)
NEG = -1e30
VMEM_LIMIT = 56 * 1024 * 1024
IN_PROJ_VMEM_LIMIT = 60 * 1024 * 1024


def _bucket_thresholds():
    max_exact = N_BUCKETS // 2

    def bucket(d):
        if d < max_exact:
            return d
        v = int(math.log(d / max_exact) / math.log(MAX_DISTANCE / max_exact) * (N_BUCKETS - max_exact))
        return min(max_exact + v, N_BUCKETS - 1)

    thr, d = [0], 0
    for b in range(1, N_BUCKETS):
        while bucket(d) < b:
            d += 1
        thr.append(d)
    return tuple(thr)


BUCKET_THRESHOLDS = _bucket_thresholds()
MOBA_BIAS_TILES = -(-(BUCKET_THRESHOLDS[-1] - 1) // MOBA_BLOCK) + 2


def _bias_lookup(dist, tab_ref, col, dmin=0, dmax=None):
    first = sum(dmin >= t for t in BUCKET_THRESHOLDS[1:])
    val = jnp.full(dist.shape, tab_ref[first, col], jnp.float32)
    for b in range(first + 1, N_BUCKETS):
        if dmax is None or BUCKET_THRESHOLDS[b] <= dmax:
            val = jnp.where(dist >= BUCKET_THRESHOLDS[b], tab_ref[b, col], val)
    return val * LOG2E


def _dil_bias_kernel(tab_ref, o_ref):
    g, h = pl.program_id(0), pl.program_id(1)
    a = lax.broadcasted_iota(jnp.int32, (A_BLOCK, 2 * A_BLOCK), 0)
    j = lax.broadcasted_iota(jnp.int32, (A_BLOCK, 2 * A_BLOCK), 1)
    delta = a + A_BLOCK - j
    dilation = lax.shift_left(jnp.int32(1), 2 * g)
    val = _bias_lookup(delta * dilation, tab_ref, g * A_HEADS_PER_GROUP + h)
    o_ref[0, 0] = jnp.where((delta >= 0) & (delta <= A_BLOCK), val, NEG)


def _moba_bias_kernel(tab_ref, o_ref):
    h = pl.program_id(0)
    j = lax.broadcasted_iota(jnp.int32, (MOBA_BLOCK, MOBA_BLOCK), 0)
    s = lax.broadcasted_iota(jnp.int32, (MOBA_BLOCK, MOBA_BLOCK), 1)
    for c in range(MOBA_BIAS_TILES):
        dist = c * MOBA_BLOCK + s - j
        lo, hi = (c - 1) * MOBA_BLOCK + 1, (c + 1) * MOBA_BLOCK - 1
        val = _bias_lookup(dist, tab_ref, A_HEADS + h, max(lo, 0), hi)
        o_ref[0, c] = jnp.where(dist >= 0, val, NEG)


def _bias_tables(rel_bias):
    smem = pl.BlockSpec(memory_space=pltpu.SMEM)
    dil = pl.pallas_call(
        _dil_bias_kernel,
        grid=(N_GROUPS, A_HEADS_PER_GROUP),
        in_specs=[smem],
        out_specs=pl.BlockSpec((1, 1, A_BLOCK, 2 * A_BLOCK), lambda g, h: (g, h, 0, 0)),
        out_shape=jax.ShapeDtypeStruct((N_GROUPS, A_HEADS_PER_GROUP, A_BLOCK, 2 * A_BLOCK), jnp.float32),
        name="dil_bias",
    )(rel_bias)
    moba = pl.pallas_call(
        _moba_bias_kernel,
        grid=(B_HEADS,),
        in_specs=[smem],
        out_specs=pl.BlockSpec((1, MOBA_BIAS_TILES, MOBA_BLOCK, MOBA_BLOCK), lambda h: (h, 0, 0, 0)),
        out_shape=jax.ShapeDtypeStruct((B_HEADS, MOBA_BIAS_TILES, MOBA_BLOCK, MOBA_BLOCK), jnp.float32),
        name="moba_bias",
    )(rel_bias)
    return dil, moba


IN_PROJ_ROW_CHUNKS = 2
IN_PROJ_K_CHUNKS = 4
IN_PROJ_PASSES = 2


def _rms_rows(x, g):
    return (x * lax.rsqrt(jnp.mean(x * x, axis=-1, keepdims=True) + EPS)) * g


def _in_proj_kernel(x_ref, g_ref, w_ref, gain_ref, o_ref, h_ref, cache_ref, *, tn):
    i, j = pl.program_id(1), pl.program_id(2)

    @pl.when(j == 0)
    def _():
        h_ref[...] = _rms_rows(x_ref[...], g_ref[...]).astype(h_ref.dtype)

    col0 = (pl.program_id(0) * cache_ref.shape[0] + j) * tn
    is_norm = (col0 < 2 * A_HEADS * HEAD_DIM) | ((col0 >= A_QKV) & (col0 < A_QKV + 2 * B_WIDTH))
    is_pad = (col0 >= QKV_WIDTH) & (col0 < QKV_PAD)
    is_gate = col0 >= QKV_PAD

    def head_norm(acc):
        return jnp.concatenate(
            [_rms_rows(acc[:, c * HEAD_DIM:(c + 1) * HEAD_DIM], gain_ref[:, c * HEAD_DIM:(c + 1) * HEAD_DIM])
             for c in range(tn // HEAD_DIM)], axis=-1)

    def project(epilogue, fill_cache):
        tm, d = h_ref.shape
        rc = tm // IN_PROJ_ROW_CHUNKS
        kc = d // IN_PROJ_K_CHUNKS
        for c in range(IN_PROJ_ROW_CHUNKS):
            rows = slice(c * rc, (c + 1) * rc)
            acc = None
            for k in range(IN_PROJ_K_CHUNKS):
                ks = slice(k * kc, (k + 1) * kc)
                if fill_cache and c == 0:
                    wk = w_ref[ks, :].astype(cache_ref.dtype)
                    cache_ref[j, ks, :] = wk
                else:
                    wk = cache_ref[j, ks, :]
                part = jnp.dot(h_ref[rows, ks], wk, preferred_element_type=jnp.float32)
                acc = part if acc is None else acc + part
            o_ref[rows, :] = epilogue(acc).astype(o_ref.dtype)

    def run(epilogue):
        @pl.when(i == 0)
        def _():
            project(epilogue, True)

        @pl.when(i > 0)
        def _():
            project(epilogue, False)

    @pl.when(is_pad)
    def _():
        o_ref[...] = jnp.zeros(o_ref.shape, o_ref.dtype)

    @pl.when(is_norm)
    def _():
        run(head_norm)

    @pl.when(is_gate)
    def _():
        run(lambda acc: 0.5 * jnp.tanh(0.5 * acc) + 0.5)

    @pl.when(jnp.logical_not(is_pad | is_norm | is_gate))
    def _():
        run(lambda acc: acc)


def _in_proj(x2, g_mix, w_in, gains, *, tm, tn):
    T, D = x2.shape
    n_qkv_tiles = QKV_WIDTH // tn
    n_pad_tiles = (QKV_PAD - QKV_WIDTH) // tn
    out_width = QKV_PAD + 2 * D
    n_out_tiles = out_width // tn
    passes = IN_PROJ_PASSES if n_out_tiles % IN_PROJ_PASSES == 0 else 1
    tpp = n_out_tiles // passes
    w_tile = lambda t: jnp.where(t < n_qkv_tiles, t, jnp.maximum(t - n_pad_tiles, n_qkv_tiles - 1))
    w_fetch = lambda p, i, j: (0, w_tile(p * tpp + jnp.where(i == 0, j, tpp - 1)))
    return pl.pallas_call(
        functools.partial(_in_proj_kernel, tn=tn),
        grid=(passes, T // tm, tpp),
        in_specs=[
            pl.BlockSpec((tm, D), lambda p, i, j: (i, 0)),
            pl.BlockSpec((1, D), lambda p, i, j: (0, 0)),
            pl.BlockSpec((D, tn), w_fetch),
            pl.BlockSpec((1, tn), lambda p, i, j: (0, w_tile(p * tpp + j))),
        ],
        out_specs=pl.BlockSpec((tm, tn), lambda p, i, j: (i, p * tpp + j)),
        out_shape=jax.ShapeDtypeStruct((T, out_width), jnp.bfloat16),
        scratch_shapes=[pltpu.VMEM((tm, D), jnp.bfloat16), pltpu.VMEM((tpp, D, tn), jnp.bfloat16)],
        compiler_params=pltpu.CompilerParams(
            dimension_semantics=("arbitrary",) * 3, vmem_limit_bytes=IN_PROJ_VMEM_LIMIT),
        name="in_proj",
    )(x2, g_mix, w_in, gains)


CHUNK = DIL_PATTERNS[-1][1] * A_BLOCK
BLOCKS_PER_CHUNK = CHUNK // A_BLOCK


def _deinterleave(dst_ref, first_block, src_ref, stage_ref, r):
    bpc = BLOCKS_PER_CHUNK // r
    if r == 1:
        dst_ref[0, first_block:first_block + bpc] = src_ref[...].reshape(bpc, A_BLOCK, A_OUT)
        return
    for h in range(A_HEADS_PER_GROUP):
        cols = slice(h * HEAD_DIM, (h + 1) * HEAD_DIM)
        stage_ref[h] = src_ref[:, cols].astype(jnp.float32)
        for rr in range(r):
            rows = stage_ref[h, pl.ds(rr, CHUNK // r, stride=r), :].astype(dst_ref.dtype)
            dst_ref[rr, first_block:first_block + bpc, :, cols] = rows.reshape(bpc, A_BLOCK, HEAD_DIM)


def _dilated_kernel(q_ref, k_ref, v_ref, bias_ref, o_ref, lse_ref, stage_ref, qd_ref, kd_ref, vd_ref, *, r):
    c = pl.program_id(1)
    bpc = BLOCKS_PER_CHUNK // r
    nblk = BLOCKS_PER_CHUNK

    @pl.when(c == 0)
    def _():
        kd_ref[:, 0] = jnp.zeros((r, A_BLOCK, A_OUT), kd_ref.dtype)
        vd_ref[:, 0] = jnp.zeros((r, A_BLOCK, A_OUT), vd_ref.dtype)

    @pl.when(c > 0)
    def _():
        kd_ref[:, 0] = kd_ref[:, bpc]
        vd_ref[:, 0] = vd_ref[:, bpc]

    _deinterleave(qd_ref, 0, q_ref, stage_ref, r)
    _deinterleave(kd_ref, 1, k_ref, stage_ref, r)
    _deinterleave(vd_ref, 1, v_ref, stage_ref, r)

    blk = lax.broadcasted_iota(jnp.int32, (nblk, A_BLOCK, 2 * A_BLOCK), 0)
    key = lax.broadcasted_iota(jnp.int32, (nblk, A_BLOCK, 2 * A_BLOCK), 2)
    no_prev = (lax.rem(blk, bpc) == 0) & (key < A_BLOCK) & (c == 0)
    lane = lax.broadcasted_iota(jnp.int32, (nblk, A_BLOCK, HEAD_DIM), 2)
    bqk = (((2,), (2,)), ((0,), (0,)))
    bqd = (((2,), (1,)), ((0,), (0,)))
    lse = jnp.zeros((nblk, A_BLOCK, HEAD_DIM), jnp.float32)
    for h in range(A_HEADS_PER_GROUP):
        cols = slice(h * HEAD_DIM, (h + 1) * HEAD_DIM)
        q = qd_ref[:, :, :, cols].reshape(nblk, A_BLOCK, HEAD_DIM)
        k2 = jnp.concatenate([kd_ref[:, 0:bpc, :, cols], kd_ref[:, 1:bpc + 1, :, cols]], axis=2)
        v2 = jnp.concatenate([vd_ref[:, 0:bpc, :, cols], vd_ref[:, 1:bpc + 1, :, cols]], axis=2)
        k2 = k2.reshape(nblk, 2 * A_BLOCK, HEAD_DIM)
        v2 = v2.reshape(nblk, 2 * A_BLOCK, HEAD_DIM)
        s = lax.dot_general(q, k2, bqk, preferred_element_type=jnp.float32) + bias_ref[0, h]
        s = jnp.where(no_prev, NEG, s)
        m = jnp.max(s, axis=-1, keepdims=True)
        p = jnp.exp2(s - m)
        den = jnp.sum(p, axis=-1, keepdims=True)
        o = lax.dot_general(p.astype(jnp.bfloat16), v2, bqd, preferred_element_type=jnp.float32) / den
        lse = jnp.where(lane == h, m + jnp.log2(den), lse)
        for u in range(nblk):
            rr, nl = divmod(u, bpc)
            tokens = pl.ds(rr + nl * (r * A_BLOCK), A_BLOCK, stride=r) if r > 1 else pl.ds(u * A_BLOCK, A_BLOCK)
            stage_ref[h, tokens, :] = o[u]
        o_ref[:, cols] = stage_ref[h].astype(o_ref.dtype)
    for u in range(nblk):
        rr, nl = divmod(u, bpc)
        tokens = pl.ds(rr + nl * (r * A_BLOCK), A_BLOCK, stride=r) if r > 1 else pl.ds(u * A_BLOCK, A_BLOCK)
        lse_ref[tokens, :] = lse[u]


def _dilated_group(proj2, bias, group, *, batch, seq):
    _, r = DIL_PATTERNS[group]
    nc = seq // CHUNK
    bpc = BLOCKS_PER_CHUNK // r
    chunk = lambda sec: pl.BlockSpec((CHUNK, A_OUT), lambda b, c: (b * nc + c, sec * N_GROUPS + group))
    return pl.pallas_call(
        functools.partial(_dilated_kernel, r=r),
        grid=(batch, nc),
        in_specs=[chunk(0), chunk(1), chunk(2),
                  pl.BlockSpec((1, A_HEADS_PER_GROUP, A_BLOCK, 2 * A_BLOCK), lambda b, c: (group, 0, 0, 0))],
        out_specs=[pl.BlockSpec((CHUNK, A_OUT), lambda b, c: (b * nc + c, 0)),
                   pl.BlockSpec((CHUNK, HEAD_DIM), lambda b, c: (b * nc + c, 0))],
        out_shape=[jax.ShapeDtypeStruct((batch * seq, A_OUT), jnp.bfloat16),
                   jax.ShapeDtypeStruct((batch * seq, HEAD_DIM), jnp.float32)],
        scratch_shapes=[pltpu.VMEM((A_HEADS_PER_GROUP, CHUNK, HEAD_DIM), jnp.float32),
                        pltpu.VMEM((r, bpc, A_BLOCK, A_OUT), jnp.bfloat16),
                        pltpu.VMEM((r, bpc + 1, A_BLOCK, A_OUT), jnp.bfloat16),
                        pltpu.VMEM((r, bpc + 1, A_BLOCK, A_OUT), jnp.bfloat16)],
        compiler_params=pltpu.CompilerParams(
            dimension_semantics=("arbitrary", "arbitrary"), vmem_limit_bytes=VMEM_LIMIT),
        name=f"dilated_{group}",
    )(proj2, proj2, proj2, bias)


def _split3(x):
    hi = x.astype(jnp.bfloat16)
    r1 = x - hi.astype(jnp.float32)
    mid = r1.astype(jnp.bfloat16)
    lo = (r1 - mid.astype(jnp.float32)).astype(jnp.bfloat16)
    return hi, mid, lo


MOBA_STREAMS = 2


def _moba_items(nb, group):
    streams = [[] for _ in range(MOBA_STREAMS)]
    for i in sorted(range(nb), key=lambda i: -(i // group)):
        min(streams, key=len).extend((i, g) for g in range(i // group + 1))
    assert len({len(s) for s in streams}) == 1 and len(streams[0]) % 2 == 0
    return streams


def _moba_kernel(items_ref, q_ref, k_ref, v_ref, bias_ref, o_ref, avg_ref, kmean_ref, ka_ref, qa_ref, va_ref, s_ref,
                 st_ref, *, nb, group, n_items):
    BS = MOBA_BLOCK
    hd = HEAD_DIM
    seq = nb * BS
    rows_per_group = group * BS
    nt = (((1,), (1,)), ((), ()))

    row = lax.broadcasted_iota(jnp.int32, (nb, seq), 0)
    col = lax.broadcasted_iota(jnp.int32, (nb, seq), 1)

    @pl.when((pl.program_id(0) == 0) & (pl.program_id(1) == 0))
    def _():
        lo_edge = row * BS
        avg_ref[...] = jnp.where((col >= lo_edge) & (col < lo_edge + BS), 1.0 / BS, 0.0).astype(avg_ref.dtype)
        qa_ref[hd + nb:, :] = jnp.zeros((hd - nb, seq), qa_ref.dtype)
        key_blk = lax.broadcasted_iota(jnp.int32, (seq, hd), 0) // BS
        key_lane = lax.broadcasted_iota(jnp.int32, (seq, hd), 1)
        ka_ref[:, hd:] = jnp.where(key_lane == key_blk, 1.0, 0.0).astype(ka_ref.dtype)
        aux_row = lax.broadcasted_iota(jnp.int32, (MOBA_AUX, seq), 0)
        va_ref[hd:, :] = jnp.where(aux_row == 0, 1.0, 0.0).astype(va_ref.dtype)

    kmean = jnp.dot(avg_ref[...], k_ref[...], preferred_element_type=jnp.float32)
    for t, part in enumerate(_split3(kmean)):
        kmean_ref[t] = part

    q_all = q_ref[...]
    gate = sum(lax.dot_general(kmean_ref[t], q_all, nt, preferred_element_type=jnp.float32) for t in range(3))
    own = col // BS
    past = row < own
    gate = jnp.where(past, gate, -jnp.inf)
    sub = lax.broadcasted_iota(jnp.int32, (8, seq), 0)
    ranks = []
    for r0 in range(0, nb, 8):
        g8 = gate[r0:r0 + 8, :]
        rank8 = jnp.zeros((8, seq), jnp.int32)
        for m in range(nb):
            gm = gate[m:m + 1, :]
            if m >= r0 + 7:
                beats = gm > g8
            elif m < r0:
                beats = gm >= g8
            else:
                beats = (gm > g8) | ((gm == g8) & (sub > m - r0))
            rank8 = rank8 + beats.astype(jnp.int32)
        ranks.append(rank8)
    rank = jnp.concatenate(ranks, axis=0)
    sel = jnp.where((past & (rank < MOBA_TOPK)) | (row == own), 0.0, NEG)

    qa_ref[:hd, :] = q_all.T
    qa_ref[hd:hd + nb, :] = sel.astype(qa_ref.dtype)
    ka_ref[:, :hd] = k_ref[...]
    va_ref[:hd, :] = v_ref[...].T

    def logits(st, w, slot):
        i, g = items_ref[st, w, 0], items_ref[st, w, 1]
        rows = pl.ds(pl.multiple_of(g * rows_per_group, rows_per_group), rows_per_group)
        qcols = pl.ds(pl.multiple_of(i * BS, BS), BS)
        s_t = jnp.dot(ka_ref[rows, :], qa_ref[:, qcols], preferred_element_type=jnp.float32)
        m8 = jnp.full((8, BS), -jnp.inf, jnp.float32)
        for u in range(group):
            tile = jnp.clip(i - (g * group + u), 0, MOBA_BIAS_TILES - 1)
            s_u = s_t[u * BS:(u + 1) * BS] + bias_ref[0, tile]
            s_ref[2 * st + slot, u * BS:(u + 1) * BS, :] = s_u
            m8 = jnp.maximum(m8, jnp.max(s_u.reshape(BS // 8, 8, BS), axis=0))
        return m8

    def accumulate(st, w, slot, m8, state):
        m_run, acc = state
        i, g = items_ref[st, w, 0], items_ref[st, w, 1]
        rows = pl.ds(pl.multiple_of(g * rows_per_group, rows_per_group), rows_per_group)
        m_prev = jnp.where(g == 0, -jnp.inf, m_run)
        m_new = jnp.maximum(m_prev, jnp.max(m8, axis=0, keepdims=True))
        alpha = jnp.exp2(m_prev - m_new)
        p = jnp.exp2(s_ref[2 * st + slot] - m_new).astype(jnp.bfloat16)
        acc = alpha * acc + jnp.dot(va_ref[:, rows], p, preferred_element_type=jnp.float32)
        st_ref[i] = acc
        return m_new, acc

    streams = range(MOBA_STREAMS)

    def pair(t, carry):
        m8a, state = carry
        w = 2 * t
        m8b = [logits(st, w + 1, 1) for st in streams]
        state = [accumulate(st, w, 0, m8a[st], state[st]) for st in streams]
        m8a = [logits(st, jnp.minimum(w + 2, n_items - 1), 0) for st in streams]
        state = [accumulate(st, w + 1, 1, m8b[st], state[st]) for st in streams]
        return m8a, state

    state0 = (jnp.full((1, BS), -jnp.inf, jnp.float32), jnp.zeros((hd + MOBA_AUX, BS), jnp.float32))
    lax.fori_loop(0, n_items // 2, pair, ([logits(st, 0, 0) for st in streams], [state0 for _ in streams]))

    for i in range(nb):
        acc = st_ref[i]
        o_ref[i * BS:(i + 1) * BS, :] = (acc[:hd] / acc[hd:hd + 1]).T.astype(o_ref.dtype)


def _moba(proj2, bias, *, batch, seq):
    BS = MOBA_BLOCK
    nb = seq // BS
    assert nb <= HEAD_DIM
    c0 = A_QKV // HEAD_DIM
    group = math.gcd(nb, MOBA_GROUP)
    items = _moba_items(nb, group)
    head_spec = lambda sec: pl.BlockSpec((seq, HEAD_DIM), lambda b, h, items: (b, c0 + sec * B_HEADS + h))
    return pl.pallas_call(
        functools.partial(_moba_kernel, nb=nb, group=group, n_items=len(items[0])),
        grid_spec=pltpu.PrefetchScalarGridSpec(
            num_scalar_prefetch=1,
            grid=(batch, B_HEADS),
            in_specs=[head_spec(0), head_spec(1), head_spec(2),
                      pl.BlockSpec((1, MOBA_BIAS_TILES, BS, BS), lambda b, h, items: (h, 0, 0, 0))],
            out_specs=pl.BlockSpec((seq, HEAD_DIM), lambda b, h, items: (b, h)),
            scratch_shapes=[pltpu.VMEM((nb, seq), jnp.bfloat16),
                            pltpu.VMEM((3, nb, HEAD_DIM), jnp.bfloat16),
                            pltpu.VMEM((seq, 2 * HEAD_DIM), jnp.bfloat16),
                            pltpu.VMEM((2 * HEAD_DIM, seq), jnp.bfloat16),
                            pltpu.VMEM((HEAD_DIM + MOBA_AUX, seq), jnp.bfloat16),
                            pltpu.VMEM((2 * MOBA_STREAMS, group * BS, BS), jnp.float32),
                            pltpu.VMEM((nb, HEAD_DIM + MOBA_AUX, BS), jnp.float32)]),
        out_shape=jax.ShapeDtypeStruct((batch * seq, B_WIDTH), jnp.bfloat16),
        compiler_params=pltpu.CompilerParams(
            dimension_semantics=("arbitrary", "arbitrary"), vmem_limit_bytes=VMEM_LIMIT),
        name="moba",
    )(jnp.asarray(items, jnp.int32), proj2, proj2, proj2, bias)


MERGE_ROW_CHUNKS = 2


def _merge_kernel(o0_ref, o1_ref, o2_ref, l0_ref, l1_ref, l2_ref, yb_ref, ga_ref, gb_ref, x_ref,
                  wa_ref, wb_ref, wo_ref, g_mlp_ref, out_ref, h_ref):
    rc = out_ref.shape[0] // MERGE_ROW_CHUNKS
    for c in range(MERGE_ROW_CHUNKS):
        rows = slice(c * rc, (c + 1) * rc)
        l0, l1, l2 = l0_ref[rows, :], l1_ref[rows, :], l2_ref[rows, :]
        mx = jnp.maximum(jnp.maximum(l0, l1), l2)
        e0, e1, e2 = jnp.exp2(l0 - mx), jnp.exp2(l1 - mx), jnp.exp2(l2 - mx)
        inv = 1.0 / (e0 + e1 + e2)
        parts = []
        for h in range(A_HEADS_PER_GROUP):
            cols = slice(h * HEAD_DIM, (h + 1) * HEAD_DIM)
            ya = ((e0 * inv)[:, h:h + 1] * o0_ref[rows, cols].astype(jnp.float32)
                  + (e1 * inv)[:, h:h + 1] * o1_ref[rows, cols].astype(jnp.float32)
                  + (e2 * inv)[:, h:h + 1] * o2_ref[rows, cols].astype(jnp.float32))
            parts.append(ya.astype(jnp.bfloat16))
        ya = jnp.concatenate(parts, axis=-1)
        pa = jnp.dot(ya, wa_ref[...], preferred_element_type=jnp.float32)
        pb = jnp.dot(yb_ref[rows, :], wb_ref[...], preferred_element_type=jnp.float32)
        mixed = ga_ref[rows, :].astype(jnp.float32) * pa + gb_ref[rows, :].astype(jnp.float32) * pb
        out = x_ref[rows, :] + jnp.dot(mixed.astype(jnp.bfloat16), wo_ref[...], preferred_element_type=jnp.float32)
        out_ref[rows, :] = out
        h_ref[rows, :] = _rms_rows(out, g_mlp_ref[...]).astype(h_ref.dtype)


def _merge(outs, lses, yb, proj2, x2, wa, wb, wo, g_mlp, *, tm):
    T, D = x2.shape
    gate0 = QKV_PAD // D
    row = lambda w: pl.BlockSpec((tm, w), lambda i: (i, 0))
    full = lambda a: pl.BlockSpec(a.shape, lambda i: (0, 0))
    return pl.pallas_call(
        _merge_kernel,
        grid=(T // tm,),
        in_specs=[row(A_OUT)] * 3 + [row(HEAD_DIM)] * 3 + [row(B_WIDTH),
                  pl.BlockSpec((tm, D), lambda i: (i, gate0)), pl.BlockSpec((tm, D), lambda i: (i, gate0 + 1)),
                  row(D), full(wa), full(wb), full(wo), full(g_mlp)],
        out_specs=[row(D), row(D)],
        out_shape=[jax.ShapeDtypeStruct((T, D), jnp.float32), jax.ShapeDtypeStruct((T, D), jnp.bfloat16)],
        compiler_params=pltpu.CompilerParams(dimension_semantics=("arbitrary",), vmem_limit_bytes=VMEM_LIMIT),
        name="merge",
    )(*outs, *lses, yb, proj2, proj2, x2, wa, wb, wo, g_mlp)


MLP_ROW_CHUNKS = 1
MLP_K_CHUNKS = 4


def _mlp_kernel(h_ref, x_ref, wu_ref, wd_ref, o_ref, wubf_ref, wdbf_ref):
    f = pl.program_id(1)

    @pl.when(f == 0)
    def _():
        o_ref[...] = jnp.zeros(o_ref.shape, o_ref.dtype)

    slab = x_ref.shape[1]
    cols = pl.ds(pl.multiple_of(f * slab, slab), slab)
    o_ref[:, cols] += x_ref[...]

    tm, d = h_ref.shape
    rc = tm // MLP_ROW_CHUNKS
    kc = d // MLP_K_CHUNKS
    for c in range(MLP_ROW_CHUNKS):
        rows = slice(c * rc, (c + 1) * rc)
        u = None
        for k in range(MLP_K_CHUNKS):
            ks = slice(k * kc, (k + 1) * kc)
            if c == 0:
                wubf_ref[ks, :] = wu_ref[ks, :].astype(wubf_ref.dtype)
                wdbf_ref[:, ks] = wd_ref[:, ks].astype(wdbf_ref.dtype)
            part = jnp.dot(h_ref[rows, ks], wubf_ref[ks, :], preferred_element_type=jnp.float32)
            u = part if u is None else u + part
        u = jnp.square(jnp.maximum(u, 0.0)).astype(jnp.bfloat16)
        o_ref[rows, :] += jnp.dot(u, wdbf_ref[...], preferred_element_type=jnp.float32)


def _mlp(h2, x2, w_up, w_down, *, tm, tf):
    T, D = x2.shape
    F = w_up.shape[1]
    n_f = F // tf
    slab = D // n_f
    assert slab % HEAD_DIM == 0
    return pl.pallas_call(
        _mlp_kernel,
        grid=(T // tm, n_f),
        in_specs=[pl.BlockSpec((tm, D), lambda i, f: (i, 0)),
                  pl.BlockSpec((tm, slab), lambda i, f: (i, f)),
                  pl.BlockSpec((D, tf), lambda i, f: (0, f)),
                  pl.BlockSpec((tf, D), lambda i, f: (f, 0))],
        out_specs=pl.BlockSpec((tm, D), lambda i, f: (i, 0)),
        out_shape=jax.ShapeDtypeStruct((T, D), jnp.float32),
        scratch_shapes=[pltpu.VMEM((D, tf), jnp.bfloat16), pltpu.VMEM((tf, D), jnp.bfloat16)],
        compiler_params=pltpu.CompilerParams(
            dimension_semantics=("arbitrary", "arbitrary"), vmem_limit_bytes=VMEM_LIMIT),
        name="mlp",
    )(h2, x2, w_up, w_down)


def _column_gains(q_norm_a, k_norm_a, q_norm_b, k_norm_b, width):
    qs = SCALE * LOG2E
    gains = jnp.concatenate([
        jnp.tile(q_norm_a * qs, A_HEADS), jnp.tile(k_norm_a, A_HEADS), jnp.ones((A_HEADS * HEAD_DIM,), jnp.float32),
        jnp.tile(q_norm_b * qs, B_HEADS), jnp.tile(k_norm_b, B_HEADS), jnp.ones((B_WIDTH,), jnp.float32),
        jnp.ones((width - QKV_WIDTH,), jnp.float32)])
    return gains.reshape(1, width)


def kernel(x, g_mix, w_in, q_norm_a, k_norm_a, q_norm_b, k_norm_b, rel_bias,
           w_branch_a, w_branch_b, w_out, g_mlp, w_up, w_down):
    batch, seq, d_model = x.shape
    T = batch * seq
    assert w_in.shape == (d_model, QKV_WIDTH + 2 * d_model)
    assert seq % (DIL_PATTERNS[-1][1] * A_BLOCK) == 0 and QKV_PAD % d_model == 0
    bf16 = jnp.bfloat16
    x2 = x.reshape(T, d_model)
    tm = min(1024, T)

    dil_bias, moba_bias = _bias_tables(rel_bias)
    gains = _column_gains(q_norm_a, k_norm_a, q_norm_b, k_norm_b, w_in.shape[1])
    proj = _in_proj(x2, g_mix.reshape(1, -1), w_in, gains, tm=tm, tn=min(512, d_model * 2))

    outs, lses = zip(*[_dilated_group(proj, dil_bias, g, batch=batch, seq=seq) for g in range(N_GROUPS)])
    yb = _moba(proj, moba_bias, batch=batch, seq=seq)

    x_mid, h_mid = _merge(outs, lses, yb, proj, x2, w_branch_a.astype(bf16), w_branch_b.astype(bf16),
                          w_out.astype(bf16), g_mlp.reshape(1, -1), tm=min(512, T))
    y = _mlp(h_mid, x_mid, w_up, w_down, tm=tm, tf=min(512, w_up.shape[1]))
    return y.reshape(batch, seq, d_model)
```

```python
import functools
import math

import jax
import jax.numpy as jnp
from jax import lax
from jax.experimental import pallas as pl
from jax.experimental.pallas import tpu as pltpu

HEAD_DIM = 128
DIL_PATTERNS = ((128, 1), (512, 4), (2048, 16))
N_GROUPS = len(DIL_PATTERNS)
A_HEADS_PER_GROUP = 4
A_HEADS = A_HEADS_PER_GROUP * N_GROUPS
A_OUT = A_HEADS_PER_GROUP * HEAD_DIM
A_BLOCK = 128
B_HEADS = 8
B_WIDTH = B_HEADS * HEAD_DIM
MOBA_BLOCK = 256
MOBA_TOPK = 3
MOBA_GROUP = 2
MOBA_AUX = 16
N_BUCKETS = 32
MAX_DISTANCE = 2048
EPS = 1e-6
SCALE = HEAD_DIM ** -0.5
A_QKV = 3 * A_HEADS * HEAD_DIM
B_QKV = 3 * B_WIDTH
QKV_WIDTH = A_QKV + B_QKV
LOG2E = math.log2(math.e)
NEG = -1e30
VMEM_LIMIT = 56 * 1024 * 1024


def _bucket_thresholds():
    max_exact = N_BUCKETS // 2

    def bucket(d):
        if d < max_exact:
            return d
        v = int(math.log(d / max_exact) / math.log(MAX_DISTANCE / max_exact) * (N_BUCKETS - max_exact))
        return min(max_exact + v, N_BUCKETS - 1)

    thr, d = [0], 0
    for b in range(1, N_BUCKETS):
        while bucket(d) < b:
            d += 1
        thr.append(d)
    return tuple(thr)


BUCKET_THRESHOLDS = _bucket_thresholds()
MOBA_BIAS_TILES = -(-(BUCKET_THRESHOLDS[-1] - 1) // MOBA_BLOCK) + 2


def _bias_lookup(dist, tab_ref, col, dmin=0, dmax=None):
    first = sum(dmin >= t for t in BUCKET_THRESHOLDS[1:])
    val = jnp.full(dist.shape, tab_ref[first, col], jnp.float32)
    for b in range(first + 1, N_BUCKETS):
        if dmax is None or BUCKET_THRESHOLDS[b] <= dmax:
            val = jnp.where(dist >= BUCKET_THRESHOLDS[b], tab_ref[b, col], val)
    return val * LOG2E


def _dil_bias_kernel(tab_ref, o_ref):
    g, h = pl.program_id(0), pl.program_id(1)
    a = lax.broadcasted_iota(jnp.int32, (A_BLOCK, 2 * A_BLOCK), 0)
    j = lax.broadcasted_iota(jnp.int32, (A_BLOCK, 2 * A_BLOCK), 1)
    delta = a + A_BLOCK - j
    dilation = lax.shift_left(jnp.int32(1), 2 * g)
    val = _bias_lookup(delta * dilation, tab_ref, g * A_HEADS_PER_GROUP + h)
    o_ref[0, 0] = jnp.where((delta >= 0) & (delta <= A_BLOCK), val, NEG)


def _moba_bias_kernel(tab_ref, o_ref):
    h = pl.program_id(0)
    j = lax.broadcasted_iota(jnp.int32, (MOBA_BLOCK, MOBA_BLOCK), 0)
    s = lax.broadcasted_iota(jnp.int32, (MOBA_BLOCK, MOBA_BLOCK), 1)
    for c in range(MOBA_BIAS_TILES):
        dist = c * MOBA_BLOCK + s - j
        lo, hi = (c - 1) * MOBA_BLOCK + 1, (c + 1) * MOBA_BLOCK - 1
        val = _bias_lookup(dist, tab_ref, A_HEADS + h, max(lo, 0), hi)
        o_ref[0, c] = jnp.where(dist >= 0, val, NEG)


def _bias_tables(rel_bias):
    smem = pl.BlockSpec(memory_space=pltpu.SMEM)
    dil = pl.pallas_call(
        _dil_bias_kernel,
        grid=(N_GROUPS, A_HEADS_PER_GROUP),
        in_specs=[smem],
        out_specs=pl.BlockSpec((1, 1, A_BLOCK, 2 * A_BLOCK), lambda g, h: (g, h, 0, 0)),
        out_shape=jax.ShapeDtypeStruct((N_GROUPS, A_HEADS_PER_GROUP, A_BLOCK, 2 * A_BLOCK), jnp.float32),
        name="dil_bias",
    )(rel_bias)
    moba = pl.pallas_call(
        _moba_bias_kernel,
        grid=(B_HEADS,),
        in_specs=[smem],
        out_specs=pl.BlockSpec((1, MOBA_BIAS_TILES, MOBA_BLOCK, MOBA_BLOCK), lambda h: (h, 0, 0, 0)),
        out_shape=jax.ShapeDtypeStruct((B_HEADS, MOBA_BIAS_TILES, MOBA_BLOCK, MOBA_BLOCK), jnp.float32),
        name="moba_bias",
    )(rel_bias)
    return dil, moba


IN_PROJ_ROW_CHUNKS = 2
IN_PROJ_K_CHUNKS = 4
IN_PROJ_TILES_PER_STEP = 2


def _rms_rows(x, g):
    return (x * lax.rsqrt(jnp.mean(x * x, axis=-1, keepdims=True) + EPS)) * g


def _in_proj_tile_kind(col0):
    if col0 < 2 * A_HEADS * HEAD_DIM or A_QKV <= col0 < A_QKV + 2 * B_WIDTH:
        return "norm"
    return "plain" if col0 < QKV_WIDTH else "gate"


def _in_proj_kernel(x_ref, g_ref, *refs, tn, n_tiles):
    tps = IN_PROJ_TILES_PER_STEP
    w_refs, gain_refs, (o_ref, h_ref, wbf_ref) = refs[:tps], refs[tps:2 * tps], refs[2 * tps:]
    s = pl.program_id(1)

    @pl.when(s == 0)
    def _():
        h_ref[...] = _rms_rows(x_ref[...], g_ref[...]).astype(h_ref.dtype)

    def head_norm(acc, gain_ref):
        return jnp.concatenate(
            [_rms_rows(acc[:, c * HEAD_DIM:(c + 1) * HEAD_DIM], gain_ref[:, c * HEAD_DIM:(c + 1) * HEAD_DIM])
             for c in range(tn // HEAD_DIM)], axis=-1)

    epilogues = {
        "norm": head_norm,
        "plain": lambda acc, gain_ref: acc,
        "gate": lambda acc, gain_ref: 0.5 * jnp.tanh(0.5 * acc) + 0.5,
    }

    def project(kinds):
        tm, d = h_ref.shape
        rc = tm // IN_PROJ_ROW_CHUNKS
        kc = d // IN_PROJ_K_CHUNKS
        for c in range(IN_PROJ_ROW_CHUNKS):
            rows = slice(c * rc, (c + 1) * rc)
            for t, kind in enumerate(kinds):
                acc = None
                for k in range(IN_PROJ_K_CHUNKS):
                    ks = slice(k * kc, (k + 1) * kc)
                    if c == 0:
                        wbf_ref[t, ks, :] = w_refs[t][ks, :].astype(wbf_ref.dtype)
                    part = jnp.dot(h_ref[rows, ks], wbf_ref[t, ks, :], preferred_element_type=jnp.float32)
                    acc = part if acc is None else acc + part
                o_ref[rows, t * tn:(t + 1) * tn] = epilogues[kind](acc, gain_refs[t]).astype(o_ref.dtype)

    steps_of = {}
    for step in range(-(-n_tiles // tps)):
        tiles = [min(step * tps + t, n_tiles - 1) for t in range(tps)]
        steps_of.setdefault(tuple(_in_proj_tile_kind(tile * tn) for tile in tiles), []).append(step)
    for kinds, steps in steps_of.items():
        @pl.when(functools.reduce(jnp.logical_or, [s == step for step in steps]))
        def _(kinds=kinds):
            project(kinds)


def _in_proj(x2, g_mix, w_in, gains, *, tm, tn):
    T, D = x2.shape
    tps = IN_PROJ_TILES_PER_STEP
    n_tiles = w_in.shape[1] // tn
    n_steps = -(-n_tiles // tps)
    tile = lambda t: (lambda i, s: (0, jnp.minimum(s * tps + t, n_tiles - 1)))
    return pl.pallas_call(
        functools.partial(_in_proj_kernel, tn=tn, n_tiles=n_tiles),
        grid=(T // tm, n_steps),
        in_specs=[pl.BlockSpec((tm, D), lambda i, s: (i, 0)), pl.BlockSpec((1, D), lambda i, s: (0, 0))]
                 + [pl.BlockSpec((D, tn), tile(t)) for t in range(tps)]
                 + [pl.BlockSpec((1, tn), tile(t)) for t in range(tps)],
        out_specs=pl.BlockSpec((tm, tps * tn), lambda i, s: (i, s)),
        out_shape=jax.ShapeDtypeStruct((T, n_steps * tps * tn), jnp.bfloat16),
        scratch_shapes=[pltpu.VMEM((tm, D), jnp.bfloat16), pltpu.VMEM((tps, D, tn), jnp.bfloat16)],
        compiler_params=pltpu.CompilerParams(
            dimension_semantics=("arbitrary", "arbitrary"), vmem_limit_bytes=VMEM_LIMIT),
        name="in_proj",
    )(x2, g_mix, *([w_in] * tps), *([gains] * tps))


CHUNK = DIL_PATTERNS[-1][1] * A_BLOCK
BLOCKS_PER_CHUNK = CHUNK // A_BLOCK


def _deinterleave(dst_ref, first_block, src_ref, stage_ref, r):
    bpc = BLOCKS_PER_CHUNK // r
    if r == 1:
        dst_ref[0, first_block:first_block + bpc] = src_ref[...].reshape(bpc, A_BLOCK, A_OUT)
        return
    for h in range(A_HEADS_PER_GROUP):
        cols = slice(h * HEAD_DIM, (h + 1) * HEAD_DIM)
        stage_ref[h] = src_ref[:, cols].astype(jnp.float32)
        for rr in range(r):
            rows = stage_ref[h, pl.ds(rr, CHUNK // r, stride=r), :].astype(dst_ref.dtype)
            dst_ref[rr, first_block:first_block + bpc, :, cols] = rows.reshape(bpc, A_BLOCK, HEAD_DIM)


def _dilated_kernel(q_ref, k_ref, v_ref, bias_ref, o_ref, lse_ref, stage_ref, qd_ref, kd_ref, vd_ref, *, r):
    c = pl.program_id(1)
    bpc = BLOCKS_PER_CHUNK // r
    nblk = BLOCKS_PER_CHUNK

    @pl.when(c == 0)
    def _():
        kd_ref[:, 0] = jnp.zeros((r, A_BLOCK, A_OUT), kd_ref.dtype)
        vd_ref[:, 0] = jnp.zeros((r, A_BLOCK, A_OUT), vd_ref.dtype)

    @pl.when(c > 0)
    def _():
        kd_ref[:, 0] = kd_ref[:, bpc]
        vd_ref[:, 0] = vd_ref[:, bpc]

    _deinterleave(qd_ref, 0, q_ref, stage_ref, r)
    _deinterleave(kd_ref, 1, k_ref, stage_ref, r)
    _deinterleave(vd_ref, 1, v_ref, stage_ref, r)

    blk = lax.broadcasted_iota(jnp.int32, (nblk, A_BLOCK, 2 * A_BLOCK), 0)
    key = lax.broadcasted_iota(jnp.int32, (nblk, A_BLOCK, 2 * A_BLOCK), 2)
    no_prev = (lax.rem(blk, bpc) == 0) & (key < A_BLOCK) & (c == 0)
    lane = lax.broadcasted_iota(jnp.int32, (nblk, A_BLOCK, HEAD_DIM), 2)
    bqk = (((2,), (2,)), ((0,), (0,)))
    bqd = (((2,), (1,)), ((0,), (0,)))
    lse = jnp.zeros((nblk, A_BLOCK, HEAD_DIM), jnp.float32)
    for h in range(A_HEADS_PER_GROUP):
        cols = slice(h * HEAD_DIM, (h + 1) * HEAD_DIM)
        q = qd_ref[:, :, :, cols].reshape(nblk, A_BLOCK, HEAD_DIM)
        k2 = jnp.concatenate([kd_ref[:, 0:bpc, :, cols], kd_ref[:, 1:bpc + 1, :, cols]], axis=2)
        v2 = jnp.concatenate([vd_ref[:, 0:bpc, :, cols], vd_ref[:, 1:bpc + 1, :, cols]], axis=2)
        k2 = k2.reshape(nblk, 2 * A_BLOCK, HEAD_DIM)
        v2 = v2.reshape(nblk, 2 * A_BLOCK, HEAD_DIM)
        s = lax.dot_general(q, k2, bqk, preferred_element_type=jnp.float32) + bias_ref[0, h]
        s = jnp.where(no_prev, NEG, s)
        m = jnp.max(s, axis=-1, keepdims=True)
        p = jnp.exp2(s - m)
        den = jnp.sum(p, axis=-1, keepdims=True)
        o = lax.dot_general(p.astype(jnp.bfloat16), v2, bqd, preferred_element_type=jnp.float32) / den
        lse = jnp.where(lane == h, m + jnp.log2(den), lse)
        for u in range(nblk):
            rr, nl = divmod(u, bpc)
            tokens = pl.ds(rr + nl * (r * A_BLOCK), A_BLOCK, stride=r) if r > 1 else pl.ds(u * A_BLOCK, A_BLOCK)
            stage_ref[h, tokens, :] = o[u]
        o_ref[:, cols] = stage_ref[h].astype(o_ref.dtype)
    for u in range(nblk):
        rr, nl = divmod(u, bpc)
        tokens = pl.ds(rr + nl * (r * A_BLOCK), A_BLOCK, stride=r) if r > 1 else pl.ds(u * A_BLOCK, A_BLOCK)
        lse_ref[tokens, :] = lse[u]


def _dilated_group(proj2, bias, group, *, batch, seq):
    _, r = DIL_PATTERNS[group]
    nc = seq // CHUNK
    bpc = BLOCKS_PER_CHUNK // r
    chunk = lambda sec: pl.BlockSpec((CHUNK, A_OUT), lambda b, c: (b * nc + c, sec * N_GROUPS + group))
    return pl.pallas_call(
        functools.partial(_dilated_kernel, r=r),
        grid=(batch, nc),
        in_specs=[chunk(0), chunk(1), chunk(2),
                  pl.BlockSpec((1, A_HEADS_PER_GROUP, A_BLOCK, 2 * A_BLOCK), lambda b, c: (group, 0, 0, 0))],
        out_specs=[pl.BlockSpec((CHUNK, A_OUT), lambda b, c: (b * nc + c, 0)),
                   pl.BlockSpec((CHUNK, HEAD_DIM), lambda b, c: (b * nc + c, 0))],
        out_shape=[jax.ShapeDtypeStruct((batch * seq, A_OUT), jnp.bfloat16),
                   jax.ShapeDtypeStruct((batch * seq, HEAD_DIM), jnp.float32)],
        scratch_shapes=[pltpu.VMEM((A_HEADS_PER_GROUP, CHUNK, HEAD_DIM), jnp.float32),
                        pltpu.VMEM((r, bpc, A_BLOCK, A_OUT), jnp.bfloat16),
                        pltpu.VMEM((r, bpc + 1, A_BLOCK, A_OUT), jnp.bfloat16),
                        pltpu.VMEM((r, bpc + 1, A_BLOCK, A_OUT), jnp.bfloat16)],
        compiler_params=pltpu.CompilerParams(
            dimension_semantics=("arbitrary", "arbitrary"), vmem_limit_bytes=VMEM_LIMIT),
        name=f"dilated_{group}",
    )(proj2, proj2, proj2, bias)


def _split3(x):
    hi = x.astype(jnp.bfloat16)
    r1 = x - hi.astype(jnp.float32)
    mid = r1.astype(jnp.bfloat16)
    lo = (r1 - mid.astype(jnp.float32)).astype(jnp.bfloat16)
    return hi, mid, lo


MOBA_STREAMS = 2


def _moba_items(nb, group):
    streams = [[] for _ in range(MOBA_STREAMS)]
    for i in sorted(range(nb), key=lambda i: -(i // group)):
        min(streams, key=len).extend((i, g) for g in range(i // group + 1))
    assert len({len(s) for s in streams}) == 1 and len(streams[0]) % 2 == 0
    return streams


def _moba_kernel(items_ref, q_ref, k_ref, v_ref, bias_ref, o_ref, avg_ref, kmean_ref, ka_ref, qa_ref, va_ref, s_ref,
                 st_ref, *, nb, group, n_items):
    BS = MOBA_BLOCK
    hd = HEAD_DIM
    seq = nb * BS
    rows_per_group = group * BS
    nt = (((1,), (1,)), ((), ()))

    row = lax.broadcasted_iota(jnp.int32, (nb, seq), 0)
    col = lax.broadcasted_iota(jnp.int32, (nb, seq), 1)

    @pl.when((pl.program_id(0) == 0) & (pl.program_id(1) == 0))
    def _():
        lo_edge = row * BS
        avg_ref[...] = jnp.where((col >= lo_edge) & (col < lo_edge + BS), 1.0 / BS, 0.0).astype(avg_ref.dtype)
        qa_ref[hd + nb:, :] = jnp.zeros((hd - nb, seq), qa_ref.dtype)
        key_blk = lax.broadcasted_iota(jnp.int32, (seq, hd), 0) // BS
        key_lane = lax.broadcasted_iota(jnp.int32, (seq, hd), 1)
        ka_ref[:, hd:] = jnp.where(key_lane == key_blk, 1.0, 0.0).astype(ka_ref.dtype)
        aux_row = lax.broadcasted_iota(jnp.int32, (MOBA_AUX, seq), 0)
        va_ref[hd:, :] = jnp.where(aux_row == 0, 1.0, 0.0).astype(va_ref.dtype)

    kmean = jnp.dot(avg_ref[...], k_ref[...], preferred_element_type=jnp.float32)
    for t, part in enumerate(_split3(kmean)):
        kmean_ref[t] = part

    q_all = q_ref[...]
    gate = sum(lax.dot_general(kmean_ref[t], q_all, nt, preferred_element_type=jnp.float32) for t in range(3))
    own = col // BS
    past = row < own
    gate = jnp.where(past, gate, -jnp.inf)
    sub = lax.broadcasted_iota(jnp.int32, (8, seq), 0)
    ranks = []
    for r0 in range(0, nb, 8):
        g8 = gate[r0:r0 + 8, :]
        rank8 = jnp.zeros((8, seq), jnp.int32)
        for m in range(nb):
            gm = gate[m:m + 1, :]
            if m >= r0 + 7:
                beats = gm > g8
            elif m < r0:
                beats = gm >= g8
            else:
                beats = (gm > g8) | ((gm == g8) & (sub > m - r0))
            rank8 = rank8 + beats.astype(jnp.int32)
        ranks.append(rank8)
    rank = jnp.concatenate(ranks, axis=0)
    sel = jnp.where((past & (rank < MOBA_TOPK)) | (row == own), 0.0, NEG)

    qa_ref[:hd, :] = q_all.T
    qa_ref[hd:hd + nb, :] = sel.astype(qa_ref.dtype)
    ka_ref[:, :hd] = k_ref[...]
    va_ref[:hd, :] = v_ref[...].T

    def logits(st, w, slot):
        i, g = items_ref[st, w, 0], items_ref[st, w, 1]
        rows = pl.ds(pl.multiple_of(g * rows_per_group, rows_per_group), rows_per_group)
        qcols = pl.ds(pl.multiple_of(i * BS, BS), BS)
        s_t = jnp.dot(ka_ref[rows, :], qa_ref[:, qcols], preferred_element_type=jnp.float32)
        m8 = jnp.full((8, BS), -jnp.inf, jnp.float32)
        for u in range(group):
            tile = jnp.clip(i - (g * group + u), 0, MOBA_BIAS_TILES - 1)
            s_u = s_t[u * BS:(u + 1) * BS] + bias_ref[0, tile]
            s_ref[2 * st + slot, u * BS:(u + 1) * BS, :] = s_u
            m8 = jnp.maximum(m8, jnp.max(s_u.reshape(BS // 8, 8, BS), axis=0))
        return m8

    def accumulate(st, w, slot, m8, state):
        m_run, acc = state
        i, g = items_ref[st, w, 0], items_ref[st, w, 1]
        rows = pl.ds(pl.multiple_of(g * rows_per_group, rows_per_group), rows_per_group)
        m_prev = jnp.where(g == 0, -jnp.inf, m_run)
        m_new = jnp.maximum(m_prev, jnp.max(m8, axis=0, keepdims=True))
        alpha = jnp.exp2(m_prev - m_new)
        p = jnp.exp2(s_ref[2 * st + slot] - m_new).astype(jnp.bfloat16)
        acc = alpha * acc + jnp.dot(va_ref[:, rows], p, preferred_element_type=jnp.float32)
        st_ref[i] = acc
        return m_new, acc

    streams = range(MOBA_STREAMS)

    def pair(t, carry):
        m8a, state = carry
        w = 2 * t
        m8b = [logits(st, w + 1, 1) for st in streams]
        state = [accumulate(st, w, 0, m8a[st], state[st]) for st in streams]
        m8a = [logits(st, jnp.minimum(w + 2, n_items - 1), 0) for st in streams]
        state = [accumulate(st, w + 1, 1, m8b[st], state[st]) for st in streams]
        return m8a, state

    state0 = (jnp.full((1, BS), -jnp.inf, jnp.float32), jnp.zeros((hd + MOBA_AUX, BS), jnp.float32))
    lax.fori_loop(0, n_items // 2, pair, ([logits(st, 0, 0) for st in streams], [state0 for _ in streams]))

    for i in range(nb):
        acc = st_ref[i]
        o_ref[i * BS:(i + 1) * BS, :] = (acc[:hd] / acc[hd:hd + 1]).T.astype(o_ref.dtype)


def _moba(proj2, bias, *, batch, seq):
    BS = MOBA_BLOCK
    nb = seq // BS
    assert nb <= HEAD_DIM
    c0 = A_QKV // HEAD_DIM
    group = math.gcd(nb, MOBA_GROUP)
    items = _moba_items(nb, group)
    head_spec = lambda sec: pl.BlockSpec((seq, HEAD_DIM), lambda b, h, items: (b, c0 + sec * B_HEADS + h))
    return pl.pallas_call(
        functools.partial(_moba_kernel, nb=nb, group=group, n_items=len(items[0])),
        grid_spec=pltpu.PrefetchScalarGridSpec(
            num_scalar_prefetch=1,
            grid=(batch, B_HEADS),
            in_specs=[head_spec(0), head_spec(1), head_spec(2),
                      pl.BlockSpec((1, MOBA_BIAS_TILES, BS, BS), lambda b, h, items: (h, 0, 0, 0))],
            out_specs=pl.BlockSpec((seq, HEAD_DIM), lambda b, h, items: (b, h)),
            scratch_shapes=[pltpu.VMEM((nb, seq), jnp.bfloat16),
                            pltpu.VMEM((3, nb, HEAD_DIM), jnp.bfloat16),
                            pltpu.VMEM((seq, 2 * HEAD_DIM), jnp.bfloat16),
                            pltpu.VMEM((2 * HEAD_DIM, seq), jnp.bfloat16),
                            pltpu.VMEM((HEAD_DIM + MOBA_AUX, seq), jnp.bfloat16),
                            pltpu.VMEM((2 * MOBA_STREAMS, group * BS, BS), jnp.float32),
                            pltpu.VMEM((nb, HEAD_DIM + MOBA_AUX, BS), jnp.float32)]),
        out_shape=jax.ShapeDtypeStruct((batch * seq, B_WIDTH), jnp.bfloat16),
        compiler_params=pltpu.CompilerParams(
            dimension_semantics=("arbitrary", "arbitrary"), vmem_limit_bytes=VMEM_LIMIT),
        name="moba",
    )(jnp.asarray(items, jnp.int32), proj2, proj2, proj2, bias)


MERGE_ROW_CHUNKS = 2


def _merge_kernel(o0_ref, o1_ref, o2_ref, l0_ref, l1_ref, l2_ref, yb_ref, x_ref,
                  wa_ref, wb_ref, wo_ref, g_mlp_ref, *refs):
    gate_refs, (out_ref, h_ref) = refs[:-2], refs[-2:]
    ga_refs, gb_refs = gate_refs[:len(gate_refs) // 2], gate_refs[len(gate_refs) // 2:]
    rc = out_ref.shape[0] // MERGE_ROW_CHUNKS
    for c in range(MERGE_ROW_CHUNKS):
        rows = slice(c * rc, (c + 1) * rc)
        l0, l1, l2 = l0_ref[rows, :], l1_ref[rows, :], l2_ref[rows, :]
        mx = jnp.maximum(jnp.maximum(l0, l1), l2)
        e0, e1, e2 = jnp.exp2(l0 - mx), jnp.exp2(l1 - mx), jnp.exp2(l2 - mx)
        inv = 1.0 / (e0 + e1 + e2)
        parts = []
        for h in range(A_HEADS_PER_GROUP):
            cols = slice(h * HEAD_DIM, (h + 1) * HEAD_DIM)
            ya = ((e0 * inv)[:, h:h + 1] * o0_ref[rows, cols].astype(jnp.float32)
                  + (e1 * inv)[:, h:h + 1] * o1_ref[rows, cols].astype(jnp.float32)
                  + (e2 * inv)[:, h:h + 1] * o2_ref[rows, cols].astype(jnp.float32))
            parts.append(ya.astype(jnp.bfloat16))
        ya = jnp.concatenate(parts, axis=-1)
        pa = jnp.dot(ya, wa_ref[...], preferred_element_type=jnp.float32)
        pb = jnp.dot(yb_ref[rows, :], wb_ref[...], preferred_element_type=jnp.float32)
        ga = jnp.concatenate([g[rows, :] for g in ga_refs], axis=-1).astype(jnp.float32)
        gb = jnp.concatenate([g[rows, :] for g in gb_refs], axis=-1).astype(jnp.float32)
        mixed = ga * pa + gb * pb
        out = x_ref[rows, :] + jnp.dot(mixed.astype(jnp.bfloat16), wo_ref[...], preferred_element_type=jnp.float32)
        out_ref[rows, :] = out
        h_ref[rows, :] = _rms_rows(out, g_mlp_ref[...]).astype(h_ref.dtype)


def _merge(outs, lses, yb, proj2, x2, wa, wb, wo, g_mlp, *, tm):
    T, D = x2.shape
    gw = math.gcd(QKV_WIDTH, D)
    n_gate_blocks = 2 * D // gw
    row = lambda w: pl.BlockSpec((tm, w), lambda i: (i, 0))
    full = lambda a: pl.BlockSpec(a.shape, lambda i: (0, 0))
    gate = lambda c: pl.BlockSpec((tm, gw), lambda i: (i, QKV_WIDTH // gw + c))
    return pl.pallas_call(
        _merge_kernel,
        grid=(T // tm,),
        in_specs=[row(A_OUT)] * 3 + [row(HEAD_DIM)] * 3 + [row(B_WIDTH), row(D),
                  full(wa), full(wb), full(wo), full(g_mlp)] + [gate(c) for c in range(n_gate_blocks)],
        out_specs=[row(D), row(D)],
        out_shape=[jax.ShapeDtypeStruct((T, D), jnp.float32), jax.ShapeDtypeStruct((T, D), jnp.bfloat16)],
        compiler_params=pltpu.CompilerParams(dimension_semantics=("arbitrary",), vmem_limit_bytes=VMEM_LIMIT),
        name="merge",
    )(*outs, *lses, yb, x2, wa, wb, wo, g_mlp, *([proj2] * n_gate_blocks))


MLP_ROW_CHUNKS = 1
MLP_K_CHUNKS = 4


def _mlp_kernel(h_ref, x_ref, wu_ref, wd_ref, o_ref, wubf_ref, wdbf_ref):
    f = pl.program_id(1)

    @pl.when(f == 0)
    def _():
        o_ref[...] = jnp.zeros(o_ref.shape, o_ref.dtype)

    slab = x_ref.shape[1]
    cols = pl.ds(pl.multiple_of(f * slab, slab), slab)
    o_ref[:, cols] += x_ref[...]

    tm, d = h_ref.shape
    rc = tm // MLP_ROW_CHUNKS
    kc = d // MLP_K_CHUNKS
    for c in range(MLP_ROW_CHUNKS):
        rows = slice(c * rc, (c + 1) * rc)
        u = None
        for k in range(MLP_K_CHUNKS):
            ks = slice(k * kc, (k + 1) * kc)
            if c == 0:
                wubf_ref[ks, :] = wu_ref[ks, :].astype(wubf_ref.dtype)
                wdbf_ref[:, ks] = wd_ref[:, ks].astype(wdbf_ref.dtype)
            part = jnp.dot(h_ref[rows, ks], wubf_ref[ks, :], preferred_element_type=jnp.float32)
            u = part if u is None else u + part
        u = jnp.square(jnp.maximum(u, 0.0)).astype(jnp.bfloat16)
        o_ref[rows, :] += jnp.dot(u, wdbf_ref[...], preferred_element_type=jnp.float32)


def _mlp(h2, x2, w_up, w_down, *, tm, tf):
    T, D = x2.shape
    F = w_up.shape[1]
    n_f = F // tf
    slab = D // n_f
    assert slab % HEAD_DIM == 0
    return pl.pallas_call(
        _mlp_kernel,
        grid=(T // tm, n_f),
        in_specs=[pl.BlockSpec((tm, D), lambda i, f: (i, 0)),
                  pl.BlockSpec((tm, slab), lambda i, f: (i, f)),
                  pl.BlockSpec((D, tf), lambda i, f: (0, f)),
                  pl.BlockSpec((tf, D), lambda i, f: (f, 0))],
        out_specs=pl.BlockSpec((tm, D), lambda i, f: (i, 0)),
        out_shape=jax.ShapeDtypeStruct((T, D), jnp.float32),
        scratch_shapes=[pltpu.VMEM((D, tf), jnp.bfloat16), pltpu.VMEM((tf, D), jnp.bfloat16)],
        compiler_params=pltpu.CompilerParams(
            dimension_semantics=("arbitrary", "arbitrary"), vmem_limit_bytes=VMEM_LIMIT),
        name="mlp",
    )(h2, x2, w_up, w_down)


def _column_gains(q_norm_a, k_norm_a, q_norm_b, k_norm_b, width):
    qs = SCALE * LOG2E
    gains = jnp.concatenate([
        jnp.tile(q_norm_a * qs, A_HEADS), jnp.tile(k_norm_a, A_HEADS), jnp.ones((A_HEADS * HEAD_DIM,), jnp.float32),
        jnp.tile(q_norm_b * qs, B_HEADS), jnp.tile(k_norm_b, B_HEADS), jnp.ones((B_WIDTH,), jnp.float32),
        jnp.ones((width - QKV_WIDTH,), jnp.float32)])
    return gains.reshape(1, width)


def kernel(x, g_mix, w_in, q_norm_a, k_norm_a, q_norm_b, k_norm_b, rel_bias,
           w_branch_a, w_branch_b, w_out, g_mlp, w_up, w_down):
    batch, seq, d_model = x.shape
    T = batch * seq
    assert w_in.shape == (d_model, QKV_WIDTH + 2 * d_model)
    assert seq % (DIL_PATTERNS[-1][1] * A_BLOCK) == 0
    bf16 = jnp.bfloat16
    x2 = x.reshape(T, d_model)
    tm = min(1024, T)

    dil_bias, moba_bias = _bias_tables(rel_bias)
    gains = _column_gains(q_norm_a, k_norm_a, q_norm_b, k_norm_b, w_in.shape[1])
    proj = _in_proj(x2, g_mix.reshape(1, -1), w_in, gains, tm=tm, tn=min(512, d_model * 2))

    outs, lses = zip(*[_dilated_group(proj, dil_bias, g, batch=batch, seq=seq) for g in range(N_GROUPS)])
    yb = _moba(proj, moba_bias, batch=batch, seq=seq)

    x_mid, h_mid = _merge(outs, lses, yb, proj, x2, w_branch_a.astype(bf16), w_branch_b.astype(bf16),
                          w_out.astype(bf16), g_mlp.reshape(1, -1), tm=min(512, T))
    y = _mlp(h_mid, x_mid, w_up, w_down, tm=tm, tf=min(512, w_up.shape[1]))
    return y.reshape(batch, seq, d_model)
```

```python
import functools
import math

import jax
import jax.numpy as jnp
from jax import lax
from jax.experimental import pallas as pl
from jax.experimental.pallas import tpu as pltpu

HEAD_DIM = 128
DIL_PATTERNS = ((128, 1), (512, 4), (2048, 16))
N_GROUPS = len(DIL_PATTERNS)
A_HEADS_PER_GROUP = 4
A_HEADS = A_HEADS_PER_GROUP * N_GROUPS
A_OUT = A_HEADS_PER_GROUP * HEAD_DIM
A_BLOCK = 128
B_HEADS = 8
B_WIDTH = B_HEADS * HEAD_DIM
MOBA_BLOCK = 256
MOBA_TOPK = 3
MOBA_GROUP = 2
MOBA_AUX = 16
N_BUCKETS = 32
MAX_DISTANCE = 2048
EPS = 1e-6
SCALE = HEAD_DIM ** -0.5
A_QKV = 3 * A_HEADS * HEAD_DIM
B_QKV = 3 * B_WIDTH
QKV_WIDTH = A_QKV + B_QKV
LOG2E = math.log2(math.e)
NEG = -1e30
VMEM_LIMIT = 56 * 1024 * 1024


def _bucket_thresholds():
    max_exact = N_BUCKETS // 2

    def bucket(d):
        if d < max_exact:
            return d
        v = int(math.log(d / max_exact) / math.log(MAX_DISTANCE / max_exact) * (N_BUCKETS - max_exact))
        return min(max_exact + v, N_BUCKETS - 1)

    thr, d = [0], 0
    for b in range(1, N_BUCKETS):
        while bucket(d) < b:
            d += 1
        thr.append(d)
    return tuple(thr)


BUCKET_THRESHOLDS = _bucket_thresholds()
MOBA_BIAS_TILES = -(-(BUCKET_THRESHOLDS[-1] - 1) // MOBA_BLOCK) + 2


def _bias_lookup(dist, tab_ref, col, dmin=0, dmax=None):
    first = sum(dmin >= t for t in BUCKET_THRESHOLDS[1:])
    val = jnp.full(dist.shape, tab_ref[first, col], jnp.float32)
    for b in range(first + 1, N_BUCKETS):
        if dmax is None or BUCKET_THRESHOLDS[b] <= dmax:
            val = jnp.where(dist >= BUCKET_THRESHOLDS[b], tab_ref[b, col], val)
    return val * LOG2E


def _dil_bias_kernel(tab_ref, o_ref):
    g, h = pl.program_id(0), pl.program_id(1)
    a = lax.broadcasted_iota(jnp.int32, (A_BLOCK, 2 * A_BLOCK), 0)
    j = lax.broadcasted_iota(jnp.int32, (A_BLOCK, 2 * A_BLOCK), 1)
    delta = a + A_BLOCK - j
    dilation = lax.shift_left(jnp.int32(1), 2 * g)
    val = _bias_lookup(delta * dilation, tab_ref, g * A_HEADS_PER_GROUP + h)
    o_ref[0, 0] = jnp.where((delta >= 0) & (delta <= A_BLOCK), val, NEG)


def _moba_bias_kernel(tab_ref, o_ref):
    h = pl.program_id(0)
    j = lax.broadcasted_iota(jnp.int32, (MOBA_BLOCK, MOBA_BLOCK), 0)
    s = lax.broadcasted_iota(jnp.int32, (MOBA_BLOCK, MOBA_BLOCK), 1)
    for c in range(MOBA_BIAS_TILES):
        dist = c * MOBA_BLOCK + s - j
        lo, hi = (c - 1) * MOBA_BLOCK + 1, (c + 1) * MOBA_BLOCK - 1
        val = _bias_lookup(dist, tab_ref, A_HEADS + h, max(lo, 0), hi)
        o_ref[0, c] = jnp.where(dist >= 0, val, NEG)


def _bias_tables(rel_bias):
    smem = pl.BlockSpec(memory_space=pltpu.SMEM)
    dil = pl.pallas_call(
        _dil_bias_kernel,
        grid=(N_GROUPS, A_HEADS_PER_GROUP),
        in_specs=[smem],
        out_specs=pl.BlockSpec((1, 1, A_BLOCK, 2 * A_BLOCK), lambda g, h: (g, h, 0, 0)),
        out_shape=jax.ShapeDtypeStruct((N_GROUPS, A_HEADS_PER_GROUP, A_BLOCK, 2 * A_BLOCK), jnp.float32),
        name="dil_bias",
    )(rel_bias)
    moba = pl.pallas_call(
        _moba_bias_kernel,
        grid=(B_HEADS,),
        in_specs=[smem],
        out_specs=pl.BlockSpec((1, MOBA_BIAS_TILES, MOBA_BLOCK, MOBA_BLOCK), lambda h: (h, 0, 0, 0)),
        out_shape=jax.ShapeDtypeStruct((B_HEADS, MOBA_BIAS_TILES, MOBA_BLOCK, MOBA_BLOCK), jnp.float32),
        name="moba_bias",
    )(rel_bias)
    return dil, moba


IN_PROJ_ROW_CHUNKS = 2
IN_PROJ_K_CHUNKS = 4
IN_PROJ_TILES_PER_STEP = 2


def _rms_rows(x, g):
    return (x * lax.rsqrt(jnp.mean(x * x, axis=-1, keepdims=True) + EPS)) * g


def _in_proj_tile_kind(col0):
    if col0 < 2 * A_HEADS * HEAD_DIM or A_QKV <= col0 < A_QKV + 2 * B_WIDTH:
        return "norm"
    return "plain" if col0 < QKV_WIDTH else "gate"


def _in_proj_kernel(x_ref, g_ref, *refs, tn, n_tiles):
    tps = IN_PROJ_TILES_PER_STEP
    w_refs, gain_refs, (o_ref, h_ref, wbf_ref) = refs[:tps], refs[tps:2 * tps], refs[2 * tps:]
    s = pl.program_id(1)

    @pl.when(s == 0)
    def _():
        h_ref[...] = _rms_rows(x_ref[...], g_ref[...]).astype(h_ref.dtype)

    def head_norm(acc, gain_ref):
        return jnp.concatenate(
            [_rms_rows(acc[:, c * HEAD_DIM:(c + 1) * HEAD_DIM], gain_ref[:, c * HEAD_DIM:(c + 1) * HEAD_DIM])
             for c in range(tn // HEAD_DIM)], axis=-1)

    epilogues = {
        "norm": head_norm,
        "plain": lambda acc, gain_ref: acc,
        "gate": lambda acc, gain_ref: 0.5 * jnp.tanh(0.5 * acc) + 0.5,
    }

    def project(kinds):
        tm, d = h_ref.shape
        rc = tm // IN_PROJ_ROW_CHUNKS
        kc = d // IN_PROJ_K_CHUNKS
        for c in range(IN_PROJ_ROW_CHUNKS):
            rows = slice(c * rc, (c + 1) * rc)
            for t, kind in enumerate(kinds):
                acc = None
                for k in range(IN_PROJ_K_CHUNKS):
                    ks = slice(k * kc, (k + 1) * kc)
                    if c == 0:
                        wbf_ref[t, ks, :] = w_refs[t][ks, :].astype(wbf_ref.dtype)
                    part = jnp.dot(h_ref[rows, ks], wbf_ref[t, ks, :], preferred_element_type=jnp.float32)
                    acc = part if acc is None else acc + part
                o_ref[rows, t * tn:(t + 1) * tn] = epilogues[kind](acc, gain_refs[t]).astype(o_ref.dtype)

    steps_of = {}
    for step in range(-(-n_tiles // tps)):
        tiles = [min(step * tps + t, n_tiles - 1) for t in range(tps)]
        steps_of.setdefault(tuple(_in_proj_tile_kind(tile * tn) for tile in tiles), []).append(step)
    for kinds, steps in steps_of.items():
        @pl.when(functools.reduce(jnp.logical_or, [s == step for step in steps]))
        def _(kinds=kinds):
            project(kinds)


def _in_proj(x2, g_mix, w_in, gains, *, tm, tn):
    T, D = x2.shape
    tps = IN_PROJ_TILES_PER_STEP
    n_tiles = w_in.shape[1] // tn
    n_steps = -(-n_tiles // tps)
    tile = lambda t: (lambda i, s: (0, jnp.minimum(s * tps + t, n_tiles - 1)))
    return pl.pallas_call(
        functools.partial(_in_proj_kernel, tn=tn, n_tiles=n_tiles),
        grid=(T // tm, n_steps),
        in_specs=[pl.BlockSpec((tm, D), lambda i, s: (i, 0)), pl.BlockSpec((1, D), lambda i, s: (0, 0))]
                 + [pl.BlockSpec((D, tn), tile(t)) for t in range(tps)]
                 + [pl.BlockSpec((1, tn), tile(t)) for t in range(tps)],
        out_specs=pl.BlockSpec((tm, tps * tn), lambda i, s: (i, s)),
        out_shape=jax.ShapeDtypeStruct((T, n_steps * tps * tn), jnp.bfloat16),
        scratch_shapes=[pltpu.VMEM((tm, D), jnp.bfloat16), pltpu.VMEM((tps, D, tn), jnp.bfloat16)],
        compiler_params=pltpu.CompilerParams(
            dimension_semantics=("arbitrary", "arbitrary"), vmem_limit_bytes=VMEM_LIMIT),
        name="in_proj",
    )(x2, g_mix, *([w_in] * tps), *([gains] * tps))


CHUNK = DIL_PATTERNS[-1][1] * A_BLOCK
BLOCKS_PER_CHUNK = CHUNK // A_BLOCK
DIL_STRIDE = 4


def _class_rows(src_ref, mid_ref, r):
    n = CHUNK // r
    if r <= DIL_STRIDE:
        return [src_ref[pl.ds(rr, n, stride=r), :] for rr in range(r)]
    r1, r2, n1 = DIL_STRIDE, r // DIL_STRIDE, CHUNK // DIL_STRIDE
    assert r2 <= DIL_STRIDE
    for a in range(r1):
        mid_ref[a * n1:(a + 1) * n1, :] = src_ref[pl.ds(a, n1, stride=r1), :]
    return [mid_ref[pl.ds((rr % r1) * n1 + rr // r1, n, stride=r2), :] for rr in range(r)]


def _store_class_rows(dst_ref, mid_ref, rows_of_class, r):
    n = CHUNK // r
    if r == 1:
        dst_ref[...] = rows_of_class[0]
    elif r <= DIL_STRIDE:
        for rr, rows in enumerate(rows_of_class):
            dst_ref[pl.ds(rr, n, stride=r), :] = rows
    else:
        r1, r2, n1 = DIL_STRIDE, r // DIL_STRIDE, CHUNK // DIL_STRIDE
        for rr, rows in enumerate(rows_of_class):
            mid_ref[pl.ds((rr % r1) * n1 + rr // r1, n, stride=r2), :] = rows
        for a in range(r1):
            dst_ref[pl.ds(a, n1, stride=r1), :] = mid_ref[a * n1:(a + 1) * n1, :]


def _deinterleave(dst_ref, first_block, src_ref, stage_ref, mid_ref, r):
    bpc = BLOCKS_PER_CHUNK // r
    if r == 1:
        dst_ref[0, first_block:first_block + bpc] = src_ref[...].reshape(bpc, A_BLOCK, A_OUT)
        return
    for h in range(A_HEADS_PER_GROUP):
        cols = slice(h * HEAD_DIM, (h + 1) * HEAD_DIM)
        stage_ref[h] = src_ref[:, cols].astype(jnp.float32)
        for rr, rows in enumerate(_class_rows(stage_ref.at[h], mid_ref, r)):
            dst_ref[rr, first_block:first_block + bpc, :, cols] = (
                rows.astype(dst_ref.dtype).reshape(bpc, A_BLOCK, HEAD_DIM))


def _dilated_kernel(q_ref, k_ref, v_ref, bias_ref, o_ref, lse_ref, stage_ref, mid_ref, qd_ref, kd_ref, vd_ref, *, r):
    c = pl.program_id(1)
    bpc = BLOCKS_PER_CHUNK // r
    nblk = BLOCKS_PER_CHUNK

    @pl.when(c == 0)
    def _():
        kd_ref[:, 0] = jnp.zeros((r, A_BLOCK, A_OUT), kd_ref.dtype)
        vd_ref[:, 0] = jnp.zeros((r, A_BLOCK, A_OUT), vd_ref.dtype)

    @pl.when(c > 0)
    def _():
        kd_ref[:, 0] = kd_ref[:, bpc]
        vd_ref[:, 0] = vd_ref[:, bpc]

    _deinterleave(qd_ref, 0, q_ref, stage_ref, mid_ref, r)
    _deinterleave(kd_ref, 1, k_ref, stage_ref, mid_ref, r)
    _deinterleave(vd_ref, 1, v_ref, stage_ref, mid_ref, r)

    blk = lax.broadcasted_iota(jnp.int32, (nblk, A_BLOCK, 2 * A_BLOCK), 0)
    key = lax.broadcasted_iota(jnp.int32, (nblk, A_BLOCK, 2 * A_BLOCK), 2)
    no_prev = (lax.rem(blk, bpc) == 0) & (key < A_BLOCK) & (c == 0)
    lane = lax.broadcasted_iota(jnp.int32, (nblk, A_BLOCK, HEAD_DIM), 2)
    bqk = (((2,), (2,)), ((0,), (0,)))
    bqd = (((2,), (1,)), ((0,), (0,)))
    class_rows = lambda t: [t[rr * bpc:(rr + 1) * bpc].reshape(bpc * A_BLOCK, HEAD_DIM) for rr in range(r)]
    lse = jnp.zeros((nblk, A_BLOCK, HEAD_DIM), jnp.float32)
    for h in range(A_HEADS_PER_GROUP):
        cols = slice(h * HEAD_DIM, (h + 1) * HEAD_DIM)
        q = qd_ref[:, :, :, cols].reshape(nblk, A_BLOCK, HEAD_DIM)
        k2 = jnp.concatenate([kd_ref[:, 0:bpc, :, cols], kd_ref[:, 1:bpc + 1, :, cols]], axis=2)
        v2 = jnp.concatenate([vd_ref[:, 0:bpc, :, cols], vd_ref[:, 1:bpc + 1, :, cols]], axis=2)
        k2 = k2.reshape(nblk, 2 * A_BLOCK, HEAD_DIM)
        v2 = v2.reshape(nblk, 2 * A_BLOCK, HEAD_DIM)
        s = lax.dot_general(q, k2, bqk, preferred_element_type=jnp.float32) + bias_ref[0, h]
        s = jnp.where(no_prev, NEG, s)
        m = jnp.max(s, axis=-1, keepdims=True)
        p = jnp.exp2(s - m)
        den = jnp.sum(p, axis=-1, keepdims=True)
        o = lax.dot_general(p.astype(jnp.bfloat16), v2, bqd, preferred_element_type=jnp.float32) / den
        lse = jnp.where(lane == h, m + jnp.log2(den), lse)
        _store_class_rows(stage_ref.at[h], mid_ref, class_rows(o), r)
        o_ref[:, cols] = stage_ref[h].astype(o_ref.dtype)
    _store_class_rows(lse_ref, mid_ref, class_rows(lse), r)


def _dilated_group(proj2, bias, group, *, batch, seq):
    _, r = DIL_PATTERNS[group]
    nc = seq // CHUNK
    bpc = BLOCKS_PER_CHUNK // r
    chunk = lambda sec: pl.BlockSpec((CHUNK, A_OUT), lambda b, c: (b * nc + c, sec * N_GROUPS + group))
    return pl.pallas_call(
        functools.partial(_dilated_kernel, r=r),
        grid=(batch, nc),
        in_specs=[chunk(0), chunk(1), chunk(2),
                  pl.BlockSpec((1, A_HEADS_PER_GROUP, A_BLOCK, 2 * A_BLOCK), lambda b, c: (group, 0, 0, 0))],
        out_specs=[pl.BlockSpec((CHUNK, A_OUT), lambda b, c: (b * nc + c, 0)),
                   pl.BlockSpec((CHUNK, HEAD_DIM), lambda b, c: (b * nc + c, 0))],
        out_shape=[jax.ShapeDtypeStruct((batch * seq, A_OUT), jnp.bfloat16),
                   jax.ShapeDtypeStruct((batch * seq, HEAD_DIM), jnp.float32)],
        scratch_shapes=[pltpu.VMEM((A_HEADS_PER_GROUP, CHUNK, HEAD_DIM), jnp.float32),
                        pltpu.VMEM((CHUNK, HEAD_DIM), jnp.float32),
                        pltpu.VMEM((r, bpc, A_BLOCK, A_OUT), jnp.bfloat16),
                        pltpu.VMEM((r, bpc + 1, A_BLOCK, A_OUT), jnp.bfloat16),
                        pltpu.VMEM((r, bpc + 1, A_BLOCK, A_OUT), jnp.bfloat16)],
        compiler_params=pltpu.CompilerParams(
            dimension_semantics=("arbitrary", "arbitrary"), vmem_limit_bytes=VMEM_LIMIT),
        name=f"dilated_{group}",
    )(proj2, proj2, proj2, bias)


def _split3(x):
    hi = x.astype(jnp.bfloat16)
    r1 = x - hi.astype(jnp.float32)
    mid = r1.astype(jnp.bfloat16)
    lo = (r1 - mid.astype(jnp.float32)).astype(jnp.bfloat16)
    return hi, mid, lo


MOBA_STREAMS = 3


def _moba_items(nb, group):
    streams = [[] for _ in range(MOBA_STREAMS)]
    for i in sorted(range(nb), key=lambda i: -(i // group)):
        min(streams, key=len).extend((i, g) for g in range(i // group + 1))
    assert len({len(s) for s in streams}) == 1 and len(streams[0]) % 2 == 0
    return streams


def _moba_kernel(items_ref, q_ref, k_ref, v_ref, bias_ref, o_ref, avg_ref, kmean_ref, ka_ref, qa_ref, va_ref, s_ref,
                 st_ref, *, nb, group, n_items):
    BS = MOBA_BLOCK
    hd = HEAD_DIM
    seq = nb * BS
    rows_per_group = group * BS
    nt = (((1,), (1,)), ((), ()))

    row = lax.broadcasted_iota(jnp.int32, (nb, seq), 0)
    col = lax.broadcasted_iota(jnp.int32, (nb, seq), 1)

    @pl.when((pl.program_id(0) == 0) & (pl.program_id(1) == 0))
    def _():
        lo_edge = row * BS
        avg_ref[...] = jnp.where((col >= lo_edge) & (col < lo_edge + BS), 1.0 / BS, 0.0).astype(avg_ref.dtype)
        qa_ref[hd + nb:, :] = jnp.zeros((hd - nb, seq), qa_ref.dtype)
        key_blk = lax.broadcasted_iota(jnp.int32, (seq, hd), 0) // BS
        key_lane = lax.broadcasted_iota(jnp.int32, (seq, hd), 1)
        ka_ref[:, hd:] = jnp.where(key_lane == key_blk, 1.0, 0.0).astype(ka_ref.dtype)
        aux_row = lax.broadcasted_iota(jnp.int32, (MOBA_AUX, seq), 0)
        va_ref[hd:, :] = jnp.where(aux_row == 0, 1.0, 0.0).astype(va_ref.dtype)

    kmean = jnp.dot(avg_ref[...], k_ref[...], preferred_element_type=jnp.float32)
    for t, part in enumerate(_split3(kmean)):
        kmean_ref[t] = part

    q_all = q_ref[...]
    gate = sum(lax.dot_general(kmean_ref[t], q_all, nt, preferred_element_type=jnp.float32) for t in range(3))
    own = col // BS
    past = row < own
    gate = jnp.where(past, gate, -jnp.inf)
    sub = lax.broadcasted_iota(jnp.int32, (8, seq), 0)
    ranks = []
    for r0 in range(0, nb, 8):
        g8 = gate[r0:r0 + 8, :]
        rank8 = jnp.zeros((8, seq), jnp.int32)
        for m in range(nb):
            gm = gate[m:m + 1, :]
            if m >= r0 + 7:
                beats = gm > g8
            elif m < r0:
                beats = gm >= g8
            else:
                beats = (gm > g8) | ((gm == g8) & (sub > m - r0))
            rank8 = rank8 + beats.astype(jnp.int32)
        ranks.append(rank8)
    rank = jnp.concatenate(ranks, axis=0)
    sel = jnp.where((past & (rank < MOBA_TOPK)) | (row == own), 0.0, NEG)

    qa_ref[:hd, :] = q_all.T
    qa_ref[hd:hd + nb, :] = sel.astype(qa_ref.dtype)
    ka_ref[:, :hd] = k_ref[...]
    va_ref[:hd, :] = v_ref[...].T

    def logits(st, w, slot):
        i, g = items_ref[st, w, 0], items_ref[st, w, 1]
        rows = pl.ds(pl.multiple_of(g * rows_per_group, rows_per_group), rows_per_group)
        qcols = pl.ds(pl.multiple_of(i * BS, BS), BS)
        s_t = jnp.dot(ka_ref[rows, :], qa_ref[:, qcols], preferred_element_type=jnp.float32)
        m8 = jnp.full((8, BS), -jnp.inf, jnp.float32)
        for u in range(group):
            tile = jnp.clip(i - (g * group + u), 0, MOBA_BIAS_TILES - 1)
            s_u = s_t[u * BS:(u + 1) * BS] + bias_ref[0, tile]
            s_ref[2 * st + slot, u * BS:(u + 1) * BS, :] = s_u
            m8 = jnp.maximum(m8, jnp.max(s_u.reshape(BS // 8, 8, BS), axis=0))
        return m8

    def accumulate(st, w, slot, m8, state):
        m_run, acc = state
        i, g = items_ref[st, w, 0], items_ref[st, w, 1]
        rows = pl.ds(pl.multiple_of(g * rows_per_group, rows_per_group), rows_per_group)
        m_prev = jnp.where(g == 0, -jnp.inf, m_run)
        m_new = jnp.maximum(m_prev, jnp.max(m8, axis=0, keepdims=True))
        alpha = jnp.exp2(m_prev - m_new)
        p = jnp.exp2(s_ref[2 * st + slot] - m_new).astype(jnp.bfloat16)
        acc = alpha * acc + jnp.dot(va_ref[:, rows], p, preferred_element_type=jnp.float32)
        st_ref[i] = acc
        return m_new, acc

    streams = range(MOBA_STREAMS)

    def pair(t, carry):
        m8a, state = carry
        w = 2 * t
        m8b = [logits(st, w + 1, 1) for st in streams]
        state = [accumulate(st, w, 0, m8a[st], state[st]) for st in streams]
        m8a = [logits(st, jnp.minimum(w + 2, n_items - 1), 0) for st in streams]
        state = [accumulate(st, w + 1, 1, m8b[st], state[st]) for st in streams]
        return m8a, state

    state0 = (jnp.full((1, BS), -jnp.inf, jnp.float32), jnp.zeros((hd + MOBA_AUX, BS), jnp.float32))
    lax.fori_loop(0, n_items // 2, pair, ([logits(st, 0, 0) for st in streams], [state0 for _ in streams]))

    for i in range(nb):
        acc = st_ref[i]
        o_ref[i * BS:(i + 1) * BS, :] = (acc[:hd] / acc[hd:hd + 1]).T.astype(o_ref.dtype)


def _moba(proj2, bias, *, batch, seq):
    BS = MOBA_BLOCK
    nb = seq // BS
    assert nb <= HEAD_DIM
    c0 = A_QKV // HEAD_DIM
    group = math.gcd(nb, MOBA_GROUP)
    items = _moba_items(nb, group)
    head_spec = lambda sec: pl.BlockSpec((seq, HEAD_DIM), lambda b, h, items: (b, c0 + sec * B_HEADS + h))
    return pl.pallas_call(
        functools.partial(_moba_kernel, nb=nb, group=group, n_items=len(items[0])),
        grid_spec=pltpu.PrefetchScalarGridSpec(
            num_scalar_prefetch=1,
            grid=(batch, B_HEADS),
            in_specs=[head_spec(0), head_spec(1), head_spec(2),
                      pl.BlockSpec((1, MOBA_BIAS_TILES, BS, BS), lambda b, h, items: (h, 0, 0, 0))],
            out_specs=pl.BlockSpec((seq, HEAD_DIM), lambda b, h, items: (b, h)),
            scratch_shapes=[pltpu.VMEM((nb, seq), jnp.bfloat16),
                            pltpu.VMEM((3, nb, HEAD_DIM), jnp.bfloat16),
                            pltpu.VMEM((seq, 2 * HEAD_DIM), jnp.bfloat16),
                            pltpu.VMEM((2 * HEAD_DIM, seq), jnp.bfloat16),
                            pltpu.VMEM((HEAD_DIM + MOBA_AUX, seq), jnp.bfloat16),
                            pltpu.VMEM((2 * MOBA_STREAMS, group * BS, BS), jnp.float32),
                            pltpu.VMEM((nb, HEAD_DIM + MOBA_AUX, BS), jnp.float32)]),
        out_shape=jax.ShapeDtypeStruct((batch * seq, B_WIDTH), jnp.bfloat16),
        compiler_params=pltpu.CompilerParams(
            dimension_semantics=("arbitrary", "arbitrary"), vmem_limit_bytes=VMEM_LIMIT),
        name="moba",
    )(jnp.asarray(items, jnp.int32), proj2, proj2, proj2, bias)


MERGE_ROW_CHUNKS = 2


def _merge_kernel(o0_ref, o1_ref, o2_ref, l0_ref, l1_ref, l2_ref, yb_ref, x_ref,
                  wa_ref, wb_ref, wo_ref, g_mlp_ref, *refs):
    gate_refs, (out_ref, h_ref) = refs[:-2], refs[-2:]
    ga_refs, gb_refs = gate_refs[:len(gate_refs) // 2], gate_refs[len(gate_refs) // 2:]
    rc = out_ref.shape[0] // MERGE_ROW_CHUNKS
    for c in range(MERGE_ROW_CHUNKS):
        rows = slice(c * rc, (c + 1) * rc)
        l0, l1, l2 = l0_ref[rows, :], l1_ref[rows, :], l2_ref[rows, :]
        mx = jnp.maximum(jnp.maximum(l0, l1), l2)
        e0, e1, e2 = jnp.exp2(l0 - mx), jnp.exp2(l1 - mx), jnp.exp2(l2 - mx)
        inv = 1.0 / (e0 + e1 + e2)
        parts = []
        for h in range(A_HEADS_PER_GROUP):
            cols = slice(h * HEAD_DIM, (h + 1) * HEAD_DIM)
            ya = ((e0 * inv)[:, h:h + 1] * o0_ref[rows, cols].astype(jnp.float32)
                  + (e1 * inv)[:, h:h + 1] * o1_ref[rows, cols].astype(jnp.float32)
                  + (e2 * inv)[:, h:h + 1] * o2_ref[rows, cols].astype(jnp.float32))
            parts.append(ya.astype(jnp.bfloat16))
        ya = jnp.concatenate(parts, axis=-1)
        pa = jnp.dot(ya, wa_ref[...], preferred_element_type=jnp.float32)
        pb = jnp.dot(yb_ref[rows, :], wb_ref[...], preferred_element_type=jnp.float32)
        ga = jnp.concatenate([g[rows, :] for g in ga_refs], axis=-1).astype(jnp.float32)
        gb = jnp.concatenate([g[rows, :] for g in gb_refs], axis=-1).astype(jnp.float32)
        mixed = ga * pa + gb * pb
        out = x_ref[rows, :] + jnp.dot(mixed.astype(jnp.bfloat16), wo_ref[...], preferred_element_type=jnp.float32)
        out_ref[rows, :] = out
        h_ref[rows, :] = _rms_rows(out, g_mlp_ref[...]).astype(h_ref.dtype)


def _merge(outs, lses, yb, proj2, x2, wa, wb, wo, g_mlp, *, tm):
    T, D = x2.shape
    gw = math.gcd(QKV_WIDTH, D)
    n_gate_blocks = 2 * D // gw
    row = lambda w: pl.BlockSpec((tm, w), lambda i: (i, 0))
    full = lambda a: pl.BlockSpec(a.shape, lambda i: (0, 0))
    gate = lambda c: pl.BlockSpec((tm, gw), lambda i: (i, QKV_WIDTH // gw + c))
    return pl.pallas_call(
        _merge_kernel,
        grid=(T // tm,),
        in_specs=[row(A_OUT)] * 3 + [row(HEAD_DIM)] * 3 + [row(B_WIDTH), row(D),
                  full(wa), full(wb), full(wo), full(g_mlp)] + [gate(c) for c in range(n_gate_blocks)],
        out_specs=[row(D), row(D)],
        out_shape=[jax.ShapeDtypeStruct((T, D), jnp.float32), jax.ShapeDtypeStruct((T, D), jnp.bfloat16)],
        compiler_params=pltpu.CompilerParams(dimension_semantics=("arbitrary",), vmem_limit_bytes=VMEM_LIMIT),
        name="merge",
    )(*outs, *lses, yb, x2, wa, wb, wo, g_mlp, *([proj2] * n_gate_blocks))


MLP_ROW_CHUNKS = 1
MLP_K_CHUNKS = 4


def _mlp_kernel(h_ref, x_ref, wu_ref, wd_ref, o_ref, wubf_ref, wdbf_ref):
    f = pl.program_id(1)

    @pl.when(f == 0)
    def _():
        o_ref[...] = jnp.zeros(o_ref.shape, o_ref.dtype)

    slab = x_ref.shape[1]
    cols = pl.ds(pl.multiple_of(f * slab, slab), slab)
    o_ref[:, cols] += x_ref[...]

    tm, d = h_ref.shape
    rc = tm // MLP_ROW_CHUNKS
    kc = d // MLP_K_CHUNKS
    for c in range(MLP_ROW_CHUNKS):
        rows = slice(c * rc, (c + 1) * rc)
        u = None
        for k in range(MLP_K_CHUNKS):
            ks = slice(k * kc, (k + 1) * kc)
            if c == 0:
                wubf_ref[ks, :] = wu_ref[ks, :].astype(wubf_ref.dtype)
                wdbf_ref[:, ks] = wd_ref[:, ks].astype(wdbf_ref.dtype)
            part = jnp.dot(h_ref[rows, ks], wubf_ref[ks, :], preferred_element_type=jnp.float32)
            u = part if u is None else u + part
        u = jnp.square(jnp.maximum(u, 0.0)).astype(jnp.bfloat16)
        o_ref[rows, :] += jnp.dot(u, wdbf_ref[...], preferred_element_type=jnp.float32)


def _mlp(h2, x2, w_up, w_down, *, tm, tf):
    T, D = x2.shape
    F = w_up.shape[1]
    n_f = F // tf
    slab = D // n_f
    assert slab % HEAD_DIM == 0
    return pl.pallas_call(
        _mlp_kernel,
        grid=(T // tm, n_f),
        in_specs=[pl.BlockSpec((tm, D), lambda i, f: (i, 0)),
                  pl.BlockSpec((tm, slab), lambda i, f: (i, f)),
                  pl.BlockSpec((D, tf), lambda i, f: (0, f)),
                  pl.BlockSpec((tf, D), lambda i, f: (f, 0))],
        out_specs=pl.BlockSpec((tm, D), lambda i, f: (i, 0)),
        out_shape=jax.ShapeDtypeStruct((T, D), jnp.float32),
        scratch_shapes=[pltpu.VMEM((D, tf), jnp.bfloat16), pltpu.VMEM((tf, D), jnp.bfloat16)],
        compiler_params=pltpu.CompilerParams(
            dimension_semantics=("arbitrary", "arbitrary"), vmem_limit_bytes=VMEM_LIMIT),
        name="mlp",
    )(h2, x2, w_up, w_down)


def _column_gains(q_norm_a, k_norm_a, q_norm_b, k_norm_b, width):
    qs = SCALE * LOG2E
    gains = jnp.concatenate([
        jnp.tile(q_norm_a * qs, A_HEADS), jnp.tile(k_norm_a, A_HEADS), jnp.ones((A_HEADS * HEAD_DIM,), jnp.float32),
        jnp.tile(q_norm_b * qs, B_HEADS), jnp.tile(k_norm_b, B_HEADS), jnp.ones((B_WIDTH,), jnp.float32),
        jnp.ones((width - QKV_WIDTH,), jnp.float32)])
    return gains.reshape(1, width)


def kernel(x, g_mix, w_in, q_norm_a, k_norm_a, q_norm_b, k_norm_b, rel_bias,
           w_branch_a, w_branch_b, w_out, g_mlp, w_up, w_down):
    batch, seq, d_model = x.shape
    T = batch * seq
    assert w_in.shape == (d_model, QKV_WIDTH + 2 * d_model)
    assert seq % (DIL_PATTERNS[-1][1] * A_BLOCK) == 0
    bf16 = jnp.bfloat16
    x2 = x.reshape(T, d_model)
    tm = min(1024, T)

    dil_bias, moba_bias = _bias_tables(rel_bias)
    gains = _column_gains(q_norm_a, k_norm_a, q_norm_b, k_norm_b, w_in.shape[1])
    proj = _in_proj(x2, g_mix.reshape(1, -1), w_in, gains, tm=tm, tn=min(512, d_model * 2))

    outs, lses = zip(*[_dilated_group(proj, dil_bias, g, batch=batch, seq=seq) for g in range(N_GROUPS)])
    yb = _moba(proj, moba_bias, batch=batch, seq=seq)

    x_mid, h_mid = _merge(outs, lses, yb, proj, x2, w_branch_a.astype(bf16), w_branch_b.astype(bf16),
                          w_out.astype(bf16), g_mlp.reshape(1, -1), tm=min(512, T))
    y = _mlp(h_mid, x_mid, w_up, w_down, tm=tm, tf=min(512, w_up.shape[1]))
    return y.reshape(batch, seq, d_model)
```

```python
import functools
import math

import jax
import jax.numpy as jnp
from jax import lax
from jax.experimental import pallas as pl
from jax.experimental.pallas import tpu as pltpu

HEAD_DIM = 128
DIL_PATTERNS = ((128, 1), (512, 4), (2048, 16))
N_GROUPS = len(DIL_PATTERNS)
A_HEADS_PER_GROUP = 4
A_HEADS = A_HEADS_PER_GROUP * N_GROUPS
A_OUT = A_HEADS_PER_GROUP * HEAD_DIM
A_BLOCK = 128
B_HEADS = 8
B_WIDTH = B_HEADS * HEAD_DIM
MOBA_BLOCK = 256
MOBA_TOPK = 3
MOBA_GROUP = 2
MOBA_AUX = 16
N_BUCKETS = 32
MAX_DISTANCE = 2048
EPS = 1e-6
SCALE = HEAD_DIM ** -0.5
A_QKV = 3 * A_HEADS * HEAD_DIM
B_QKV = 3 * B_WIDTH
QKV_WIDTH = A_QKV + B_QKV
LOG2E = math.log2(math.e)
NEG = -1e30
VMEM_LIMIT = 56 * 1024 * 1024
ROW_TILE = 1024
COL_TILE = 512
MERGE_ROW_TILE = 512


def _bucket_thresholds():
    max_exact = N_BUCKETS // 2

    def bucket(d):
        if d < max_exact:
            return d
        v = int(math.log(d / max_exact) / math.log(MAX_DISTANCE / max_exact) * (N_BUCKETS - max_exact))
        return min(max_exact + v, N_BUCKETS - 1)

    thr, d = [0], 0
    for b in range(1, N_BUCKETS):
        while bucket(d) < b:
            d += 1
        thr.append(d)
    return tuple(thr)


BUCKET_THRESHOLDS = _bucket_thresholds()
MOBA_BIAS_TILES = -(-(BUCKET_THRESHOLDS[-1] - 1) // MOBA_BLOCK) + 2


def _bias_lookup(dist, tab_ref, col, dmin=0, dmax=None):
    first = sum(dmin >= t for t in BUCKET_THRESHOLDS[1:])
    val = jnp.full(dist.shape, tab_ref[first, col], jnp.float32)
    for b in range(first + 1, N_BUCKETS):
        if dmax is None or BUCKET_THRESHOLDS[b] <= dmax:
            val = jnp.where(dist >= BUCKET_THRESHOLDS[b], tab_ref[b, col], val)
    return val * LOG2E


def _dil_bias_kernel(tab_ref, o_ref):
    g, h = pl.program_id(0), pl.program_id(1)
    a = lax.broadcasted_iota(jnp.int32, (A_BLOCK, 2 * A_BLOCK), 0)
    j = lax.broadcasted_iota(jnp.int32, (A_BLOCK, 2 * A_BLOCK), 1)
    delta = a + A_BLOCK - j
    dilation = lax.shift_left(jnp.int32(1), 2 * g)
    val = _bias_lookup(delta * dilation, tab_ref, g * A_HEADS_PER_GROUP + h)
    o_ref[0, 0] = jnp.where((delta >= 0) & (delta <= A_BLOCK), val, NEG)


def _moba_bias_kernel(tab_ref, o_ref):
    h = pl.program_id(0)
    j = lax.broadcasted_iota(jnp.int32, (MOBA_BLOCK, MOBA_BLOCK), 0)
    s = lax.broadcasted_iota(jnp.int32, (MOBA_BLOCK, MOBA_BLOCK), 1)
    for c in range(MOBA_BIAS_TILES):
        dist = c * MOBA_BLOCK + s - j
        lo, hi = (c - 1) * MOBA_BLOCK + 1, (c + 1) * MOBA_BLOCK - 1
        val = _bias_lookup(dist, tab_ref, A_HEADS + h, max(lo, 0), hi)
        o_ref[0, c] = jnp.where(dist >= 0, val, NEG)


def _bias_tables(rel_bias):
    smem = pl.BlockSpec(memory_space=pltpu.SMEM)
    dil = pl.pallas_call(
        _dil_bias_kernel,
        grid=(N_GROUPS, A_HEADS_PER_GROUP),
        in_specs=[smem],
        out_specs=pl.BlockSpec((1, 1, A_BLOCK, 2 * A_BLOCK), lambda g, h: (g, h, 0, 0)),
        out_shape=jax.ShapeDtypeStruct((N_GROUPS, A_HEADS_PER_GROUP, A_BLOCK, 2 * A_BLOCK), jnp.float32),
        name="dil_bias",
    )(rel_bias)
    moba = pl.pallas_call(
        _moba_bias_kernel,
        grid=(B_HEADS,),
        in_specs=[smem],
        out_specs=pl.BlockSpec((1, MOBA_BIAS_TILES, MOBA_BLOCK, MOBA_BLOCK), lambda h: (h, 0, 0, 0)),
        out_shape=jax.ShapeDtypeStruct((B_HEADS, MOBA_BIAS_TILES, MOBA_BLOCK, MOBA_BLOCK), jnp.float32),
        name="moba_bias",
    )(rel_bias)
    return dil, moba


IN_PROJ_ROW_CHUNKS = 2
IN_PROJ_K_CHUNKS = 4
IN_PROJ_TILES_PER_STEP = 2


def _rms_rows(x, g):
    return (x * lax.rsqrt(jnp.mean(x * x, axis=-1, keepdims=True) + EPS)) * g


def _in_proj_tile_kind(col0):
    if col0 < 2 * A_HEADS * HEAD_DIM or A_QKV <= col0 < A_QKV + 2 * B_WIDTH:
        return "norm"
    return "plain" if col0 < QKV_WIDTH else "gate"


def _in_proj_kernel(x_ref, g_ref, *refs, tn, n_tiles):
    tps = IN_PROJ_TILES_PER_STEP
    w_refs, gain_refs, (o_ref, h_ref, wbf_ref) = refs[:tps], refs[tps:2 * tps], refs[2 * tps:]
    s = pl.program_id(1)

    @pl.when(s == 0)
    def _():
        h_ref[...] = _rms_rows(x_ref[...], g_ref[...]).astype(h_ref.dtype)

    def head_norm(acc, gain_ref):
        return jnp.concatenate(
            [_rms_rows(acc[:, c * HEAD_DIM:(c + 1) * HEAD_DIM], gain_ref[:, c * HEAD_DIM:(c + 1) * HEAD_DIM])
             for c in range(tn // HEAD_DIM)], axis=-1)

    epilogues = {
        "norm": head_norm,
        "plain": lambda acc, gain_ref: acc,
        "gate": lambda acc, gain_ref: 0.5 * jnp.tanh(0.5 * acc) + 0.5,
    }

    def project(kinds):
        tm, d = h_ref.shape
        rc = tm // IN_PROJ_ROW_CHUNKS
        kc = d // IN_PROJ_K_CHUNKS
        for c in range(IN_PROJ_ROW_CHUNKS):
            rows = slice(c * rc, (c + 1) * rc)
            for t, kind in enumerate(kinds):
                acc = None
                for k in range(IN_PROJ_K_CHUNKS):
                    ks = slice(k * kc, (k + 1) * kc)
                    if c == 0:
                        wbf_ref[t, ks, :] = w_refs[t][ks, :].astype(wbf_ref.dtype)
                    part = jnp.dot(h_ref[rows, ks], wbf_ref[t, ks, :], preferred_element_type=jnp.float32)
                    acc = part if acc is None else acc + part
                o_ref[rows, t * tn:(t + 1) * tn] = epilogues[kind](acc, gain_refs[t]).astype(o_ref.dtype)

    steps_of = {}
    for step in range(-(-n_tiles // tps)):
        tiles = [min(step * tps + t, n_tiles - 1) for t in range(tps)]
        steps_of.setdefault(tuple(_in_proj_tile_kind(tile * tn) for tile in tiles), []).append(step)
    for kinds, steps in steps_of.items():
        @pl.when(functools.reduce(jnp.logical_or, [s == step for step in steps]))
        def _(kinds=kinds):
            project(kinds)


def _in_proj(x2, g_mix, w_in, gains, *, tm, tn):
    T, D = x2.shape
    tps = IN_PROJ_TILES_PER_STEP
    n_tiles = w_in.shape[1] // tn
    n_steps = -(-n_tiles // tps)
    tile = lambda t: (lambda i, s: (0, jnp.minimum(s * tps + t, n_tiles - 1)))
    return pl.pallas_call(
        functools.partial(_in_proj_kernel, tn=tn, n_tiles=n_tiles),
        grid=(T // tm, n_steps),
        in_specs=[pl.BlockSpec((tm, D), lambda i, s: (i, 0)), pl.BlockSpec((1, D), lambda i, s: (0, 0))]
                 + [pl.BlockSpec((D, tn), tile(t)) for t in range(tps)]
                 + [pl.BlockSpec((1, tn), tile(t)) for t in range(tps)],
        out_specs=pl.BlockSpec((tm, tps * tn), lambda i, s: (i, s)),
        out_shape=jax.ShapeDtypeStruct((T, n_steps * tps * tn), jnp.bfloat16),
        scratch_shapes=[pltpu.VMEM((tm, D), jnp.bfloat16), pltpu.VMEM((tps, D, tn), jnp.bfloat16)],
        compiler_params=pltpu.CompilerParams(
            dimension_semantics=("arbitrary", "arbitrary"), vmem_limit_bytes=VMEM_LIMIT),
        name="in_proj",
    )(x2, g_mix, *([w_in] * tps), *([gains] * tps))


CHUNK = DIL_PATTERNS[-1][1] * A_BLOCK
BLOCKS_PER_CHUNK = CHUNK // A_BLOCK
DIL_STRIDE = 4


def _class_rows(src_ref, mid_ref, r):
    n = CHUNK // r
    if r <= DIL_STRIDE:
        return [src_ref[pl.ds(rr, n, stride=r), :] for rr in range(r)]
    r1, r2, n1 = DIL_STRIDE, r // DIL_STRIDE, CHUNK // DIL_STRIDE
    assert r2 <= DIL_STRIDE
    for a in range(r1):
        mid_ref[a * n1:(a + 1) * n1, :] = src_ref[pl.ds(a, n1, stride=r1), :]
    return [mid_ref[pl.ds((rr % r1) * n1 + rr // r1, n, stride=r2), :] for rr in range(r)]


def _store_class_rows(dst_ref, mid_ref, rows_of_class, r):
    n = CHUNK // r
    if r == 1:
        dst_ref[...] = rows_of_class[0]
    elif r <= DIL_STRIDE:
        for rr, rows in enumerate(rows_of_class):
            dst_ref[pl.ds(rr, n, stride=r), :] = rows
    else:
        r1, r2, n1 = DIL_STRIDE, r // DIL_STRIDE, CHUNK // DIL_STRIDE
        for rr, rows in enumerate(rows_of_class):
            mid_ref[pl.ds((rr % r1) * n1 + rr // r1, n, stride=r2), :] = rows
        for a in range(r1):
            dst_ref[pl.ds(a, n1, stride=r1), :] = mid_ref[a * n1:(a + 1) * n1, :]


def _deinterleave(dst_ref, first_block, src_ref, stage_ref, mid_ref, r):
    bpc = BLOCKS_PER_CHUNK // r
    if r == 1:
        dst_ref[0, first_block:first_block + bpc] = src_ref[...].reshape(bpc, A_BLOCK, A_OUT)
        return
    for h in range(A_HEADS_PER_GROUP):
        cols = slice(h * HEAD_DIM, (h + 1) * HEAD_DIM)
        stage_ref[h] = src_ref[:, cols].astype(jnp.float32)
        for rr, rows in enumerate(_class_rows(stage_ref.at[h], mid_ref, r)):
            dst_ref[rr, first_block:first_block + bpc, :, cols] = (
                rows.astype(dst_ref.dtype).reshape(bpc, A_BLOCK, HEAD_DIM))


def _dilated_kernel(q_ref, k_ref, v_ref, bias_ref, o_ref, lse_ref, stage_ref, mid_ref, qd_ref, kd_ref, vd_ref, *, r):
    c = pl.program_id(1)
    bpc = BLOCKS_PER_CHUNK // r
    nblk = BLOCKS_PER_CHUNK

    @pl.when(c == 0)
    def _():
        kd_ref[:, 0] = jnp.zeros((r, A_BLOCK, A_OUT), kd_ref.dtype)
        vd_ref[:, 0] = jnp.zeros((r, A_BLOCK, A_OUT), vd_ref.dtype)

    @pl.when(c > 0)
    def _():
        kd_ref[:, 0] = kd_ref[:, bpc]
        vd_ref[:, 0] = vd_ref[:, bpc]

    _deinterleave(qd_ref, 0, q_ref, stage_ref, mid_ref, r)
    _deinterleave(kd_ref, 1, k_ref, stage_ref, mid_ref, r)
    _deinterleave(vd_ref, 1, v_ref, stage_ref, mid_ref, r)

    blk = lax.broadcasted_iota(jnp.int32, (nblk, A_BLOCK, 2 * A_BLOCK), 0)
    key = lax.broadcasted_iota(jnp.int32, (nblk, A_BLOCK, 2 * A_BLOCK), 2)
    no_prev = (lax.rem(blk, bpc) == 0) & (key < A_BLOCK) & (c == 0)
    lane = lax.broadcasted_iota(jnp.int32, (nblk, A_BLOCK, HEAD_DIM), 2)
    bqk = (((2,), (2,)), ((0,), (0,)))
    bqd = (((2,), (1,)), ((0,), (0,)))
    class_rows = lambda t: [t[rr * bpc:(rr + 1) * bpc].reshape(bpc * A_BLOCK, HEAD_DIM) for rr in range(r)]
    lse = jnp.zeros((nblk, A_BLOCK, HEAD_DIM), jnp.float32)
    for h in range(A_HEADS_PER_GROUP):
        cols = slice(h * HEAD_DIM, (h + 1) * HEAD_DIM)
        q = qd_ref[:, :, :, cols].reshape(nblk, A_BLOCK, HEAD_DIM)
        k2 = jnp.concatenate([kd_ref[:, 0:bpc, :, cols], kd_ref[:, 1:bpc + 1, :, cols]], axis=2)
        v2 = jnp.concatenate([vd_ref[:, 0:bpc, :, cols], vd_ref[:, 1:bpc + 1, :, cols]], axis=2)
        k2 = k2.reshape(nblk, 2 * A_BLOCK, HEAD_DIM)
        v2 = v2.reshape(nblk, 2 * A_BLOCK, HEAD_DIM)
        s = lax.dot_general(q, k2, bqk, preferred_element_type=jnp.float32) + bias_ref[0, h]
        s = jnp.where(no_prev, NEG, s)
        m = jnp.max(s, axis=-1, keepdims=True)
        p = jnp.exp2(s - m)
        den = jnp.sum(p, axis=-1, keepdims=True)
        o = lax.dot_general(p.astype(jnp.bfloat16), v2, bqd, preferred_element_type=jnp.float32) / den
        lse = jnp.where(lane == h, m + jnp.log2(den), lse)
        _store_class_rows(stage_ref.at[h], mid_ref, class_rows(o), r)
        o_ref[:, cols] = stage_ref[h].astype(o_ref.dtype)
    _store_class_rows(lse_ref, mid_ref, class_rows(lse), r)


def _dilated_group(proj2, bias, group, *, batch, seq):
    _, r = DIL_PATTERNS[group]
    nc = seq // CHUNK
    bpc = BLOCKS_PER_CHUNK // r
    chunk = lambda sec: pl.BlockSpec((CHUNK, A_OUT), lambda b, c: (b * nc + c, sec * N_GROUPS + group))
    return pl.pallas_call(
        functools.partial(_dilated_kernel, r=r),
        grid=(batch, nc),
        in_specs=[chunk(0), chunk(1), chunk(2),
                  pl.BlockSpec((1, A_HEADS_PER_GROUP, A_BLOCK, 2 * A_BLOCK), lambda b, c: (group, 0, 0, 0))],
        out_specs=[pl.BlockSpec((CHUNK, A_OUT), lambda b, c: (b * nc + c, 0)),
                   pl.BlockSpec((CHUNK, HEAD_DIM), lambda b, c: (b * nc + c, 0))],
        out_shape=[jax.ShapeDtypeStruct((batch * seq, A_OUT), jnp.bfloat16),
                   jax.ShapeDtypeStruct((batch * seq, HEAD_DIM), jnp.float32)],
        scratch_shapes=[pltpu.VMEM((A_HEADS_PER_GROUP, CHUNK, HEAD_DIM), jnp.float32),
                        pltpu.VMEM((CHUNK, HEAD_DIM), jnp.float32),
                        pltpu.VMEM((r, bpc, A_BLOCK, A_OUT), jnp.bfloat16),
                        pltpu.VMEM((r, bpc + 1, A_BLOCK, A_OUT), jnp.bfloat16),
                        pltpu.VMEM((r, bpc + 1, A_BLOCK, A_OUT), jnp.bfloat16)],
        compiler_params=pltpu.CompilerParams(
            dimension_semantics=("arbitrary", "arbitrary"), vmem_limit_bytes=VMEM_LIMIT),
        name=f"dilated_{group}",
    )(proj2, proj2, proj2, bias)


def _split3(x):
    hi = x.astype(jnp.bfloat16)
    r1 = x - hi.astype(jnp.float32)
    mid = r1.astype(jnp.bfloat16)
    lo = (r1 - mid.astype(jnp.float32)).astype(jnp.bfloat16)
    return hi, mid, lo


MOBA_STREAMS = 3


def _moba_items(nb, group):
    streams = [[] for _ in range(MOBA_STREAMS)]
    for i in sorted(range(nb), key=lambda i: -(i // group)):
        min(streams, key=len).extend((i, g) for g in range(i // group + 1))
    assert len({len(s) for s in streams}) == 1 and len(streams[0]) % 2 == 0
    return streams


def _moba_kernel(items_ref, q_ref, k_ref, v_ref, bias_ref, o_ref, avg_ref, kmean_ref, ka_ref, qa_ref, va_ref, s_ref,
                 st_ref, *, nb, group, n_items):
    BS = MOBA_BLOCK
    hd = HEAD_DIM
    seq = nb * BS
    rows_per_group = group * BS
    nt = (((1,), (1,)), ((), ()))

    row = lax.broadcasted_iota(jnp.int32, (nb, seq), 0)
    col = lax.broadcasted_iota(jnp.int32, (nb, seq), 1)

    @pl.when((pl.program_id(0) == 0) & (pl.program_id(1) == 0))
    def _():
        lo_edge = row * BS
        avg_ref[...] = jnp.where((col >= lo_edge) & (col < lo_edge + BS), 1.0 / BS, 0.0).astype(avg_ref.dtype)
        qa_ref[hd + nb:, :] = jnp.zeros((hd - nb, seq), qa_ref.dtype)
        key_blk = lax.broadcasted_iota(jnp.int32, (seq, hd), 0) // BS
        key_lane = lax.broadcasted_iota(jnp.int32, (seq, hd), 1)
        ka_ref[:, hd:] = jnp.where(key_lane == key_blk, 1.0, 0.0).astype(ka_ref.dtype)
        aux_row = lax.broadcasted_iota(jnp.int32, (MOBA_AUX, seq), 0)
        va_ref[hd:, :] = jnp.where(aux_row == 0, 1.0, 0.0).astype(va_ref.dtype)

    kmean = jnp.dot(avg_ref[...], k_ref[...], preferred_element_type=jnp.float32)
    for t, part in enumerate(_split3(kmean)):
        kmean_ref[t] = part

    q_all = q_ref[...]
    gate3 = lax.dot_general(kmean_ref[...].reshape(3 * nb, hd), q_all, nt, preferred_element_type=jnp.float32)
    gate = gate3[:nb] + gate3[nb:2 * nb] + gate3[2 * nb:]
    own = col // BS
    past = row < own
    gate = jnp.where(past, gate, -jnp.inf)
    sub = lax.broadcasted_iota(jnp.int32, (8, seq), 0)
    ranks = []
    for r0 in range(0, nb, 8):
        g8 = gate[r0:r0 + 8, :]
        rank8 = jnp.zeros((8, seq), jnp.int32)
        for m in range(nb):
            gm = gate[m:m + 1, :]
            if m >= r0 + 7:
                beats = gm > g8
            elif m < r0:
                beats = gm >= g8
            else:
                beats = (gm > g8) | ((gm == g8) & (sub > m - r0))
            rank8 = rank8 + beats.astype(jnp.int32)
        ranks.append(rank8)
    rank = jnp.concatenate(ranks, axis=0)
    sel = jnp.where((past & (rank < MOBA_TOPK)) | (row == own), 0.0, NEG)

    qa_ref[:hd, :] = q_all.T
    qa_ref[hd:hd + nb, :] = sel.astype(qa_ref.dtype)
    ka_ref[:, :hd] = k_ref[...]
    va_ref[:hd, :] = v_ref[...].T

    def logits(st, w, slot):
        i, g = items_ref[st, w, 0], items_ref[st, w, 1]
        rows = pl.ds(pl.multiple_of(g * rows_per_group, rows_per_group), rows_per_group)
        qcols = pl.ds(pl.multiple_of(i * BS, BS), BS)
        s_t = jnp.dot(ka_ref[rows, :], qa_ref[:, qcols], preferred_element_type=jnp.float32)
        m8 = jnp.full((8, BS), -jnp.inf, jnp.float32)
        for u in range(group):
            tile = jnp.clip(i - (g * group + u), 0, MOBA_BIAS_TILES - 1)
            s_u = s_t[u * BS:(u + 1) * BS] + bias_ref[0, tile]
            s_ref[2 * st + slot, u * BS:(u + 1) * BS, :] = s_u
            m8 = jnp.maximum(m8, jnp.max(s_u.reshape(BS // 8, 8, BS), axis=0))
        return m8

    def accumulate(st, w, slot, m8, state):
        m_run, acc = state
        i, g = items_ref[st, w, 0], items_ref[st, w, 1]
        rows = pl.ds(pl.multiple_of(g * rows_per_group, rows_per_group), rows_per_group)
        m_prev = jnp.where(g == 0, -jnp.inf, m_run)
        m_new = jnp.maximum(m_prev, jnp.max(m8, axis=0, keepdims=True))
        alpha = jnp.exp2(m_prev - m_new)
        p = jnp.exp2(s_ref[2 * st + slot] - m_new).astype(jnp.bfloat16)
        acc = alpha * acc + jnp.dot(va_ref[:, rows], p, preferred_element_type=jnp.float32)
        st_ref[i] = acc
        return m_new, acc

    streams = range(MOBA_STREAMS)

    def pair(t, carry):
        m8a, state = carry
        w = 2 * t
        m8b = [logits(st, w + 1, 1) for st in streams]
        state = [accumulate(st, w, 0, m8a[st], state[st]) for st in streams]
        m8a = [logits(st, jnp.minimum(w + 2, n_items - 1), 0) for st in streams]
        state = [accumulate(st, w + 1, 1, m8b[st], state[st]) for st in streams]
        return m8a, state

    state0 = (jnp.full((1, BS), -jnp.inf, jnp.float32), jnp.zeros((hd + MOBA_AUX, BS), jnp.float32))
    lax.fori_loop(0, n_items // 2, pair, ([logits(st, 0, 0) for st in streams], [state0 for _ in streams]))

    for i in range(nb):
        acc = st_ref[i]
        o_ref[i * BS:(i + 1) * BS, :] = (acc[:hd] / acc[hd:hd + 1]).T.astype(o_ref.dtype)


def _moba(proj2, bias, *, batch, seq):
    BS = MOBA_BLOCK
    nb = seq // BS
    assert nb <= HEAD_DIM
    c0 = A_QKV // HEAD_DIM
    group = math.gcd(nb, MOBA_GROUP)
    items = _moba_items(nb, group)
    head_spec = lambda sec: pl.BlockSpec((seq, HEAD_DIM), lambda b, h, items: (b, c0 + sec * B_HEADS + h))
    return pl.pallas_call(
        functools.partial(_moba_kernel, nb=nb, group=group, n_items=len(items[0])),
        grid_spec=pltpu.PrefetchScalarGridSpec(
            num_scalar_prefetch=1,
            grid=(batch, B_HEADS),
            in_specs=[head_spec(0), head_spec(1), head_spec(2),
                      pl.BlockSpec((1, MOBA_BIAS_TILES, BS, BS), lambda b, h, items: (h, 0, 0, 0))],
            out_specs=pl.BlockSpec((seq, HEAD_DIM), lambda b, h, items: (b, h)),
            scratch_shapes=[pltpu.VMEM((nb, seq), jnp.bfloat16),
                            pltpu.VMEM((3, nb, HEAD_DIM), jnp.bfloat16),
                            pltpu.VMEM((seq, 2 * HEAD_DIM), jnp.bfloat16),
                            pltpu.VMEM((2 * HEAD_DIM, seq), jnp.bfloat16),
                            pltpu.VMEM((HEAD_DIM + MOBA_AUX, seq), jnp.bfloat16),
                            pltpu.VMEM((2 * MOBA_STREAMS, group * BS, BS), jnp.float32),
                            pltpu.VMEM((nb, HEAD_DIM + MOBA_AUX, BS), jnp.float32)]),
        out_shape=jax.ShapeDtypeStruct((batch * seq, B_WIDTH), jnp.bfloat16),
        compiler_params=pltpu.CompilerParams(
            dimension_semantics=("arbitrary", "arbitrary"), vmem_limit_bytes=VMEM_LIMIT),
        name="moba",
    )(jnp.asarray(items, jnp.int32), proj2, proj2, proj2, bias)


MERGE_ROW_CHUNKS = 2


def _merge_kernel(o0_ref, o1_ref, o2_ref, l0_ref, l1_ref, l2_ref, yb_ref, x_ref,
                  wa_ref, wb_ref, wo_ref, g_mlp_ref, *refs):
    gate_refs, (out_ref, h_ref) = refs[:-2], refs[-2:]
    ga_refs, gb_refs = gate_refs[:len(gate_refs) // 2], gate_refs[len(gate_refs) // 2:]
    rc = out_ref.shape[0] // MERGE_ROW_CHUNKS
    for c in range(MERGE_ROW_CHUNKS):
        rows = slice(c * rc, (c + 1) * rc)
        l0, l1, l2 = l0_ref[rows, :], l1_ref[rows, :], l2_ref[rows, :]
        mx = jnp.maximum(jnp.maximum(l0, l1), l2)
        e0, e1, e2 = jnp.exp2(l0 - mx), jnp.exp2(l1 - mx), jnp.exp2(l2 - mx)
        inv = 1.0 / (e0 + e1 + e2)
        parts = []
        for h in range(A_HEADS_PER_GROUP):
            cols = slice(h * HEAD_DIM, (h + 1) * HEAD_DIM)
            ya = ((e0 * inv)[:, h:h + 1] * o0_ref[rows, cols].astype(jnp.float32)
                  + (e1 * inv)[:, h:h + 1] * o1_ref[rows, cols].astype(jnp.float32)
                  + (e2 * inv)[:, h:h + 1] * o2_ref[rows, cols].astype(jnp.float32))
            parts.append(ya.astype(jnp.bfloat16))
        ya = jnp.concatenate(parts, axis=-1)
        pa = jnp.dot(ya, wa_ref[...], preferred_element_type=jnp.float32)
        pb = jnp.dot(yb_ref[rows, :], wb_ref[...], preferred_element_type=jnp.float32)
        ga = jnp.concatenate([g[rows, :] for g in ga_refs], axis=-1).astype(jnp.float32)
        gb = jnp.concatenate([g[rows, :] for g in gb_refs], axis=-1).astype(jnp.float32)
        mixed = ga * pa + gb * pb
        out = x_ref[rows, :] + jnp.dot(mixed.astype(jnp.bfloat16), wo_ref[...], preferred_element_type=jnp.float32)
        out_ref[rows, :] = out
        h_ref[rows, :] = _rms_rows(out, g_mlp_ref[...]).astype(h_ref.dtype)


def _merge(outs, lses, yb, proj2, x2, wa, wb, wo, g_mlp, *, tm):
    T, D = x2.shape
    gw = math.gcd(QKV_WIDTH, D)
    n_gate_blocks = 2 * D // gw
    row = lambda w: pl.BlockSpec((tm, w), lambda i: (i, 0))
    full = lambda a: pl.BlockSpec(a.shape, lambda i: (0, 0))
    gate = lambda c: pl.BlockSpec((tm, gw), lambda i: (i, QKV_WIDTH // gw + c))
    return pl.pallas_call(
        _merge_kernel,
        grid=(T // tm,),
        in_specs=[row(A_OUT)] * 3 + [row(HEAD_DIM)] * 3 + [row(B_WIDTH), row(D),
                  full(wa), full(wb), full(wo), full(g_mlp)] + [gate(c) for c in range(n_gate_blocks)],
        out_specs=[row(D), row(D)],
        out_shape=[jax.ShapeDtypeStruct((T, D), jnp.float32), jax.ShapeDtypeStruct((T, D), jnp.bfloat16)],
        compiler_params=pltpu.CompilerParams(dimension_semantics=("arbitrary",), vmem_limit_bytes=VMEM_LIMIT),
        name="merge",
    )(*outs, *lses, yb, x2, wa, wb, wo, g_mlp, *([proj2] * n_gate_blocks))


MLP_ROW_CHUNKS = 1
MLP_K_CHUNKS = 4


def _mlp_kernel(h_ref, x_ref, wu_ref, wd_ref, o_ref, wubf_ref, wdbf_ref):
    f = pl.program_id(1)

    @pl.when(f == 0)
    def _():
        o_ref[...] = jnp.zeros(o_ref.shape, o_ref.dtype)

    slab = x_ref.shape[1]
    cols = pl.ds(pl.multiple_of(f * slab, slab), slab)
    o_ref[:, cols] += x_ref[...]

    tm, d = h_ref.shape
    rc = tm // MLP_ROW_CHUNKS
    kc = d // MLP_K_CHUNKS
    for c in range(MLP_ROW_CHUNKS):
        rows = slice(c * rc, (c + 1) * rc)
        u = None
        for k in range(MLP_K_CHUNKS):
            ks = slice(k * kc, (k + 1) * kc)
            if c == 0:
                wubf_ref[ks, :] = wu_ref[ks, :].astype(wubf_ref.dtype)
                wdbf_ref[:, ks] = wd_ref[:, ks].astype(wdbf_ref.dtype)
            part = jnp.dot(h_ref[rows, ks], wubf_ref[ks, :], preferred_element_type=jnp.float32)
            u = part if u is None else u + part
        u = jnp.square(jnp.maximum(u, 0.0)).astype(jnp.bfloat16)
        o_ref[rows, :] += jnp.dot(u, wdbf_ref[...], preferred_element_type=jnp.float32)


def _mlp(h2, x2, w_up, w_down, *, tm, tf):
    T, D = x2.shape
    F = w_up.shape[1]
    n_f = F // tf
    slab = D // n_f
    assert slab % HEAD_DIM == 0
    return pl.pallas_call(
        _mlp_kernel,
        grid=(T // tm, n_f),
        in_specs=[pl.BlockSpec((tm, D), lambda i, f: (i, 0)),
                  pl.BlockSpec((tm, slab), lambda i, f: (i, f)),
                  pl.BlockSpec((D, tf), lambda i, f: (0, f)),
                  pl.BlockSpec((tf, D), lambda i, f: (f, 0))],
        out_specs=pl.BlockSpec((tm, D), lambda i, f: (i, 0)),
        out_shape=jax.ShapeDtypeStruct((T, D), jnp.float32),
        scratch_shapes=[pltpu.VMEM((D, tf), jnp.bfloat16), pltpu.VMEM((tf, D), jnp.bfloat16)],
        compiler_params=pltpu.CompilerParams(
            dimension_semantics=("arbitrary", "arbitrary"), vmem_limit_bytes=VMEM_LIMIT),
        name="mlp",
    )(h2, x2, w_up, w_down)


def _column_gains(q_norm_a, k_norm_a, q_norm_b, k_norm_b, width):
    qs = SCALE * LOG2E
    gains = jnp.concatenate([
        jnp.tile(q_norm_a * qs, A_HEADS), jnp.tile(k_norm_a, A_HEADS), jnp.ones((A_HEADS * HEAD_DIM,), jnp.float32),
        jnp.tile(q_norm_b * qs, B_HEADS), jnp.tile(k_norm_b, B_HEADS), jnp.ones((B_WIDTH,), jnp.float32),
        jnp.ones((width - QKV_WIDTH,), jnp.float32)])
    return gains.reshape(1, width)


def kernel(x, g_mix, w_in, q_norm_a, k_norm_a, q_norm_b, k_norm_b, rel_bias,
           w_branch_a, w_branch_b, w_out, g_mlp, w_up, w_down):
    batch, seq, d_model = x.shape
    T = batch * seq
    assert w_in.shape == (d_model, QKV_WIDTH + 2 * d_model)
    assert seq % (DIL_PATTERNS[-1][1] * A_BLOCK) == 0
    bf16 = jnp.bfloat16
    x2 = x.reshape(T, d_model)
    tm = min(ROW_TILE, T)

    dil_bias, moba_bias = _bias_tables(rel_bias)
    gains = _column_gains(q_norm_a, k_norm_a, q_norm_b, k_norm_b, w_in.shape[1])
    proj = _in_proj(x2, g_mix.reshape(1, -1), w_in, gains, tm=tm, tn=min(COL_TILE, d_model * 2))

    outs, lses = zip(*[_dilated_group(proj, dil_bias, g, batch=batch, seq=seq) for g in range(N_GROUPS)])
    yb = _moba(proj, moba_bias, batch=batch, seq=seq)

    x_mid, h_mid = _merge(outs, lses, yb, proj, x2, w_branch_a.astype(bf16), w_branch_b.astype(bf16),
                          w_out.astype(bf16), g_mlp.reshape(1, -1), tm=min(MERGE_ROW_TILE, T))
    y = _mlp(h_mid, x_mid, w_up, w_down, tm=tm, tf=min(COL_TILE, w_up.shape[1]))
    return y.reshape(batch, seq, d_model)
```

```python
import functools
import math

import jax
import jax.numpy as jnp
from jax import lax
from jax.experimental import pallas as pl
from jax.experimental.pallas import tpu as pltpu

HEAD_DIM = 128
DIL_PATTERNS = ((128, 1), (512, 4), (2048, 16))
N_GROUPS = len(DIL_PATTERNS)
A_HEADS_PER_GROUP = 4
A_HEADS = A_HEADS_PER_GROUP * N_GROUPS
A_OUT = A_HEADS_PER_GROUP * HEAD_DIM
A_BLOCK = 128
B_HEADS = 8
B_WIDTH = B_HEADS * HEAD_DIM
MOBA_BLOCK = 256
MOBA_TOPK = 3
MOBA_GROUP = 2
MOBA_AUX = 16
N_BUCKETS = 32
MAX_DISTANCE = 2048
EPS = 1e-6
SCALE = HEAD_DIM ** -0.5
A_QKV = 3 * A_HEADS * HEAD_DIM
B_QKV = 3 * B_WIDTH
QKV_WIDTH = A_QKV + B_QKV
LOG2E = math.log2(math.e)
NEG = -1e30
VMEM_LIMIT = 56 * 1024 * 1024
ROW_TILE = 1024
COL_TILE = 512
MERGE_ROW_TILE = 512


def _bucket_thresholds():
    max_exact = N_BUCKETS // 2

    def bucket(d):
        if d < max_exact:
            return d
        v = int(math.log(d / max_exact) / math.log(MAX_DISTANCE / max_exact) * (N_BUCKETS - max_exact))
        return min(max_exact + v, N_BUCKETS - 1)

    thr, d = [0], 0
    for b in range(1, N_BUCKETS):
        while bucket(d) < b:
            d += 1
        thr.append(d)
    return tuple(thr)


BUCKET_THRESHOLDS = _bucket_thresholds()
MOBA_BIAS_TILES = -(-(BUCKET_THRESHOLDS[-1] - 1) // MOBA_BLOCK) + 2


def _bias_lookup(dist, tab_ref, col, dmin=0, dmax=None):
    first = sum(dmin >= t for t in BUCKET_THRESHOLDS[1:])
    val = jnp.full(dist.shape, tab_ref[first, col], jnp.float32)
    for b in range(first + 1, N_BUCKETS):
        if dmax is None or BUCKET_THRESHOLDS[b] <= dmax:
            val = jnp.where(dist >= BUCKET_THRESHOLDS[b], tab_ref[b, col], val)
    return val * LOG2E


def _dil_bias_kernel(tab_ref, o_ref):
    g, h = pl.program_id(0), pl.program_id(1)
    a = lax.broadcasted_iota(jnp.int32, (A_BLOCK, 2 * A_BLOCK), 0)
    j = lax.broadcasted_iota(jnp.int32, (A_BLOCK, 2 * A_BLOCK), 1)
    delta = a + A_BLOCK - j
    dilation = lax.shift_left(jnp.int32(1), 2 * g)
    val = _bias_lookup(delta * dilation, tab_ref, g * A_HEADS_PER_GROUP + h)
    o_ref[0, 0] = jnp.where((delta >= 0) & (delta <= A_BLOCK), val, NEG)


def _moba_bias_kernel(tab_ref, o_ref):
    h = pl.program_id(0)
    j = lax.broadcasted_iota(jnp.int32, (MOBA_BLOCK, MOBA_BLOCK), 0)
    s = lax.broadcasted_iota(jnp.int32, (MOBA_BLOCK, MOBA_BLOCK), 1)
    for c in range(MOBA_BIAS_TILES):
        dist = c * MOBA_BLOCK + s - j
        lo, hi = (c - 1) * MOBA_BLOCK + 1, (c + 1) * MOBA_BLOCK - 1
        val = _bias_lookup(dist, tab_ref, A_HEADS + h, max(lo, 0), hi)
        o_ref[0, c] = jnp.where(dist >= 0, val, NEG)


def _bias_tables(rel_bias):
    smem = pl.BlockSpec(memory_space=pltpu.SMEM)
    dil = pl.pallas_call(
        _dil_bias_kernel,
        grid=(N_GROUPS, A_HEADS_PER_GROUP),
        in_specs=[smem],
        out_specs=pl.BlockSpec((1, 1, A_BLOCK, 2 * A_BLOCK), lambda g, h: (g, h, 0, 0)),
        out_shape=jax.ShapeDtypeStruct((N_GROUPS, A_HEADS_PER_GROUP, A_BLOCK, 2 * A_BLOCK), jnp.float32),
        name="dil_bias",
    )(rel_bias)
    moba = pl.pallas_call(
        _moba_bias_kernel,
        grid=(B_HEADS,),
        in_specs=[smem],
        out_specs=pl.BlockSpec((1, MOBA_BIAS_TILES, MOBA_BLOCK, MOBA_BLOCK), lambda h: (h, 0, 0, 0)),
        out_shape=jax.ShapeDtypeStruct((B_HEADS, MOBA_BIAS_TILES, MOBA_BLOCK, MOBA_BLOCK), jnp.float32),
        name="moba_bias",
    )(rel_bias)
    return dil, moba


IN_PROJ_ROW_CHUNKS = 2
IN_PROJ_K_CHUNKS = 4
IN_PROJ_TILES_PER_STEP = 2


def _rms_rows(x, g):
    return (x * lax.rsqrt(jnp.mean(x * x, axis=-1, keepdims=True) + EPS)) * g


def _in_proj_tile_kind(col0):
    if col0 < 2 * A_HEADS * HEAD_DIM or A_QKV <= col0 < A_QKV + 2 * B_WIDTH:
        return "norm"
    return "plain" if col0 < QKV_WIDTH else "gate"


def _in_proj_kernel(x_ref, g_ref, *refs, tn, n_tiles):
    tps = IN_PROJ_TILES_PER_STEP
    w_refs, gain_refs, (o_ref, h_ref, wbf_ref) = refs[:tps], refs[tps:2 * tps], refs[2 * tps:]
    s = pl.program_id(1)

    def head_norm(acc, gain_ref):
        return jnp.concatenate(
            [_rms_rows(acc[:, c * HEAD_DIM:(c + 1) * HEAD_DIM], gain_ref[:, c * HEAD_DIM:(c + 1) * HEAD_DIM])
             for c in range(tn // HEAD_DIM)], axis=-1)

    epilogues = {
        "norm": head_norm,
        "plain": lambda acc, gain_ref: acc,
        "gate": lambda acc, gain_ref: 0.5 * jnp.tanh(0.5 * acc) + 0.5,
    }

    def project(kinds, normalise_input):
        tm, d = h_ref.shape
        rc = tm // IN_PROJ_ROW_CHUNKS
        kc = d // IN_PROJ_K_CHUNKS
        for c in range(IN_PROJ_ROW_CHUNKS):
            rows = slice(c * rc, (c + 1) * rc)
            if normalise_input:
                h_ref[rows, :] = _rms_rows(x_ref[rows, :], g_ref[...]).astype(h_ref.dtype)
            for t, kind in enumerate(kinds):
                acc = None
                for k in range(IN_PROJ_K_CHUNKS):
                    ks = slice(k * kc, (k + 1) * kc)
                    if c == 0:
                        wbf_ref[t, ks, :] = w_refs[t][ks, :].astype(wbf_ref.dtype)
                    part = jnp.dot(h_ref[rows, ks], wbf_ref[t, ks, :], preferred_element_type=jnp.float32)
                    acc = part if acc is None else acc + part
                o_ref[rows, t * tn:(t + 1) * tn] = epilogues[kind](acc, gain_refs[t]).astype(o_ref.dtype)

    steps_of = {}
    for step in range(-(-n_tiles // tps)):
        tiles = [min(step * tps + t, n_tiles - 1) for t in range(tps)]
        kinds = tuple(_in_proj_tile_kind(tile * tn) for tile in tiles)
        steps_of.setdefault((kinds, step == 0), []).append(step)
    for (kinds, first), steps in steps_of.items():
        @pl.when(functools.reduce(jnp.logical_or, [s == step for step in steps]))
        def _(kinds=kinds, first=first):
            project(kinds, first)


def _in_proj(x2, g_mix, w_in, gains, *, tm, tn):
    T, D = x2.shape
    tps = IN_PROJ_TILES_PER_STEP
    n_tiles = w_in.shape[1] // tn
    n_steps = -(-n_tiles // tps)
    tile = lambda t: (lambda i, s: (0, jnp.minimum(s * tps + t, n_tiles - 1)))
    return pl.pallas_call(
        functools.partial(_in_proj_kernel, tn=tn, n_tiles=n_tiles),
        grid=(T // tm, n_steps),
        in_specs=[pl.BlockSpec((tm, D), lambda i, s: (i, 0)), pl.BlockSpec((1, D), lambda i, s: (0, 0))]
                 + [pl.BlockSpec((D, tn), tile(t)) for t in range(tps)]
                 + [pl.BlockSpec((1, tn), tile(t)) for t in range(tps)],
        out_specs=pl.BlockSpec((tm, tps * tn), lambda i, s: (i, s)),
        out_shape=jax.ShapeDtypeStruct((T, n_steps * tps * tn), jnp.bfloat16),
        scratch_shapes=[pltpu.VMEM((tm, D), jnp.bfloat16), pltpu.VMEM((tps, D, tn), jnp.bfloat16)],
        compiler_params=pltpu.CompilerParams(
            dimension_semantics=("arbitrary", "arbitrary"), vmem_limit_bytes=VMEM_LIMIT),
        name="in_proj",
    )(x2, g_mix, *([w_in] * tps), *([gains] * tps))


CHUNK = DIL_PATTERNS[-1][1] * A_BLOCK
BLOCKS_PER_CHUNK = CHUNK // A_BLOCK
DIL_STRIDE = 4


def _class_rows(src_ref, mid_ref, r):
    n = CHUNK // r
    if r <= DIL_STRIDE:
        return [src_ref[pl.ds(rr, n, stride=r), :] for rr in range(r)]
    r1, r2, n1 = DIL_STRIDE, r // DIL_STRIDE, CHUNK // DIL_STRIDE
    assert r2 <= DIL_STRIDE
    for a in range(r1):
        mid_ref[a * n1:(a + 1) * n1, :] = src_ref[pl.ds(a, n1, stride=r1), :]
    return [mid_ref[pl.ds((rr % r1) * n1 + rr // r1, n, stride=r2), :] for rr in range(r)]


def _store_class_rows(dst_ref, mid_ref, rows_of_class, r):
    n = CHUNK // r
    if r == 1:
        dst_ref[...] = rows_of_class[0]
    elif r <= DIL_STRIDE:
        for rr, rows in enumerate(rows_of_class):
            dst_ref[pl.ds(rr, n, stride=r), :] = rows
    else:
        r1, r2, n1 = DIL_STRIDE, r // DIL_STRIDE, CHUNK // DIL_STRIDE
        for rr, rows in enumerate(rows_of_class):
            mid_ref[pl.ds((rr % r1) * n1 + rr // r1, n, stride=r2), :] = rows
        for a in range(r1):
            dst_ref[pl.ds(a, n1, stride=r1), :] = mid_ref[a * n1:(a + 1) * n1, :]


def _deinterleave(dst_ref, first_block, src_ref, stage_ref, mid_ref, r):
    bpc = BLOCKS_PER_CHUNK // r
    if r == 1:
        dst_ref[0, first_block:first_block + bpc] = src_ref[...].reshape(bpc, A_BLOCK, A_OUT)
        return
    for h in range(A_HEADS_PER_GROUP):
        cols = slice(h * HEAD_DIM, (h + 1) * HEAD_DIM)
        stage_ref[h] = src_ref[:, cols].astype(jnp.float32)
        for rr, rows in enumerate(_class_rows(stage_ref.at[h], mid_ref, r)):
            dst_ref[rr, first_block:first_block + bpc, :, cols] = (
                rows.astype(dst_ref.dtype).reshape(bpc, A_BLOCK, HEAD_DIM))


def _dilated_kernel(q_ref, k_ref, v_ref, bias_ref, o_ref, lse_ref, stage_ref, mid_ref, qd_ref, kd_ref, vd_ref, *, r):
    c = pl.program_id(1)
    bpc = BLOCKS_PER_CHUNK // r
    nblk = BLOCKS_PER_CHUNK

    @pl.when(c == 0)
    def _():
        kd_ref[:, 0] = jnp.zeros((r, A_BLOCK, A_OUT), kd_ref.dtype)
        vd_ref[:, 0] = jnp.zeros((r, A_BLOCK, A_OUT), vd_ref.dtype)

    @pl.when(c > 0)
    def _():
        kd_ref[:, 0] = kd_ref[:, bpc]
        vd_ref[:, 0] = vd_ref[:, bpc]

    _deinterleave(qd_ref, 0, q_ref, stage_ref, mid_ref, r)
    _deinterleave(kd_ref, 1, k_ref, stage_ref, mid_ref, r)
    _deinterleave(vd_ref, 1, v_ref, stage_ref, mid_ref, r)

    blk = lax.broadcasted_iota(jnp.int32, (nblk, A_BLOCK, 2 * A_BLOCK), 0)
    key = lax.broadcasted_iota(jnp.int32, (nblk, A_BLOCK, 2 * A_BLOCK), 2)
    no_prev = (lax.rem(blk, bpc) == 0) & (key < A_BLOCK) & (c == 0)
    lane = lax.broadcasted_iota(jnp.int32, (nblk, A_BLOCK, HEAD_DIM), 2)
    bqk = (((2,), (2,)), ((0,), (0,)))
    bqd = (((2,), (1,)), ((0,), (0,)))
    class_rows = lambda t: [t[rr * bpc:(rr + 1) * bpc].reshape(bpc * A_BLOCK, HEAD_DIM) for rr in range(r)]
    lse = jnp.zeros((nblk, A_BLOCK, HEAD_DIM), jnp.float32)
    ones = jnp.ones((nblk, 2 * A_BLOCK, HEAD_DIM), jnp.bfloat16)
    for h in range(A_HEADS_PER_GROUP):
        cols = slice(h * HEAD_DIM, (h + 1) * HEAD_DIM)
        q = qd_ref[:, :, :, cols].reshape(nblk, A_BLOCK, HEAD_DIM)
        k2 = jnp.concatenate([kd_ref[:, 0:bpc, :, cols], kd_ref[:, 1:bpc + 1, :, cols]], axis=2)
        v2 = jnp.concatenate([vd_ref[:, 0:bpc, :, cols], vd_ref[:, 1:bpc + 1, :, cols]], axis=2)
        k2 = k2.reshape(nblk, 2 * A_BLOCK, HEAD_DIM)
        v2 = v2.reshape(nblk, 2 * A_BLOCK, HEAD_DIM)
        s = lax.dot_general(q, k2, bqk, preferred_element_type=jnp.float32) + bias_ref[0, h]
        s = jnp.where(no_prev, NEG, s)
        m = jnp.max(s, axis=-1, keepdims=True)
        p = jnp.exp2(s - m).astype(jnp.bfloat16)
        pv = lax.dot_general(p, jnp.concatenate([v2, ones], axis=-1), bqd, preferred_element_type=jnp.float32)
        den = pv[:, :, HEAD_DIM:]
        o = pv[:, :, :HEAD_DIM] / den
        lse = jnp.where(lane == h, m + jnp.log2(den), lse)
        _store_class_rows(stage_ref.at[h], mid_ref, class_rows(o), r)
        o_ref[:, cols] = stage_ref[h].astype(o_ref.dtype)
    _store_class_rows(lse_ref, mid_ref, class_rows(lse), r)


def _dilated_group(proj2, bias, group, *, batch, seq):
    _, r = DIL_PATTERNS[group]
    nc = seq // CHUNK
    bpc = BLOCKS_PER_CHUNK // r
    chunk = lambda sec: pl.BlockSpec((CHUNK, A_OUT), lambda b, c: (b * nc + c, sec * N_GROUPS + group))
    return pl.pallas_call(
        functools.partial(_dilated_kernel, r=r),
        grid=(batch, nc),
        in_specs=[chunk(0), chunk(1), chunk(2),
                  pl.BlockSpec((1, A_HEADS_PER_GROUP, A_BLOCK, 2 * A_BLOCK), lambda b, c: (group, 0, 0, 0))],
        out_specs=[pl.BlockSpec((CHUNK, A_OUT), lambda b, c: (b * nc + c, 0)),
                   pl.BlockSpec((CHUNK, HEAD_DIM), lambda b, c: (b * nc + c, 0))],
        out_shape=[jax.ShapeDtypeStruct((batch * seq, A_OUT), jnp.bfloat16),
                   jax.ShapeDtypeStruct((batch * seq, HEAD_DIM), jnp.float32)],
        scratch_shapes=[pltpu.VMEM((A_HEADS_PER_GROUP, CHUNK, HEAD_DIM), jnp.float32),
                        pltpu.VMEM((CHUNK, HEAD_DIM), jnp.float32),
                        pltpu.VMEM((r, bpc, A_BLOCK, A_OUT), jnp.bfloat16),
                        pltpu.VMEM((r, bpc + 1, A_BLOCK, A_OUT), jnp.bfloat16),
                        pltpu.VMEM((r, bpc + 1, A_BLOCK, A_OUT), jnp.bfloat16)],
        compiler_params=pltpu.CompilerParams(
            dimension_semantics=("arbitrary", "arbitrary"), vmem_limit_bytes=VMEM_LIMIT),
        name=f"dilated_{group}",
    )(proj2, proj2, proj2, bias)


def _split3(x):
    hi = x.astype(jnp.bfloat16)
    r1 = x - hi.astype(jnp.float32)
    mid = r1.astype(jnp.bfloat16)
    lo = (r1 - mid.astype(jnp.float32)).astype(jnp.bfloat16)
    return hi, mid, lo


MOBA_STREAMS = 3


def _moba_items(nb, group):
    streams = [[] for _ in range(MOBA_STREAMS)]
    for i in sorted(range(nb), key=lambda i: -(i // group)):
        min(streams, key=len).extend((i, g) for g in range(i // group + 1))
    assert len({len(s) for s in streams}) == 1 and len(streams[0]) % 2 == 0
    return streams


def _moba_kernel(items_ref, q_ref, k_ref, v_ref, bias_ref, o_ref, avg_ref, kmean_ref, ka_ref, qa_ref, va_ref, s_ref,
                 st_ref, *, nb, group, n_items):
    BS = MOBA_BLOCK
    hd = HEAD_DIM
    seq = nb * BS
    rows_per_group = group * BS
    nt = (((1,), (1,)), ((), ()))

    row = lax.broadcasted_iota(jnp.int32, (nb, seq), 0)
    col = lax.broadcasted_iota(jnp.int32, (nb, seq), 1)

    @pl.when((pl.program_id(0) == 0) & (pl.program_id(1) == 0))
    def _():
        lo_edge = row * BS
        avg_ref[...] = jnp.where((col >= lo_edge) & (col < lo_edge + BS), 1.0 / BS, 0.0).astype(avg_ref.dtype)
        qa_ref[hd + nb:, :] = jnp.zeros((hd - nb, seq), qa_ref.dtype)
        key_blk = lax.broadcasted_iota(jnp.int32, (seq, hd), 0) // BS
        key_lane = lax.broadcasted_iota(jnp.int32, (seq, hd), 1)
        ka_ref[:, hd:] = jnp.where(key_lane == key_blk, 1.0, 0.0).astype(ka_ref.dtype)
        aux_row = lax.broadcasted_iota(jnp.int32, (MOBA_AUX, seq), 0)
        va_ref[hd:, :] = jnp.where(aux_row == 0, 1.0, 0.0).astype(va_ref.dtype)

    kmean = jnp.dot(avg_ref[...], k_ref[...], preferred_element_type=jnp.float32)
    for t, part in enumerate(_split3(kmean)):
        kmean_ref[t] = part

    q_all = q_ref[...]
    gate3 = lax.dot_general(kmean_ref[...].reshape(3 * nb, hd), q_all, nt, preferred_element_type=jnp.float32)
    gate = gate3[:nb] + gate3[nb:2 * nb] + gate3[2 * nb:]
    own = col // BS
    past = row < own
    gate = jnp.where(past, gate, -jnp.inf)
    sub = lax.broadcasted_iota(jnp.int32, (8, seq), 0)
    ranks = []
    for r0 in range(0, nb, 8):
        g8 = gate[r0:r0 + 8, :]
        rank8 = jnp.zeros((8, seq), jnp.int32)
        for m in range(nb):
            gm = gate[m:m + 1, :]
            if m >= r0 + 7:
                beats = gm > g8
            elif m < r0:
                beats = gm >= g8
            else:
                beats = (gm > g8) | ((gm == g8) & (sub > m - r0))
            rank8 = rank8 + beats.astype(jnp.int32)
        ranks.append(rank8)
    rank = jnp.concatenate(ranks, axis=0)
    sel = jnp.where((past & (rank < MOBA_TOPK)) | (row == own), 0.0, NEG)

    qa_ref[:hd, :] = q_all.T
    qa_ref[hd:hd + nb, :] = sel.astype(qa_ref.dtype)
    ka_ref[:, :hd] = k_ref[...]
    va_ref[:hd, :] = v_ref[...].T

    def logits(st, w, slot):
        i, g = items_ref[st, w, 0], items_ref[st, w, 1]
        rows = pl.ds(pl.multiple_of(g * rows_per_group, rows_per_group), rows_per_group)
        qcols = pl.ds(pl.multiple_of(i * BS, BS), BS)
        s_t = jnp.dot(ka_ref[rows, :], qa_ref[:, qcols], preferred_element_type=jnp.float32)
        m8 = jnp.full((8, BS), -jnp.inf, jnp.float32)
        for u in range(group):
            tile = jnp.clip(i - (g * group + u), 0, MOBA_BIAS_TILES - 1)
            s_u = s_t[u * BS:(u + 1) * BS] + bias_ref[0, tile]
            s_ref[2 * st + slot, u * BS:(u + 1) * BS, :] = s_u
            m8 = jnp.maximum(m8, jnp.max(s_u.reshape(BS // 8, 8, BS), axis=0))
        return m8

    def accumulate(st, w, slot, m8, state):
        m_run, acc = state
        i, g = items_ref[st, w, 0], items_ref[st, w, 1]
        rows = pl.ds(pl.multiple_of(g * rows_per_group, rows_per_group), rows_per_group)
        m_prev = jnp.where(g == 0, -jnp.inf, m_run)
        m_new = jnp.maximum(m_prev, jnp.max(m8, axis=0, keepdims=True))
        alpha = jnp.exp2(m_prev - m_new)
        p = jnp.exp2(s_ref[2 * st + slot] - m_new).astype(jnp.bfloat16)
        acc = alpha * acc + jnp.dot(va_ref[:, rows], p, preferred_element_type=jnp.float32)
        st_ref[i] = acc
        return m_new, acc

    streams = range(MOBA_STREAMS)

    def pair(t, carry):
        m8a, state = carry
        w = 2 * t
        m8b = [logits(st, w + 1, 1) for st in streams]
        state = [accumulate(st, w, 0, m8a[st], state[st]) for st in streams]
        m8a = [logits(st, jnp.minimum(w + 2, n_items - 1), 0) for st in streams]
        state = [accumulate(st, w + 1, 1, m8b[st], state[st]) for st in streams]
        return m8a, state

    state0 = (jnp.full((1, BS), -jnp.inf, jnp.float32), jnp.zeros((hd + MOBA_AUX, BS), jnp.float32))
    lax.fori_loop(0, n_items // 2, pair, ([logits(st, 0, 0) for st in streams], [state0 for _ in streams]))

    for i in range(nb):
        acc = st_ref[i]
        o_ref[i * BS:(i + 1) * BS, :] = (acc[:hd] / acc[hd:hd + 1]).T.astype(o_ref.dtype)


def _moba(proj2, bias, *, batch, seq):
    BS = MOBA_BLOCK
    nb = seq // BS
    assert nb <= HEAD_DIM
    c0 = A_QKV // HEAD_DIM
    group = math.gcd(nb, MOBA_GROUP)
    items = _moba_items(nb, group)
    head_spec = lambda sec: pl.BlockSpec((seq, HEAD_DIM), lambda b, h, items: (b, c0 + sec * B_HEADS + h))
    return pl.pallas_call(
        functools.partial(_moba_kernel, nb=nb, group=group, n_items=len(items[0])),
        grid_spec=pltpu.PrefetchScalarGridSpec(
            num_scalar_prefetch=1,
            grid=(batch, B_HEADS),
            in_specs=[head_spec(0), head_spec(1), head_spec(2),
                      pl.BlockSpec((1, MOBA_BIAS_TILES, BS, BS), lambda b, h, items: (h, 0, 0, 0))],
            out_specs=pl.BlockSpec((seq, HEAD_DIM), lambda b, h, items: (b, h)),
            scratch_shapes=[pltpu.VMEM((nb, seq), jnp.bfloat16),
                            pltpu.VMEM((3, nb, HEAD_DIM), jnp.bfloat16),
                            pltpu.VMEM((seq, 2 * HEAD_DIM), jnp.bfloat16),
                            pltpu.VMEM((2 * HEAD_DIM, seq), jnp.bfloat16),
                            pltpu.VMEM((HEAD_DIM + MOBA_AUX, seq), jnp.bfloat16),
                            pltpu.VMEM((2 * MOBA_STREAMS, group * BS, BS), jnp.float32),
                            pltpu.VMEM((nb, HEAD_DIM + MOBA_AUX, BS), jnp.float32)]),
        out_shape=jax.ShapeDtypeStruct((batch * seq, B_WIDTH), jnp.bfloat16),
        compiler_params=pltpu.CompilerParams(
            dimension_semantics=("arbitrary", "arbitrary"), vmem_limit_bytes=VMEM_LIMIT),
        name="moba",
    )(jnp.asarray(items, jnp.int32), proj2, proj2, proj2, bias)


MERGE_ROW_CHUNKS = 2


def _merge_kernel(o0_ref, o1_ref, o2_ref, l0_ref, l1_ref, l2_ref, yb_ref, x_ref,
                  wa_ref, wb_ref, wo_ref, g_mlp_ref, *refs):
    gate_refs, (out_ref, h_ref) = refs[:-2], refs[-2:]
    ga_refs, gb_refs = gate_refs[:len(gate_refs) // 2], gate_refs[len(gate_refs) // 2:]
    rc = out_ref.shape[0] // MERGE_ROW_CHUNKS
    for c in range(MERGE_ROW_CHUNKS):
        rows = slice(c * rc, (c + 1) * rc)
        l0, l1, l2 = l0_ref[rows, :], l1_ref[rows, :], l2_ref[rows, :]
        mx = jnp.maximum(jnp.maximum(l0, l1), l2)
        e0, e1, e2 = jnp.exp2(l0 - mx), jnp.exp2(l1 - mx), jnp.exp2(l2 - mx)
        inv = 1.0 / (e0 + e1 + e2)
        parts = []
        for h in range(A_HEADS_PER_GROUP):
            cols = slice(h * HEAD_DIM, (h + 1) * HEAD_DIM)
            ya = ((e0 * inv)[:, h:h + 1] * o0_ref[rows, cols].astype(jnp.float32)
                  + (e1 * inv)[:, h:h + 1] * o1_ref[rows, cols].astype(jnp.float32)
                  + (e2 * inv)[:, h:h + 1] * o2_ref[rows, cols].astype(jnp.float32))
            parts.append(ya.astype(jnp.bfloat16))
        ya = jnp.concatenate(parts, axis=-1)
        pa = jnp.dot(ya, wa_ref[...], preferred_element_type=jnp.float32)
        pb = jnp.dot(yb_ref[rows, :], wb_ref[...], preferred_element_type=jnp.float32)
        ga = jnp.concatenate([g[rows, :] for g in ga_refs], axis=-1).astype(jnp.float32)
        gb = jnp.concatenate([g[rows, :] for g in gb_refs], axis=-1).astype(jnp.float32)
        mixed = ga * pa + gb * pb
        out = x_ref[rows, :] + jnp.dot(mixed.astype(jnp.bfloat16), wo_ref[...], preferred_element_type=jnp.float32)
        out_ref[rows, :] = out
        h_ref[rows, :] = _rms_rows(out, g_mlp_ref[...]).astype(h_ref.dtype)


def _merge(outs, lses, yb, proj2, x2, wa, wb, wo, g_mlp, *, tm):
    T, D = x2.shape
    gw = math.gcd(QKV_WIDTH, D)
    n_gate_blocks = 2 * D // gw
    row = lambda w: pl.BlockSpec((tm, w), lambda i: (i, 0))
    full = lambda a: pl.BlockSpec(a.shape, lambda i: (0, 0))
    gate = lambda c: pl.BlockSpec((tm, gw), lambda i: (i, QKV_WIDTH // gw + c))
    return pl.pallas_call(
        _merge_kernel,
        grid=(T // tm,),
        in_specs=[row(A_OUT)] * 3 + [row(HEAD_DIM)] * 3 + [row(B_WIDTH), row(D),
                  full(wa), full(wb), full(wo), full(g_mlp)] + [gate(c) for c in range(n_gate_blocks)],
        out_specs=[row(D), row(D)],
        out_shape=[jax.ShapeDtypeStruct((T, D), jnp.float32), jax.ShapeDtypeStruct((T, D), jnp.bfloat16)],
        compiler_params=pltpu.CompilerParams(dimension_semantics=("arbitrary",), vmem_limit_bytes=VMEM_LIMIT),
        name="merge",
    )(*outs, *lses, yb, x2, wa, wb, wo, g_mlp, *([proj2] * n_gate_blocks))


MLP_ROW_CHUNKS = 1
MLP_K_CHUNKS = 4


def _mlp_kernel(h_ref, x_ref, wu_ref, wd_ref, o_ref, wubf_ref, wdbf_ref):
    f = pl.program_id(1)

    @pl.when(f == 0)
    def _():
        o_ref[...] = jnp.zeros(o_ref.shape, o_ref.dtype)

    slab = x_ref.shape[1]
    cols = pl.ds(pl.multiple_of(f * slab, slab), slab)
    o_ref[:, cols] += x_ref[...]

    tm, d = h_ref.shape
    rc = tm // MLP_ROW_CHUNKS
    kc = d // MLP_K_CHUNKS
    for c in range(MLP_ROW_CHUNKS):
        rows = slice(c * rc, (c + 1) * rc)
        u = None
        for k in range(MLP_K_CHUNKS):
            ks = slice(k * kc, (k + 1) * kc)
            if c == 0:
                wubf_ref[ks, :] = wu_ref[ks, :].astype(wubf_ref.dtype)
                wdbf_ref[:, ks] = wd_ref[:, ks].astype(wdbf_ref.dtype)
            part = jnp.dot(h_ref[rows, ks], wubf_ref[ks, :], preferred_element_type=jnp.float32)
            u = part if u is None else u + part
        u = jnp.square(jnp.maximum(u, 0.0)).astype(jnp.bfloat16)
        o_ref[rows, :] += jnp.dot(u, wdbf_ref[...], preferred_element_type=jnp.float32)


def _mlp(h2, x2, w_up, w_down, *, tm, tf):
    T, D = x2.shape
    F = w_up.shape[1]
    n_f = F // tf
    slab = D // n_f
    assert slab % HEAD_DIM == 0
    return pl.pallas_call(
        _mlp_kernel,
        grid=(T // tm, n_f),
        in_specs=[pl.BlockSpec((tm, D), lambda i, f: (i, 0)),
                  pl.BlockSpec((tm, slab), lambda i, f: (i, f)),
                  pl.BlockSpec((D, tf), lambda i, f: (0, f)),
                  pl.BlockSpec((tf, D), lambda i, f: (f, 0))],
        out_specs=pl.BlockSpec((tm, D), lambda i, f: (i, 0)),
        out_shape=jax.ShapeDtypeStruct((T, D), jnp.float32),
        scratch_shapes=[pltpu.VMEM((D, tf), jnp.bfloat16), pltpu.VMEM((tf, D), jnp.bfloat16)],
        compiler_params=pltpu.CompilerParams(
            dimension_semantics=("arbitrary", "arbitrary"), vmem_limit_bytes=VMEM_LIMIT),
        name="mlp",
    )(h2, x2, w_up, w_down)


def _column_gains(q_norm_a, k_norm_a, q_norm_b, k_norm_b, width):
    qs = SCALE * LOG2E
    gains = jnp.concatenate([
        jnp.tile(q_norm_a * qs, A_HEADS), jnp.tile(k_norm_a, A_HEADS), jnp.ones((A_HEADS * HEAD_DIM,), jnp.float32),
        jnp.tile(q_norm_b * qs, B_HEADS), jnp.tile(k_norm_b, B_HEADS), jnp.ones((B_WIDTH,), jnp.float32),
        jnp.ones((width - QKV_WIDTH,), jnp.float32)])
    return gains.reshape(1, width)


def kernel(x, g_mix, w_in, q_norm_a, k_norm_a, q_norm_b, k_norm_b, rel_bias,
           w_branch_a, w_branch_b, w_out, g_mlp, w_up, w_down):
    batch, seq, d_model = x.shape
    T = batch * seq
    assert w_in.shape == (d_model, QKV_WIDTH + 2 * d_model)
    assert seq % (DIL_PATTERNS[-1][1] * A_BLOCK) == 0
    bf16 = jnp.bfloat16
    x2 = x.reshape(T, d_model)
    tm = min(ROW_TILE, T)

    dil_bias, moba_bias = _bias_tables(rel_bias)
    gains = _column_gains(q_norm_a, k_norm_a, q_norm_b, k_norm_b, w_in.shape[1])
    proj = _in_proj(x2, g_mix.reshape(1, -1), w_in, gains, tm=tm, tn=min(COL_TILE, d_model * 2))

    outs, lses = zip(*[_dilated_group(proj, dil_bias, g, batch=batch, seq=seq) for g in range(N_GROUPS)])
    yb = _moba(proj, moba_bias, batch=batch, seq=seq)

    x_mid, h_mid = _merge(outs, lses, yb, proj, x2, w_branch_a.astype(bf16), w_branch_b.astype(bf16),
                          w_out.astype(bf16), g_mlp.reshape(1, -1), tm=min(MERGE_ROW_TILE, T))
    y = _mlp(h_mid, x_mid, w_up, w_down, tm=tm, tf=min(COL_TILE, w_up.shape[1]))
    return y.reshape(batch, seq, d_model)
```

```python
import functools
import math

import jax
import jax.numpy as jnp
from jax import lax
from jax.experimental import pallas as pl
from jax.experimental.pallas import tpu as pltpu

HEAD_DIM = 128
DIL_PATTERNS = ((128, 1), (512, 4), (2048, 16))
N_GROUPS = len(DIL_PATTERNS)
A_HEADS_PER_GROUP = 4
A_HEADS = A_HEADS_PER_GROUP * N_GROUPS
A_OUT = A_HEADS_PER_GROUP * HEAD_DIM
A_BLOCK = 128
B_HEADS = 8
B_WIDTH = B_HEADS * HEAD_DIM
MOBA_BLOCK = 256
MOBA_TOPK = 3
MOBA_GROUP = 2
MOBA_AUX = 16
N_BUCKETS = 32
MAX_DISTANCE = 2048
EPS = 1e-6
SCALE = HEAD_DIM ** -0.5
A_QKV = 3 * A_HEADS * HEAD_DIM
B_QKV = 3 * B_WIDTH
QKV_WIDTH = A_QKV + B_QKV
LOG2E = math.log2(math.e)
NEG = -1e30
VMEM_LIMIT = 56 * 1024 * 1024
ROW_TILE = 1024
COL_TILE = 512
MERGE_ROW_TILE = 512


def _bucket_thresholds():
    max_exact = N_BUCKETS // 2

    def bucket(d):
        if d < max_exact:
            return d
        v = int(math.log(d / max_exact) / math.log(MAX_DISTANCE / max_exact) * (N_BUCKETS - max_exact))
        return min(max_exact + v, N_BUCKETS - 1)

    thr, d = [0], 0
    for b in range(1, N_BUCKETS):
        while bucket(d) < b:
            d += 1
        thr.append(d)
    return tuple(thr)


BUCKET_THRESHOLDS = _bucket_thresholds()
MOBA_BIAS_TILES = -(-(BUCKET_THRESHOLDS[-1] - 1) // MOBA_BLOCK) + 2


def _bias_lookup(dist, tab_ref, col, dmin=0, dmax=None):
    first = sum(dmin >= t for t in BUCKET_THRESHOLDS[1:])
    val = jnp.full(dist.shape, tab_ref[first, col], jnp.float32)
    for b in range(first + 1, N_BUCKETS):
        if dmax is None or BUCKET_THRESHOLDS[b] <= dmax:
            val = jnp.where(dist >= BUCKET_THRESHOLDS[b], tab_ref[b, col], val)
    return val * LOG2E


def _dil_bias_kernel(tab_ref, o_ref):
    g, h = pl.program_id(0), pl.program_id(1)
    a = lax.broadcasted_iota(jnp.int32, (A_BLOCK, 2 * A_BLOCK), 0)
    j = lax.broadcasted_iota(jnp.int32, (A_BLOCK, 2 * A_BLOCK), 1)
    delta = a + A_BLOCK - j
    dilation = lax.shift_left(jnp.int32(1), 2 * g)
    val = _bias_lookup(delta * dilation, tab_ref, g * A_HEADS_PER_GROUP + h)
    o_ref[0, 0] = jnp.where((delta >= 0) & (delta <= A_BLOCK), val, NEG)


def _moba_bias_kernel(tab_ref, o_ref):
    h = pl.program_id(0)
    j = lax.broadcasted_iota(jnp.int32, (MOBA_BLOCK, MOBA_BLOCK), 0)
    s = lax.broadcasted_iota(jnp.int32, (MOBA_BLOCK, MOBA_BLOCK), 1)
    for c in range(MOBA_BIAS_TILES):
        dist = c * MOBA_BLOCK + s - j
        lo, hi = (c - 1) * MOBA_BLOCK + 1, (c + 1) * MOBA_BLOCK - 1
        val = _bias_lookup(dist, tab_ref, A_HEADS + h, max(lo, 0), hi)
        o_ref[0, c] = jnp.where(dist >= 0, val, NEG)


def _bias_tables(rel_bias):
    smem = pl.BlockSpec(memory_space=pltpu.SMEM)
    dil = pl.pallas_call(
        _dil_bias_kernel,
        grid=(N_GROUPS, A_HEADS_PER_GROUP),
        in_specs=[smem],
        out_specs=pl.BlockSpec((1, 1, A_BLOCK, 2 * A_BLOCK), lambda g, h: (g, h, 0, 0)),
        out_shape=jax.ShapeDtypeStruct((N_GROUPS, A_HEADS_PER_GROUP, A_BLOCK, 2 * A_BLOCK), jnp.float32),
        name="dil_bias",
    )(rel_bias)
    moba = pl.pallas_call(
        _moba_bias_kernel,
        grid=(B_HEADS,),
        in_specs=[smem],
        out_specs=pl.BlockSpec((1, MOBA_BIAS_TILES, MOBA_BLOCK, MOBA_BLOCK), lambda h: (h, 0, 0, 0)),
        out_shape=jax.ShapeDtypeStruct((B_HEADS, MOBA_BIAS_TILES, MOBA_BLOCK, MOBA_BLOCK), jnp.float32),
        name="moba_bias",
    )(rel_bias)
    return dil, moba


IN_PROJ_ROW_CHUNKS = 2
IN_PROJ_K_CHUNKS = 4
IN_PROJ_TILES_PER_STEP = 2


def _rms_rows(x, g):
    return (x * lax.rsqrt(jnp.mean(x * x, axis=-1, keepdims=True) + EPS)) * g


def _in_proj_tile_kind(col0):
    if col0 < 2 * A_HEADS * HEAD_DIM or A_QKV <= col0 < A_QKV + 2 * B_WIDTH:
        return "norm"
    return "plain" if col0 < QKV_WIDTH else "gate"


def _in_proj_kernel(x_ref, g_ref, *refs, tn, n_tiles):
    tps = IN_PROJ_TILES_PER_STEP
    w_refs, gain_refs, (o_ref, h_ref, wbf_ref) = refs[:tps], refs[tps:2 * tps], refs[2 * tps:]
    s = pl.program_id(1)

    def head_norm(acc, gain_ref):
        return jnp.concatenate(
            [_rms_rows(acc[:, c * HEAD_DIM:(c + 1) * HEAD_DIM], gain_ref[:, c * HEAD_DIM:(c + 1) * HEAD_DIM])
             for c in range(tn // HEAD_DIM)], axis=-1)

    epilogues = {
        "norm": head_norm,
        "plain": lambda acc, gain_ref: acc,
        "gate": lambda acc, gain_ref: 0.5 * jnp.tanh(0.5 * acc) + 0.5,
    }

    def project(kinds, normalise_input):
        tm, d = h_ref.shape
        rc = tm // IN_PROJ_ROW_CHUNKS
        kc = d // IN_PROJ_K_CHUNKS
        for c in range(IN_PROJ_ROW_CHUNKS):
            rows = slice(c * rc, (c + 1) * rc)
            if normalise_input:
                h_ref[rows, :] = _rms_rows(x_ref[rows, :], g_ref[...]).astype(h_ref.dtype)
            for t, kind in enumerate(kinds):
                if kind is None:
                    o_ref[rows, t * tn:(t + 1) * tn] = jnp.zeros((rc, tn), o_ref.dtype)
                    continue
                acc = None
                for k in range(IN_PROJ_K_CHUNKS):
                    ks = slice(k * kc, (k + 1) * kc)
                    if c == 0:
                        wbf_ref[t, ks, :] = w_refs[t][ks, :].astype(wbf_ref.dtype)
                    part = jnp.dot(h_ref[rows, ks], wbf_ref[t, ks, :], preferred_element_type=jnp.float32)
                    acc = part if acc is None else acc + part
                o_ref[rows, t * tn:(t + 1) * tn] = epilogues[kind](acc, gain_refs[t]).astype(o_ref.dtype)

    steps_of = {}
    for step in range(-(-n_tiles // tps)):
        tiles = [step * tps + t for t in range(tps)]
        kinds = tuple(_in_proj_tile_kind(tile * tn) if tile < n_tiles else None for tile in tiles)
        steps_of.setdefault((kinds, step == 0), []).append(step)
    for (kinds, first), steps in steps_of.items():
        @pl.when(functools.reduce(jnp.logical_or, [s == step for step in steps]))
        def _(kinds=kinds, first=first):
            project(kinds, first)


def _in_proj(x2, g_mix, w_in, gains, *, tm, tn):
    T, D = x2.shape
    tps = IN_PROJ_TILES_PER_STEP
    n_tiles = w_in.shape[1] // tn
    n_steps = -(-n_tiles // tps)
    tile = lambda t: (lambda i, s: (0, jnp.minimum(s * tps + t, n_tiles - 1)))
    return pl.pallas_call(
        functools.partial(_in_proj_kernel, tn=tn, n_tiles=n_tiles),
        grid=(T // tm, n_steps),
        in_specs=[pl.BlockSpec((tm, D), lambda i, s: (i, 0)), pl.BlockSpec((1, D), lambda i, s: (0, 0))]
                 + [pl.BlockSpec((D, tn), tile(t)) for t in range(tps)]
                 + [pl.BlockSpec((1, tn), tile(t)) for t in range(tps)],
        out_specs=pl.BlockSpec((tm, tps * tn), lambda i, s: (i, s)),
        out_shape=jax.ShapeDtypeStruct((T, n_steps * tps * tn), jnp.bfloat16),
        scratch_shapes=[pltpu.VMEM((tm, D), jnp.bfloat16), pltpu.VMEM((tps, D, tn), jnp.bfloat16)],
        compiler_params=pltpu.CompilerParams(
            dimension_semantics=("arbitrary", "arbitrary"), vmem_limit_bytes=VMEM_LIMIT),
        name="in_proj",
    )(x2, g_mix, *([w_in] * tps), *([gains] * tps))


CHUNK = DIL_PATTERNS[-1][1] * A_BLOCK
BLOCKS_PER_CHUNK = CHUNK // A_BLOCK
DIL_STRIDE = 4


def _class_rows(src_ref, mid_ref, r):
    n = CHUNK // r
    if r <= DIL_STRIDE:
        return [src_ref[pl.ds(rr, n, stride=r), :] for rr in range(r)]
    r1, r2, n1 = DIL_STRIDE, r // DIL_STRIDE, CHUNK // DIL_STRIDE
    assert r2 <= DIL_STRIDE
    for a in range(r1):
        mid_ref[a * n1:(a + 1) * n1, :] = src_ref[pl.ds(a, n1, stride=r1), :]
    return [mid_ref[pl.ds((rr % r1) * n1 + rr // r1, n, stride=r2), :] for rr in range(r)]


def _store_class_rows(dst_ref, mid_ref, rows_of_class, r):
    n = CHUNK // r
    if r == 1:
        dst_ref[...] = rows_of_class[0]
    elif r <= DIL_STRIDE:
        for rr, rows in enumerate(rows_of_class):
            dst_ref[pl.ds(rr, n, stride=r), :] = rows
    else:
        r1, r2, n1 = DIL_STRIDE, r // DIL_STRIDE, CHUNK // DIL_STRIDE
        for rr, rows in enumerate(rows_of_class):
            mid_ref[pl.ds((rr % r1) * n1 + rr // r1, n, stride=r2), :] = rows
        for a in range(r1):
            dst_ref[pl.ds(a, n1, stride=r1), :] = mid_ref[a * n1:(a + 1) * n1, :]


def _deinterleave(dst_ref, first_block, src_ref, stage_ref, mid_ref, r):
    bpc = BLOCKS_PER_CHUNK // r
    if r == 1:
        dst_ref[0, first_block:first_block + bpc] = src_ref[...].reshape(bpc, A_BLOCK, A_OUT)
        return
    for h in range(A_HEADS_PER_GROUP):
        cols = slice(h * HEAD_DIM, (h + 1) * HEAD_DIM)
        stage_ref[h] = src_ref[:, cols].astype(jnp.float32)
        for rr, rows in enumerate(_class_rows(stage_ref.at[h], mid_ref, r)):
            dst_ref[rr, first_block:first_block + bpc, :, cols] = (
                rows.astype(dst_ref.dtype).reshape(bpc, A_BLOCK, HEAD_DIM))


def _dilated_kernel(q_ref, k_ref, v_ref, bias_ref, o_ref, lse_ref, stage_ref, mid_ref, qd_ref, kd_ref, vd_ref, *, r):
    c = pl.program_id(1)
    bpc = BLOCKS_PER_CHUNK // r
    nblk = BLOCKS_PER_CHUNK

    @pl.when(c == 0)
    def _():
        kd_ref[:, 0] = jnp.zeros((r, A_BLOCK, A_OUT), kd_ref.dtype)
        vd_ref[:, 0] = jnp.zeros((r, A_BLOCK, A_OUT), vd_ref.dtype)

    @pl.when(c > 0)
    def _():
        kd_ref[:, 0] = kd_ref[:, bpc]
        vd_ref[:, 0] = vd_ref[:, bpc]

    _deinterleave(qd_ref, 0, q_ref, stage_ref, mid_ref, r)
    _deinterleave(kd_ref, 1, k_ref, stage_ref, mid_ref, r)
    _deinterleave(vd_ref, 1, v_ref, stage_ref, mid_ref, r)

    blk = lax.broadcasted_iota(jnp.int32, (nblk, A_BLOCK, 2 * A_BLOCK), 0)
    key = lax.broadcasted_iota(jnp.int32, (nblk, A_BLOCK, 2 * A_BLOCK), 2)
    no_prev = (lax.rem(blk, bpc) == 0) & (key < A_BLOCK) & (c == 0)
    lane = lax.broadcasted_iota(jnp.int32, (nblk, A_BLOCK, HEAD_DIM), 2)
    bqk = (((2,), (2,)), ((0,), (0,)))
    bqd = (((2,), (1,)), ((0,), (0,)))
    class_rows = lambda t: [t[rr * bpc:(rr + 1) * bpc].reshape(bpc * A_BLOCK, HEAD_DIM) for rr in range(r)]
    lse = jnp.zeros((nblk, A_BLOCK, HEAD_DIM), jnp.float32)
    ones = jnp.ones((nblk, 2 * A_BLOCK, HEAD_DIM), jnp.bfloat16)
    for h in range(A_HEADS_PER_GROUP):
        cols = slice(h * HEAD_DIM, (h + 1) * HEAD_DIM)
        q = qd_ref[:, :, :, cols].reshape(nblk, A_BLOCK, HEAD_DIM)
        k2 = jnp.concatenate([kd_ref[:, 0:bpc, :, cols], kd_ref[:, 1:bpc + 1, :, cols]], axis=2)
        v2 = jnp.concatenate([vd_ref[:, 0:bpc, :, cols], vd_ref[:, 1:bpc + 1, :, cols]], axis=2)
        k2 = k2.reshape(nblk, 2 * A_BLOCK, HEAD_DIM)
        v2 = v2.reshape(nblk, 2 * A_BLOCK, HEAD_DIM)
        s = lax.dot_general(q, k2, bqk, preferred_element_type=jnp.float32) + bias_ref[0, h]
        s = jnp.where(no_prev, NEG, s)
        m = jnp.max(s, axis=-1, keepdims=True)
        p = jnp.exp2(s - m).astype(jnp.bfloat16)
        pv = lax.dot_general(p, jnp.concatenate([v2, ones], axis=-1), bqd, preferred_element_type=jnp.float32)
        den = pv[:, :, HEAD_DIM:]
        o = pv[:, :, :HEAD_DIM] / den
        lse = jnp.where(lane == h, m + jnp.log2(den), lse)
        _store_class_rows(stage_ref.at[h], mid_ref, class_rows(o), r)
        o_ref[:, cols] = stage_ref[h].astype(o_ref.dtype)
    _store_class_rows(lse_ref, mid_ref, class_rows(lse), r)


def _dilated_group(proj2, bias, group, *, batch, seq):
    _, r = DIL_PATTERNS[group]
    nc = seq // CHUNK
    bpc = BLOCKS_PER_CHUNK // r
    chunk = lambda sec: pl.BlockSpec((CHUNK, A_OUT), lambda b, c: (b * nc + c, sec * N_GROUPS + group))
    return pl.pallas_call(
        functools.partial(_dilated_kernel, r=r),
        grid=(batch, nc),
        in_specs=[chunk(0), chunk(1), chunk(2),
                  pl.BlockSpec((1, A_HEADS_PER_GROUP, A_BLOCK, 2 * A_BLOCK), lambda b, c: (group, 0, 0, 0))],
        out_specs=[pl.BlockSpec((CHUNK, A_OUT), lambda b, c: (b * nc + c, 0)),
                   pl.BlockSpec((CHUNK, HEAD_DIM), lambda b, c: (b * nc + c, 0))],
        out_shape=[jax.ShapeDtypeStruct((batch * seq, A_OUT), jnp.bfloat16),
                   jax.ShapeDtypeStruct((batch * seq, HEAD_DIM), jnp.float32)],
        scratch_shapes=[pltpu.VMEM((A_HEADS_PER_GROUP, CHUNK, HEAD_DIM), jnp.float32),
                        pltpu.VMEM((CHUNK, HEAD_DIM), jnp.float32),
                        pltpu.VMEM((r, bpc, A_BLOCK, A_OUT), jnp.bfloat16),
                        pltpu.VMEM((r, bpc + 1, A_BLOCK, A_OUT), jnp.bfloat16),
                        pltpu.VMEM((r, bpc + 1, A_BLOCK, A_OUT), jnp.bfloat16)],
        compiler_params=pltpu.CompilerParams(
            dimension_semantics=("arbitrary", "arbitrary"), vmem_limit_bytes=VMEM_LIMIT),
        name=f"dilated_{group}",
    )(proj2, proj2, proj2, bias)


def _split3(x):
    hi = x.astype(jnp.bfloat16)
    r1 = x - hi.astype(jnp.float32)
    mid = r1.astype(jnp.bfloat16)
    lo = (r1 - mid.astype(jnp.float32)).astype(jnp.bfloat16)
    return hi, mid, lo


MOBA_STREAMS = 3


def _moba_items(nb, group):
    streams = [[] for _ in range(MOBA_STREAMS)]
    for i in sorted(range(nb), key=lambda i: -(i // group)):
        min(streams, key=len).extend((i, g) for g in range(i // group + 1))
    assert len({len(s) for s in streams}) == 1 and len(streams[0]) % 2 == 0
    return streams


def _moba_kernel(items_ref, q_ref, k_ref, v_ref, bias_ref, o_ref, avg_ref, kmean_ref, ka_ref, qa_ref, va_ref, s_ref,
                 st_ref, *, nb, group, n_items):
    BS = MOBA_BLOCK
    hd = HEAD_DIM
    seq = nb * BS
    rows_per_group = group * BS
    nt = (((1,), (1,)), ((), ()))

    row = lax.broadcasted_iota(jnp.int32, (nb, seq), 0)
    col = lax.broadcasted_iota(jnp.int32, (nb, seq), 1)

    @pl.when((pl.program_id(0) == 0) & (pl.program_id(1) == 0))
    def _():
        lo_edge = row * BS
        avg_ref[...] = jnp.where((col >= lo_edge) & (col < lo_edge + BS), 1.0 / BS, 0.0).astype(avg_ref.dtype)
        qa_ref[hd + nb:, :] = jnp.zeros((hd - nb, seq), qa_ref.dtype)
        key_blk = lax.broadcasted_iota(jnp.int32, (seq, hd), 0) // BS
        key_lane = lax.broadcasted_iota(jnp.int32, (seq, hd), 1)
        ka_ref[:, hd:] = jnp.where(key_lane == key_blk, 1.0, 0.0).astype(ka_ref.dtype)
        aux_row = lax.broadcasted_iota(jnp.int32, (MOBA_AUX, seq), 0)
        va_ref[hd:, :] = jnp.where(aux_row == 0, 1.0, 0.0).astype(va_ref.dtype)

    kmean = jnp.dot(avg_ref[...], k_ref[...], preferred_element_type=jnp.float32)
    for t, part in enumerate(_split3(kmean)):
        kmean_ref[t] = part

    q_all = q_ref[...]
    gate3 = lax.dot_general(kmean_ref[...].reshape(3 * nb, hd), q_all, nt, preferred_element_type=jnp.float32)
    gate = gate3[:nb] + gate3[nb:2 * nb] + gate3[2 * nb:]
    own = col // BS
    past = row < own
    gate = jnp.where(past, gate, -jnp.inf)
    sub = lax.broadcasted_iota(jnp.int32, (8, seq), 0)
    ranks = []
    for r0 in range(0, nb, 8):
        g8 = gate[r0:r0 + 8, :]
        rank8 = jnp.zeros((8, seq), jnp.int32)
        for m in range(nb):
            gm = gate[m:m + 1, :]
            if m >= r0 + 7:
                beats = gm > g8
            elif m < r0:
                beats = gm >= g8
            else:
                beats = (gm > g8) | ((gm == g8) & (sub > m - r0))
            rank8 = rank8 + beats.astype(jnp.int32)
        ranks.append(rank8)
    rank = jnp.concatenate(ranks, axis=0)
    sel = jnp.where((past & (rank < MOBA_TOPK)) | (row == own), 0.0, NEG)

    qa_ref[:hd, :] = q_all.T
    qa_ref[hd:hd + nb, :] = sel.astype(qa_ref.dtype)
    ka_ref[:, :hd] = k_ref[...]
    va_ref[:hd, :] = v_ref[...].T

    def logits(st, w, slot):
        i, g = items_ref[st, w, 0], items_ref[st, w, 1]
        rows = pl.ds(pl.multiple_of(g * rows_per_group, rows_per_group), rows_per_group)
        qcols = pl.ds(pl.multiple_of(i * BS, BS), BS)
        s_t = jnp.dot(ka_ref[rows, :], qa_ref[:, qcols], preferred_element_type=jnp.float32)
        m8 = jnp.full((8, BS), -jnp.inf, jnp.float32)
        for u in range(group):
            tile = jnp.clip(i - (g * group + u), 0, MOBA_BIAS_TILES - 1)
            s_u = s_t[u * BS:(u + 1) * BS] + bias_ref[0, tile]
            s_ref[2 * st + slot, u * BS:(u + 1) * BS, :] = s_u
            m8 = jnp.maximum(m8, jnp.max(s_u.reshape(BS // 8, 8, BS), axis=0))
        return m8

    def accumulate(st, w, slot, m8, state):
        m_run, acc = state
        i, g = items_ref[st, w, 0], items_ref[st, w, 1]
        rows = pl.ds(pl.multiple_of(g * rows_per_group, rows_per_group), rows_per_group)
        m_prev = jnp.where(g == 0, -jnp.inf, m_run)
        m_new = jnp.maximum(m_prev, jnp.max(m8, axis=0, keepdims=True))
        alpha = jnp.exp2(m_prev - m_new)
        p = jnp.exp2(s_ref[2 * st + slot] - m_new).astype(jnp.bfloat16)
        acc = alpha * acc + jnp.dot(va_ref[:, rows], p, preferred_element_type=jnp.float32)
        st_ref[i] = acc
        return m_new, acc

    streams = range(MOBA_STREAMS)

    def pair(t, carry):
        m8a, state = carry
        w = 2 * t
        m8b = [logits(st, w + 1, 1) for st in streams]
        state = [accumulate(st, w, 0, m8a[st], state[st]) for st in streams]
        m8a = [logits(st, jnp.minimum(w + 2, n_items - 1), 0) for st in streams]
        state = [accumulate(st, w + 1, 1, m8b[st], state[st]) for st in streams]
        return m8a, state

    state0 = (jnp.full((1, BS), -jnp.inf, jnp.float32), jnp.zeros((hd + MOBA_AUX, BS), jnp.float32))
    lax.fori_loop(0, n_items // 2, pair, ([logits(st, 0, 0) for st in streams], [state0 for _ in streams]))

    for i in range(nb):
        acc = st_ref[i]
        o_ref[i * BS:(i + 1) * BS, :] = (acc[:hd] / acc[hd:hd + 1]).T.astype(o_ref.dtype)


def _moba(proj2, bias, *, batch, seq):
    BS = MOBA_BLOCK
    nb = seq // BS
    assert nb <= HEAD_DIM
    c0 = A_QKV // HEAD_DIM
    group = math.gcd(nb, MOBA_GROUP)
    items = _moba_items(nb, group)
    head_spec = lambda sec: pl.BlockSpec((seq, HEAD_DIM), lambda b, h, items: (b, c0 + sec * B_HEADS + h))
    return pl.pallas_call(
        functools.partial(_moba_kernel, nb=nb, group=group, n_items=len(items[0])),
        grid_spec=pltpu.PrefetchScalarGridSpec(
            num_scalar_prefetch=1,
            grid=(batch, B_HEADS),
            in_specs=[head_spec(0), head_spec(1), head_spec(2),
                      pl.BlockSpec((1, MOBA_BIAS_TILES, BS, BS), lambda b, h, items: (h, 0, 0, 0))],
            out_specs=pl.BlockSpec((seq, HEAD_DIM), lambda b, h, items: (b, h)),
            scratch_shapes=[pltpu.VMEM((nb, seq), jnp.bfloat16),
                            pltpu.VMEM((3, nb, HEAD_DIM), jnp.bfloat16),
                            pltpu.VMEM((seq, 2 * HEAD_DIM), jnp.bfloat16),
                            pltpu.VMEM((2 * HEAD_DIM, seq), jnp.bfloat16),
                            pltpu.VMEM((HEAD_DIM + MOBA_AUX, seq), jnp.bfloat16),
                            pltpu.VMEM((2 * MOBA_STREAMS, group * BS, BS), jnp.float32),
                            pltpu.VMEM((nb, HEAD_DIM + MOBA_AUX, BS), jnp.float32)]),
        out_shape=jax.ShapeDtypeStruct((batch * seq, B_WIDTH), jnp.bfloat16),
        compiler_params=pltpu.CompilerParams(
            dimension_semantics=("arbitrary", "arbitrary"), vmem_limit_bytes=VMEM_LIMIT),
        name="moba",
    )(jnp.asarray(items, jnp.int32), proj2, proj2, proj2, bias)


MERGE_ROW_CHUNKS = 2


def _merge_kernel(o0_ref, o1_ref, o2_ref, l0_ref, l1_ref, l2_ref, yb_ref, x_ref,
                  wa_ref, wb_ref, wo_ref, g_mlp_ref, *refs):
    gate_refs, (out_ref, h_ref) = refs[:-2], refs[-2:]
    ga_refs, gb_refs = gate_refs[:len(gate_refs) // 2], gate_refs[len(gate_refs) // 2:]
    rc = out_ref.shape[0] // MERGE_ROW_CHUNKS
    for c in range(MERGE_ROW_CHUNKS):
        rows = slice(c * rc, (c + 1) * rc)
        l0, l1, l2 = l0_ref[rows, :], l1_ref[rows, :], l2_ref[rows, :]
        mx = jnp.maximum(jnp.maximum(l0, l1), l2)
        e0, e1, e2 = jnp.exp2(l0 - mx), jnp.exp2(l1 - mx), jnp.exp2(l2 - mx)
        inv = 1.0 / (e0 + e1 + e2)
        parts = []
        for h in range(A_HEADS_PER_GROUP):
            cols = slice(h * HEAD_DIM, (h + 1) * HEAD_DIM)
            ya = ((e0 * inv)[:, h:h + 1] * o0_ref[rows, cols].astype(jnp.float32)
                  + (e1 * inv)[:, h:h + 1] * o1_ref[rows, cols].astype(jnp.float32)
                  + (e2 * inv)[:, h:h + 1] * o2_ref[rows, cols].astype(jnp.float32))
            parts.append(ya.astype(jnp.bfloat16))
        ya = jnp.concatenate(parts, axis=-1)
        pa = jnp.dot(ya, wa_ref[...], preferred_element_type=jnp.float32)
        pb = jnp.dot(yb_ref[rows, :], wb_ref[...], preferred_element_type=jnp.float32)
        ga = jnp.concatenate([g[rows, :] for g in ga_refs], axis=-1).astype(jnp.float32)
        gb = jnp.concatenate([g[rows, :] for g in gb_refs], axis=-1).astype(jnp.float32)
        mixed = ga * pa + gb * pb
        out = x_ref[rows, :] + jnp.dot(mixed.astype(jnp.bfloat16), wo_ref[...], preferred_element_type=jnp.float32)
        out_ref[rows, :] = out
        h_ref[rows, :] = _rms_rows(out, g_mlp_ref[...]).astype(h_ref.dtype)


def _merge(outs, lses, yb, proj2, x2, wa, wb, wo, g_mlp, *, tm):
    T, D = x2.shape
    gw = math.gcd(QKV_WIDTH, D)
    n_gate_blocks = 2 * D // gw
    row = lambda w: pl.BlockSpec((tm, w), lambda i: (i, 0))
    full = lambda a: pl.BlockSpec(a.shape, lambda i: (0, 0))
    gate = lambda c: pl.BlockSpec((tm, gw), lambda i: (i, QKV_WIDTH // gw + c))
    return pl.pallas_call(
        _merge_kernel,
        grid=(T // tm,),
        in_specs=[row(A_OUT)] * 3 + [row(HEAD_DIM)] * 3 + [row(B_WIDTH), row(D),
                  full(wa), full(wb), full(wo), full(g_mlp)] + [gate(c) for c in range(n_gate_blocks)],
        out_specs=[row(D), row(D)],
        out_shape=[jax.ShapeDtypeStruct((T, D), jnp.float32), jax.ShapeDtypeStruct((T, D), jnp.bfloat16)],
        compiler_params=pltpu.CompilerParams(dimension_semantics=("arbitrary",), vmem_limit_bytes=VMEM_LIMIT),
        name="merge",
    )(*outs, *lses, yb, x2, wa, wb, wo, g_mlp, *([proj2] * n_gate_blocks))


MLP_ROW_CHUNKS = 1
MLP_K_CHUNKS = 4


def _mlp_kernel(h_ref, x_ref, wu_ref, wd_ref, o_ref, wubf_ref, wdbf_ref):
    f = pl.program_id(1)

    @pl.when(f == 0)
    def _():
        o_ref[...] = jnp.zeros(o_ref.shape, o_ref.dtype)

    slab = x_ref.shape[1]
    cols = pl.ds(pl.multiple_of(f * slab, slab), slab)
    o_ref[:, cols] += x_ref[...]

    tm, d = h_ref.shape
    rc = tm // MLP_ROW_CHUNKS
    kc = d // MLP_K_CHUNKS
    for c in range(MLP_ROW_CHUNKS):
        rows = slice(c * rc, (c + 1) * rc)
        u = None
        for k in range(MLP_K_CHUNKS):
            ks = slice(k * kc, (k + 1) * kc)
            if c == 0:
                wubf_ref[ks, :] = wu_ref[ks, :].astype(wubf_ref.dtype)
                wdbf_ref[:, ks] = wd_ref[:, ks].astype(wdbf_ref.dtype)
            part = jnp.dot(h_ref[rows, ks], wubf_ref[ks, :], preferred_element_type=jnp.float32)
            u = part if u is None else u + part
        u = jnp.square(jnp.maximum(u, 0.0)).astype(jnp.bfloat16)
        o_ref[rows, :] += jnp.dot(u, wdbf_ref[...], preferred_element_type=jnp.float32)


def _mlp(h2, x2, w_up, w_down, *, tm, tf):
    T, D = x2.shape
    F = w_up.shape[1]
    n_f = F // tf
    slab = D // n_f
    assert slab % HEAD_DIM == 0
    return pl.pallas_call(
        _mlp_kernel,
        grid=(T // tm, n_f),
        in_specs=[pl.BlockSpec((tm, D), lambda i, f: (i, 0)),
                  pl.BlockSpec((tm, slab), lambda i, f: (i, f)),
                  pl.BlockSpec((D, tf), lambda i, f: (0, f)),
                  pl.BlockSpec((tf, D), lambda i, f: (f, 0))],
        out_specs=pl.BlockSpec((tm, D), lambda i, f: (i, 0)),
        out_shape=jax.ShapeDtypeStruct((T, D), jnp.float32),
        scratch_shapes=[pltpu.VMEM((D, tf), jnp.bfloat16), pltpu.VMEM((tf, D), jnp.bfloat16)],
        compiler_params=pltpu.CompilerParams(
            dimension_semantics=("arbitrary", "arbitrary"), vmem_limit_bytes=VMEM_LIMIT),
        name="mlp",
    )(h2, x2, w_up, w_down)


def _column_gains(q_norm_a, k_norm_a, q_norm_b, k_norm_b, width):
    qs = SCALE * LOG2E
    gains = jnp.concatenate([
        jnp.tile(q_norm_a * qs, A_HEADS), jnp.tile(k_norm_a, A_HEADS), jnp.ones((A_HEADS * HEAD_DIM,), jnp.float32),
        jnp.tile(q_norm_b * qs, B_HEADS), jnp.tile(k_norm_b, B_HEADS), jnp.ones((B_WIDTH,), jnp.float32),
        jnp.ones((width - QKV_WIDTH,), jnp.float32)])
    return gains.reshape(1, width)


def kernel(x, g_mix, w_in, q_norm_a, k_norm_a, q_norm_b, k_norm_b, rel_bias,
           w_branch_a, w_branch_b, w_out, g_mlp, w_up, w_down):
    batch, seq, d_model = x.shape
    T = batch * seq
    assert w_in.shape == (d_model, QKV_WIDTH + 2 * d_model)
    assert seq % (DIL_PATTERNS[-1][1] * A_BLOCK) == 0
    bf16 = jnp.bfloat16
    x2 = x.reshape(T, d_model)
    tm = min(ROW_TILE, T)

    dil_bias, moba_bias = _bias_tables(rel_bias)
    gains = _column_gains(q_norm_a, k_norm_a, q_norm_b, k_norm_b, w_in.shape[1])
    proj = _in_proj(x2, g_mix.reshape(1, -1), w_in, gains, tm=tm, tn=min(COL_TILE, d_model * 2))

    outs, lses = zip(*[_dilated_group(proj, dil_bias, g, batch=batch, seq=seq) for g in range(N_GROUPS)])
    yb = _moba(proj, moba_bias, batch=batch, seq=seq)

    x_mid, h_mid = _merge(outs, lses, yb, proj, x2, w_branch_a.astype(bf16), w_branch_b.astype(bf16),
                          w_out.astype(bf16), g_mlp.reshape(1, -1), tm=min(MERGE_ROW_TILE, T))
    y = _mlp(h_mid, x_mid, w_up, w_down, tm=tm, tf=min(COL_TILE, w_up.shape[1]))
    return y.reshape(batch, seq, d_model)
```

```python
import functools
import math

import jax
import jax.numpy as jnp
from jax import lax
from jax.experimental import pallas as pl
from jax.experimental.pallas import tpu as pltpu

HEAD_DIM = 128
DIL_PATTERNS = ((128, 1), (512, 4), (2048, 16))
N_GROUPS = len(DIL_PATTERNS)
A_HEADS_PER_GROUP = 4
A_HEADS = A_HEADS_PER_GROUP * N_GROUPS
A_OUT = A_HEADS_PER_GROUP * HEAD_DIM
A_BLOCK = 128
B_HEADS = 8
B_WIDTH = B_HEADS * HEAD_DIM
MOBA_BLOCK = 256
MOBA_TOPK = 3
MOBA_GROUP = 2
MOBA_AUX = 16
N_BUCKETS = 32
MAX_DISTANCE = 2048
EPS = 1e-6
SCALE = HEAD_DIM ** -0.5
A_QKV = 3 * A_HEADS * HEAD_DIM
B_QKV = 3 * B_WIDTH
QKV_WIDTH = A_QKV + B_QKV
LOG2E = math.log2(math.e)
NEG = -1e30
VMEM_LIMIT = 56 * 1024 * 1024
ROW_TILE = 1024
COL_TILE = 512
MERGE_ROW_TILE = 512
MLP_FF_TILE = 1024


def _bucket_thresholds():
    max_exact = N_BUCKETS // 2

    def bucket(d):
        if d < max_exact:
            return d
        v = int(math.log(d / max_exact) / math.log(MAX_DISTANCE / max_exact) * (N_BUCKETS - max_exact))
        return min(max_exact + v, N_BUCKETS - 1)

    thr, d = [0], 0
    for b in range(1, N_BUCKETS):
        while bucket(d) < b:
            d += 1
        thr.append(d)
    return tuple(thr)


BUCKET_THRESHOLDS = _bucket_thresholds()
MOBA_BIAS_TILES = -(-(BUCKET_THRESHOLDS[-1] - 1) // MOBA_BLOCK) + 2


def _bias_lookup(dist, tab_ref, col, dmin=0, dmax=None):
    first = sum(dmin >= t for t in BUCKET_THRESHOLDS[1:])
    val = jnp.full(dist.shape, tab_ref[first, col], jnp.float32)
    for b in range(first + 1, N_BUCKETS):
        if dmax is None or BUCKET_THRESHOLDS[b] <= dmax:
            val = jnp.where(dist >= BUCKET_THRESHOLDS[b], tab_ref[b, col], val)
    return val * LOG2E


def _dil_bias_kernel(tab_ref, o_ref):
    g, h = pl.program_id(0), pl.program_id(1)
    a = lax.broadcasted_iota(jnp.int32, (A_BLOCK, 2 * A_BLOCK), 0)
    j = lax.broadcasted_iota(jnp.int32, (A_BLOCK, 2 * A_BLOCK), 1)
    delta = a + A_BLOCK - j
    dilation = lax.shift_left(jnp.int32(1), 2 * g)
    val = _bias_lookup(delta * dilation, tab_ref, g * A_HEADS_PER_GROUP + h)
    o_ref[0, 0] = jnp.where((delta >= 0) & (delta <= A_BLOCK), val, NEG)


def _moba_bias_kernel(tab_ref, o_ref):
    h = pl.program_id(0)
    j = lax.broadcasted_iota(jnp.int32, (MOBA_BLOCK, MOBA_BLOCK), 0)
    s = lax.broadcasted_iota(jnp.int32, (MOBA_BLOCK, MOBA_BLOCK), 1)
    for c in range(MOBA_BIAS_TILES):
        dist = c * MOBA_BLOCK + s - j
        lo, hi = (c - 1) * MOBA_BLOCK + 1, (c + 1) * MOBA_BLOCK - 1
        val = _bias_lookup(dist, tab_ref, A_HEADS + h, max(lo, 0), hi)
        o_ref[0, c] = jnp.where(dist >= 0, val, NEG)


def _bias_tables(rel_bias):
    smem = pl.BlockSpec(memory_space=pltpu.SMEM)
    dil = pl.pallas_call(
        _dil_bias_kernel,
        grid=(N_GROUPS, A_HEADS_PER_GROUP),
        in_specs=[smem],
        out_specs=pl.BlockSpec((1, 1, A_BLOCK, 2 * A_BLOCK), lambda g, h: (g, h, 0, 0)),
        out_shape=jax.ShapeDtypeStruct((N_GROUPS, A_HEADS_PER_GROUP, A_BLOCK, 2 * A_BLOCK), jnp.float32),
        name="dil_bias",
    )(rel_bias)
    moba = pl.pallas_call(
        _moba_bias_kernel,
        grid=(B_HEADS,),
        in_specs=[smem],
        out_specs=pl.BlockSpec((1, MOBA_BIAS_TILES, MOBA_BLOCK, MOBA_BLOCK), lambda h: (h, 0, 0, 0)),
        out_shape=jax.ShapeDtypeStruct((B_HEADS, MOBA_BIAS_TILES, MOBA_BLOCK, MOBA_BLOCK), jnp.float32),
        name="moba_bias",
    )(rel_bias)
    return dil, moba


IN_PROJ_ROW_CHUNKS = 2
IN_PROJ_K_CHUNKS = 4
IN_PROJ_TILES_PER_STEP = 2


def _rms_rows(x, g):
    return (x * lax.rsqrt(jnp.mean(x * x, axis=-1, keepdims=True) + EPS)) * g


def _in_proj_tile_kind(col0):
    if col0 < 2 * A_HEADS * HEAD_DIM or A_QKV <= col0 < A_QKV + 2 * B_WIDTH:
        return "norm"
    return "plain" if col0 < QKV_WIDTH else "gate"


def _in_proj_kernel(x_ref, g_ref, *refs, tn, n_tiles):
    tps = IN_PROJ_TILES_PER_STEP
    w_refs, gain_refs, (o_ref, h_ref, wbf_ref) = refs[:tps], refs[tps:2 * tps], refs[2 * tps:]
    s = pl.program_id(1)

    def head_norm(acc, gain_ref):
        return jnp.concatenate(
            [_rms_rows(acc[:, c * HEAD_DIM:(c + 1) * HEAD_DIM], gain_ref[:, c * HEAD_DIM:(c + 1) * HEAD_DIM])
             for c in range(tn // HEAD_DIM)], axis=-1)

    epilogues = {
        "norm": head_norm,
        "plain": lambda acc, gain_ref: acc,
        "gate": lambda acc, gain_ref: 0.5 * jnp.tanh(0.5 * acc) + 0.5,
    }

    def project(kinds, normalise_input):
        tm, d = h_ref.shape
        rc = tm // IN_PROJ_ROW_CHUNKS
        kc = d // IN_PROJ_K_CHUNKS
        for c in range(IN_PROJ_ROW_CHUNKS):
            rows = slice(c * rc, (c + 1) * rc)
            if normalise_input:
                h_ref[rows, :] = _rms_rows(x_ref[rows, :], g_ref[...]).astype(h_ref.dtype)
            for t, kind in enumerate(kinds):
                if kind is None:
                    o_ref[rows, t * tn:(t + 1) * tn] = jnp.zeros((rc, tn), o_ref.dtype)
                    continue
                acc = None
                for k in range(IN_PROJ_K_CHUNKS):
                    ks = slice(k * kc, (k + 1) * kc)
                    if c == 0:
                        wbf_ref[t, ks, :] = w_refs[t][ks, :].astype(wbf_ref.dtype)
                    part = jnp.dot(h_ref[rows, ks], wbf_ref[t, ks, :], preferred_element_type=jnp.float32)
                    acc = part if acc is None else acc + part
                o_ref[rows, t * tn:(t + 1) * tn] = epilogues[kind](acc, gain_refs[t]).astype(o_ref.dtype)

    steps_of = {}
    for step in range(-(-n_tiles // tps)):
        tiles = [step * tps + t for t in range(tps)]
        kinds = tuple(_in_proj_tile_kind(tile * tn) if tile < n_tiles else None for tile in tiles)
        steps_of.setdefault((kinds, step == 0), []).append(step)
    for (kinds, first), steps in steps_of.items():
        @pl.when(functools.reduce(jnp.logical_or, [s == step for step in steps]))
        def _(kinds=kinds, first=first):
            project(kinds, first)


def _in_proj(x2, g_mix, w_in, gains, *, tm, tn):
    T, D = x2.shape
    tps = IN_PROJ_TILES_PER_STEP
    n_tiles = w_in.shape[1] // tn
    n_steps = -(-n_tiles // tps)
    tile = lambda t: (lambda i, s: (0, jnp.minimum(s * tps + t, n_tiles - 1)))
    return pl.pallas_call(
        functools.partial(_in_proj_kernel, tn=tn, n_tiles=n_tiles),
        grid=(T // tm, n_steps),
        in_specs=[pl.BlockSpec((tm, D), lambda i, s: (i, 0)), pl.BlockSpec((1, D), lambda i, s: (0, 0))]
                 + [pl.BlockSpec((D, tn), tile(t)) for t in range(tps)]
                 + [pl.BlockSpec((1, tn), tile(t)) for t in range(tps)],
        out_specs=pl.BlockSpec((tm, tps * tn), lambda i, s: (i, s)),
        out_shape=jax.ShapeDtypeStruct((T, n_steps * tps * tn), jnp.bfloat16),
        scratch_shapes=[pltpu.VMEM((tm, D), jnp.bfloat16), pltpu.VMEM((tps, D, tn), jnp.bfloat16)],
        compiler_params=pltpu.CompilerParams(
            dimension_semantics=("arbitrary", "arbitrary"), vmem_limit_bytes=VMEM_LIMIT),
        name="in_proj",
    )(x2, g_mix, *([w_in] * tps), *([gains] * tps))


CHUNK = DIL_PATTERNS[-1][1] * A_BLOCK
BLOCKS_PER_CHUNK = CHUNK // A_BLOCK
DIL_STRIDE = 4


def _class_rows(src_ref, mid_ref, r):
    n = CHUNK // r
    if r <= DIL_STRIDE:
        return [src_ref[pl.ds(rr, n, stride=r), :] for rr in range(r)]
    r1, r2, n1 = DIL_STRIDE, r // DIL_STRIDE, CHUNK // DIL_STRIDE
    assert r2 <= DIL_STRIDE
    for a in range(r1):
        mid_ref[a * n1:(a + 1) * n1, :] = src_ref[pl.ds(a, n1, stride=r1), :]
    return [mid_ref[pl.ds((rr % r1) * n1 + rr // r1, n, stride=r2), :] for rr in range(r)]


def _store_class_rows(dst_ref, mid_ref, rows_of_class, r):
    n = CHUNK // r
    if r == 1:
        dst_ref[...] = rows_of_class[0]
    elif r <= DIL_STRIDE:
        for rr, rows in enumerate(rows_of_class):
            dst_ref[pl.ds(rr, n, stride=r), :] = rows
    else:
        r1, r2, n1 = DIL_STRIDE, r // DIL_STRIDE, CHUNK // DIL_STRIDE
        for rr, rows in enumerate(rows_of_class):
            mid_ref[pl.ds((rr % r1) * n1 + rr // r1, n, stride=r2), :] = rows
        for a in range(r1):
            dst_ref[pl.ds(a, n1, stride=r1), :] = mid_ref[a * n1:(a + 1) * n1, :]


def _deinterleave(dst_ref, first_block, src_ref, stage_ref, mid_ref, r):
    bpc = BLOCKS_PER_CHUNK // r
    if r == 1:
        dst_ref[0, first_block:first_block + bpc] = src_ref[...].reshape(bpc, A_BLOCK, A_OUT)
        return
    for h in range(A_HEADS_PER_GROUP):
        cols = slice(h * HEAD_DIM, (h + 1) * HEAD_DIM)
        stage_ref[h] = src_ref[:, cols].astype(jnp.float32)
        for rr, rows in enumerate(_class_rows(stage_ref.at[h], mid_ref, r)):
            dst_ref[rr, first_block:first_block + bpc, :, cols] = (
                rows.astype(dst_ref.dtype).reshape(bpc, A_BLOCK, HEAD_DIM))


def _dilated_kernel(q_ref, k_ref, v_ref, bias_ref, o_ref, lse_ref, stage_ref, mid_ref, qd_ref, kd_ref, vd_ref, *, r):
    c = pl.program_id(1)
    bpc = BLOCKS_PER_CHUNK // r
    nblk = BLOCKS_PER_CHUNK

    @pl.when(c == 0)
    def _():
        kd_ref[:, 0] = jnp.zeros((r, A_BLOCK, A_OUT), kd_ref.dtype)
        vd_ref[:, 0] = jnp.zeros((r, A_BLOCK, A_OUT), vd_ref.dtype)

    @pl.when(c > 0)
    def _():
        kd_ref[:, 0] = kd_ref[:, bpc]
        vd_ref[:, 0] = vd_ref[:, bpc]

    _deinterleave(qd_ref, 0, q_ref, stage_ref, mid_ref, r)
    _deinterleave(kd_ref, 1, k_ref, stage_ref, mid_ref, r)
    _deinterleave(vd_ref, 1, v_ref, stage_ref, mid_ref, r)

    blk = lax.broadcasted_iota(jnp.int32, (nblk, A_BLOCK, 2 * A_BLOCK), 0)
    key = lax.broadcasted_iota(jnp.int32, (nblk, A_BLOCK, 2 * A_BLOCK), 2)
    no_prev = (lax.rem(blk, bpc) == 0) & (key < A_BLOCK) & (c == 0)
    lane = lax.broadcasted_iota(jnp.int32, (nblk, A_BLOCK, HEAD_DIM), 2)
    bqk = (((2,), (2,)), ((0,), (0,)))
    bqd = (((2,), (1,)), ((0,), (0,)))
    class_rows = lambda t: [t[rr * bpc:(rr + 1) * bpc].reshape(bpc * A_BLOCK, HEAD_DIM) for rr in range(r)]
    lse = jnp.zeros((nblk, A_BLOCK, HEAD_DIM), jnp.float32)
    ones = jnp.ones((nblk, 2 * A_BLOCK, HEAD_DIM), jnp.bfloat16)
    for h in range(A_HEADS_PER_GROUP):
        cols = slice(h * HEAD_DIM, (h + 1) * HEAD_DIM)
        q = qd_ref[:, :, :, cols].reshape(nblk, A_BLOCK, HEAD_DIM)
        k2 = jnp.concatenate([kd_ref[:, 0:bpc, :, cols], kd_ref[:, 1:bpc + 1, :, cols]], axis=2)
        v2 = jnp.concatenate([vd_ref[:, 0:bpc, :, cols], vd_ref[:, 1:bpc + 1, :, cols]], axis=2)
        k2 = k2.reshape(nblk, 2 * A_BLOCK, HEAD_DIM)
        v2 = v2.reshape(nblk, 2 * A_BLOCK, HEAD_DIM)
        s = lax.dot_general(q, k2, bqk, preferred_element_type=jnp.float32) + bias_ref[0, h]
        s = jnp.where(no_prev, NEG, s)
        m = jnp.max(s, axis=-1, keepdims=True)
        p = jnp.exp2(s - m).astype(jnp.bfloat16)
        pv = lax.dot_general(p, jnp.concatenate([v2, ones], axis=-1), bqd, preferred_element_type=jnp.float32)
        den = pv[:, :, HEAD_DIM:]
        o = pv[:, :, :HEAD_DIM] / den
        lse = jnp.where(lane == h, m + jnp.log2(den), lse)
        _store_class_rows(stage_ref.at[h], mid_ref, class_rows(o), r)
        o_ref[:, cols] = stage_ref[h].astype(o_ref.dtype)
    _store_class_rows(lse_ref, mid_ref, class_rows(lse), r)


def _dilated_group(proj2, bias, group, *, batch, seq):
    _, r = DIL_PATTERNS[group]
    nc = seq // CHUNK
    bpc = BLOCKS_PER_CHUNK // r
    chunk = lambda sec: pl.BlockSpec((CHUNK, A_OUT), lambda b, c: (b * nc + c, sec * N_GROUPS + group))
    return pl.pallas_call(
        functools.partial(_dilated_kernel, r=r),
        grid=(batch, nc),
        in_specs=[chunk(0), chunk(1), chunk(2),
                  pl.BlockSpec((1, A_HEADS_PER_GROUP, A_BLOCK, 2 * A_BLOCK), lambda b, c: (group, 0, 0, 0))],
        out_specs=[pl.BlockSpec((CHUNK, A_OUT), lambda b, c: (b * nc + c, 0)),
                   pl.BlockSpec((CHUNK, HEAD_DIM), lambda b, c: (b * nc + c, 0))],
        out_shape=[jax.ShapeDtypeStruct((batch * seq, A_OUT), jnp.bfloat16),
                   jax.ShapeDtypeStruct((batch * seq, HEAD_DIM), jnp.float32)],
        scratch_shapes=[pltpu.VMEM((A_HEADS_PER_GROUP, CHUNK, HEAD_DIM), jnp.float32),
                        pltpu.VMEM((CHUNK, HEAD_DIM), jnp.float32),
                        pltpu.VMEM((r, bpc, A_BLOCK, A_OUT), jnp.bfloat16),
                        pltpu.VMEM((r, bpc + 1, A_BLOCK, A_OUT), jnp.bfloat16),
                        pltpu.VMEM((r, bpc + 1, A_BLOCK, A_OUT), jnp.bfloat16)],
        compiler_params=pltpu.CompilerParams(
            dimension_semantics=("arbitrary", "arbitrary"), vmem_limit_bytes=VMEM_LIMIT),
        name=f"dilated_{group}",
    )(proj2, proj2, proj2, bias)


def _split3(x):
    hi = x.astype(jnp.bfloat16)
    r1 = x - hi.astype(jnp.float32)
    mid = r1.astype(jnp.bfloat16)
    lo = (r1 - mid.astype(jnp.float32)).astype(jnp.bfloat16)
    return hi, mid, lo


MOBA_STREAMS = 3


def _moba_items(nb, group):
    streams = [[] for _ in range(MOBA_STREAMS)]
    for i in sorted(range(nb), key=lambda i: -(i // group)):
        min(streams, key=len).extend((i, g) for g in range(i // group + 1))
    assert len({len(s) for s in streams}) == 1 and len(streams[0]) % 2 == 0
    return streams


def _moba_kernel(items_ref, q_ref, k_ref, v_ref, bias_ref, *refs, nb, group, n_items, n_weights):
    w_refs, o_ref, wbf_refs = refs[:n_weights], refs[n_weights], refs[n_weights + 1:2 * n_weights + 1]
    avg_ref, kmean_ref, ka_ref, qa_ref, va_ref, s_ref, st_ref = refs[2 * n_weights + 1:]
    for w_ref, wbf_ref in zip(w_refs, wbf_refs):
        wbf_ref[...] = w_ref[...].astype(wbf_ref.dtype)

    BS = MOBA_BLOCK
    hd = HEAD_DIM
    seq = nb * BS
    rows_per_group = group * BS
    nt = (((1,), (1,)), ((), ()))

    row = lax.broadcasted_iota(jnp.int32, (nb, seq), 0)
    col = lax.broadcasted_iota(jnp.int32, (nb, seq), 1)

    @pl.when((pl.program_id(0) == 0) & (pl.program_id(1) == 0))
    def _():
        lo_edge = row * BS
        avg_ref[...] = jnp.where((col >= lo_edge) & (col < lo_edge + BS), 1.0 / BS, 0.0).astype(avg_ref.dtype)
        qa_ref[hd + nb:, :] = jnp.zeros((hd - nb, seq), qa_ref.dtype)
        key_blk = lax.broadcasted_iota(jnp.int32, (seq, hd), 0) // BS
        key_lane = lax.broadcasted_iota(jnp.int32, (seq, hd), 1)
        ka_ref[:, hd:] = jnp.where(key_lane == key_blk, 1.0, 0.0).astype(ka_ref.dtype)
        aux_row = lax.broadcasted_iota(jnp.int32, (MOBA_AUX, seq), 0)
        va_ref[hd:, :] = jnp.where(aux_row == 0, 1.0, 0.0).astype(va_ref.dtype)

    kmean = jnp.dot(avg_ref[...], k_ref[...], preferred_element_type=jnp.float32)
    for t, part in enumerate(_split3(kmean)):
        kmean_ref[t] = part

    q_all = q_ref[...]
    gate3 = lax.dot_general(kmean_ref[...].reshape(3 * nb, hd), q_all, nt, preferred_element_type=jnp.float32)
    gate = gate3[:nb] + gate3[nb:2 * nb] + gate3[2 * nb:]
    own = col // BS
    past = row < own
    gate = jnp.where(past, gate, -jnp.inf)
    sub = lax.broadcasted_iota(jnp.int32, (8, seq), 0)
    ranks = []
    for r0 in range(0, nb, 8):
        g8 = gate[r0:r0 + 8, :]
        rank8 = jnp.zeros((8, seq), jnp.int32)
        for m in range(nb):
            gm = gate[m:m + 1, :]
            if m >= r0 + 7:
                beats = gm > g8
            elif m < r0:
                beats = gm >= g8
            else:
                beats = (gm > g8) | ((gm == g8) & (sub > m - r0))
            rank8 = rank8 + beats.astype(jnp.int32)
        ranks.append(rank8)
    rank = jnp.concatenate(ranks, axis=0)
    sel = jnp.where((past & (rank < MOBA_TOPK)) | (row == own), 0.0, NEG)

    qa_ref[:hd, :] = q_all.T
    qa_ref[hd:hd + nb, :] = sel.astype(qa_ref.dtype)
    ka_ref[:, :hd] = k_ref[...]
    va_ref[:hd, :] = v_ref[...].T

    def logits(st, w, slot):
        i, g = items_ref[st, w, 0], items_ref[st, w, 1]
        rows = pl.ds(pl.multiple_of(g * rows_per_group, rows_per_group), rows_per_group)
        qcols = pl.ds(pl.multiple_of(i * BS, BS), BS)
        s_t = jnp.dot(ka_ref[rows, :], qa_ref[:, qcols], preferred_element_type=jnp.float32)
        m8 = jnp.full((8, BS), -jnp.inf, jnp.float32)
        for u in range(group):
            tile = jnp.clip(i - (g * group + u), 0, MOBA_BIAS_TILES - 1)
            s_u = s_t[u * BS:(u + 1) * BS] + bias_ref[0, tile]
            s_ref[2 * st + slot, u * BS:(u + 1) * BS, :] = s_u
            m8 = jnp.maximum(m8, jnp.max(s_u.reshape(BS // 8, 8, BS), axis=0))
        return m8

    def accumulate(st, w, slot, m8, state):
        m_run, acc = state
        i, g = items_ref[st, w, 0], items_ref[st, w, 1]
        rows = pl.ds(pl.multiple_of(g * rows_per_group, rows_per_group), rows_per_group)
        m_prev = jnp.where(g == 0, -jnp.inf, m_run)
        m_new = jnp.maximum(m_prev, jnp.max(m8, axis=0, keepdims=True))
        alpha = jnp.exp2(m_prev - m_new)
        p = jnp.exp2(s_ref[2 * st + slot] - m_new).astype(jnp.bfloat16)
        acc = alpha * acc + jnp.dot(va_ref[:, rows], p, preferred_element_type=jnp.float32)
        st_ref[i] = acc
        return m_new, acc

    streams = range(MOBA_STREAMS)

    def pair(t, carry):
        m8a, state = carry
        w = 2 * t
        m8b = [logits(st, w + 1, 1) for st in streams]
        state = [accumulate(st, w, 0, m8a[st], state[st]) for st in streams]
        m8a = [logits(st, jnp.minimum(w + 2, n_items - 1), 0) for st in streams]
        state = [accumulate(st, w + 1, 1, m8b[st], state[st]) for st in streams]
        return m8a, state

    state0 = (jnp.full((1, BS), -jnp.inf, jnp.float32), jnp.zeros((hd + MOBA_AUX, BS), jnp.float32))
    lax.fori_loop(0, n_items // 2, pair, ([logits(st, 0, 0) for st in streams], [state0 for _ in streams]))

    for i in range(nb):
        acc = st_ref[i]
        o_ref[i * BS:(i + 1) * BS, :] = (acc[:hd] / acc[hd:hd + 1]).T.astype(o_ref.dtype)


def _moba(proj2, bias, weights, *, batch, seq):
    BS = MOBA_BLOCK
    nb = seq // BS
    assert nb <= HEAD_DIM
    c0 = A_QKV // HEAD_DIM
    group = math.gcd(nb, MOBA_GROUP)
    items = _moba_items(nb, group)
    head_spec = lambda sec: pl.BlockSpec((seq, HEAD_DIM), lambda b, h, items: (b, c0 + sec * B_HEADS + h))
    n_steps = batch * B_HEADS
    assert all(w.shape[0] % (n_steps * 16) == 0 for w in weights)
    slab = lambda w: pl.BlockSpec((w.shape[0] // n_steps, w.shape[1]), lambda b, h, items: (b * B_HEADS + h, 0))
    outs = pl.pallas_call(
        functools.partial(_moba_kernel, nb=nb, group=group, n_items=len(items[0]), n_weights=len(weights)),
        grid_spec=pltpu.PrefetchScalarGridSpec(
            num_scalar_prefetch=1,
            grid=(batch, B_HEADS),
            in_specs=[head_spec(0), head_spec(1), head_spec(2),
                      pl.BlockSpec((1, MOBA_BIAS_TILES, BS, BS), lambda b, h, items: (h, 0, 0, 0))]
                     + [slab(w) for w in weights],
            out_specs=[pl.BlockSpec((seq, HEAD_DIM), lambda b, h, items: (b, h))] + [slab(w) for w in weights],
            scratch_shapes=[pltpu.VMEM((nb, seq), jnp.bfloat16),
                            pltpu.VMEM((3, nb, HEAD_DIM), jnp.bfloat16),
                            pltpu.VMEM((seq, 2 * HEAD_DIM), jnp.bfloat16),
                            pltpu.VMEM((2 * HEAD_DIM, seq), jnp.bfloat16),
                            pltpu.VMEM((HEAD_DIM + MOBA_AUX, seq), jnp.bfloat16),
                            pltpu.VMEM((2 * MOBA_STREAMS, group * BS, BS), jnp.float32),
                            pltpu.VMEM((nb, HEAD_DIM + MOBA_AUX, BS), jnp.float32)]),
        out_shape=[jax.ShapeDtypeStruct((batch * seq, B_WIDTH), jnp.bfloat16)]
                  + [jax.ShapeDtypeStruct(w.shape, jnp.bfloat16) for w in weights],
        compiler_params=pltpu.CompilerParams(
            dimension_semantics=("arbitrary", "arbitrary"), vmem_limit_bytes=VMEM_LIMIT),
        name="moba",
    )(jnp.asarray(items, jnp.int32), proj2, proj2, proj2, bias, *weights)
    return outs[0], outs[1:]


MERGE_ROW_CHUNKS = 2


def _merge_kernel(o0_ref, o1_ref, o2_ref, l0_ref, l1_ref, l2_ref, yb_ref, x_ref,
                  wa_ref, wb_ref, wo_ref, g_mlp_ref, *refs):
    gate_refs, (out_ref, h_ref) = refs[:-2], refs[-2:]
    ga_refs, gb_refs = gate_refs[:len(gate_refs) // 2], gate_refs[len(gate_refs) // 2:]
    rc = out_ref.shape[0] // MERGE_ROW_CHUNKS
    for c in range(MERGE_ROW_CHUNKS):
        rows = slice(c * rc, (c + 1) * rc)
        l0, l1, l2 = l0_ref[rows, :], l1_ref[rows, :], l2_ref[rows, :]
        mx = jnp.maximum(jnp.maximum(l0, l1), l2)
        e0, e1, e2 = jnp.exp2(l0 - mx), jnp.exp2(l1 - mx), jnp.exp2(l2 - mx)
        inv = 1.0 / (e0 + e1 + e2)
        parts = []
        for h in range(A_HEADS_PER_GROUP):
            cols = slice(h * HEAD_DIM, (h + 1) * HEAD_DIM)
            ya = ((e0 * inv)[:, h:h + 1] * o0_ref[rows, cols].astype(jnp.float32)
                  + (e1 * inv)[:, h:h + 1] * o1_ref[rows, cols].astype(jnp.float32)
                  + (e2 * inv)[:, h:h + 1] * o2_ref[rows, cols].astype(jnp.float32))
            parts.append(ya.astype(jnp.bfloat16))
        ya = jnp.concatenate(parts, axis=-1)
        pa = jnp.dot(ya, wa_ref[...], preferred_element_type=jnp.float32)
        pb = jnp.dot(yb_ref[rows, :], wb_ref[...], preferred_element_type=jnp.float32)
        ga = jnp.concatenate([g[rows, :] for g in ga_refs], axis=-1).astype(jnp.float32)
        gb = jnp.concatenate([g[rows, :] for g in gb_refs], axis=-1).astype(jnp.float32)
        mixed = ga * pa + gb * pb
        out = x_ref[rows, :] + jnp.dot(mixed.astype(jnp.bfloat16), wo_ref[...], preferred_element_type=jnp.float32)
        out_ref[rows, :] = out
        h_ref[rows, :] = _rms_rows(out, g_mlp_ref[...]).astype(h_ref.dtype)


def _merge(outs, lses, yb, proj2, x2, wa, wb, wo, g_mlp, *, tm):
    T, D = x2.shape
    gw = math.gcd(QKV_WIDTH, D)
    n_gate_blocks = 2 * D // gw
    row = lambda w: pl.BlockSpec((tm, w), lambda i: (i, 0))
    full = lambda a: pl.BlockSpec(a.shape, lambda i: (0, 0))
    gate = lambda c: pl.BlockSpec((tm, gw), lambda i: (i, QKV_WIDTH // gw + c))
    return pl.pallas_call(
        _merge_kernel,
        grid=(T // tm,),
        in_specs=[row(A_OUT)] * 3 + [row(HEAD_DIM)] * 3 + [row(B_WIDTH), row(D),
                  full(wa), full(wb), full(wo), full(g_mlp)] + [gate(c) for c in range(n_gate_blocks)],
        out_specs=[row(D), row(D)],
        out_shape=[jax.ShapeDtypeStruct((T, D), jnp.float32), jax.ShapeDtypeStruct((T, D), jnp.bfloat16)],
        compiler_params=pltpu.CompilerParams(dimension_semantics=("arbitrary",), vmem_limit_bytes=VMEM_LIMIT),
        name="merge",
    )(*outs, *lses, yb, x2, wa, wb, wo, g_mlp, *([proj2] * n_gate_blocks))


def _mlp_kernel(h_ref, x_ref, wu_ref, wd_ref, o_ref):
    f = pl.program_id(1)

    @pl.when(f == 0)
    def _():
        o_ref[...] = jnp.zeros(o_ref.shape, o_ref.dtype)

    slab = x_ref.shape[1]
    cols = pl.ds(pl.multiple_of(f * slab, slab), slab)
    o_ref[:, cols] += x_ref[...]

    u = jnp.dot(h_ref[...], wu_ref[...], preferred_element_type=jnp.float32)
    u = jnp.square(jnp.maximum(u, 0.0)).astype(jnp.bfloat16)
    o_ref[...] += jnp.dot(u, wd_ref[...], preferred_element_type=jnp.float32)


def _mlp(h2, x2, w_up, w_down, *, tm, tf):
    T, D = x2.shape
    F = w_up.shape[1]
    n_f = F // tf
    slab = D // n_f
    assert slab % HEAD_DIM == 0
    return pl.pallas_call(
        _mlp_kernel,
        grid=(T // tm, n_f),
        in_specs=[pl.BlockSpec((tm, D), lambda i, f: (i, 0)),
                  pl.BlockSpec((tm, slab), lambda i, f: (i, f)),
                  pl.BlockSpec((D, tf), lambda i, f: (0, f)),
                  pl.BlockSpec((tf, D), lambda i, f: (f, 0))],
        out_specs=pl.BlockSpec((tm, D), lambda i, f: (i, 0)),
        out_shape=jax.ShapeDtypeStruct((T, D), jnp.float32),
        compiler_params=pltpu.CompilerParams(
            dimension_semantics=("arbitrary", "arbitrary"), vmem_limit_bytes=VMEM_LIMIT),
        name="mlp",
    )(h2, x2, w_up, w_down)


def _column_gains(q_norm_a, k_norm_a, q_norm_b, k_norm_b, width):
    qs = SCALE * LOG2E
    gains = jnp.concatenate([
        jnp.tile(q_norm_a * qs, A_HEADS), jnp.tile(k_norm_a, A_HEADS), jnp.ones((A_HEADS * HEAD_DIM,), jnp.float32),
        jnp.tile(q_norm_b * qs, B_HEADS), jnp.tile(k_norm_b, B_HEADS), jnp.ones((B_WIDTH,), jnp.float32),
        jnp.ones((width - QKV_WIDTH,), jnp.float32)])
    return gains.reshape(1, width)


def kernel(x, g_mix, w_in, q_norm_a, k_norm_a, q_norm_b, k_norm_b, rel_bias,
           w_branch_a, w_branch_b, w_out, g_mlp, w_up, w_down):
    batch, seq, d_model = x.shape
    T = batch * seq
    assert w_in.shape == (d_model, QKV_WIDTH + 2 * d_model)
    assert seq % (DIL_PATTERNS[-1][1] * A_BLOCK) == 0
    x2 = x.reshape(T, d_model)
    tm = min(ROW_TILE, T)

    dil_bias, moba_bias = _bias_tables(rel_bias)
    gains = _column_gains(q_norm_a, k_norm_a, q_norm_b, k_norm_b, w_in.shape[1])
    proj = _in_proj(x2, g_mix.reshape(1, -1), w_in, gains, tm=tm, tn=min(COL_TILE, d_model * 2))

    outs, lses = zip(*[_dilated_group(proj, dil_bias, g, batch=batch, seq=seq) for g in range(N_GROUPS)])
    yb, (wa, wb, wo, wu, wd) = _moba(proj, moba_bias, (w_branch_a, w_branch_b, w_out, w_up, w_down),
                                     batch=batch, seq=seq)

    x_mid, h_mid = _merge(outs, lses, yb, proj, x2, wa, wb, wo, g_mlp.reshape(1, -1), tm=min(MERGE_ROW_TILE, T))
    y = _mlp(h_mid, x_mid, wu, wd, tm=tm, tf=min(MLP_FF_TILE, w_up.shape[1]))
    return y.reshape(batch, seq, d_model)
```

```python
import functools
import math

import jax
import jax.numpy as jnp
from jax import lax
from jax.experimental import pallas as pl
from jax.experimental.pallas import tpu as pltpu

HEAD_DIM = 128
DIL_PATTERNS = ((128, 1), (512, 4), (2048, 16))
N_GROUPS = len(DIL_PATTERNS)
A_HEADS_PER_GROUP = 4
A_HEADS = A_HEADS_PER_GROUP * N_GROUPS
A_OUT = A_HEADS_PER_GROUP * HEAD_DIM
A_BLOCK = 128
B_HEADS = 8
B_WIDTH = B_HEADS * HEAD_DIM
MOBA_BLOCK = 256
MOBA_TOPK = 3
MOBA_GROUP = 2
MOBA_AUX = 16
N_BUCKETS = 32
MAX_DISTANCE = 2048
EPS = 1e-6
SCALE = HEAD_DIM ** -0.5
A_QKV = 3 * A_HEADS * HEAD_DIM
B_QKV = 3 * B_WIDTH
QKV_WIDTH = A_QKV + B_QKV
LOG2E = math.log2(math.e)
NEG = -1e30
VMEM_LIMIT = 56 * 1024 * 1024
ROW_TILE = 1024
COL_TILE = 512
MERGE_ROW_TILE = 512
MLP_FF_TILE = 1024


def _bucket_thresholds():
    max_exact = N_BUCKETS // 2

    def bucket(d):
        if d < max_exact:
            return d
        v = int(math.log(d / max_exact) / math.log(MAX_DISTANCE / max_exact) * (N_BUCKETS - max_exact))
        return min(max_exact + v, N_BUCKETS - 1)

    thr, d = [0], 0
    for b in range(1, N_BUCKETS):
        while bucket(d) < b:
            d += 1
        thr.append(d)
    return tuple(thr)


BUCKET_THRESHOLDS = _bucket_thresholds()
MOBA_BIAS_TILES = -(-(BUCKET_THRESHOLDS[-1] - 1) // MOBA_BLOCK) + 2


def _bias_lookup(dist, tab_ref, col, dmin=0, dmax=None):
    first = sum(dmin >= t for t in BUCKET_THRESHOLDS[1:])
    val = jnp.full(dist.shape, tab_ref[first, col], jnp.float32)
    for b in range(first + 1, N_BUCKETS):
        if dmax is None or BUCKET_THRESHOLDS[b] <= dmax:
            val = jnp.where(dist >= BUCKET_THRESHOLDS[b], tab_ref[b, col], val)
    return val * LOG2E


def _dil_bias_kernel(tab_ref, o_ref):
    g, h = pl.program_id(0), pl.program_id(1)
    a = lax.broadcasted_iota(jnp.int32, (A_BLOCK, 2 * A_BLOCK), 0)
    j = lax.broadcasted_iota(jnp.int32, (A_BLOCK, 2 * A_BLOCK), 1)
    delta = a + A_BLOCK - j
    dilation = lax.shift_left(jnp.int32(1), 2 * g)
    val = _bias_lookup(delta * dilation, tab_ref, g * A_HEADS_PER_GROUP + h)
    o_ref[0, 0] = jnp.where((delta >= 0) & (delta <= A_BLOCK), val, NEG)


def _moba_bias_kernel(tab_ref, o_ref):
    h = pl.program_id(0)
    j = lax.broadcasted_iota(jnp.int32, (MOBA_BLOCK, MOBA_BLOCK), 0)
    s = lax.broadcasted_iota(jnp.int32, (MOBA_BLOCK, MOBA_BLOCK), 1)
    for c in range(MOBA_BIAS_TILES):
        dist = c * MOBA_BLOCK + s - j
        lo, hi = (c - 1) * MOBA_BLOCK + 1, (c + 1) * MOBA_BLOCK - 1
        val = _bias_lookup(dist, tab_ref, A_HEADS + h, max(lo, 0), hi)
        o_ref[0, c] = jnp.where(dist >= 0, val, NEG)


def _bias_tables(rel_bias):
    smem = pl.BlockSpec(memory_space=pltpu.SMEM)
    dil = pl.pallas_call(
        _dil_bias_kernel,
        grid=(N_GROUPS, A_HEADS_PER_GROUP),
        in_specs=[smem],
        out_specs=pl.BlockSpec((1, 1, A_BLOCK, 2 * A_BLOCK), lambda g, h: (g, h, 0, 0)),
        out_shape=jax.ShapeDtypeStruct((N_GROUPS, A_HEADS_PER_GROUP, A_BLOCK, 2 * A_BLOCK), jnp.float32),
        name="dil_bias",
    )(rel_bias)
    moba = pl.pallas_call(
        _moba_bias_kernel,
        grid=(B_HEADS,),
        in_specs=[smem],
        out_specs=pl.BlockSpec((1, MOBA_BIAS_TILES, MOBA_BLOCK, MOBA_BLOCK), lambda h: (h, 0, 0, 0)),
        out_shape=jax.ShapeDtypeStruct((B_HEADS, MOBA_BIAS_TILES, MOBA_BLOCK, MOBA_BLOCK), jnp.float32),
        name="moba_bias",
    )(rel_bias)
    return dil, moba


IN_PROJ_ROW_CHUNKS = 1
IN_PROJ_K_CHUNKS = 4
IN_PROJ_TILES_PER_STEP = 3


def _rms_rows(x, g):
    return (x * lax.rsqrt(jnp.mean(x * x, axis=-1, keepdims=True) + EPS)) * g


def _in_proj_tile_kind(col0):
    if col0 < 2 * A_HEADS * HEAD_DIM or A_QKV <= col0 < A_QKV + 2 * B_WIDTH:
        return "norm"
    return "plain" if col0 < QKV_WIDTH else "gate"


def _in_proj_kernel(x_ref, g_ref, *refs, tn, n_tiles):
    tps = IN_PROJ_TILES_PER_STEP
    w_refs, gain_refs, (o_ref, h_ref, wbf_ref) = refs[:tps], refs[tps:2 * tps], refs[2 * tps:]
    s = pl.program_id(1)

    def head_norm(acc, gain_ref):
        return jnp.concatenate(
            [_rms_rows(acc[:, c * HEAD_DIM:(c + 1) * HEAD_DIM], gain_ref[:, c * HEAD_DIM:(c + 1) * HEAD_DIM])
             for c in range(tn // HEAD_DIM)], axis=-1)

    epilogues = {
        "norm": head_norm,
        "plain": lambda acc, gain_ref: acc,
        "gate": lambda acc, gain_ref: 0.5 * jnp.tanh(0.5 * acc) + 0.5,
    }

    def project(kinds, normalise_input):
        tm, d = h_ref.shape
        rc = tm // IN_PROJ_ROW_CHUNKS
        kc = d // IN_PROJ_K_CHUNKS
        for c in range(IN_PROJ_ROW_CHUNKS):
            rows = slice(c * rc, (c + 1) * rc)
            if normalise_input:
                h_ref[rows, :] = _rms_rows(x_ref[rows, :], g_ref[...]).astype(h_ref.dtype)
            for t, kind in enumerate(kinds):
                if kind is None:
                    o_ref[rows, t * tn:(t + 1) * tn] = jnp.zeros((rc, tn), o_ref.dtype)
                    continue
                acc = None
                for k in range(IN_PROJ_K_CHUNKS):
                    ks = slice(k * kc, (k + 1) * kc)
                    if IN_PROJ_ROW_CHUNKS == 1:
                        wk = w_refs[t][ks, :].astype(jnp.bfloat16)
                    else:
                        if c == 0:
                            wbf_ref[t, ks, :] = w_refs[t][ks, :].astype(wbf_ref.dtype)
                        wk = wbf_ref[t, ks, :]
                    part = jnp.dot(h_ref[rows, ks], wk, preferred_element_type=jnp.float32)
                    acc = part if acc is None else acc + part
                o_ref[rows, t * tn:(t + 1) * tn] = epilogues[kind](acc, gain_refs[t]).astype(o_ref.dtype)

    steps_of = {}
    for step in range(-(-n_tiles // tps)):
        tiles = [step * tps + t for t in range(tps)]
        kinds = tuple(_in_proj_tile_kind(tile * tn) if tile < n_tiles else None for tile in tiles)
        steps_of.setdefault((kinds, step == 0), []).append(step)
    for (kinds, first), steps in steps_of.items():
        @pl.when(functools.reduce(jnp.logical_or, [s == step for step in steps]))
        def _(kinds=kinds, first=first):
            project(kinds, first)


def _in_proj(x2, g_mix, w_in, gains, *, tm, tn):
    T, D = x2.shape
    tps = IN_PROJ_TILES_PER_STEP
    n_tiles = w_in.shape[1] // tn
    n_steps = -(-n_tiles // tps)
    tile = lambda t: (lambda i, s: (0, jnp.minimum(s * tps + t, n_tiles - 1)))
    return pl.pallas_call(
        functools.partial(_in_proj_kernel, tn=tn, n_tiles=n_tiles),
        grid=(T // tm, n_steps),
        in_specs=[pl.BlockSpec((tm, D), lambda i, s: (i, 0)), pl.BlockSpec((1, D), lambda i, s: (0, 0))]
                 + [pl.BlockSpec((D, tn), tile(t)) for t in range(tps)]
                 + [pl.BlockSpec((1, tn), tile(t)) for t in range(tps)],
        out_specs=pl.BlockSpec((tm, tps * tn), lambda i, s: (i, s)),
        out_shape=jax.ShapeDtypeStruct((T, n_steps * tps * tn), jnp.bfloat16),
        scratch_shapes=[pltpu.VMEM((tm, D), jnp.bfloat16),
                        pltpu.VMEM((tps, D, tn) if IN_PROJ_ROW_CHUNKS > 1 else (1, 16, HEAD_DIM), jnp.bfloat16)],
        compiler_params=pltpu.CompilerParams(
            dimension_semantics=("arbitrary", "arbitrary"), vmem_limit_bytes=VMEM_LIMIT),
        name="in_proj",
    )(x2, g_mix, *([w_in] * tps), *([gains] * tps))


CHUNK = DIL_PATTERNS[-1][1] * A_BLOCK
BLOCKS_PER_CHUNK = CHUNK // A_BLOCK
DIL_STRIDE = 4


def _class_rows(src_ref, mid_ref, r):
    n = CHUNK // r
    if r <= DIL_STRIDE:
        return [src_ref[pl.ds(rr, n, stride=r), :] for rr in range(r)]
    r1, r2, n1 = DIL_STRIDE, r // DIL_STRIDE, CHUNK // DIL_STRIDE
    assert r2 <= DIL_STRIDE
    for a in range(r1):
        mid_ref[a * n1:(a + 1) * n1, :] = src_ref[pl.ds(a, n1, stride=r1), :]
    return [mid_ref[pl.ds((rr % r1) * n1 + rr // r1, n, stride=r2), :] for rr in range(r)]


def _store_class_rows(dst_ref, mid_ref, rows_of_class, r):
    n = CHUNK // r
    if r == 1:
        dst_ref[...] = rows_of_class[0]
    elif r <= DIL_STRIDE:
        for rr, rows in enumerate(rows_of_class):
            dst_ref[pl.ds(rr, n, stride=r), :] = rows
    else:
        r1, r2, n1 = DIL_STRIDE, r // DIL_STRIDE, CHUNK // DIL_STRIDE
        for rr, rows in enumerate(rows_of_class):
            mid_ref[pl.ds((rr % r1) * n1 + rr // r1, n, stride=r2), :] = rows
        for a in range(r1):
            dst_ref[pl.ds(a, n1, stride=r1), :] = mid_ref[a * n1:(a + 1) * n1, :]


def _deinterleave(dst_ref, first_block, src_ref, stage_ref, mid_ref, r):
    bpc = BLOCKS_PER_CHUNK // r
    if r == 1:
        dst_ref[0, first_block:first_block + bpc] = src_ref[...].reshape(bpc, A_BLOCK, A_OUT)
        return
    for h in range(A_HEADS_PER_GROUP):
        cols = slice(h * HEAD_DIM, (h + 1) * HEAD_DIM)
        stage_ref[h] = src_ref[:, cols].astype(jnp.float32)
        for rr, rows in enumerate(_class_rows(stage_ref.at[h], mid_ref, r)):
            dst_ref[rr, first_block:first_block + bpc, :, cols] = (
                rows.astype(dst_ref.dtype).reshape(bpc, A_BLOCK, HEAD_DIM))


def _dilated_kernel(q_ref, k_ref, v_ref, bias_ref, o_ref, lse_ref, stage_ref, mid_ref, qd_ref, kd_ref, vd_ref, *, r):
    c = pl.program_id(1)
    bpc = BLOCKS_PER_CHUNK // r
    nblk = BLOCKS_PER_CHUNK

    @pl.when(c == 0)
    def _():
        kd_ref[:, 0] = jnp.zeros((r, A_BLOCK, A_OUT), kd_ref.dtype)
        vd_ref[:, 0] = jnp.zeros((r, A_BLOCK, A_OUT), vd_ref.dtype)

    @pl.when(c > 0)
    def _():
        kd_ref[:, 0] = kd_ref[:, bpc]
        vd_ref[:, 0] = vd_ref[:, bpc]

    _deinterleave(qd_ref, 0, q_ref, stage_ref, mid_ref, r)
    _deinterleave(kd_ref, 1, k_ref, stage_ref, mid_ref, r)
    _deinterleave(vd_ref, 1, v_ref, stage_ref, mid_ref, r)

    blk = lax.broadcasted_iota(jnp.int32, (nblk, A_BLOCK, 2 * A_BLOCK), 0)
    key = lax.broadcasted_iota(jnp.int32, (nblk, A_BLOCK, 2 * A_BLOCK), 2)
    no_prev = (lax.rem(blk, bpc) == 0) & (key < A_BLOCK) & (c == 0)
    lane = lax.broadcasted_iota(jnp.int32, (nblk, A_BLOCK, HEAD_DIM), 2)
    bqk = (((2,), (2,)), ((0,), (0,)))
    bqd = (((2,), (1,)), ((0,), (0,)))
    class_rows = lambda t: [t[rr * bpc:(rr + 1) * bpc].reshape(bpc * A_BLOCK, HEAD_DIM) for rr in range(r)]
    lse = jnp.zeros((nblk, A_BLOCK, HEAD_DIM), jnp.float32)
    ones = jnp.ones((nblk, 2 * A_BLOCK, HEAD_DIM), jnp.bfloat16)
    for h in range(A_HEADS_PER_GROUP):
        cols = slice(h * HEAD_DIM, (h + 1) * HEAD_DIM)
        q = qd_ref[:, :, :, cols].reshape(nblk, A_BLOCK, HEAD_DIM)
        k2 = jnp.concatenate([kd_ref[:, 0:bpc, :, cols], kd_ref[:, 1:bpc + 1, :, cols]], axis=2)
        v2 = jnp.concatenate([vd_ref[:, 0:bpc, :, cols], vd_ref[:, 1:bpc + 1, :, cols]], axis=2)
        k2 = k2.reshape(nblk, 2 * A_BLOCK, HEAD_DIM)
        v2 = v2.reshape(nblk, 2 * A_BLOCK, HEAD_DIM)
        s = lax.dot_general(q, k2, bqk, preferred_element_type=jnp.float32) + bias_ref[0, h]
        s = jnp.where(no_prev, NEG, s)
        m = jnp.max(s, axis=-1, keepdims=True)
        p = jnp.exp2(s - m).astype(jnp.bfloat16)
        pv = lax.dot_general(p, jnp.concatenate([v2, ones], axis=-1), bqd, preferred_element_type=jnp.float32)
        den = pv[:, :, HEAD_DIM:]
        o = pv[:, :, :HEAD_DIM] / den
        lse = jnp.where(lane == h, m + jnp.log2(den), lse)
        _store_class_rows(stage_ref.at[h], mid_ref, class_rows(o), r)
        o_ref[:, cols] = stage_ref[h].astype(o_ref.dtype)
    _store_class_rows(lse_ref, mid_ref, class_rows(lse), r)


def _dilated_group(proj2, bias, group, *, batch, seq):
    _, r = DIL_PATTERNS[group]
    nc = seq // CHUNK
    bpc = BLOCKS_PER_CHUNK // r
    chunk = lambda sec: pl.BlockSpec((CHUNK, A_OUT), lambda b, c: (b * nc + c, sec * N_GROUPS + group))
    return pl.pallas_call(
        functools.partial(_dilated_kernel, r=r),
        grid=(batch, nc),
        in_specs=[chunk(0), chunk(1), chunk(2),
                  pl.BlockSpec((1, A_HEADS_PER_GROUP, A_BLOCK, 2 * A_BLOCK), lambda b, c: (group, 0, 0, 0))],
        out_specs=[pl.BlockSpec((CHUNK, A_OUT), lambda b, c: (b * nc + c, 0)),
                   pl.BlockSpec((CHUNK, HEAD_DIM), lambda b, c: (b * nc + c, 0))],
        out_shape=[jax.ShapeDtypeStruct((batch * seq, A_OUT), jnp.bfloat16),
                   jax.ShapeDtypeStruct((batch * seq, HEAD_DIM), jnp.float32)],
        scratch_shapes=[pltpu.VMEM((A_HEADS_PER_GROUP, CHUNK, HEAD_DIM), jnp.float32),
                        pltpu.VMEM((CHUNK, HEAD_DIM), jnp.float32),
                        pltpu.VMEM((r, bpc, A_BLOCK, A_OUT), jnp.bfloat16),
                        pltpu.VMEM((r, bpc + 1, A_BLOCK, A_OUT), jnp.bfloat16),
                        pltpu.VMEM((r, bpc + 1, A_BLOCK, A_OUT), jnp.bfloat16)],
        compiler_params=pltpu.CompilerParams(
            dimension_semantics=("arbitrary", "arbitrary"), vmem_limit_bytes=VMEM_LIMIT),
        name=f"dilated_{group}",
    )(proj2, proj2, proj2, bias)


def _split3(x):
    hi = x.astype(jnp.bfloat16)
    r1 = x - hi.astype(jnp.float32)
    mid = r1.astype(jnp.bfloat16)
    lo = (r1 - mid.astype(jnp.float32)).astype(jnp.bfloat16)
    return hi, mid, lo


MOBA_STREAMS = 3


def _moba_items(nb, group):
    streams = [[] for _ in range(MOBA_STREAMS)]
    for i in sorted(range(nb), key=lambda i: -(i // group)):
        min(streams, key=len).extend((i, g) for g in range(i // group + 1))
    assert len({len(s) for s in streams}) == 1 and len(streams[0]) % 2 == 0
    return streams


def _moba_kernel(items_ref, q_ref, k_ref, v_ref, bias_ref, *refs, nb, group, n_items, n_weights):
    w_refs, o_ref, wbf_refs = refs[:n_weights], refs[n_weights], refs[n_weights + 1:2 * n_weights + 1]
    avg_ref, kmean_ref, ka_ref, qa_ref, va_ref, s_ref, st_ref = refs[2 * n_weights + 1:]
    for w_ref, wbf_ref in zip(w_refs, wbf_refs):
        wbf_ref[...] = w_ref[...].astype(wbf_ref.dtype)

    BS = MOBA_BLOCK
    hd = HEAD_DIM
    seq = nb * BS
    rows_per_group = group * BS
    nt = (((1,), (1,)), ((), ()))

    row = lax.broadcasted_iota(jnp.int32, (nb, seq), 0)
    col = lax.broadcasted_iota(jnp.int32, (nb, seq), 1)

    @pl.when((pl.program_id(0) == 0) & (pl.program_id(1) == 0))
    def _():
        lo_edge = row * BS
        avg_ref[...] = jnp.where((col >= lo_edge) & (col < lo_edge + BS), 1.0 / BS, 0.0).astype(avg_ref.dtype)
        qa_ref[hd + nb:, :] = jnp.zeros((hd - nb, seq), qa_ref.dtype)
        key_blk = lax.broadcasted_iota(jnp.int32, (seq, hd), 0) // BS
        key_lane = lax.broadcasted_iota(jnp.int32, (seq, hd), 1)
        ka_ref[:, hd:] = jnp.where(key_lane == key_blk, 1.0, 0.0).astype(ka_ref.dtype)
        aux_row = lax.broadcasted_iota(jnp.int32, (MOBA_AUX, seq), 0)
        va_ref[hd:, :] = jnp.where(aux_row == 0, 1.0, 0.0).astype(va_ref.dtype)

    kmean = jnp.dot(avg_ref[...], k_ref[...], preferred_element_type=jnp.float32)
    for t, part in enumerate(_split3(kmean)):
        kmean_ref[t] = part

    q_all = q_ref[...]
    gate3 = lax.dot_general(kmean_ref[...].reshape(3 * nb, hd), q_all, nt, preferred_element_type=jnp.float32)
    gate = gate3[:nb] + gate3[nb:2 * nb] + gate3[2 * nb:]
    own = col // BS
    past = row < own
    gate = jnp.where(past, gate, -jnp.inf)
    sub = lax.broadcasted_iota(jnp.int32, (8, seq), 0)
    ranks = []
    for r0 in range(0, nb, 8):
        g8 = gate[r0:r0 + 8, :]
        rank8 = jnp.zeros((8, seq), jnp.int32)
        for m in range(nb):
            gm = gate[m:m + 1, :]
            if m >= r0 + 7:
                beats = gm > g8
            elif m < r0:
                beats = gm >= g8
            else:
                beats = (gm > g8) | ((gm == g8) & (sub > m - r0))
            rank8 = rank8 + beats.astype(jnp.int32)
        ranks.append(rank8)
    rank = jnp.concatenate(ranks, axis=0)
    sel = jnp.where((past & (rank < MOBA_TOPK)) | (row == own), 0.0, NEG)

    qa_ref[:hd, :] = q_all.T
    qa_ref[hd:hd + nb, :] = sel.astype(qa_ref.dtype)
    ka_ref[:, :hd] = k_ref[...]
    va_ref[:hd, :] = v_ref[...].T

    def logits(st, w, slot):
        i, g = items_ref[st, w, 0], items_ref[st, w, 1]
        rows = pl.ds(pl.multiple_of(g * rows_per_group, rows_per_group), rows_per_group)
        qcols = pl.ds(pl.multiple_of(i * BS, BS), BS)
        s_t = jnp.dot(ka_ref[rows, :], qa_ref[:, qcols], preferred_element_type=jnp.float32)
        m8 = jnp.full((8, BS), -jnp.inf, jnp.float32)
        for u in range(group):
            tile = jnp.clip(i - (g * group + u), 0, MOBA_BIAS_TILES - 1)
            s_u = s_t[u * BS:(u + 1) * BS] + bias_ref[0, tile]
            s_ref[2 * st + slot, u * BS:(u + 1) * BS, :] = s_u
            m8 = jnp.maximum(m8, jnp.max(s_u.reshape(BS // 8, 8, BS), axis=0))
        return m8

    def accumulate(st, w, slot, m8, state):
        m_run, acc = state
        i, g = items_ref[st, w, 0], items_ref[st, w, 1]
        rows = pl.ds(pl.multiple_of(g * rows_per_group, rows_per_group), rows_per_group)
        m_prev = jnp.where(g == 0, -jnp.inf, m_run)
        m_new = jnp.maximum(m_prev, jnp.max(m8, axis=0, keepdims=True))
        alpha = jnp.exp2(m_prev - m_new)
        p = jnp.exp2(s_ref[2 * st + slot] - m_new).astype(jnp.bfloat16)
        acc = alpha * acc + jnp.dot(va_ref[:, rows], p, preferred_element_type=jnp.float32)
        st_ref[i] = acc
        return m_new, acc

    streams = range(MOBA_STREAMS)

    def pair(t, carry):
        m8a, state = carry
        w = 2 * t
        m8b = [logits(st, w + 1, 1) for st in streams]
        state = [accumulate(st, w, 0, m8a[st], state[st]) for st in streams]
        m8a = [logits(st, jnp.minimum(w + 2, n_items - 1), 0) for st in streams]
        state = [accumulate(st, w + 1, 1, m8b[st], state[st]) for st in streams]
        return m8a, state

    state0 = (jnp.full((1, BS), -jnp.inf, jnp.float32), jnp.zeros((hd + MOBA_AUX, BS), jnp.float32))
    lax.fori_loop(0, n_items // 2, pair, ([logits(st, 0, 0) for st in streams], [state0 for _ in streams]))

    for i in range(nb):
        acc = st_ref[i]
        o_ref[i * BS:(i + 1) * BS, :] = (acc[:hd] / acc[hd:hd + 1]).T.astype(o_ref.dtype)


def _moba(proj2, bias, weights, *, batch, seq):
    BS = MOBA_BLOCK
    nb = seq // BS
    assert nb <= HEAD_DIM
    c0 = A_QKV // HEAD_DIM
    group = math.gcd(nb, MOBA_GROUP)
    items = _moba_items(nb, group)
    head_spec = lambda sec: pl.BlockSpec((seq, HEAD_DIM), lambda b, h, items: (b, c0 + sec * B_HEADS + h))
    n_steps = batch * B_HEADS
    assert all(w.shape[0] % (n_steps * 16) == 0 for w in weights)
    slab = lambda w: pl.BlockSpec((w.shape[0] // n_steps, w.shape[1]), lambda b, h, items: (b * B_HEADS + h, 0))
    outs = pl.pallas_call(
        functools.partial(_moba_kernel, nb=nb, group=group, n_items=len(items[0]), n_weights=len(weights)),
        grid_spec=pltpu.PrefetchScalarGridSpec(
            num_scalar_prefetch=1,
            grid=(batch, B_HEADS),
            in_specs=[head_spec(0), head_spec(1), head_spec(2),
                      pl.BlockSpec((1, MOBA_BIAS_TILES, BS, BS), lambda b, h, items: (h, 0, 0, 0))]
                     + [slab(w) for w in weights],
            out_specs=[pl.BlockSpec((seq, HEAD_DIM), lambda b, h, items: (b, h))] + [slab(w) for w in weights],
            scratch_shapes=[pltpu.VMEM((nb, seq), jnp.bfloat16),
                            pltpu.VMEM((3, nb, HEAD_DIM), jnp.bfloat16),
                            pltpu.VMEM((seq, 2 * HEAD_DIM), jnp.bfloat16),
                            pltpu.VMEM((2 * HEAD_DIM, seq), jnp.bfloat16),
                            pltpu.VMEM((HEAD_DIM + MOBA_AUX, seq), jnp.bfloat16),
                            pltpu.VMEM((2 * MOBA_STREAMS, group * BS, BS), jnp.float32),
                            pltpu.VMEM((nb, HEAD_DIM + MOBA_AUX, BS), jnp.float32)]),
        out_shape=[jax.ShapeDtypeStruct((batch * seq, B_WIDTH), jnp.bfloat16)]
                  + [jax.ShapeDtypeStruct(w.shape, jnp.bfloat16) for w in weights],
        compiler_params=pltpu.CompilerParams(
            dimension_semantics=("arbitrary", "arbitrary"), vmem_limit_bytes=VMEM_LIMIT),
        name="moba",
    )(jnp.asarray(items, jnp.int32), proj2, proj2, proj2, bias, *weights)
    return outs[0], outs[1:]


MERGE_ROW_CHUNKS = 2


def _merge_kernel(o0_ref, o1_ref, o2_ref, l0_ref, l1_ref, l2_ref, yb_ref, x_ref,
                  wa_ref, wb_ref, wo_ref, g_mlp_ref, *refs):
    gate_refs, (out_ref, h_ref) = refs[:-2], refs[-2:]
    ga_refs, gb_refs = gate_refs[:len(gate_refs) // 2], gate_refs[len(gate_refs) // 2:]
    rc = out_ref.shape[0] // MERGE_ROW_CHUNKS
    for c in range(MERGE_ROW_CHUNKS):
        rows = slice(c * rc, (c + 1) * rc)
        l0, l1, l2 = l0_ref[rows, :], l1_ref[rows, :], l2_ref[rows, :]
        mx = jnp.maximum(jnp.maximum(l0, l1), l2)
        e0, e1, e2 = jnp.exp2(l0 - mx), jnp.exp2(l1 - mx), jnp.exp2(l2 - mx)
        inv = 1.0 / (e0 + e1 + e2)
        parts = []
        for h in range(A_HEADS_PER_GROUP):
            cols = slice(h * HEAD_DIM, (h + 1) * HEAD_DIM)
            ya = ((e0 * inv)[:, h:h + 1] * o0_ref[rows, cols].astype(jnp.float32)
                  + (e1 * inv)[:, h:h + 1] * o1_ref[rows, cols].astype(jnp.float32)
                  + (e2 * inv)[:, h:h + 1] * o2_ref[rows, cols].astype(jnp.float32))
            parts.append(ya.astype(jnp.bfloat16))
        ya = jnp.concatenate(parts, axis=-1)
        pa = jnp.dot(ya, wa_ref[...], preferred_element_type=jnp.float32)
        pb = jnp.dot(yb_ref[rows, :], wb_ref[...], preferred_element_type=jnp.float32)
        ga = jnp.concatenate([g[rows, :] for g in ga_refs], axis=-1).astype(jnp.float32)
        gb = jnp.concatenate([g[rows, :] for g in gb_refs], axis=-1).astype(jnp.float32)
        mixed = ga * pa + gb * pb
        out = x_ref[rows, :] + jnp.dot(mixed.astype(jnp.bfloat16), wo_ref[...], preferred_element_type=jnp.float32)
        out_ref[rows, :] = out
        h_ref[rows, :] = _rms_rows(out, g_mlp_ref[...]).astype(h_ref.dtype)


def _merge(outs, lses, yb, proj2, x2, wa, wb, wo, g_mlp, *, tm):
    T, D = x2.shape
    gw = math.gcd(QKV_WIDTH, D)
    n_gate_blocks = 2 * D // gw
    row = lambda w: pl.BlockSpec((tm, w), lambda i: (i, 0))
    full = lambda a: pl.BlockSpec(a.shape, lambda i: (0, 0))
    gate = lambda c: pl.BlockSpec((tm, gw), lambda i: (i, QKV_WIDTH // gw + c))
    return pl.pallas_call(
        _merge_kernel,
        grid=(T // tm,),
        in_specs=[row(A_OUT)] * 3 + [row(HEAD_DIM)] * 3 + [row(B_WIDTH), row(D),
                  full(wa), full(wb), full(wo), full(g_mlp)] + [gate(c) for c in range(n_gate_blocks)],
        out_specs=[row(D), row(D)],
        out_shape=[jax.ShapeDtypeStruct((T, D), jnp.float32), jax.ShapeDtypeStruct((T, D), jnp.bfloat16)],
        compiler_params=pltpu.CompilerParams(dimension_semantics=("arbitrary",), vmem_limit_bytes=VMEM_LIMIT),
        name="merge",
    )(*outs, *lses, yb, x2, wa, wb, wo, g_mlp, *([proj2] * n_gate_blocks))


def _mlp_kernel(h_ref, x_ref, wu_ref, wd_ref, o_ref):
    f = pl.program_id(1)

    @pl.when(f == 0)
    def _():
        o_ref[...] = jnp.zeros(o_ref.shape, o_ref.dtype)

    slab = x_ref.shape[1]
    cols = pl.ds(pl.multiple_of(f * slab, slab), slab)
    o_ref[:, cols] += x_ref[...]

    u = jnp.dot(h_ref[...], wu_ref[...], preferred_element_type=jnp.float32)
    u = jnp.square(jnp.maximum(u, 0.0)).astype(jnp.bfloat16)
    o_ref[...] += jnp.dot(u, wd_ref[...], preferred_element_type=jnp.float32)


def _mlp(h2, x2, w_up, w_down, *, tm, tf):
    T, D = x2.shape
    F = w_up.shape[1]
    n_f = F // tf
    slab = D // n_f
    assert slab % HEAD_DIM == 0
    return pl.pallas_call(
        _mlp_kernel,
        grid=(T // tm, n_f),
        in_specs=[pl.BlockSpec((tm, D), lambda i, f: (i, 0)),
                  pl.BlockSpec((tm, slab), lambda i, f: (i, f)),
                  pl.BlockSpec((D, tf), lambda i, f: (0, f)),
                  pl.BlockSpec((tf, D), lambda i, f: (f, 0))],
        out_specs=pl.BlockSpec((tm, D), lambda i, f: (i, 0)),
        out_shape=jax.ShapeDtypeStruct((T, D), jnp.float32),
        compiler_params=pltpu.CompilerParams(
            dimension_semantics=("arbitrary", "arbitrary"), vmem_limit_bytes=VMEM_LIMIT),
        name="mlp",
    )(h2, x2, w_up, w_down)


def _column_gains(q_norm_a, k_norm_a, q_norm_b, k_norm_b, width):
    qs = SCALE * LOG2E
    gains = jnp.concatenate([
        jnp.tile(q_norm_a * qs, A_HEADS), jnp.tile(k_norm_a, A_HEADS), jnp.ones((A_HEADS * HEAD_DIM,), jnp.float32),
        jnp.tile(q_norm_b * qs, B_HEADS), jnp.tile(k_norm_b, B_HEADS), jnp.ones((B_WIDTH,), jnp.float32),
        jnp.ones((width - QKV_WIDTH,), jnp.float32)])
    return gains.reshape(1, width)


def kernel(x, g_mix, w_in, q_norm_a, k_norm_a, q_norm_b, k_norm_b, rel_bias,
           w_branch_a, w_branch_b, w_out, g_mlp, w_up, w_down):
    batch, seq, d_model = x.shape
    T = batch * seq
    assert w_in.shape == (d_model, QKV_WIDTH + 2 * d_model)
    assert seq % (DIL_PATTERNS[-1][1] * A_BLOCK) == 0
    x2 = x.reshape(T, d_model)
    tm = min(ROW_TILE, T)

    dil_bias, moba_bias = _bias_tables(rel_bias)
    gains = _column_gains(q_norm_a, k_norm_a, q_norm_b, k_norm_b, w_in.shape[1])
    proj = _in_proj(x2, g_mix.reshape(1, -1), w_in, gains, tm=tm, tn=min(COL_TILE, d_model * 2))

    outs, lses = zip(*[_dilated_group(proj, dil_bias, g, batch=batch, seq=seq) for g in range(N_GROUPS)])
    yb, (wa, wb, wo, wu, wd) = _moba(proj, moba_bias, (w_branch_a, w_branch_b, w_out, w_up, w_down),
                                     batch=batch, seq=seq)

    x_mid, h_mid = _merge(outs, lses, yb, proj, x2, wa, wb, wo, g_mlp.reshape(1, -1), tm=min(MERGE_ROW_TILE, T))
    y = _mlp(h_mid, x_mid, wu, wd, tm=tm, tf=min(MLP_FF_TILE, w_up.shape[1]))
    return y.reshape(batch, seq, d_model)
```

```python
import functools
import math

import jax
import jax.numpy as jnp
from jax import lax
from jax.experimental import pallas as pl
from jax.experimental.pallas import tpu as pltpu

HEAD_DIM = 128
DIL_PATTERNS = ((128, 1), (512, 4), (2048, 16))
N_GROUPS = len(DIL_PATTERNS)
A_HEADS_PER_GROUP = 4
A_HEADS = A_HEADS_PER_GROUP * N_GROUPS
A_OUT = A_HEADS_PER_GROUP * HEAD_DIM
A_BLOCK = 128
B_HEADS = 8
B_WIDTH = B_HEADS * HEAD_DIM
MOBA_BLOCK = 256
MOBA_TOPK = 3
MOBA_GROUP = 2
MOBA_AUX = 16
N_BUCKETS = 32
MAX_DISTANCE = 2048
EPS = 1e-6
SCALE = HEAD_DIM ** -0.5
A_QKV = 3 * A_HEADS * HEAD_DIM
B_QKV = 3 * B_WIDTH
QKV_WIDTH = A_QKV + B_QKV
LOG2E = math.log2(math.e)
NEG = -1e30
VMEM_LIMIT = 56 * 1024 * 1024
ROW_TILE = 1024
COL_TILE = 512
MERGE_ROW_TILE = 512
MLP_FF_TILE = 1024


def _bucket_thresholds():
    max_exact = N_BUCKETS // 2

    def bucket(d):
        if d < max_exact:
            return d
        v = int(math.log(d / max_exact) / math.log(MAX_DISTANCE / max_exact) * (N_BUCKETS - max_exact))
        return min(max_exact + v, N_BUCKETS - 1)

    thr, d = [0], 0
    for b in range(1, N_BUCKETS):
        while bucket(d) < b:
            d += 1
        thr.append(d)
    return tuple(thr)


BUCKET_THRESHOLDS = _bucket_thresholds()
MOBA_BIAS_TILES = -(-(BUCKET_THRESHOLDS[-1] - 1) // MOBA_BLOCK) + 2


def _bias_lookup(dist, tab_ref, col, dmin=0, dmax=None):
    first = sum(dmin >= t for t in BUCKET_THRESHOLDS[1:])
    val = jnp.full(dist.shape, tab_ref[first, col], jnp.float32)
    for b in range(first + 1, N_BUCKETS):
        if dmax is None or BUCKET_THRESHOLDS[b] <= dmax:
            val = jnp.where(dist >= BUCKET_THRESHOLDS[b], tab_ref[b, col], val)
    return val * LOG2E


def _dil_bias_kernel(tab_ref, o_ref):
    g, h = pl.program_id(0), pl.program_id(1)
    a = lax.broadcasted_iota(jnp.int32, (A_BLOCK, 2 * A_BLOCK), 0)
    j = lax.broadcasted_iota(jnp.int32, (A_BLOCK, 2 * A_BLOCK), 1)
    delta = a + A_BLOCK - j
    dilation = lax.shift_left(jnp.int32(1), 2 * g)
    val = _bias_lookup(delta * dilation, tab_ref, g * A_HEADS_PER_GROUP + h)
    o_ref[0, 0] = jnp.where((delta >= 0) & (delta <= A_BLOCK), val, NEG)


def _moba_bias_kernel(tab_ref, o_ref):
    h = pl.program_id(0)
    j = lax.broadcasted_iota(jnp.int32, (MOBA_BLOCK, MOBA_BLOCK), 0)
    s = lax.broadcasted_iota(jnp.int32, (MOBA_BLOCK, MOBA_BLOCK), 1)
    for c in range(MOBA_BIAS_TILES):
        dist = c * MOBA_BLOCK + s - j
        lo, hi = (c - 1) * MOBA_BLOCK + 1, (c + 1) * MOBA_BLOCK - 1
        val = _bias_lookup(dist, tab_ref, A_HEADS + h, max(lo, 0), hi)
        o_ref[0, c] = jnp.where(dist >= 0, val, NEG)


def _bias_tables(rel_bias):
    smem = pl.BlockSpec(memory_space=pltpu.SMEM)
    dil = pl.pallas_call(
        _dil_bias_kernel,
        grid=(N_GROUPS, A_HEADS_PER_GROUP),
        in_specs=[smem],
        out_specs=pl.BlockSpec((1, 1, A_BLOCK, 2 * A_BLOCK), lambda g, h: (g, h, 0, 0)),
        out_shape=jax.ShapeDtypeStruct((N_GROUPS, A_HEADS_PER_GROUP, A_BLOCK, 2 * A_BLOCK), jnp.float32),
        name="dil_bias",
    )(rel_bias)
    moba = pl.pallas_call(
        _moba_bias_kernel,
        grid=(B_HEADS,),
        in_specs=[smem],
        out_specs=pl.BlockSpec((1, MOBA_BIAS_TILES, MOBA_BLOCK, MOBA_BLOCK), lambda h: (h, 0, 0, 0)),
        out_shape=jax.ShapeDtypeStruct((B_HEADS, MOBA_BIAS_TILES, MOBA_BLOCK, MOBA_BLOCK), jnp.float32),
        name="moba_bias",
    )(rel_bias)
    return dil, moba


IN_PROJ_FIRST_STEP_ROW_CHUNKS = 2
IN_PROJ_K_CHUNKS = 4
IN_PROJ_TILES_PER_STEP = 3


def _rms_rows(x, g):
    return (x * lax.rsqrt(jnp.mean(x * x, axis=-1, keepdims=True) + EPS)) * g


def _in_proj_tile_kind(col0):
    if col0 < 2 * A_HEADS * HEAD_DIM or A_QKV <= col0 < A_QKV + 2 * B_WIDTH:
        return "norm"
    return "plain" if col0 < QKV_WIDTH else "gate"


def _in_proj_kernel(x_ref, g_ref, *refs, tn, n_tiles):
    tps = IN_PROJ_TILES_PER_STEP
    w_refs, gain_refs, (o_ref, h_ref) = refs[:tps], refs[tps:2 * tps], refs[2 * tps:]
    s = pl.program_id(1)

    def head_norm(acc, gain_ref):
        return jnp.concatenate(
            [_rms_rows(acc[:, c * HEAD_DIM:(c + 1) * HEAD_DIM], gain_ref[:, c * HEAD_DIM:(c + 1) * HEAD_DIM])
             for c in range(tn // HEAD_DIM)], axis=-1)

    epilogues = {
        "norm": head_norm,
        "plain": lambda acc, gain_ref: acc,
        "gate": lambda acc, gain_ref: 0.5 * jnp.tanh(0.5 * acc) + 0.5,
    }

    def project(kinds, normalise_input):
        tm, d = h_ref.shape
        n_row_chunks = IN_PROJ_FIRST_STEP_ROW_CHUNKS if normalise_input else 1
        rc = tm // n_row_chunks
        kc = d // IN_PROJ_K_CHUNKS
        for c in range(n_row_chunks):
            rows = slice(c * rc, (c + 1) * rc)
            if normalise_input:
                h_ref[rows, :] = _rms_rows(x_ref[rows, :], g_ref[...]).astype(h_ref.dtype)
            for t, kind in enumerate(kinds):
                if kind is None:
                    o_ref[rows, t * tn:(t + 1) * tn] = jnp.zeros((rc, tn), o_ref.dtype)
                    continue
                acc = None
                for k in range(IN_PROJ_K_CHUNKS):
                    ks = slice(k * kc, (k + 1) * kc)
                    part = jnp.dot(h_ref[rows, ks], w_refs[t][ks, :].astype(h_ref.dtype),
                                   preferred_element_type=jnp.float32)
                    acc = part if acc is None else acc + part
                o_ref[rows, t * tn:(t + 1) * tn] = epilogues[kind](acc, gain_refs[t]).astype(o_ref.dtype)

    steps_of = {}
    for step in range(-(-n_tiles // tps)):
        tiles = [step * tps + t for t in range(tps)]
        kinds = tuple(_in_proj_tile_kind(tile * tn) if tile < n_tiles else None for tile in tiles)
        steps_of.setdefault((kinds, step == 0), []).append(step)
    for (kinds, first), steps in steps_of.items():
        @pl.when(functools.reduce(jnp.logical_or, [s == step for step in steps]))
        def _(kinds=kinds, first=first):
            project(kinds, first)


def _in_proj(x2, g_mix, w_in, gains, *, tm, tn):
    T, D = x2.shape
    tps = IN_PROJ_TILES_PER_STEP
    n_tiles = w_in.shape[1] // tn
    n_steps = -(-n_tiles // tps)
    tile = lambda t: (lambda i, s: (0, jnp.minimum(s * tps + t, n_tiles - 1)))
    return pl.pallas_call(
        functools.partial(_in_proj_kernel, tn=tn, n_tiles=n_tiles),
        grid=(T // tm, n_steps),
        in_specs=[pl.BlockSpec((tm, D), lambda i, s: (i, 0)), pl.BlockSpec((1, D), lambda i, s: (0, 0))]
                 + [pl.BlockSpec((D, tn), tile(t)) for t in range(tps)]
                 + [pl.BlockSpec((1, tn), tile(t)) for t in range(tps)],
        out_specs=pl.BlockSpec((tm, tps * tn), lambda i, s: (i, s)),
        out_shape=jax.ShapeDtypeStruct((T, n_steps * tps * tn), jnp.bfloat16),
        scratch_shapes=[pltpu.VMEM((tm, D), jnp.bfloat16)],
        compiler_params=pltpu.CompilerParams(
            dimension_semantics=("arbitrary", "arbitrary"), vmem_limit_bytes=VMEM_LIMIT),
        name="in_proj",
    )(x2, g_mix, *([w_in] * tps), *([gains] * tps))


CHUNK = DIL_PATTERNS[-1][1] * A_BLOCK
BLOCKS_PER_CHUNK = CHUNK // A_BLOCK
DIL_STRIDE = 4


def _class_rows(src_ref, mid_ref, r):
    n = CHUNK // r
    if r <= DIL_STRIDE:
        return [src_ref[pl.ds(rr, n, stride=r), :] for rr in range(r)]
    r1, r2, n1 = DIL_STRIDE, r // DIL_STRIDE, CHUNK // DIL_STRIDE
    assert r2 <= DIL_STRIDE
    for a in range(r1):
        mid_ref[a * n1:(a + 1) * n1, :] = src_ref[pl.ds(a, n1, stride=r1), :]
    return [mid_ref[pl.ds((rr % r1) * n1 + rr // r1, n, stride=r2), :] for rr in range(r)]


def _store_class_rows(dst_ref, mid_ref, rows_of_class, r):
    n = CHUNK // r
    if r == 1:
        dst_ref[...] = rows_of_class[0]
    elif r <= DIL_STRIDE:
        for rr, rows in enumerate(rows_of_class):
            dst_ref[pl.ds(rr, n, stride=r), :] = rows
    else:
        r1, r2, n1 = DIL_STRIDE, r // DIL_STRIDE, CHUNK // DIL_STRIDE
        for rr, rows in enumerate(rows_of_class):
            mid_ref[pl.ds((rr % r1) * n1 + rr // r1, n, stride=r2), :] = rows
        for a in range(r1):
            dst_ref[pl.ds(a, n1, stride=r1), :] = mid_ref[a * n1:(a + 1) * n1, :]


def _deinterleave(dst_ref, first_block, src_ref, stage_ref, mid_ref, r):
    bpc = BLOCKS_PER_CHUNK // r
    if r == 1:
        dst_ref[0, first_block:first_block + bpc] = src_ref[...].reshape(bpc, A_BLOCK, A_OUT)
        return
    for h in range(A_HEADS_PER_GROUP):
        cols = slice(h * HEAD_DIM, (h + 1) * HEAD_DIM)
        stage_ref[h] = src_ref[:, cols].astype(jnp.float32)
        for rr, rows in enumerate(_class_rows(stage_ref.at[h], mid_ref, r)):
            dst_ref[rr, first_block:first_block + bpc, :, cols] = (
                rows.astype(dst_ref.dtype).reshape(bpc, A_BLOCK, HEAD_DIM))


def _dilated_kernel(q_ref, k_ref, v_ref, bias_ref, o_ref, lse_ref, stage_ref, mid_ref, qd_ref, kd_ref, vd_ref, *, r):
    c = pl.program_id(1)
    bpc = BLOCKS_PER_CHUNK // r
    nblk = BLOCKS_PER_CHUNK

    @pl.when(c == 0)
    def _():
        kd_ref[:, 0] = jnp.zeros((r, A_BLOCK, A_OUT), kd_ref.dtype)
        vd_ref[:, 0] = jnp.zeros((r, A_BLOCK, A_OUT), vd_ref.dtype)

    @pl.when(c > 0)
    def _():
        kd_ref[:, 0] = kd_ref[:, bpc]
        vd_ref[:, 0] = vd_ref[:, bpc]

    _deinterleave(qd_ref, 0, q_ref, stage_ref, mid_ref, r)
    _deinterleave(kd_ref, 1, k_ref, stage_ref, mid_ref, r)
    _deinterleave(vd_ref, 1, v_ref, stage_ref, mid_ref, r)

    blk = lax.broadcasted_iota(jnp.int32, (nblk, A_BLOCK, 2 * A_BLOCK), 0)
    key = lax.broadcasted_iota(jnp.int32, (nblk, A_BLOCK, 2 * A_BLOCK), 2)
    no_prev = (lax.rem(blk, bpc) == 0) & (key < A_BLOCK) & (c == 0)
    lane = lax.broadcasted_iota(jnp.int32, (nblk, A_BLOCK, HEAD_DIM), 2)
    bqk = (((2,), (2,)), ((0,), (0,)))
    bqd = (((2,), (1,)), ((0,), (0,)))
    class_rows = lambda t: [t[rr * bpc:(rr + 1) * bpc].reshape(bpc * A_BLOCK, HEAD_DIM) for rr in range(r)]
    lse = jnp.zeros((nblk, A_BLOCK, HEAD_DIM), jnp.float32)
    ones = jnp.ones((nblk, 2 * A_BLOCK, HEAD_DIM), jnp.bfloat16)
    for h in range(A_HEADS_PER_GROUP):
        cols = slice(h * HEAD_DIM, (h + 1) * HEAD_DIM)
        q = qd_ref[:, :, :, cols].reshape(nblk, A_BLOCK, HEAD_DIM)
        k2 = jnp.concatenate([kd_ref[:, 0:bpc, :, cols], kd_ref[:, 1:bpc + 1, :, cols]], axis=2)
        v2 = jnp.concatenate([vd_ref[:, 0:bpc, :, cols], vd_ref[:, 1:bpc + 1, :, cols]], axis=2)
        k2 = k2.reshape(nblk, 2 * A_BLOCK, HEAD_DIM)
        v2 = v2.reshape(nblk, 2 * A_BLOCK, HEAD_DIM)
        s = lax.dot_general(q, k2, bqk, preferred_element_type=jnp.float32) + bias_ref[0, h]
        s = jnp.where(no_prev, NEG, s)
        m = jnp.max(s, axis=-1, keepdims=True)
        p = jnp.exp2(s - m).astype(jnp.bfloat16)
        pv = lax.dot_general(p, jnp.concatenate([v2, ones], axis=-1), bqd, preferred_element_type=jnp.float32)
        den = pv[:, :, HEAD_DIM:]
        o = pv[:, :, :HEAD_DIM] / den
        lse = jnp.where(lane == h, m + jnp.log2(den), lse)
        _store_class_rows(stage_ref.at[h], mid_ref, class_rows(o), r)
        o_ref[:, cols] = stage_ref[h].astype(o_ref.dtype)
    _store_class_rows(lse_ref, mid_ref, class_rows(lse), r)


def _dilated_group(proj2, bias, group, *, batch, seq):
    _, r = DIL_PATTERNS[group]
    nc = seq // CHUNK
    bpc = BLOCKS_PER_CHUNK // r
    chunk = lambda sec: pl.BlockSpec((CHUNK, A_OUT), lambda b, c: (b * nc + c, sec * N_GROUPS + group))
    return pl.pallas_call(
        functools.partial(_dilated_kernel, r=r),
        grid=(batch, nc),
        in_specs=[chunk(0), chunk(1), chunk(2),
                  pl.BlockSpec((1, A_HEADS_PER_GROUP, A_BLOCK, 2 * A_BLOCK), lambda b, c: (group, 0, 0, 0))],
        out_specs=[pl.BlockSpec((CHUNK, A_OUT), lambda b, c: (b * nc + c, 0)),
                   pl.BlockSpec((CHUNK, HEAD_DIM), lambda b, c: (b * nc + c, 0))],
        out_shape=[jax.ShapeDtypeStruct((batch * seq, A_OUT), jnp.bfloat16),
                   jax.ShapeDtypeStruct((batch * seq, HEAD_DIM), jnp.float32)],
        scratch_shapes=[pltpu.VMEM((A_HEADS_PER_GROUP, CHUNK, HEAD_DIM), jnp.float32),
                        pltpu.VMEM((CHUNK, HEAD_DIM), jnp.float32),
                        pltpu.VMEM((r, bpc, A_BLOCK, A_OUT), jnp.bfloat16),
                        pltpu.VMEM((r, bpc + 1, A_BLOCK, A_OUT), jnp.bfloat16),
                        pltpu.VMEM((r, bpc + 1, A_BLOCK, A_OUT), jnp.bfloat16)],
        compiler_params=pltpu.CompilerParams(
            dimension_semantics=("arbitrary", "arbitrary"), vmem_limit_bytes=VMEM_LIMIT),
        name=f"dilated_{group}",
    )(proj2, proj2, proj2, bias)


def _split3(x):
    hi = x.astype(jnp.bfloat16)
    r1 = x - hi.astype(jnp.float32)
    mid = r1.astype(jnp.bfloat16)
    lo = (r1 - mid.astype(jnp.float32)).astype(jnp.bfloat16)
    return hi, mid, lo


MOBA_STREAMS = 3


def _moba_items(nb, group):
    streams = [[] for _ in range(MOBA_STREAMS)]
    for i in sorted(range(nb), key=lambda i: -(i // group)):
        min(streams, key=len).extend((i, g) for g in range(i // group + 1))
    assert len({len(s) for s in streams}) == 1 and len(streams[0]) % 2 == 0
    return streams


def _moba_kernel(items_ref, q_ref, k_ref, v_ref, bias_ref, *refs, nb, group, n_items, n_weights):
    w_refs, o_ref, wbf_refs = refs[:n_weights], refs[n_weights], refs[n_weights + 1:2 * n_weights + 1]
    avg_ref, kmean_ref, ka_ref, qa_ref, va_ref, s_ref, st_ref = refs[2 * n_weights + 1:]
    for w_ref, wbf_ref in zip(w_refs, wbf_refs):
        wbf_ref[...] = w_ref[...].astype(wbf_ref.dtype)

    BS = MOBA_BLOCK
    hd = HEAD_DIM
    seq = nb * BS
    rows_per_group = group * BS
    nt = (((1,), (1,)), ((), ()))

    row = lax.broadcasted_iota(jnp.int32, (nb, seq), 0)
    col = lax.broadcasted_iota(jnp.int32, (nb, seq), 1)

    @pl.when((pl.program_id(0) == 0) & (pl.program_id(1) == 0))
    def _():
        lo_edge = row * BS
        avg_ref[...] = jnp.where((col >= lo_edge) & (col < lo_edge + BS), 1.0 / BS, 0.0).astype(avg_ref.dtype)
        qa_ref[hd + nb:, :] = jnp.zeros((hd - nb, seq), qa_ref.dtype)
        key_blk = lax.broadcasted_iota(jnp.int32, (seq, hd), 0) // BS
        key_lane = lax.broadcasted_iota(jnp.int32, (seq, hd), 1)
        ka_ref[:, hd:] = jnp.where(key_lane == key_blk, 1.0, 0.0).astype(ka_ref.dtype)
        aux_row = lax.broadcasted_iota(jnp.int32, (MOBA_AUX, seq), 0)
        va_ref[hd:, :] = jnp.where(aux_row == 0, 1.0, 0.0).astype(va_ref.dtype)

    kmean = jnp.dot(avg_ref[...], k_ref[...], preferred_element_type=jnp.float32)
    for t, part in enumerate(_split3(kmean)):
        kmean_ref[t] = part

    q_all = q_ref[...]
    gate3 = lax.dot_general(kmean_ref[...].reshape(3 * nb, hd), q_all, nt, preferred_element_type=jnp.float32)
    gate = gate3[:nb] + gate3[nb:2 * nb] + gate3[2 * nb:]
    own = col // BS
    past = row < own
    gate = jnp.where(past, gate, -jnp.inf)
    sub = lax.broadcasted_iota(jnp.int32, (8, seq), 0)
    ranks = []
    for r0 in range(0, nb, 8):
        g8 = gate[r0:r0 + 8, :]
        rank8 = jnp.zeros((8, seq), jnp.int32)
        for m in range(nb):
            gm = gate[m:m + 1, :]
            if m >= r0 + 7:
                beats = gm > g8
            elif m < r0:
                beats = gm >= g8
            else:
                beats = (gm > g8) | ((gm == g8) & (sub > m - r0))
            rank8 = rank8 + beats.astype(jnp.int32)
        ranks.append(rank8)
    rank = jnp.concatenate(ranks, axis=0)
    sel = jnp.where((past & (rank < MOBA_TOPK)) | (row == own), 0.0, NEG)

    qa_ref[:hd, :] = q_all.T
    qa_ref[hd:hd + nb, :] = sel.astype(qa_ref.dtype)
    ka_ref[:, :hd] = k_ref[...]
    va_ref[:hd, :] = v_ref[...].T

    def logits(st, w, slot):
        i, g = items_ref[st, w, 0], items_ref[st, w, 1]
        rows = pl.ds(pl.multiple_of(g * rows_per_group, rows_per_group), rows_per_group)
        qcols = pl.ds(pl.multiple_of(i * BS, BS), BS)
        s_t = jnp.dot(ka_ref[rows, :], qa_ref[:, qcols], preferred_element_type=jnp.float32)
        m8 = jnp.full((8, BS), -jnp.inf, jnp.float32)
        for u in range(group):
            tile = jnp.clip(i - (g * group + u), 0, MOBA_BIAS_TILES - 1)
            s_u = s_t[u * BS:(u + 1) * BS] + bias_ref[0, tile]
            s_ref[2 * st + slot, u * BS:(u + 1) * BS, :] = s_u
            m8 = jnp.maximum(m8, jnp.max(s_u.reshape(BS // 8, 8, BS), axis=0))
        return m8

    def accumulate(st, w, slot, m8, state):
        m_run, acc = state
        i, g = items_ref[st, w, 0], items_ref[st, w, 1]
        rows = pl.ds(pl.multiple_of(g * rows_per_group, rows_per_group), rows_per_group)
        m_prev = jnp.where(g == 0, -jnp.inf, m_run)
        m_new = jnp.maximum(m_prev, jnp.max(m8, axis=0, keepdims=True))
        alpha = jnp.exp2(m_prev - m_new)
        p = jnp.exp2(s_ref[2 * st + slot] - m_new).astype(jnp.bfloat16)
        acc = alpha * acc + jnp.dot(va_ref[:, rows], p, preferred_element_type=jnp.float32)
        st_ref[i] = acc
        return m_new, acc

    streams = range(MOBA_STREAMS)

    def pair(t, carry):
        m8a, state = carry
        w = 2 * t
        m8b = [logits(st, w + 1, 1) for st in streams]
        state = [accumulate(st, w, 0, m8a[st], state[st]) for st in streams]
        m8a = [logits(st, jnp.minimum(w + 2, n_items - 1), 0) for st in streams]
        state = [accumulate(st, w + 1, 1, m8b[st], state[st]) for st in streams]
        return m8a, state

    state0 = (jnp.full((1, BS), -jnp.inf, jnp.float32), jnp.zeros((hd + MOBA_AUX, BS), jnp.float32))
    lax.fori_loop(0, n_items // 2, pair, ([logits(st, 0, 0) for st in streams], [state0 for _ in streams]))

    for i in range(nb):
        acc = st_ref[i]
        o_ref[i * BS:(i + 1) * BS, :] = (acc[:hd] / acc[hd:hd + 1]).T.astype(o_ref.dtype)


def _moba(proj2, bias, weights, *, batch, seq):
    BS = MOBA_BLOCK
    nb = seq // BS
    assert nb <= HEAD_DIM
    c0 = A_QKV // HEAD_DIM
    group = math.gcd(nb, MOBA_GROUP)
    items = _moba_items(nb, group)
    head_spec = lambda sec: pl.BlockSpec((seq, HEAD_DIM), lambda b, h, items: (b, c0 + sec * B_HEADS + h))
    n_steps = batch * B_HEADS
    assert all(w.shape[0] % (n_steps * 16) == 0 for w in weights)
    slab = lambda w: pl.BlockSpec((w.shape[0] // n_steps, w.shape[1]), lambda b, h, items: (b * B_HEADS + h, 0))
    outs = pl.pallas_call(
        functools.partial(_moba_kernel, nb=nb, group=group, n_items=len(items[0]), n_weights=len(weights)),
        grid_spec=pltpu.PrefetchScalarGridSpec(
            num_scalar_prefetch=1,
            grid=(batch, B_HEADS),
            in_specs=[head_spec(0), head_spec(1), head_spec(2),
                      pl.BlockSpec((1, MOBA_BIAS_TILES, BS, BS), lambda b, h, items: (h, 0, 0, 0))]
                     + [slab(w) for w in weights],
            out_specs=[pl.BlockSpec((seq, HEAD_DIM), lambda b, h, items: (b, h))] + [slab(w) for w in weights],
            scratch_shapes=[pltpu.VMEM((nb, seq), jnp.bfloat16),
                            pltpu.VMEM((3, nb, HEAD_DIM), jnp.bfloat16),
                            pltpu.VMEM((seq, 2 * HEAD_DIM), jnp.bfloat16),
                            pltpu.VMEM((2 * HEAD_DIM, seq), jnp.bfloat16),
                            pltpu.VMEM((HEAD_DIM + MOBA_AUX, seq), jnp.bfloat16),
                            pltpu.VMEM((2 * MOBA_STREAMS, group * BS, BS), jnp.float32),
                            pltpu.VMEM((nb, HEAD_DIM + MOBA_AUX, BS), jnp.float32)]),
        out_shape=[jax.ShapeDtypeStruct((batch * seq, B_WIDTH), jnp.bfloat16)]
                  + [jax.ShapeDtypeStruct(w.shape, jnp.bfloat16) for w in weights],
        compiler_params=pltpu.CompilerParams(
            dimension_semantics=("arbitrary", "arbitrary"), vmem_limit_bytes=VMEM_LIMIT),
        name="moba",
    )(jnp.asarray(items, jnp.int32), proj2, proj2, proj2, bias, *weights)
    return outs[0], outs[1:]


MERGE_ROW_CHUNKS = 2


def _merge_kernel(o0_ref, o1_ref, o2_ref, l0_ref, l1_ref, l2_ref, yb_ref, x_ref,
                  wa_ref, wb_ref, wo_ref, g_mlp_ref, *refs):
    gate_refs, (out_ref, h_ref) = refs[:-2], refs[-2:]
    ga_refs, gb_refs = gate_refs[:len(gate_refs) // 2], gate_refs[len(gate_refs) // 2:]
    rc = out_ref.shape[0] // MERGE_ROW_CHUNKS
    for c in range(MERGE_ROW_CHUNKS):
        rows = slice(c * rc, (c + 1) * rc)
        l0, l1, l2 = l0_ref[rows, :], l1_ref[rows, :], l2_ref[rows, :]
        mx = jnp.maximum(jnp.maximum(l0, l1), l2)
        e0, e1, e2 = jnp.exp2(l0 - mx), jnp.exp2(l1 - mx), jnp.exp2(l2 - mx)
        inv = 1.0 / (e0 + e1 + e2)
        parts = []
        for h in range(A_HEADS_PER_GROUP):
            cols = slice(h * HEAD_DIM, (h + 1) * HEAD_DIM)
            ya = ((e0 * inv)[:, h:h + 1] * o0_ref[rows, cols].astype(jnp.float32)
                  + (e1 * inv)[:, h:h + 1] * o1_ref[rows, cols].astype(jnp.float32)
                  + (e2 * inv)[:, h:h + 1] * o2_ref[rows, cols].astype(jnp.float32))
            parts.append(ya.astype(jnp.bfloat16))
        ya = jnp.concatenate(parts, axis=-1)
        pa = jnp.dot(ya, wa_ref[...], preferred_element_type=jnp.float32)
        pb = jnp.dot(yb_ref[rows, :], wb_ref[...], preferred_element_type=jnp.float32)
        ga = jnp.concatenate([g[rows, :] for g in ga_refs], axis=-1).astype(jnp.float32)
        gb = jnp.concatenate([g[rows, :] for g in gb_refs], axis=-1).astype(jnp.float32)
        mixed = ga * pa + gb * pb
        out = x_ref[rows, :] + jnp.dot(mixed.astype(jnp.bfloat16), wo_ref[...], preferred_element_type=jnp.float32)
        out_ref[rows, :] = out
        h_ref[rows, :] = _rms_rows(out, g_mlp_ref[...]).astype(h_ref.dtype)


def _merge(outs, lses, yb, proj2, x2, wa, wb, wo, g_mlp, *, tm):
    T, D = x2.shape
    gw = math.gcd(QKV_WIDTH, D)
    n_gate_blocks = 2 * D // gw
    row = lambda w: pl.BlockSpec((tm, w), lambda i: (i, 0))
    full = lambda a: pl.BlockSpec(a.shape, lambda i: (0, 0))
    gate = lambda c: pl.BlockSpec((tm, gw), lambda i: (i, QKV_WIDTH // gw + c))
    return pl.pallas_call(
        _merge_kernel,
        grid=(T // tm,),
        in_specs=[row(A_OUT)] * 3 + [row(HEAD_DIM)] * 3 + [row(B_WIDTH), row(D),
                  full(wa), full(wb), full(wo), full(g_mlp)] + [gate(c) for c in range(n_gate_blocks)],
        out_specs=[row(D), row(D)],
        out_shape=[jax.ShapeDtypeStruct((T, D), jnp.float32), jax.ShapeDtypeStruct((T, D), jnp.bfloat16)],
        compiler_params=pltpu.CompilerParams(dimension_semantics=("arbitrary",), vmem_limit_bytes=VMEM_LIMIT),
        name="merge",
    )(*outs, *lses, yb, x2, wa, wb, wo, g_mlp, *([proj2] * n_gate_blocks))


def _mlp_kernel(h_ref, x_ref, wu_ref, wd_ref, o_ref):
    f = pl.program_id(1)

    @pl.when(f == 0)
    def _():
        o_ref[...] = jnp.zeros(o_ref.shape, o_ref.dtype)

    slab = x_ref.shape[1]
    cols = pl.ds(pl.multiple_of(f * slab, slab), slab)
    o_ref[:, cols] += x_ref[...]

    u = jnp.dot(h_ref[...], wu_ref[...], preferred_element_type=jnp.float32)
    u = jnp.square(jnp.maximum(u, 0.0)).astype(jnp.bfloat16)
    o_ref[...] += jnp.dot(u, wd_ref[...], preferred_element_type=jnp.float32)


def _mlp(h2, x2, w_up, w_down, *, tm, tf):
    T, D = x2.shape
    F = w_up.shape[1]
    n_f = F // tf
    slab = D // n_f
    assert slab % HEAD_DIM == 0
    return pl.pallas_call(
        _mlp_kernel,
        grid=(T // tm, n_f),
        in_specs=[pl.BlockSpec((tm, D), lambda i, f: (i, 0)),
                  pl.BlockSpec((tm, slab), lambda i, f: (i, f)),
                  pl.BlockSpec((D, tf), lambda i, f: (0, f)),
                  pl.BlockSpec((tf, D), lambda i, f: (f, 0))],
        out_specs=pl.BlockSpec((tm, D), lambda i, f: (i, 0)),
        out_shape=jax.ShapeDtypeStruct((T, D), jnp.float32),
        compiler_params=pltpu.CompilerParams(
            dimension_semantics=("arbitrary", "arbitrary"), vmem_limit_bytes=VMEM_LIMIT),
        name="mlp",
    )(h2, x2, w_up, w_down)


def _column_gains(q_norm_a, k_norm_a, q_norm_b, k_norm_b, width):
    qs = SCALE * LOG2E
    gains = jnp.concatenate([
        jnp.tile(q_norm_a * qs, A_HEADS), jnp.tile(k_norm_a, A_HEADS), jnp.ones((A_HEADS * HEAD_DIM,), jnp.float32),
        jnp.tile(q_norm_b * qs, B_HEADS), jnp.tile(k_norm_b, B_HEADS), jnp.ones((B_WIDTH,), jnp.float32),
        jnp.ones((width - QKV_WIDTH,), jnp.float32)])
    return gains.reshape(1, width)


def kernel(x, g_mix, w_in, q_norm_a, k_norm_a, q_norm_b, k_norm_b, rel_bias,
           w_branch_a, w_branch_b, w_out, g_mlp, w_up, w_down):
    batch, seq, d_model = x.shape
    T = batch * seq
    assert w_in.shape == (d_model, QKV_WIDTH + 2 * d_model)
    assert seq % (DIL_PATTERNS[-1][1] * A_BLOCK) == 0
    x2 = x.reshape(T, d_model)
    tm = min(ROW_TILE, T)

    dil_bias, moba_bias = _bias_tables(rel_bias)
    gains = _column_gains(q_norm_a, k_norm_a, q_norm_b, k_norm_b, w_in.shape[1])
    proj = _in_proj(x2, g_mix.reshape(1, -1), w_in, gains, tm=tm, tn=min(COL_TILE, d_model * 2))

    outs, lses = zip(*[_dilated_group(proj, dil_bias, g, batch=batch, seq=seq) for g in range(N_GROUPS)])
    yb, (wa, wb, wo, wu, wd) = _moba(proj, moba_bias, (w_branch_a, w_branch_b, w_out, w_up, w_down),
                                     batch=batch, seq=seq)

    x_mid, h_mid = _merge(outs, lses, yb, proj, x2, wa, wb, wo, g_mlp.reshape(1, -1), tm=min(MERGE_ROW_TILE, T))
    y = _mlp(h_mid, x_mid, wu, wd, tm=tm, tf=min(MLP_FF_TILE, w_up.shape[1]))
    return y.reshape(batch, seq, d_model)
```

```python
import functools
import math

import jax
import jax.numpy as jnp
from jax import lax
from jax.experimental import pallas as pl
from jax.experimental.pallas import tpu as pltpu

HEAD_DIM = 128
DIL_PATTERNS = ((128, 1), (512, 4), (2048, 16))
N_GROUPS = len(DIL_PATTERNS)
A_HEADS_PER_GROUP = 4
A_HEADS = A_HEADS_PER_GROUP * N_GROUPS
A_OUT = A_HEADS_PER_GROUP * HEAD_DIM
A_BLOCK = 128
B_HEADS = 8
B_WIDTH = B_HEADS * HEAD_DIM
MOBA_BLOCK = 256
MOBA_TOPK = 3
MOBA_GROUP = 2
MOBA_AUX = 16
N_BUCKETS = 32
MAX_DISTANCE = 2048
EPS = 1e-6
SCALE = HEAD_DIM ** -0.5
A_QKV = 3 * A_HEADS * HEAD_DIM
B_QKV = 3 * B_WIDTH
QKV_WIDTH = A_QKV + B_QKV
LOG2E = math.log2(math.e)
NEG = -1e30
VMEM_LIMIT = 56 * 1024 * 1024
ROW_TILE = 1024
COL_TILE = 512
MERGE_ROW_TILE = 512
MLP_FF_TILE = 1024


def _bucket_thresholds():
    max_exact = N_BUCKETS // 2

    def bucket(d):
        if d < max_exact:
            return d
        v = int(math.log(d / max_exact) / math.log(MAX_DISTANCE / max_exact) * (N_BUCKETS - max_exact))
        return min(max_exact + v, N_BUCKETS - 1)

    thr, d = [0], 0
    for b in range(1, N_BUCKETS):
        while bucket(d) < b:
            d += 1
        thr.append(d)
    return tuple(thr)


BUCKET_THRESHOLDS = _bucket_thresholds()
MOBA_BIAS_TILES = -(-(BUCKET_THRESHOLDS[-1] - 1) // MOBA_BLOCK) + 2


def _bias_lookup(dist, tab_ref, col, dmin=0, dmax=None):
    first = sum(dmin >= t for t in BUCKET_THRESHOLDS[1:])
    val = jnp.full(dist.shape, tab_ref[first, col], jnp.float32)
    for b in range(first + 1, N_BUCKETS):
        if dmax is None or BUCKET_THRESHOLDS[b] <= dmax:
            val = jnp.where(dist >= BUCKET_THRESHOLDS[b], tab_ref[b, col], val)
    return val * LOG2E


def _dil_bias_kernel(tab_ref, o_ref):
    g, h = pl.program_id(0), pl.program_id(1)
    a = lax.broadcasted_iota(jnp.int32, (A_BLOCK, 2 * A_BLOCK), 0)
    j = lax.broadcasted_iota(jnp.int32, (A_BLOCK, 2 * A_BLOCK), 1)
    delta = a + A_BLOCK - j
    dilation = lax.shift_left(jnp.int32(1), 2 * g)
    val = _bias_lookup(delta * dilation, tab_ref, g * A_HEADS_PER_GROUP + h)
    o_ref[0, 0] = jnp.where((delta >= 0) & (delta <= A_BLOCK), val, NEG)


def _moba_bias_kernel(tab_ref, o_ref):
    h = pl.program_id(0)
    j = lax.broadcasted_iota(jnp.int32, (MOBA_BLOCK, MOBA_BLOCK), 0)
    s = lax.broadcasted_iota(jnp.int32, (MOBA_BLOCK, MOBA_BLOCK), 1)
    for c in range(MOBA_BIAS_TILES):
        dist = c * MOBA_BLOCK + s - j
        lo, hi = (c - 1) * MOBA_BLOCK + 1, (c + 1) * MOBA_BLOCK - 1
        val = _bias_lookup(dist, tab_ref, A_HEADS + h, max(lo, 0), hi)
        o_ref[0, c] = jnp.where(dist >= 0, val, NEG)


def _bias_tables(rel_bias):
    smem = pl.BlockSpec(memory_space=pltpu.SMEM)
    dil = pl.pallas_call(
        _dil_bias_kernel,
        grid=(N_GROUPS, A_HEADS_PER_GROUP),
        in_specs=[smem],
        out_specs=pl.BlockSpec((1, 1, A_BLOCK, 2 * A_BLOCK), lambda g, h: (g, h, 0, 0)),
        out_shape=jax.ShapeDtypeStruct((N_GROUPS, A_HEADS_PER_GROUP, A_BLOCK, 2 * A_BLOCK), jnp.float32),
        name="dil_bias",
    )(rel_bias)
    moba = pl.pallas_call(
        _moba_bias_kernel,
        grid=(B_HEADS,),
        in_specs=[smem],
        out_specs=pl.BlockSpec((1, MOBA_BIAS_TILES, MOBA_BLOCK, MOBA_BLOCK), lambda h: (h, 0, 0, 0)),
        out_shape=jax.ShapeDtypeStruct((B_HEADS, MOBA_BIAS_TILES, MOBA_BLOCK, MOBA_BLOCK), jnp.float32),
        name="moba_bias",
    )(rel_bias)
    return dil, moba


IN_PROJ_FIRST_STEP_ROW_CHUNKS = 2
IN_PROJ_K_CHUNKS = 4
IN_PROJ_TILES_PER_STEP = 3


def _rms_rows(x, g):
    return (x * lax.rsqrt(jnp.mean(x * x, axis=-1, keepdims=True) + EPS)) * g


def _in_proj_tile_kind(col0):
    if col0 < 2 * A_HEADS * HEAD_DIM or A_QKV <= col0 < A_QKV + 2 * B_WIDTH:
        return "norm"
    return "plain" if col0 < QKV_WIDTH else "gate"


def _in_proj_kernel(x_ref, g_ref, *refs, tn, n_tiles):
    tps = IN_PROJ_TILES_PER_STEP
    w_refs, gain_refs, (o_ref, h_ref) = refs[:tps], refs[tps:2 * tps], refs[2 * tps:]
    s = pl.program_id(1)

    def head_norm(acc, gain_ref):
        return jnp.concatenate(
            [_rms_rows(acc[:, c * HEAD_DIM:(c + 1) * HEAD_DIM], gain_ref[:, c * HEAD_DIM:(c + 1) * HEAD_DIM])
             for c in range(tn // HEAD_DIM)], axis=-1)

    epilogues = {
        "norm": head_norm,
        "plain": lambda acc, gain_ref: acc,
        "gate": lambda acc, gain_ref: 0.5 * jnp.tanh(0.5 * acc) + 0.5,
    }

    def project(kinds, normalise_input):
        tm, d = h_ref.shape
        n_row_chunks = IN_PROJ_FIRST_STEP_ROW_CHUNKS if normalise_input else 1
        rc = tm // n_row_chunks
        kc = d // IN_PROJ_K_CHUNKS
        for c in range(n_row_chunks):
            rows = slice(c * rc, (c + 1) * rc)
            if normalise_input:
                h_ref[rows, :] = _rms_rows(x_ref[rows, :], g_ref[...]).astype(h_ref.dtype)
            for t, kind in enumerate(kinds):
                if kind is None:
                    o_ref[rows, t * tn:(t + 1) * tn] = jnp.zeros((rc, tn), o_ref.dtype)
                    continue
                acc = None
                for k in range(IN_PROJ_K_CHUNKS):
                    ks = slice(k * kc, (k + 1) * kc)
                    part = jnp.dot(h_ref[rows, ks], w_refs[t][ks, :].astype(h_ref.dtype),
                                   preferred_element_type=jnp.float32)
                    acc = part if acc is None else acc + part
                o_ref[rows, t * tn:(t + 1) * tn] = epilogues[kind](acc, gain_refs[t]).astype(o_ref.dtype)

    steps_of = {}
    for step in range(-(-n_tiles // tps)):
        tiles = [step * tps + t for t in range(tps)]
        kinds = tuple(_in_proj_tile_kind(tile * tn) if tile < n_tiles else None for tile in tiles)
        steps_of.setdefault((kinds, step == 0), []).append(step)
    for (kinds, first), steps in steps_of.items():
        @pl.when(functools.reduce(jnp.logical_or, [s == step for step in steps]))
        def _(kinds=kinds, first=first):
            project(kinds, first)


def _in_proj(x2, g_mix, w_in, gains, *, tm, tn):
    T, D = x2.shape
    tps = IN_PROJ_TILES_PER_STEP
    n_tiles = w_in.shape[1] // tn
    n_steps = -(-n_tiles // tps)
    tile = lambda t: (lambda i, s: (0, jnp.minimum(s * tps + t, n_tiles - 1)))
    return pl.pallas_call(
        functools.partial(_in_proj_kernel, tn=tn, n_tiles=n_tiles),
        grid=(T // tm, n_steps),
        in_specs=[pl.BlockSpec((tm, D), lambda i, s: (i, 0)), pl.BlockSpec((1, D), lambda i, s: (0, 0))]
                 + [pl.BlockSpec((D, tn), tile(t)) for t in range(tps)]
                 + [pl.BlockSpec((1, tn), tile(t)) for t in range(tps)],
        out_specs=pl.BlockSpec((tm, tps * tn), lambda i, s: (i, s)),
        out_shape=jax.ShapeDtypeStruct((T, n_steps * tps * tn), jnp.bfloat16),
        scratch_shapes=[pltpu.VMEM((tm, D), jnp.bfloat16)],
        compiler_params=pltpu.CompilerParams(
            dimension_semantics=("arbitrary", "arbitrary"), vmem_limit_bytes=VMEM_LIMIT),
        name="in_proj",
    )(x2, g_mix, *([w_in] * tps), *([gains] * tps))


CHUNK = DIL_PATTERNS[-1][1] * A_BLOCK
BLOCKS_PER_CHUNK = CHUNK // A_BLOCK
DIL_BATCH_PARTS = 2
DIL_STRIDE = 4


def _class_rows(src_ref, mid_ref, r):
    n = CHUNK // r
    if r <= DIL_STRIDE:
        return [src_ref[pl.ds(rr, n, stride=r), :] for rr in range(r)]
    r1, r2, n1 = DIL_STRIDE, r // DIL_STRIDE, CHUNK // DIL_STRIDE
    assert r2 <= DIL_STRIDE
    for a in range(r1):
        mid_ref[a * n1:(a + 1) * n1, :] = src_ref[pl.ds(a, n1, stride=r1), :]
    return [mid_ref[pl.ds((rr % r1) * n1 + rr // r1, n, stride=r2), :] for rr in range(r)]


def _store_class_rows(dst_ref, mid_ref, rows_of_class, r):
    n = CHUNK // r
    if r == 1:
        dst_ref[...] = rows_of_class[0]
    elif r <= DIL_STRIDE:
        for rr, rows in enumerate(rows_of_class):
            dst_ref[pl.ds(rr, n, stride=r), :] = rows
    else:
        r1, r2, n1 = DIL_STRIDE, r // DIL_STRIDE, CHUNK // DIL_STRIDE
        for rr, rows in enumerate(rows_of_class):
            mid_ref[pl.ds((rr % r1) * n1 + rr // r1, n, stride=r2), :] = rows
        for a in range(r1):
            dst_ref[pl.ds(a, n1, stride=r1), :] = mid_ref[a * n1:(a + 1) * n1, :]


def _deinterleave(dst_ref, first_block, src_ref, stage_ref, mid_ref, r):
    bpc = BLOCKS_PER_CHUNK // r
    if r == 1:
        dst_ref[0, first_block:first_block + bpc] = src_ref[...].reshape(bpc, A_BLOCK, A_OUT)
        return
    for h in range(A_HEADS_PER_GROUP):
        cols = slice(h * HEAD_DIM, (h + 1) * HEAD_DIM)
        stage_ref[h] = src_ref[:, cols].astype(jnp.float32)
        for rr, rows in enumerate(_class_rows(stage_ref.at[h], mid_ref, r)):
            dst_ref[rr, first_block:first_block + bpc, :, cols] = (
                rows.astype(dst_ref.dtype).reshape(bpc, A_BLOCK, HEAD_DIM))


def _dilated_kernel(q_ref, k_ref, v_ref, bias_ref, o_ref, lse_ref, stage_ref, mid_ref, qd_ref, kd_ref, vd_ref, *, r):
    c = pl.program_id(1)
    bpc = BLOCKS_PER_CHUNK // r
    nblk = BLOCKS_PER_CHUNK

    @pl.when(c == 0)
    def _():
        kd_ref[:, 0] = jnp.zeros((r, A_BLOCK, A_OUT), kd_ref.dtype)
        vd_ref[:, 0] = jnp.zeros((r, A_BLOCK, A_OUT), vd_ref.dtype)

    @pl.when(c > 0)
    def _():
        kd_ref[:, 0] = kd_ref[:, bpc]
        vd_ref[:, 0] = vd_ref[:, bpc]

    _deinterleave(qd_ref, 0, q_ref, stage_ref, mid_ref, r)
    _deinterleave(kd_ref, 1, k_ref, stage_ref, mid_ref, r)
    _deinterleave(vd_ref, 1, v_ref, stage_ref, mid_ref, r)

    part_blocks = nblk // DIL_BATCH_PARTS
    blk = lax.broadcasted_iota(jnp.int32, (part_blocks, A_BLOCK, 2 * A_BLOCK), 0)
    key = lax.broadcasted_iota(jnp.int32, (part_blocks, A_BLOCK, 2 * A_BLOCK), 2)
    no_prev = [(lax.rem(blk + part * part_blocks, bpc) == 0) & (key < A_BLOCK) & (c == 0)
               for part in range(DIL_BATCH_PARTS)]
    lane = lax.broadcasted_iota(jnp.int32, (nblk, A_BLOCK, HEAD_DIM), 2)
    bqk = (((2,), (2,)), ((0,), (0,)))
    bqd = (((2,), (1,)), ((0,), (0,)))
    class_rows = lambda t: [t[rr * bpc:(rr + 1) * bpc].reshape(bpc * A_BLOCK, HEAD_DIM) for rr in range(r)]
    lse = jnp.zeros((nblk, A_BLOCK, HEAD_DIM), jnp.float32)
    ones = jnp.ones((part_blocks, 2 * A_BLOCK, HEAD_DIM), jnp.bfloat16)
    for h in range(A_HEADS_PER_GROUP):
        cols = slice(h * HEAD_DIM, (h + 1) * HEAD_DIM)
        q = qd_ref[:, :, :, cols].reshape(nblk, A_BLOCK, HEAD_DIM)
        k2 = jnp.concatenate([kd_ref[:, 0:bpc, :, cols], kd_ref[:, 1:bpc + 1, :, cols]], axis=2)
        v2 = jnp.concatenate([vd_ref[:, 0:bpc, :, cols], vd_ref[:, 1:bpc + 1, :, cols]], axis=2)
        k2 = k2.reshape(nblk, 2 * A_BLOCK, HEAD_DIM)
        v2 = v2.reshape(nblk, 2 * A_BLOCK, HEAD_DIM)
        o_parts, lse_parts = [], []
        for part in range(DIL_BATCH_PARTS):
            bs = slice(part * nblk // DIL_BATCH_PARTS, (part + 1) * nblk // DIL_BATCH_PARTS)
            s = lax.dot_general(q[bs], k2[bs], bqk, preferred_element_type=jnp.float32) + bias_ref[0, h]
            s = jnp.where(no_prev[part], NEG, s)
            m = jnp.max(s, axis=-1, keepdims=True)
            p = jnp.exp2(s - m).astype(jnp.bfloat16)
            pv = lax.dot_general(p, jnp.concatenate([v2[bs], ones], axis=-1), bqd,
                                 preferred_element_type=jnp.float32)
            den = pv[:, :, HEAD_DIM:]
            o_parts.append(pv[:, :, :HEAD_DIM] / den)
            lse_parts.append(m + jnp.log2(den))
        o = jnp.concatenate(o_parts, axis=0)
        lse = jnp.where(lane == h, jnp.concatenate(lse_parts, axis=0), lse)
        _store_class_rows(stage_ref.at[h], mid_ref, class_rows(o), r)
        o_ref[:, cols] = stage_ref[h].astype(o_ref.dtype)
    _store_class_rows(lse_ref, mid_ref, class_rows(lse), r)


def _dilated_group(proj2, bias, group, *, batch, seq):
    _, r = DIL_PATTERNS[group]
    nc = seq // CHUNK
    bpc = BLOCKS_PER_CHUNK // r
    chunk = lambda sec: pl.BlockSpec((CHUNK, A_OUT), lambda b, c: (b * nc + c, sec * N_GROUPS + group))
    return pl.pallas_call(
        functools.partial(_dilated_kernel, r=r),
        grid=(batch, nc),
        in_specs=[chunk(0), chunk(1), chunk(2),
                  pl.BlockSpec((1, A_HEADS_PER_GROUP, A_BLOCK, 2 * A_BLOCK), lambda b, c: (group, 0, 0, 0))],
        out_specs=[pl.BlockSpec((CHUNK, A_OUT), lambda b, c: (b * nc + c, 0)),
                   pl.BlockSpec((CHUNK, HEAD_DIM), lambda b, c: (b * nc + c, 0))],
        out_shape=[jax.ShapeDtypeStruct((batch * seq, A_OUT), jnp.bfloat16),
                   jax.ShapeDtypeStruct((batch * seq, HEAD_DIM), jnp.float32)],
        scratch_shapes=[pltpu.VMEM((A_HEADS_PER_GROUP, CHUNK, HEAD_DIM), jnp.float32),
                        pltpu.VMEM((CHUNK, HEAD_DIM), jnp.float32),
                        pltpu.VMEM((r, bpc, A_BLOCK, A_OUT), jnp.bfloat16),
                        pltpu.VMEM((r, bpc + 1, A_BLOCK, A_OUT), jnp.bfloat16),
                        pltpu.VMEM((r, bpc + 1, A_BLOCK, A_OUT), jnp.bfloat16)],
        compiler_params=pltpu.CompilerParams(
            dimension_semantics=("arbitrary", "arbitrary"), vmem_limit_bytes=VMEM_LIMIT),
        name=f"dilated_{group}",
    )(proj2, proj2, proj2, bias)


def _split3(x):
    hi = x.astype(jnp.bfloat16)
    r1 = x - hi.astype(jnp.float32)
    mid = r1.astype(jnp.bfloat16)
    lo = (r1 - mid.astype(jnp.float32)).astype(jnp.bfloat16)
    return hi, mid, lo


MOBA_STREAMS = 3


def _moba_items(nb, group):
    streams = [[] for _ in range(MOBA_STREAMS)]
    for i in sorted(range(nb), key=lambda i: -(i // group)):
        min(streams, key=len).extend((i, g) for g in range(i // group + 1))
    assert len({len(s) for s in streams}) == 1 and len(streams[0]) % 2 == 0
    return streams


def _moba_kernel(items_ref, q_ref, k_ref, v_ref, bias_ref, *refs, nb, group, n_items, n_weights):
    w_refs, o_ref, wbf_refs = refs[:n_weights], refs[n_weights], refs[n_weights + 1:2 * n_weights + 1]
    avg_ref, kmean_ref, ka_ref, qa_ref, va_ref, s_ref, st_ref = refs[2 * n_weights + 1:]
    for w_ref, wbf_ref in zip(w_refs, wbf_refs):
        wbf_ref[...] = w_ref[...].astype(wbf_ref.dtype)

    BS = MOBA_BLOCK
    hd = HEAD_DIM
    seq = nb * BS
    rows_per_group = group * BS
    nt = (((1,), (1,)), ((), ()))

    row = lax.broadcasted_iota(jnp.int32, (nb, seq), 0)
    col = lax.broadcasted_iota(jnp.int32, (nb, seq), 1)

    @pl.when((pl.program_id(0) == 0) & (pl.program_id(1) == 0))
    def _():
        lo_edge = row * BS
        avg_ref[...] = jnp.where((col >= lo_edge) & (col < lo_edge + BS), 1.0 / BS, 0.0).astype(avg_ref.dtype)
        qa_ref[hd + nb:, :] = jnp.zeros((hd - nb, seq), qa_ref.dtype)
        key_blk = lax.broadcasted_iota(jnp.int32, (seq, hd), 0) // BS
        key_lane = lax.broadcasted_iota(jnp.int32, (seq, hd), 1)
        ka_ref[:, hd:] = jnp.where(key_lane == key_blk, 1.0, 0.0).astype(ka_ref.dtype)
        aux_row = lax.broadcasted_iota(jnp.int32, (MOBA_AUX, seq), 0)
        va_ref[hd:, :] = jnp.where(aux_row == 0, 1.0, 0.0).astype(va_ref.dtype)

    kmean = jnp.dot(avg_ref[...], k_ref[...], preferred_element_type=jnp.float32)
    for t, part in enumerate(_split3(kmean)):
        kmean_ref[t] = part

    q_all = q_ref[...]
    gate3 = lax.dot_general(kmean_ref[...].reshape(3 * nb, hd), q_all, nt, preferred_element_type=jnp.float32)
    gate = gate3[:nb] + gate3[nb:2 * nb] + gate3[2 * nb:]
    own = col // BS
    past = row < own
    gate = jnp.where(past, gate, -jnp.inf)
    sub = lax.broadcasted_iota(jnp.int32, (8, seq), 0)
    ranks = []
    for r0 in range(0, nb, 8):
        g8 = gate[r0:r0 + 8, :]
        rank8 = jnp.zeros((8, seq), jnp.int32)
        for m in range(nb):
            gm = gate[m:m + 1, :]
            if m >= r0 + 7:
                beats = gm > g8
            elif m < r0:
                beats = gm >= g8
            else:
                beats = (gm > g8) | ((gm == g8) & (sub > m - r0))
            rank8 = rank8 + beats.astype(jnp.int32)
        ranks.append(rank8)
    rank = jnp.concatenate(ranks, axis=0)
    sel = jnp.where((past & (rank < MOBA_TOPK)) | (row == own), 0.0, NEG)

    qa_ref[:hd, :] = q_all.T
    qa_ref[hd:hd + nb, :] = sel.astype(qa_ref.dtype)
    ka_ref[:, :hd] = k_ref[...]
    va_ref[:hd, :] = v_ref[...].T

    def logits(st, w, slot):
        i, g = items_ref[st, w, 0], items_ref[st, w, 1]
        rows = pl.ds(pl.multiple_of(g * rows_per_group, rows_per_group), rows_per_group)
        qcols = pl.ds(pl.multiple_of(i * BS, BS), BS)
        s_t = jnp.dot(ka_ref[rows, :], qa_ref[:, qcols], preferred_element_type=jnp.float32)
        m8 = jnp.full((8, BS), -jnp.inf, jnp.float32)
        for u in range(group):
            tile = jnp.clip(i - (g * group + u), 0, MOBA_BIAS_TILES - 1)
            s_u = s_t[u * BS:(u + 1) * BS] + bias_ref[0, tile]
            s_ref[2 * st + slot, u * BS:(u + 1) * BS, :] = s_u
            m8 = jnp.maximum(m8, jnp.max(s_u.reshape(BS // 8, 8, BS), axis=0))
        return m8

    def accumulate(st, w, slot, m8, state):
        m_run, acc = state
        i, g = items_ref[st, w, 0], items_ref[st, w, 1]
        rows = pl.ds(pl.multiple_of(g * rows_per_group, rows_per_group), rows_per_group)
        m_prev = jnp.where(g == 0, -jnp.inf, m_run)
        m_new = jnp.maximum(m_prev, jnp.max(m8, axis=0, keepdims=True))
        alpha = jnp.exp2(m_prev - m_new)
        p = jnp.exp2(s_ref[2 * st + slot] - m_new).astype(jnp.bfloat16)
        acc = alpha * acc + jnp.dot(va_ref[:, rows], p, preferred_element_type=jnp.float32)
        st_ref[i] = acc
        return m_new, acc

    streams = range(MOBA_STREAMS)

    def pair(t, carry):
        m8a, state = carry
        w = 2 * t
        m8b = [logits(st, w + 1, 1) for st in streams]
        state = [accumulate(st, w, 0, m8a[st], state[st]) for st in streams]
        m8a = [logits(st, jnp.minimum(w + 2, n_items - 1), 0) for st in streams]
        state = [accumulate(st, w + 1, 1, m8b[st], state[st]) for st in streams]
        return m8a, state

    state0 = (jnp.full((1, BS), -jnp.inf, jnp.float32), jnp.zeros((hd + MOBA_AUX, BS), jnp.float32))
    lax.fori_loop(0, n_items // 2, pair, ([logits(st, 0, 0) for st in streams], [state0 for _ in streams]))

    for i in range(nb):
        acc = st_ref[i]
        o_ref[i * BS:(i + 1) * BS, :] = (acc[:hd] / acc[hd:hd + 1]).T.astype(o_ref.dtype)


def _moba(proj2, bias, weights, *, batch, seq):
    BS = MOBA_BLOCK
    nb = seq // BS
    assert nb <= HEAD_DIM
    c0 = A_QKV // HEAD_DIM
    group = math.gcd(nb, MOBA_GROUP)
    items = _moba_items(nb, group)
    head_spec = lambda sec: pl.BlockSpec((seq, HEAD_DIM), lambda b, h, items: (b, c0 + sec * B_HEADS + h))
    n_steps = batch * B_HEADS
    assert all(w.shape[0] % (n_steps * 16) == 0 for w in weights)
    slab = lambda w: pl.BlockSpec((w.shape[0] // n_steps, w.shape[1]), lambda b, h, items: (b * B_HEADS + h, 0))
    outs = pl.pallas_call(
        functools.partial(_moba_kernel, nb=nb, group=group, n_items=len(items[0]), n_weights=len(weights)),
        grid_spec=pltpu.PrefetchScalarGridSpec(
            num_scalar_prefetch=1,
            grid=(batch, B_HEADS),
            in_specs=[head_spec(0), head_spec(1), head_spec(2),
                      pl.BlockSpec((1, MOBA_BIAS_TILES, BS, BS), lambda b, h, items: (h, 0, 0, 0))]
                     + [slab(w) for w in weights],
            out_specs=[pl.BlockSpec((seq, HEAD_DIM), lambda b, h, items: (b, h))] + [slab(w) for w in weights],
            scratch_shapes=[pltpu.VMEM((nb, seq), jnp.bfloat16),
                            pltpu.VMEM((3, nb, HEAD_DIM), jnp.bfloat16),
                            pltpu.VMEM((seq, 2 * HEAD_DIM), jnp.bfloat16),
                            pltpu.VMEM((2 * HEAD_DIM, seq), jnp.bfloat16),
                            pltpu.VMEM((HEAD_DIM + MOBA_AUX, seq), jnp.bfloat16),
                            pltpu.VMEM((2 * MOBA_STREAMS, group * BS, BS), jnp.float32),
                            pltpu.VMEM((nb, HEAD_DIM + MOBA_AUX, BS), jnp.float32)]),
        out_shape=[jax.ShapeDtypeStruct((batch * seq, B_WIDTH), jnp.bfloat16)]
                  + [jax.ShapeDtypeStruct(w.shape, jnp.bfloat16) for w in weights],
        compiler_params=pltpu.CompilerParams(
            dimension_semantics=("arbitrary", "arbitrary"), vmem_limit_bytes=VMEM_LIMIT),
        name="moba",
    )(jnp.asarray(items, jnp.int32), proj2, proj2, proj2, bias, *weights)
    return outs[0], outs[1:]


MERGE_ROW_CHUNKS = 2


def _merge_kernel(o0_ref, o1_ref, o2_ref, l0_ref, l1_ref, l2_ref, yb_ref, x_ref,
                  wa_ref, wb_ref, wo_ref, g_mlp_ref, *refs):
    gate_refs, (out_ref, h_ref) = refs[:-2], refs[-2:]
    ga_refs, gb_refs = gate_refs[:len(gate_refs) // 2], gate_refs[len(gate_refs) // 2:]
    rc = out_ref.shape[0] // MERGE_ROW_CHUNKS
    for c in range(MERGE_ROW_CHUNKS):
        rows = slice(c * rc, (c + 1) * rc)
        l0, l1, l2 = l0_ref[rows, :], l1_ref[rows, :], l2_ref[rows, :]
        mx = jnp.maximum(jnp.maximum(l0, l1), l2)
        e0, e1, e2 = jnp.exp2(l0 - mx), jnp.exp2(l1 - mx), jnp.exp2(l2 - mx)
        inv = 1.0 / (e0 + e1 + e2)
        parts = []
        for h in range(A_HEADS_PER_GROUP):
            cols = slice(h * HEAD_DIM, (h + 1) * HEAD_DIM)
            ya = ((e0 * inv)[:, h:h + 1] * o0_ref[rows, cols].astype(jnp.float32)
                  + (e1 * inv)[:, h:h + 1] * o1_ref[rows, cols].astype(jnp.float32)
                  + (e2 * inv)[:, h:h + 1] * o2_ref[rows, cols].astype(jnp.float32))
            parts.append(ya.astype(jnp.bfloat16))
        ya = jnp.concatenate(parts, axis=-1)
        pa = jnp.dot(ya, wa_ref[...], preferred_element_type=jnp.float32)
        pb = jnp.dot(yb_ref[rows, :], wb_ref[...], preferred_element_type=jnp.float32)
        ga = jnp.concatenate([g[rows, :] for g in ga_refs], axis=-1).astype(jnp.float32)
        gb = jnp.concatenate([g[rows, :] for g in gb_refs], axis=-1).astype(jnp.float32)
        mixed = ga * pa + gb * pb
        out = x_ref[rows, :] + jnp.dot(mixed.astype(jnp.bfloat16), wo_ref[...], preferred_element_type=jnp.float32)
        out_ref[rows, :] = out
        h_ref[rows, :] = _rms_rows(out, g_mlp_ref[...]).astype(h_ref.dtype)


def _merge(outs, lses, yb, proj2, x2, wa, wb, wo, g_mlp, *, tm):
    T, D = x2.shape
    gw = math.gcd(QKV_WIDTH, D)
    n_gate_blocks = 2 * D // gw
    row = lambda w: pl.BlockSpec((tm, w), lambda i: (i, 0))
    full = lambda a: pl.BlockSpec(a.shape, lambda i: (0, 0))
    gate = lambda c: pl.BlockSpec((tm, gw), lambda i: (i, QKV_WIDTH // gw + c))
    return pl.pallas_call(
        _merge_kernel,
        grid=(T // tm,),
        in_specs=[row(A_OUT)] * 3 + [row(HEAD_DIM)] * 3 + [row(B_WIDTH), row(D),
                  full(wa), full(wb), full(wo), full(g_mlp)] + [gate(c) for c in range(n_gate_blocks)],
        out_specs=[row(D), row(D)],
        out_shape=[jax.ShapeDtypeStruct((T, D), jnp.float32), jax.ShapeDtypeStruct((T, D), jnp.bfloat16)],
        compiler_params=pltpu.CompilerParams(dimension_semantics=("arbitrary",), vmem_limit_bytes=VMEM_LIMIT),
        name="merge",
    )(*outs, *lses, yb, x2, wa, wb, wo, g_mlp, *([proj2] * n_gate_blocks))


def _mlp_kernel(h_ref, x_ref, wu_ref, wd_ref, o_ref):
    f = pl.program_id(1)

    @pl.when(f == 0)
    def _():
        o_ref[...] = jnp.zeros(o_ref.shape, o_ref.dtype)

    slab = x_ref.shape[1]
    cols = pl.ds(pl.multiple_of(f * slab, slab), slab)
    o_ref[:, cols] += x_ref[...]

    u = jnp.dot(h_ref[...], wu_ref[...], preferred_element_type=jnp.float32)
    u = jnp.square(jnp.maximum(u, 0.0)).astype(jnp.bfloat16)
    o_ref[...] += jnp.dot(u, wd_ref[...], preferred_element_type=jnp.float32)


def _mlp(h2, x2, w_up, w_down, *, tm, tf):
    T, D = x2.shape
    F = w_up.shape[1]
    n_f = F // tf
    slab = D // n_f
    assert slab % HEAD_DIM == 0
    return pl.pallas_call(
        _mlp_kernel,
        grid=(T // tm, n_f),
        in_specs=[pl.BlockSpec((tm, D), lambda i, f: (i, 0)),
                  pl.BlockSpec((tm, slab), lambda i, f: (i, f)),
                  pl.BlockSpec((D, tf), lambda i, f: (0, f)),
                  pl.BlockSpec((tf, D), lambda i, f: (f, 0))],
        out_specs=pl.BlockSpec((tm, D), lambda i, f: (i, 0)),
        out_shape=jax.ShapeDtypeStruct((T, D), jnp.float32),
        compiler_params=pltpu.CompilerParams(
            dimension_semantics=("arbitrary", "arbitrary"), vmem_limit_bytes=VMEM_LIMIT),
        name="mlp",
    )(h2, x2, w_up, w_down)


def _column_gains(q_norm_a, k_norm_a, q_norm_b, k_norm_b, width):
    qs = SCALE * LOG2E
    gains = jnp.concatenate([
        jnp.tile(q_norm_a * qs, A_HEADS), jnp.tile(k_norm_a, A_HEADS), jnp.ones((A_HEADS * HEAD_DIM,), jnp.float32),
        jnp.tile(q_norm_b * qs, B_HEADS), jnp.tile(k_norm_b, B_HEADS), jnp.ones((B_WIDTH,), jnp.float32),
        jnp.ones((width - QKV_WIDTH,), jnp.float32)])
    return gains.reshape(1, width)


def kernel(x, g_mix, w_in, q_norm_a, k_norm_a, q_norm_b, k_norm_b, rel_bias,
           w_branch_a, w_branch_b, w_out, g_mlp, w_up, w_down):
    batch, seq, d_model = x.shape
    T = batch * seq
    assert w_in.shape == (d_model, QKV_WIDTH + 2 * d_model)
    assert seq % (DIL_PATTERNS[-1][1] * A_BLOCK) == 0
    x2 = x.reshape(T, d_model)
    tm = min(ROW_TILE, T)

    dil_bias, moba_bias = _bias_tables(rel_bias)
    gains = _column_gains(q_norm_a, k_norm_a, q_norm_b, k_norm_b, w_in.shape[1])
    proj = _in_proj(x2, g_mix.reshape(1, -1), w_in, gains, tm=tm, tn=min(COL_TILE, d_model * 2))

    outs, lses = zip(*[_dilated_group(proj, dil_bias, g, batch=batch, seq=seq) for g in range(N_GROUPS)])
    yb, (wa, wb, wo, wu, wd) = _moba(proj, moba_bias, (w_branch_a, w_branch_b, w_out, w_up, w_down),
                                     batch=batch, seq=seq)

    x_mid, h_mid = _merge(outs, lses, yb, proj, x2, wa, wb, wo, g_mlp.reshape(1, -1), tm=min(MERGE_ROW_TILE, T))
    y = _mlp(h_mid, x_mid, wu, wd, tm=tm, tf=min(MLP_FF_TILE, w_up.shape[1]))
    return y.reshape(batch, seq, d_model)
```

```python
import functools
import math

import jax
import jax.numpy as jnp
from jax import lax
from jax.experimental import pallas as pl
from jax.experimental.pallas import tpu as pltpu

HEAD_DIM = 128
DIL_PATTERNS = ((128, 1), (512, 4), (2048, 16))
N_GROUPS = len(DIL_PATTERNS)
A_HEADS_PER_GROUP = 4
A_HEADS = A_HEADS_PER_GROUP * N_GROUPS
A_OUT = A_HEADS_PER_GROUP * HEAD_DIM
A_BLOCK = 128
B_HEADS = 8
B_WIDTH = B_HEADS * HEAD_DIM
MOBA_BLOCK = 256
MOBA_TOPK = 3
MOBA_GROUP = 2
MOBA_AUX = 16
N_BUCKETS = 32
MAX_DISTANCE = 2048
EPS = 1e-6
SCALE = HEAD_DIM ** -0.5
A_QKV = 3 * A_HEADS * HEAD_DIM
B_QKV = 3 * B_WIDTH
QKV_WIDTH = A_QKV + B_QKV
LOG2E = math.log2(math.e)
NEG = -1e30
VMEM_LIMIT = 56 * 1024 * 1024
ROW_TILE = 1024
COL_TILE = 512
MERGE_ROW_TILE = 512
MLP_FF_TILE = 1024


def _bucket_thresholds():
    max_exact = N_BUCKETS // 2

    def bucket(d):
        if d < max_exact:
            return d
        v = int(math.log(d / max_exact) / math.log(MAX_DISTANCE / max_exact) * (N_BUCKETS - max_exact))
        return min(max_exact + v, N_BUCKETS - 1)

    thr, d = [0], 0
    for b in range(1, N_BUCKETS):
        while bucket(d) < b:
            d += 1
        thr.append(d)
    return tuple(thr)


BUCKET_THRESHOLDS = _bucket_thresholds()
MOBA_BIAS_TILES = -(-(BUCKET_THRESHOLDS[-1] - 1) // MOBA_BLOCK) + 2


def _bias_lookup(dist, tab_ref, col, dmin=0, dmax=None):
    first = sum(dmin >= t for t in BUCKET_THRESHOLDS[1:])
    val = jnp.full(dist.shape, tab_ref[first, col], jnp.float32)
    for b in range(first + 1, N_BUCKETS):
        if dmax is None or BUCKET_THRESHOLDS[b] <= dmax:
            val = jnp.where(dist >= BUCKET_THRESHOLDS[b], tab_ref[b, col], val)
    return val * LOG2E


def _dil_bias_kernel(tab_ref, o_ref):
    g, h = pl.program_id(0), pl.program_id(1)
    a = lax.broadcasted_iota(jnp.int32, (A_BLOCK, 2 * A_BLOCK), 0)
    j = lax.broadcasted_iota(jnp.int32, (A_BLOCK, 2 * A_BLOCK), 1)
    delta = a + A_BLOCK - j
    dilation = lax.shift_left(jnp.int32(1), 2 * g)
    val = _bias_lookup(delta * dilation, tab_ref, g * A_HEADS_PER_GROUP + h)
    o_ref[0, 0] = jnp.where((delta >= 0) & (delta <= A_BLOCK), val, NEG)


def _moba_bias_kernel(tab_ref, o_ref):
    h = pl.program_id(0)
    j = lax.broadcasted_iota(jnp.int32, (MOBA_BLOCK, MOBA_BLOCK), 0)
    s = lax.broadcasted_iota(jnp.int32, (MOBA_BLOCK, MOBA_BLOCK), 1)
    for c in range(MOBA_BIAS_TILES):
        dist = c * MOBA_BLOCK + s - j
        lo, hi = (c - 1) * MOBA_BLOCK + 1, (c + 1) * MOBA_BLOCK - 1
        val = _bias_lookup(dist, tab_ref, A_HEADS + h, max(lo, 0), hi)
        o_ref[0, c] = jnp.where(dist >= 0, val, NEG)


def _bias_tables(rel_bias):
    smem = pl.BlockSpec(memory_space=pltpu.SMEM)
    dil = pl.pallas_call(
        _dil_bias_kernel,
        grid=(N_GROUPS, A_HEADS_PER_GROUP),
        in_specs=[smem],
        out_specs=pl.BlockSpec((1, 1, A_BLOCK, 2 * A_BLOCK), lambda g, h: (g, h, 0, 0)),
        out_shape=jax.ShapeDtypeStruct((N_GROUPS, A_HEADS_PER_GROUP, A_BLOCK, 2 * A_BLOCK), jnp.float32),
        name="dil_bias",
    )(rel_bias)
    moba = pl.pallas_call(
        _moba_bias_kernel,
        grid=(B_HEADS,),
        in_specs=[smem],
        out_specs=pl.BlockSpec((1, MOBA_BIAS_TILES, MOBA_BLOCK, MOBA_BLOCK), lambda h: (h, 0, 0, 0)),
        out_shape=jax.ShapeDtypeStruct((B_HEADS, MOBA_BIAS_TILES, MOBA_BLOCK, MOBA_BLOCK), jnp.float32),
        name="moba_bias",
    )(rel_bias)
    return dil, moba


IN_PROJ_FIRST_STEP_ROW_CHUNKS = 2
IN_PROJ_K_CHUNKS = 4
IN_PROJ_TILES_PER_STEP = 3


def _rms_rows(x, g):
    return (x * lax.rsqrt(jnp.mean(x * x, axis=-1, keepdims=True) + EPS)) * g


def _in_proj_tile_kind(col0):
    if col0 < 2 * A_HEADS * HEAD_DIM or A_QKV <= col0 < A_QKV + 2 * B_WIDTH:
        return "norm"
    return "plain" if col0 < QKV_WIDTH else "gate"


def _in_proj_kernel(x_ref, g_ref, *refs, tn, n_tiles):
    tps = IN_PROJ_TILES_PER_STEP
    w_refs, gain_refs, (o_ref, h_ref) = refs[:tps], refs[tps:2 * tps], refs[2 * tps:]
    s = pl.program_id(1)

    def head_norm(acc, gain_ref):
        return jnp.concatenate(
            [_rms_rows(acc[:, c * HEAD_DIM:(c + 1) * HEAD_DIM], gain_ref[:, c * HEAD_DIM:(c + 1) * HEAD_DIM])
             for c in range(tn // HEAD_DIM)], axis=-1)

    epilogues = {
        "norm": head_norm,
        "plain": lambda acc, gain_ref: acc,
        "gate": lambda acc, gain_ref: 0.5 * jnp.tanh(0.5 * acc) + 0.5,
    }

    def project(kinds, normalise_input):
        tm, d = h_ref.shape
        n_row_chunks = IN_PROJ_FIRST_STEP_ROW_CHUNKS if normalise_input else 1
        rc = tm // n_row_chunks
        kc = d // IN_PROJ_K_CHUNKS
        for c in range(n_row_chunks):
            rows = slice(c * rc, (c + 1) * rc)
            if normalise_input:
                h_ref[rows, :] = _rms_rows(x_ref[rows, :], g_ref[...]).astype(h_ref.dtype)
            for t, kind in enumerate(kinds):
                if kind is None:
                    o_ref[rows, t * tn:(t + 1) * tn] = jnp.zeros((rc, tn), o_ref.dtype)
                    continue
                acc = None
                for k in range(IN_PROJ_K_CHUNKS):
                    ks = slice(k * kc, (k + 1) * kc)
                    part = jnp.dot(h_ref[rows, ks], w_refs[t][ks, :].astype(h_ref.dtype),
                                   preferred_element_type=jnp.float32)
                    acc = part if acc is None else acc + part
                o_ref[rows, t * tn:(t + 1) * tn] = epilogues[kind](acc, gain_refs[t]).astype(o_ref.dtype)

    steps_of = {}
    for step in range(-(-n_tiles // tps)):
        tiles = [step * tps + t for t in range(tps)]
        kinds = tuple(_in_proj_tile_kind(tile * tn) if tile < n_tiles else None for tile in tiles)
        steps_of.setdefault((kinds, step == 0), []).append(step)
    for (kinds, first), steps in steps_of.items():
        @pl.when(functools.reduce(jnp.logical_or, [s == step for step in steps]))
        def _(kinds=kinds, first=first):
            project(kinds, first)


def _in_proj(x2, g_mix, w_in, gains, *, tm, tn):
    T, D = x2.shape
    tps = IN_PROJ_TILES_PER_STEP
    n_tiles = w_in.shape[1] // tn
    n_steps = -(-n_tiles // tps)
    tile = lambda t: (lambda i, s: (0, jnp.minimum(s * tps + t, n_tiles - 1)))
    return pl.pallas_call(
        functools.partial(_in_proj_kernel, tn=tn, n_tiles=n_tiles),
        grid=(T // tm, n_steps),
        in_specs=[pl.BlockSpec((tm, D), lambda i, s: (i, 0)), pl.BlockSpec((1, D), lambda i, s: (0, 0))]
                 + [pl.BlockSpec((D, tn), tile(t)) for t in range(tps)]
                 + [pl.BlockSpec((1, tn), tile(t)) for t in range(tps)],
        out_specs=pl.BlockSpec((tm, tps * tn), lambda i, s: (i, s)),
        out_shape=jax.ShapeDtypeStruct((T, n_steps * tps * tn), jnp.bfloat16),
        scratch_shapes=[pltpu.VMEM((tm, D), jnp.bfloat16)],
        compiler_params=pltpu.CompilerParams(
            dimension_semantics=("arbitrary", "arbitrary"), vmem_limit_bytes=VMEM_LIMIT),
        name="in_proj",
    )(x2, g_mix, *([w_in] * tps), *([gains] * tps))


CHUNK = DIL_PATTERNS[-1][1] * A_BLOCK
BLOCKS_PER_CHUNK = CHUNK // A_BLOCK
DIL_BATCH_PARTS = 2
DIL_STRIDE = 4


def _class_rows(src_ref, mid_ref, r):
    n = CHUNK // r
    if r <= DIL_STRIDE:
        return [src_ref[pl.ds(rr, n, stride=r), :] for rr in range(r)]
    r1, r2, n1 = DIL_STRIDE, r // DIL_STRIDE, CHUNK // DIL_STRIDE
    assert r2 <= DIL_STRIDE
    for a in range(r1):
        mid_ref[a * n1:(a + 1) * n1, :] = src_ref[pl.ds(a, n1, stride=r1), :]
    return [mid_ref[pl.ds((rr % r1) * n1 + rr // r1, n, stride=r2), :] for rr in range(r)]


def _store_class_rows(dst_ref, mid_ref, rows_of_class, r):
    n = CHUNK // r
    if r == 1:
        dst_ref[...] = rows_of_class[0]
    elif r <= DIL_STRIDE:
        for rr, rows in enumerate(rows_of_class):
            dst_ref[pl.ds(rr, n, stride=r), :] = rows
    else:
        r1, r2, n1 = DIL_STRIDE, r // DIL_STRIDE, CHUNK // DIL_STRIDE
        for rr, rows in enumerate(rows_of_class):
            mid_ref[pl.ds((rr % r1) * n1 + rr // r1, n, stride=r2), :] = rows
        for a in range(r1):
            dst_ref[pl.ds(a, n1, stride=r1), :] = mid_ref[a * n1:(a + 1) * n1, :]


def _deinterleave(dst_ref, first_block, src_ref, stage_ref, mid_ref, r):
    bpc = BLOCKS_PER_CHUNK // r
    if r == 1:
        dst_ref[0, first_block:first_block + bpc] = src_ref[...].reshape(bpc, A_BLOCK, A_OUT)
        return
    for h in range(A_HEADS_PER_GROUP):
        cols = slice(h * HEAD_DIM, (h + 1) * HEAD_DIM)
        stage_ref[h] = src_ref[:, cols].astype(jnp.float32)
        for rr, rows in enumerate(_class_rows(stage_ref.at[h], mid_ref, r)):
            dst_ref[rr, first_block:first_block + bpc, :, cols] = (
                rows.astype(dst_ref.dtype).reshape(bpc, A_BLOCK, HEAD_DIM))


def _dilated_kernel(q_ref, k_ref, v_ref, bias_ref, o_ref, lse_ref, stage_ref, mid_ref, qd_ref, kd_ref, vd_ref, *, r):
    c = pl.program_id(1)
    bpc = BLOCKS_PER_CHUNK // r
    nblk = BLOCKS_PER_CHUNK

    @pl.when(c == 0)
    def _():
        kd_ref[:, 0] = jnp.zeros((r, A_BLOCK, A_OUT), kd_ref.dtype)
        vd_ref[:, 0] = jnp.zeros((r, A_BLOCK, A_OUT), vd_ref.dtype)

    @pl.when(c > 0)
    def _():
        kd_ref[:, 0] = kd_ref[:, bpc]
        vd_ref[:, 0] = vd_ref[:, bpc]

    _deinterleave(qd_ref, 0, q_ref, stage_ref, mid_ref, r)
    _deinterleave(kd_ref, 1, k_ref, stage_ref, mid_ref, r)
    _deinterleave(vd_ref, 1, v_ref, stage_ref, mid_ref, r)

    part_blocks = nblk // DIL_BATCH_PARTS
    blk = lax.broadcasted_iota(jnp.int32, (part_blocks, A_BLOCK, 2 * A_BLOCK), 0)
    key = lax.broadcasted_iota(jnp.int32, (part_blocks, A_BLOCK, 2 * A_BLOCK), 2)
    no_prev = [(lax.rem(blk + part * part_blocks, bpc) == 0) & (key < A_BLOCK) & (c == 0)
               for part in range(DIL_BATCH_PARTS)]
    lane = lax.broadcasted_iota(jnp.int32, (nblk, A_BLOCK, HEAD_DIM), 2)
    bqk = (((2,), (2,)), ((0,), (0,)))
    bqd = (((2,), (1,)), ((0,), (0,)))
    class_rows = lambda t: [t[rr * bpc:(rr + 1) * bpc].reshape(bpc * A_BLOCK, HEAD_DIM) for rr in range(r)]
    lse = jnp.zeros((nblk, A_BLOCK, HEAD_DIM), jnp.float32)
    ones = jnp.ones((part_blocks, 2 * A_BLOCK, HEAD_DIM), jnp.bfloat16)
    for h in range(A_HEADS_PER_GROUP):
        cols = slice(h * HEAD_DIM, (h + 1) * HEAD_DIM)
        q = qd_ref[:, :, :, cols].reshape(nblk, A_BLOCK, HEAD_DIM)
        k2 = jnp.concatenate([kd_ref[:, 0:bpc, :, cols], kd_ref[:, 1:bpc + 1, :, cols]], axis=2)
        v2 = jnp.concatenate([vd_ref[:, 0:bpc, :, cols], vd_ref[:, 1:bpc + 1, :, cols]], axis=2)
        k2 = k2.reshape(nblk, 2 * A_BLOCK, HEAD_DIM)
        v2 = v2.reshape(nblk, 2 * A_BLOCK, HEAD_DIM)
        o_parts, lse_parts = [], []
        for part in range(DIL_BATCH_PARTS):
            bs = slice(part * nblk // DIL_BATCH_PARTS, (part + 1) * nblk // DIL_BATCH_PARTS)
            s = lax.dot_general(q[bs], k2[bs], bqk, preferred_element_type=jnp.float32) + bias_ref[0, h]
            s = jnp.where(no_prev[part], NEG, s)
            m = jnp.max(s, axis=-1, keepdims=True)
            p = jnp.exp2(s - m).astype(jnp.bfloat16)
            pv = lax.dot_general(p, jnp.concatenate([v2[bs], ones], axis=-1), bqd,
                                 preferred_element_type=jnp.float32)
            den = pv[:, :, HEAD_DIM:]
            o_parts.append(pv[:, :, :HEAD_DIM] / den)
            lse_parts.append(m + jnp.log2(den))
        o = jnp.concatenate(o_parts, axis=0)
        lse = jnp.where(lane == h, jnp.concatenate(lse_parts, axis=0), lse)
        _store_class_rows(stage_ref.at[h], mid_ref, class_rows(o), r)
        o_ref[:, cols] = stage_ref[h].astype(o_ref.dtype)
    _store_class_rows(lse_ref, mid_ref, class_rows(lse), r)


def _dilated_group(proj2, bias, group, *, batch, seq):
    _, r = DIL_PATTERNS[group]
    nc = seq // CHUNK
    bpc = BLOCKS_PER_CHUNK // r
    chunk = lambda sec: pl.BlockSpec((CHUNK, A_OUT), lambda b, c: (b * nc + c, sec * N_GROUPS + group))
    return pl.pallas_call(
        functools.partial(_dilated_kernel, r=r),
        grid=(batch, nc),
        in_specs=[chunk(0), chunk(1), chunk(2),
                  pl.BlockSpec((1, A_HEADS_PER_GROUP, A_BLOCK, 2 * A_BLOCK), lambda b, c: (group, 0, 0, 0))],
        out_specs=[pl.BlockSpec((CHUNK, A_OUT), lambda b, c: (b * nc + c, 0)),
                   pl.BlockSpec((CHUNK, HEAD_DIM), lambda b, c: (b * nc + c, 0))],
        out_shape=[jax.ShapeDtypeStruct((batch * seq, A_OUT), jnp.bfloat16),
                   jax.ShapeDtypeStruct((batch * seq, HEAD_DIM), jnp.float32)],
        scratch_shapes=[pltpu.VMEM((A_HEADS_PER_GROUP, CHUNK, HEAD_DIM), jnp.float32),
                        pltpu.VMEM((CHUNK, HEAD_DIM), jnp.float32),
                        pltpu.VMEM((r, bpc, A_BLOCK, A_OUT), jnp.bfloat16),
                        pltpu.VMEM((r, bpc + 1, A_BLOCK, A_OUT), jnp.bfloat16),
                        pltpu.VMEM((r, bpc + 1, A_BLOCK, A_OUT), jnp.bfloat16)],
        compiler_params=pltpu.CompilerParams(
            dimension_semantics=("arbitrary", "arbitrary"), vmem_limit_bytes=VMEM_LIMIT),
        name=f"dilated_{group}",
    )(proj2, proj2, proj2, bias)


def _split3(x):
    hi = x.astype(jnp.bfloat16)
    r1 = x - hi.astype(jnp.float32)
    mid = r1.astype(jnp.bfloat16)
    lo = (r1 - mid.astype(jnp.float32)).astype(jnp.bfloat16)
    return hi, mid, lo


MOBA_ROUTE_CHUNKS = 4
MOBA_STREAMS = 3


def _moba_items(nb, group):
    streams = [[] for _ in range(MOBA_STREAMS)]
    for i in sorted(range(nb), key=lambda i: -(i // group)):
        min(streams, key=len).extend((i, g) for g in range(i // group + 1))
    assert len({len(s) for s in streams}) == 1 and len(streams[0]) % 2 == 0
    return streams


def _moba_kernel(items_ref, q_ref, k_ref, v_ref, bias_ref, *refs, nb, group, n_items, n_weights):
    w_refs, o_ref, wbf_refs = refs[:n_weights], refs[n_weights], refs[n_weights + 1:2 * n_weights + 1]
    avg_ref, kmean_ref, ka_ref, qa_ref, va_ref, s_ref, st_ref = refs[2 * n_weights + 1:]
    for w_ref, wbf_ref in zip(w_refs, wbf_refs):
        wbf_ref[...] = w_ref[...].astype(wbf_ref.dtype)

    BS = MOBA_BLOCK
    hd = HEAD_DIM
    seq = nb * BS
    rows_per_group = group * BS
    nt = (((1,), (1,)), ((), ()))

    row = lax.broadcasted_iota(jnp.int32, (nb, seq), 0)
    col = lax.broadcasted_iota(jnp.int32, (nb, seq), 1)

    @pl.when((pl.program_id(0) == 0) & (pl.program_id(1) == 0))
    def _():
        lo_edge = row * BS
        avg_ref[...] = jnp.where((col >= lo_edge) & (col < lo_edge + BS), 1.0 / BS, 0.0).astype(avg_ref.dtype)
        qa_ref[hd + nb:, :] = jnp.zeros((hd - nb, seq), qa_ref.dtype)
        key_blk = lax.broadcasted_iota(jnp.int32, (seq, hd), 0) // BS
        key_lane = lax.broadcasted_iota(jnp.int32, (seq, hd), 1)
        ka_ref[:, hd:] = jnp.where(key_lane == key_blk, 1.0, 0.0).astype(ka_ref.dtype)
        aux_row = lax.broadcasted_iota(jnp.int32, (MOBA_AUX, seq), 0)
        va_ref[hd:, :] = jnp.where(aux_row == 0, 1.0, 0.0).astype(va_ref.dtype)

    kmean = jnp.dot(avg_ref[...], k_ref[...], preferred_element_type=jnp.float32)
    for t, part in enumerate(_split3(kmean)):
        kmean_ref[t] = part

    kmean3 = kmean_ref[...].reshape(3 * nb, hd)
    qc = seq // MOBA_ROUTE_CHUNKS
    row_c = lax.broadcasted_iota(jnp.int32, (nb, qc), 0)
    col_c = lax.broadcasted_iota(jnp.int32, (nb, qc), 1)
    sub = lax.broadcasted_iota(jnp.int32, (8, qc), 0)
    for ch in range(MOBA_ROUTE_CHUNKS):
        qs = slice(ch * qc, (ch + 1) * qc)
        q_c = q_ref[qs, :]
        gate3 = lax.dot_general(kmean3, q_c, nt, preferred_element_type=jnp.float32)
        gate = gate3[:nb] + gate3[nb:2 * nb] + gate3[2 * nb:]
        own = (col_c + ch * qc) // BS
        past = row_c < own
        gate = jnp.where(past, gate, -jnp.inf)
        ranks = []
        for r0 in range(0, nb, 8):
            g8 = gate[r0:r0 + 8, :]
            rank8 = jnp.zeros((8, qc), jnp.int32)
            for m in range(nb):
                gm = gate[m:m + 1, :]
                if m >= r0 + 7:
                    beats = gm > g8
                elif m < r0:
                    beats = gm >= g8
                else:
                    beats = (gm > g8) | ((gm == g8) & (sub > m - r0))
                rank8 = rank8 + beats.astype(jnp.int32)
            ranks.append(rank8)
        rank = jnp.concatenate(ranks, axis=0)
        sel = jnp.where((past & (rank < MOBA_TOPK)) | (row_c == own), 0.0, NEG)
        qa_ref[:hd, qs] = q_c.T
        qa_ref[hd:hd + nb, qs] = sel.astype(qa_ref.dtype)
    ka_ref[:, :hd] = k_ref[...]
    va_ref[:hd, :] = v_ref[...].T

    def logits(st, w, slot):
        i, g = items_ref[st, w, 0], items_ref[st, w, 1]
        rows = pl.ds(pl.multiple_of(g * rows_per_group, rows_per_group), rows_per_group)
        qcols = pl.ds(pl.multiple_of(i * BS, BS), BS)
        s_t = jnp.dot(ka_ref[rows, :], qa_ref[:, qcols], preferred_element_type=jnp.float32)
        m8 = jnp.full((8, BS), -jnp.inf, jnp.float32)
        for u in range(group):
            tile = jnp.clip(i - (g * group + u), 0, MOBA_BIAS_TILES - 1)
            s_u = s_t[u * BS:(u + 1) * BS] + bias_ref[0, tile]
            s_ref[2 * st + slot, u * BS:(u + 1) * BS, :] = s_u
            m8 = jnp.maximum(m8, jnp.max(s_u.reshape(BS // 8, 8, BS), axis=0))
        return m8

    def accumulate(st, w, slot, m8, state):
        m_run, acc = state
        i, g = items_ref[st, w, 0], items_ref[st, w, 1]
        rows = pl.ds(pl.multiple_of(g * rows_per_group, rows_per_group), rows_per_group)
        m_prev = jnp.where(g == 0, -jnp.inf, m_run)
        m_new = jnp.maximum(m_prev, jnp.max(m8, axis=0, keepdims=True))
        alpha = jnp.exp2(m_prev - m_new)
        p = jnp.exp2(s_ref[2 * st + slot] - m_new).astype(jnp.bfloat16)
        acc = alpha * acc + jnp.dot(va_ref[:, rows], p, preferred_element_type=jnp.float32)
        st_ref[i] = acc
        return m_new, acc

    streams = range(MOBA_STREAMS)

    def pair(t, carry):
        m8a, state = carry
        w = 2 * t
        m8b = [logits(st, w + 1, 1) for st in streams]
        state = [accumulate(st, w, 0, m8a[st], state[st]) for st in streams]
        m8a = [logits(st, jnp.minimum(w + 2, n_items - 1), 0) for st in streams]
        state = [accumulate(st, w + 1, 1, m8b[st], state[st]) for st in streams]
        return m8a, state

    state0 = (jnp.full((1, BS), -jnp.inf, jnp.float32), jnp.zeros((hd + MOBA_AUX, BS), jnp.float32))
    lax.fori_loop(0, n_items // 2, pair, ([logits(st, 0, 0) for st in streams], [state0 for _ in streams]))

    for i in range(nb):
        acc = st_ref[i]
        o_ref[i * BS:(i + 1) * BS, :] = (acc[:hd] / acc[hd:hd + 1]).T.astype(o_ref.dtype)


def _moba(proj2, bias, weights, *, batch, seq):
    BS = MOBA_BLOCK
    nb = seq // BS
    assert nb <= HEAD_DIM
    c0 = A_QKV // HEAD_DIM
    group = math.gcd(nb, MOBA_GROUP)
    items = _moba_items(nb, group)
    head_spec = lambda sec: pl.BlockSpec((seq, HEAD_DIM), lambda b, h, items: (b, c0 + sec * B_HEADS + h))
    n_steps = batch * B_HEADS
    assert all(w.shape[0] % (n_steps * 16) == 0 for w in weights)
    slab = lambda w: pl.BlockSpec((w.shape[0] // n_steps, w.shape[1]), lambda b, h, items: (b * B_HEADS + h, 0))
    outs = pl.pallas_call(
        functools.partial(_moba_kernel, nb=nb, group=group, n_items=len(items[0]), n_weights=len(weights)),
        grid_spec=pltpu.PrefetchScalarGridSpec(
            num_scalar_prefetch=1,
            grid=(batch, B_HEADS),
            in_specs=[head_spec(0), head_spec(1), head_spec(2),
                      pl.BlockSpec((1, MOBA_BIAS_TILES, BS, BS), lambda b, h, items: (h, 0, 0, 0))]
                     + [slab(w) for w in weights],
            out_specs=[pl.BlockSpec((seq, HEAD_DIM), lambda b, h, items: (b, h))] + [slab(w) for w in weights],
            scratch_shapes=[pltpu.VMEM((nb, seq), jnp.bfloat16),
                            pltpu.VMEM((3, nb, HEAD_DIM), jnp.bfloat16),
                            pltpu.VMEM((seq, 2 * HEAD_DIM), jnp.bfloat16),
                            pltpu.VMEM((2 * HEAD_DIM, seq), jnp.bfloat16),
                            pltpu.VMEM((HEAD_DIM + MOBA_AUX, seq), jnp.bfloat16),
                            pltpu.VMEM((2 * MOBA_STREAMS, group * BS, BS), jnp.float32),
                            pltpu.VMEM((nb, HEAD_DIM + MOBA_AUX, BS), jnp.float32)]),
        out_shape=[jax.ShapeDtypeStruct((batch * seq, B_WIDTH), jnp.bfloat16)]
                  + [jax.ShapeDtypeStruct(w.shape, jnp.bfloat16) for w in weights],
        compiler_params=pltpu.CompilerParams(
            dimension_semantics=("arbitrary", "arbitrary"), vmem_limit_bytes=VMEM_LIMIT),
        name="moba",
    )(jnp.asarray(items, jnp.int32), proj2, proj2, proj2, bias, *weights)
    return outs[0], outs[1:]


MERGE_ROW_CHUNKS = 2


def _merge_kernel(o0_ref, o1_ref, o2_ref, l0_ref, l1_ref, l2_ref, yb_ref, x_ref,
                  wa_ref, wb_ref, wo_ref, g_mlp_ref, *refs):
    gate_refs, (out_ref, h_ref) = refs[:-2], refs[-2:]
    ga_refs, gb_refs = gate_refs[:len(gate_refs) // 2], gate_refs[len(gate_refs) // 2:]
    rc = out_ref.shape[0] // MERGE_ROW_CHUNKS
    for c in range(MERGE_ROW_CHUNKS):
        rows = slice(c * rc, (c + 1) * rc)
        l0, l1, l2 = l0_ref[rows, :], l1_ref[rows, :], l2_ref[rows, :]
        mx = jnp.maximum(jnp.maximum(l0, l1), l2)
        e0, e1, e2 = jnp.exp2(l0 - mx), jnp.exp2(l1 - mx), jnp.exp2(l2 - mx)
        inv = 1.0 / (e0 + e1 + e2)
        parts = []
        for h in range(A_HEADS_PER_GROUP):
            cols = slice(h * HEAD_DIM, (h + 1) * HEAD_DIM)
            ya = ((e0 * inv)[:, h:h + 1] * o0_ref[rows, cols].astype(jnp.float32)
                  + (e1 * inv)[:, h:h + 1] * o1_ref[rows, cols].astype(jnp.float32)
                  + (e2 * inv)[:, h:h + 1] * o2_ref[rows, cols].astype(jnp.float32))
            parts.append(ya.astype(jnp.bfloat16))
        ya = jnp.concatenate(parts, axis=-1)
        pa = jnp.dot(ya, wa_ref[...], preferred_element_type=jnp.float32)
        pb = jnp.dot(yb_ref[rows, :], wb_ref[...], preferred_element_type=jnp.float32)
        ga = jnp.concatenate([g[rows, :] for g in ga_refs], axis=-1).astype(jnp.float32)
        gb = jnp.concatenate([g[rows, :] for g in gb_refs], axis=-1).astype(jnp.float32)
        mixed = ga * pa + gb * pb
        out = x_ref[rows, :] + jnp.dot(mixed.astype(jnp.bfloat16), wo_ref[...], preferred_element_type=jnp.float32)
        out_ref[rows, :] = out
        h_ref[rows, :] = _rms_rows(out, g_mlp_ref[...]).astype(h_ref.dtype)


def _merge(outs, lses, yb, proj2, x2, wa, wb, wo, g_mlp, *, tm):
    T, D = x2.shape
    gw = math.gcd(QKV_WIDTH, D)
    n_gate_blocks = 2 * D // gw
    row = lambda w: pl.BlockSpec((tm, w), lambda i: (i, 0))
    full = lambda a: pl.BlockSpec(a.shape, lambda i: (0, 0))
    gate = lambda c: pl.BlockSpec((tm, gw), lambda i: (i, QKV_WIDTH // gw + c))
    return pl.pallas_call(
        _merge_kernel,
        grid=(T // tm,),
        in_specs=[row(A_OUT)] * 3 + [row(HEAD_DIM)] * 3 + [row(B_WIDTH), row(D),
                  full(wa), full(wb), full(wo), full(g_mlp)] + [gate(c) for c in range(n_gate_blocks)],
        out_specs=[row(D), row(D)],
        out_shape=[jax.ShapeDtypeStruct((T, D), jnp.float32), jax.ShapeDtypeStruct((T, D), jnp.bfloat16)],
        compiler_params=pltpu.CompilerParams(dimension_semantics=("arbitrary",), vmem_limit_bytes=VMEM_LIMIT),
        name="merge",
    )(*outs, *lses, yb, x2, wa, wb, wo, g_mlp, *([proj2] * n_gate_blocks))


def _mlp_kernel(h_ref, x_ref, wu_ref, wd_ref, o_ref):
    f = pl.program_id(1)

    @pl.when(f == 0)
    def _():
        o_ref[...] = jnp.zeros(o_ref.shape, o_ref.dtype)

    slab = x_ref.shape[1]
    cols = pl.ds(pl.multiple_of(f * slab, slab), slab)
    o_ref[:, cols] += x_ref[...]

    u = jnp.dot(h_ref[...], wu_ref[...], preferred_element_type=jnp.float32)
    u = jnp.square(jnp.maximum(u, 0.0)).astype(jnp.bfloat16)
    o_ref[...] += jnp.dot(u, wd_ref[...], preferred_element_type=jnp.float32)


def _mlp(h2, x2, w_up, w_down, *, tm, tf):
    T, D = x2.shape
    F = w_up.shape[1]
    n_f = F // tf
    slab = D // n_f
    assert slab % HEAD_DIM == 0
    return pl.pallas_call(
        _mlp_kernel,
        grid=(T // tm, n_f),
        in_specs=[pl.BlockSpec((tm, D), lambda i, f: (i, 0)),
                  pl.BlockSpec((tm, slab), lambda i, f: (i, f)),
                  pl.BlockSpec((D, tf), lambda i, f: (0, f)),
                  pl.BlockSpec((tf, D), lambda i, f: (f, 0))],
        out_specs=pl.BlockSpec((tm, D), lambda i, f: (i, 0)),
        out_shape=jax.ShapeDtypeStruct((T, D), jnp.float32),
        compiler_params=pltpu.CompilerParams(
            dimension_semantics=("arbitrary", "arbitrary"), vmem_limit_bytes=VMEM_LIMIT),
        name="mlp",
    )(h2, x2, w_up, w_down)


def _column_gains(q_norm_a, k_norm_a, q_norm_b, k_norm_b, width):
    qs = SCALE * LOG2E
    gains = jnp.concatenate([
        jnp.tile(q_norm_a * qs, A_HEADS), jnp.tile(k_norm_a, A_HEADS), jnp.ones((A_HEADS * HEAD_DIM,), jnp.float32),
        jnp.tile(q_norm_b * qs, B_HEADS), jnp.tile(k_norm_b, B_HEADS), jnp.ones((B_WIDTH,), jnp.float32),
        jnp.ones((width - QKV_WIDTH,), jnp.float32)])
    return gains.reshape(1, width)


def kernel(x, g_mix, w_in, q_norm_a, k_norm_a, q_norm_b, k_norm_b, rel_bias,
           w_branch_a, w_branch_b, w_out, g_mlp, w_up, w_down):
    batch, seq, d_model = x.shape
    T = batch * seq
    assert w_in.shape == (d_model, QKV_WIDTH + 2 * d_model)
    assert seq % (DIL_PATTERNS[-1][1] * A_BLOCK) == 0
    x2 = x.reshape(T, d_model)
    tm = min(ROW_TILE, T)

    dil_bias, moba_bias = _bias_tables(rel_bias)
    gains = _column_gains(q_norm_a, k_norm_a, q_norm_b, k_norm_b, w_in.shape[1])
    proj = _in_proj(x2, g_mix.reshape(1, -1), w_in, gains, tm=tm, tn=min(COL_TILE, d_model * 2))

    outs, lses = zip(*[_dilated_group(proj, dil_bias, g, batch=batch, seq=seq) for g in range(N_GROUPS)])
    yb, (wa, wb, wo, wu, wd) = _moba(proj, moba_bias, (w_branch_a, w_branch_b, w_out, w_up, w_down),
                                     batch=batch, seq=seq)

    x_mid, h_mid = _merge(outs, lses, yb, proj, x2, wa, wb, wo, g_mlp.reshape(1, -1), tm=min(MERGE_ROW_TILE, T))
    y = _mlp(h_mid, x_mid, wu, wd, tm=tm, tf=min(MLP_FF_TILE, w_up.shape[1]))
    return y.reshape(batch, seq, d_model)
```
